```python
import jax
import jax.numpy as jnp
from jax import lax
import numpy as np

D_MODEL = 1024
BATCH = 8
SEQ = 4096
DEPTH = 4

CTX_LEN = 256
GRID_W = 64
N_MIXERS = 3
Q_BLOCK = 128
ROPE_THETA = 10000.0
NORM_EPS = 1e-6
NEG_INF = -1e30

A_HEADS = 8
A_KV_HEADS = 2
A_HEAD_DIM = D_MODEL // A_HEADS
B_HEADS = 16
B_KV_HEADS = 4
B_HEAD_DIM = D_MODEL // B_HEADS
WINDOW = 128
C_HEAD_DIM = 64
C_HEADS = D_MODEL // C_HEAD_DIM
C_DECAY_LORA = 64
C_AAA_LORA = 64
C_GATE_LORA = 128
C_GN_EPS = C_HEAD_DIM * 1e-5
N_EXPERTS = 16
EXPERT_FF = 2 * D_MODEL
CAPACITY_FACTOR = 2

N_A = (DEPTH + 2) // N_MIXERS
N_B = (DEPTH + 1) // N_MIXERS
N_C = DEPTH // N_MIXERS

kernel_name = 'hybrid_diffusion_trunk_gqa_swa_rwkv7_ecmoe'


def rms_norm(x):
    x32 = x.astype(jnp.float32)
    return (x32 * lax.rsqrt(jnp.mean(x32 * x32, axis=-1, keepdims=True) + NORM_EPS)).astype(x.dtype)


def modulate(x, shift, scale):
    return rms_norm(x) * (1 + scale) + shift


def grid_positions(n_tokens):
    n_rows = n_tokens // GRID_W
    rows = jnp.repeat(jnp.arange(n_rows, dtype=jnp.int32), GRID_W)
    cols = jnp.tile(jnp.arange(GRID_W, dtype=jnp.int32), n_rows)
    return rows, cols


def axial_rope_tables(rows, cols, head_dim):
    n_freq = head_dim // 4
    inv_freq = ROPE_THETA ** (-jnp.arange(n_freq, dtype=jnp.float32) / n_freq)
    ang = jnp.concatenate([rows.astype(jnp.float32)[:, None] * inv_freq,
                           cols.astype(jnp.float32)[:, None] * inv_freq], axis=-1)
    return jnp.cos(ang), jnp.sin(ang)


def apply_rope(x, cos, sin):
    bshape = (1, cos.shape[0]) + (1,) * (x.ndim - 3) + (cos.shape[1],)
    cos, sin = cos.reshape(bshape), sin.reshape(bshape)
    x1, x2 = jnp.split(x.astype(jnp.float32), 2, axis=-1)
    return jnp.concatenate([x1 * cos - x2 * sin, x1 * sin + x2 * cos], axis=-1).astype(x.dtype)


def project_qkv(h, w_qkv, n_heads, n_kv, head_dim):
    b, t, _ = h.shape
    q, k, v = jnp.split(h @ w_qkv, [n_heads * head_dim, (n_heads + n_kv) * head_dim], axis=-1)
    return (q.reshape(b, t, n_kv, n_heads // n_kv, head_dim),
            k.reshape(b, t, n_kv, head_dim), v.reshape(b, t, n_kv, head_dim))


def dense_attend(q, k, v, scale):
    s = jnp.einsum('bqhgd,bkhd->bhgqk', q, k).astype(jnp.float32) * scale
    p = jax.nn.softmax(s, axis=-1).astype(v.dtype)
    return jnp.einsum('bhgqk,bkhd->bqhgd', p, v)


def global_gqa(hc, hl, w_qkv, w_o, q_gain, k_gain, cos, sin, last):
    b, n_lat, _ = hl.shape
    scale = A_HEAD_DIM ** -0.5
    qc, kc, vc = project_qkv(hc, w_qkv, A_HEADS, A_KV_HEADS, A_HEAD_DIM)
    ql, kl, vl = project_qkv(hl, w_qkv, A_HEADS, A_KV_HEADS, A_HEAD_DIM)
    qc, ql = rms_norm(qc) * q_gain, rms_norm(ql) * q_gain
    kc, kl = rms_norm(kc) * k_gain, rms_norm(kl) * k_gain
    ql, kl = apply_rope(ql, cos, sin), apply_rope(kl, cos, sin)
    k_all = jnp.concatenate([kl, kc], axis=1)
    v_all = jnp.concatenate([vl, vc], axis=1)
    n_blk = n_lat // Q_BLOCK
    q_blocks = jnp.moveaxis(ql.reshape(b, n_blk, Q_BLOCK, *ql.shape[2:]), 1, 0)
    o = lax.map(lambda q: dense_attend(q, k_all, v_all, scale), q_blocks)
    yl = jnp.moveaxis(o, 0, 1).reshape(b, n_lat, D_MODEL) @ w_o
    if last:
        return None, yl
    yc = dense_attend(qc, kc, vc, scale).reshape(hc.shape) @ w_o
    return yc, yl


def sink_softmax(scores, sink):
    col = jnp.broadcast_to(sink.astype(jnp.float32)[:, :, None, None], scores[0].shape[:-1] + (1,))
    p = jax.nn.softmax(jnp.concatenate(scores + [col], axis=-1), axis=-1)
    return p[..., :-1]


def window_gqa_sink(hc, hl, w_qkv, w_o, sink, cos, sin, last):
    b, n_lat, _ = hl.shape
    scale = B_HEAD_DIM ** -0.5
    sink = sink.reshape(B_KV_HEADS, B_HEADS // B_KV_HEADS)
    qc, kc, vc = project_qkv(hc, w_qkv, B_HEADS, B_KV_HEADS, B_HEAD_DIM)
    ql, kl, vl = project_qkv(hl, w_qkv, B_HEADS, B_KV_HEADS, B_HEAD_DIM)
    ql, kl = apply_rope(ql, cos, sin), apply_rope(kl, cos, sin)
    pad = ((0, 0), (Q_BLOCK, Q_BLOCK), (0, 0), (0, 0))
    kp, vp = jnp.pad(kl, pad), jnp.pad(vl, pad)
    n_blk = n_lat // Q_BLOCK
    q_blocks = jnp.moveaxis(ql.reshape(b, n_blk, Q_BLOCK, *ql.shape[2:]), 1, 0)

    def block(args):
        blk, q = args
        start = blk * Q_BLOCK
        kw = lax.dynamic_slice_in_dim(kp, start, 3 * Q_BLOCK, axis=1)
        vw = lax.dynamic_slice_in_dim(vp, start, 3 * Q_BLOCK, axis=1)
        q_pos = start + jnp.arange(Q_BLOCK)
        k_pos = start - Q_BLOCK + jnp.arange(3 * Q_BLOCK)
        band = ((jnp.abs(k_pos[None, :] - q_pos[:, None]) <= WINDOW)
                & (k_pos >= 0)[None, :] & (k_pos < n_lat)[None, :])
        s_win = jnp.where(band, jnp.einsum('bqhgd,bkhd->bhgqk', q, kw).astype(jnp.float32) * scale, NEG_INF)
        s_ctx = jnp.einsum('bqhgd,bkhd->bhgqk', q, kc).astype(jnp.float32) * scale
        p = sink_softmax([s_win, s_ctx], sink).astype(vw.dtype)
        return (jnp.einsum('bhgqk,bkhd->bqhgd', p[..., :3 * Q_BLOCK], vw)
                + jnp.einsum('bhgqk,bkhd->bqhgd', p[..., 3 * Q_BLOCK:], vc))

    o = lax.map(block, (jnp.arange(n_blk), q_blocks))
    yl = jnp.moveaxis(o, 0, 1).reshape(b, n_lat, D_MODEL) @ w_o
    if last:
        return None, yl
    s = jnp.einsum('bqhgd,bkhd->bhgqk', qc, kc).astype(jnp.float32) * scale
    p = sink_softmax([s], sink).astype(vc.dtype)
    yc = jnp.einsum('bhgqk,bkhd->bqhgd', p, vc).reshape(hc.shape) @ w_o
    return yc, yl


def centred_shift(x):
    xp = jnp.pad(x, ((0, 0), (1, 1), (0, 0)))
    return 0.5 * (xp[:, :-2] + xp[:, 2:]) - x


def rwkv_features(h, mu, w_rkv, w0, w1, w2, a0, a1, a2, g1, g2, k_k, k_a):
    f32 = jnp.float32
    heads = lambda t: t.reshape(*t.shape[:-1], C_HEADS, C_HEAD_DIM)
    xx = centred_shift(h)
    xr, xw, xk, xv, xa, xg = h[None] + xx[None] * mu[:, None, None, :]
    r, k, v = jnp.einsum('jbtd,jde->jbte', jnp.stack([xr, xk, xv]), w_rkv)
    w_lora = jnp.einsum('jbtr,jrd->jbtd', jnp.tanh(jnp.einsum('btd,jdr->jbtr', xw, w1)), w2)
    log_w = -jax.nn.softplus(-(w0[:, None, None, :] + w_lora).astype(f32)) - 0.5
    decay = jnp.exp(-jnp.exp(log_w))
    a = jax.nn.sigmoid((a0[:, None, None, :]
                        + jnp.einsum('jbtr,jrd->jbtd', jnp.einsum('btd,jdr->jbtr', xa, a1), a2)).astype(f32))
    g = jax.nn.sigmoid(xg @ g1) @ g2
    kk = heads((k * k_k).astype(f32))
    kk = kk * lax.rsqrt(jnp.maximum(jnp.sum(kk * kk, axis=-1, keepdims=True), 1e-24))
    k_dir = heads(k.astype(f32)[None] * (1 + (a - 1) * k_a.astype(f32)))
    return heads(r.astype(f32)), heads(decay), k_dir, heads(v.astype(f32)), -kk, kk[None] * heads(a), g


def wkv_scan(r, w, k, v, a, b, state0, reverse):
    def step(s, inp):
        r_t, w_t, k_t, v_t, a_t, b_t = inp
        sa = jnp.einsum('bhvk,bhk->bhv', s, a_t)
        s = s * w_t[:, :, None, :] + sa[..., None] * b_t[:, :, None, :] + v_t[..., None] * k_t[:, :, None, :]
        return s, jnp.einsum('bhvk,bhk->bhv', s, r_t)
    xs = tuple(jnp.swapaxes(t, 0, 1) for t in (r, w, k, v, a, b))
    s_final, o = lax.scan(step, state0, xs, reverse=reverse)
    return s_final, jnp.swapaxes(o, 0, 1)


def rwkv7_bidir(hc, hl, mu, w_rkv, w_o, w0, w1, w2, a0, a1, a2, g1, g2, k_k, k_a, r_k, ln_w, ln_b, last):
    fc = rwkv_features(hc, mu, w_rkv, w0, w1, w2, a0, a1, a2, g1, g2, k_k, k_a)
    fl = rwkv_features(hl, mu, w_rkv, w0, w1, w2, a0, a1, a2, g1, g2, k_k, k_a)
    zero = jnp.zeros((hl.shape[0], C_HEADS, C_HEAD_DIM, C_HEAD_DIM), jnp.float32)

    def scan_dir(f, d, state0):
        r, w, k, v, a_vec, b_vec, _ = f
        return wkv_scan(r, w[d], k[d], v, a_vec, b_vec[d], state0, reverse=(d == 1))

    s_fc, o_fc = scan_dir(fc, 0, zero)
    s_bc, o_bc = scan_dir(fc, 1, zero)
    _, o_fl = scan_dir(fl, 0, s_fc)
    _, o_bl = scan_dir(fl, 1, s_bc)

    def readout(f, o_f, o_b, dtype):
        r, _, k, v, _, _, g = f
        o = o_f + o_b
        o = o - jnp.mean(o, axis=-1, keepdims=True)
        o = o * lax.rsqrt(jnp.mean(o * o, axis=-1, keepdims=True) + C_GN_EPS)
        o = o.reshape(*o.shape[:2], D_MODEL) * ln_w + ln_b
        bonus = jnp.sum(jnp.sum(r[None] * k * r_k[:, None, None], axis=-1, keepdims=True), axis=0) * v
        return ((o + bonus.reshape(o.shape)) * g).astype(dtype) @ w_o

    yl = readout(fl, o_fl, o_bl, hl.dtype)
    if last:
        return None, yl
    return readout(fc, o_fc, o_bc, hc.dtype), yl


def expert_choice_ffn(h, router_w, w1, w3, w2):
    b, t, d = h.shape
    cap = CAPACITY_FACTOR * t // N_EXPERTS
    aff = jax.nn.softmax((h @ router_w).astype(jnp.float32), axis=-1)
    gate, idx = lax.top_k(jnp.swapaxes(aff, 1, 2), cap)
    xs = jax.vmap(lambda hb, ib: hb[ib])(h, idx)
    hid = jax.nn.silu(jnp.einsum('becd,edf->becf', xs, w1)) * jnp.einsum('becd,edf->becf', xs, w3)
    out = jnp.einsum('becf,efd->becd', hid, w2) * gate[..., None].astype(h.dtype)
    return jax.vmap(lambda ib, ob: jnp.zeros((t, d), h.dtype).at[ib.reshape(-1)].add(ob.reshape(-1, d)))(idx, out)


def setup_inputs(seed: int = 0) -> dict:
    key = jax.random.key(seed)
    ks = iter(jax.random.split(key, 48))
    f32 = jnp.float32
    D = D_MODEL

    def nrm(shape, scale):
        return jax.random.normal(next(ks), shape, f32) * scale

    qkv_a = (A_HEADS + 2 * A_KV_HEADS) * A_HEAD_DIM
    qkv_b = (B_HEADS + 2 * B_KV_HEADS) * B_HEAD_DIM
    return {
        'x': nrm((BATCH, SEQ, D), 1.0),
        'c': nrm((BATCH, D), 1.0),
        'ctx': nrm((BATCH, CTX_LEN, D), 1.0),
        'c_ctx': nrm((D,), 1.0),
        'mod_w': nrm((DEPTH, D, 6 * D), 0.5 * D ** -0.5),
        'mod_b': nrm((DEPTH, 6 * D), 0.02),
        'a_w_qkv': nrm((N_A, D, qkv_a), D ** -0.5),
        'a_w_o': nrm((N_A, D, D), D ** -0.5),
        'a_q_norm': 1.0 + nrm((N_A, A_HEAD_DIM), 0.02),
        'a_k_norm': 1.0 + nrm((N_A, A_HEAD_DIM), 0.02),
        'b_w_qkv': nrm((N_B, D, qkv_b), D ** -0.5),
        'b_w_o': nrm((N_B, D, D), D ** -0.5),
        'b_sink': nrm((N_B, B_HEADS), 0.5),
        'c_mu': jax.random.uniform(next(ks), (N_C, 6, D), f32),
        'c_w_rkv': nrm((N_C, 3, D, D), D ** -0.5),
        'c_w_o': nrm((N_C, D, D), D ** -0.5),
        'c_w0': jax.random.uniform(next(ks), (N_C, 2, D), f32, minval=-6.0, maxval=0.0),
        'c_w1': nrm((N_C, 2, D, C_DECAY_LORA), 0.1 * D ** -0.5),
        'c_w2': nrm((N_C, 2, C_DECAY_LORA, D), 0.1 * C_DECAY_LORA ** -0.5),
        'c_a0': nrm((N_C, 2, D), 0.1),
        'c_a1': nrm((N_C, 2, D, C_AAA_LORA), 0.1 * D ** -0.5),
        'c_a2': nrm((N_C, 2, C_AAA_LORA, D), 0.1 * C_AAA_LORA ** -0.5),
        'c_g1': nrm((N_C, D, C_GATE_LORA), D ** -0.5),
        'c_g2': nrm((N_C, C_GATE_LORA, D), C_GATE_LORA ** -0.5),
        'c_k_k': 0.85 + nrm((N_C, D), 0.02),
        'c_k_a': 1.0 + nrm((N_C, D), 0.02),
        'c_r_k': nrm((N_C, 2, C_HEADS, C_HEAD_DIM), 0.1),
        'c_ln_w': 1.0 + nrm((N_C, D), 0.02),
        'c_ln_b': nrm((N_C, D), 0.02),
        'router_w': nrm((DEPTH, D, N_EXPERTS), D ** -0.5),
        'ffn_w1': nrm((DEPTH, N_EXPERTS, D, EXPERT_FF), D ** -0.5),
        'ffn_w3': nrm((DEPTH, N_EXPERTS, D, EXPERT_FF), D ** -0.5),
        'ffn_w2': nrm((DEPTH, N_EXPERTS, EXPERT_FF, D), EXPERT_FF ** -0.5),
        'final_norm': 1.0 + nrm((D,), 0.02),
    }


def reference(x, c, ctx, c_ctx, mod_w, mod_b, a_w_qkv, a_w_o, a_q_norm, a_k_norm, b_w_qkv, b_w_o, b_sink,
              c_mu, c_w_rkv, c_w_o, c_w0, c_w1, c_w2, c_a0, c_a1, c_a2, c_g1, c_g2, c_k_k, c_k_a, c_r_k,
              c_ln_w, c_ln_b, router_w, ffn_w1, ffn_w3, ffn_w2, final_norm):
    n_lat = x.shape[1]
    rows, cols = grid_positions(n_lat)
    cos_a, sin_a = axial_rope_tables(rows, cols, A_HEAD_DIM)
    cos_b, sin_b = axial_rope_tables(rows, cols, B_HEAD_DIM)
    cond_l = jax.nn.silu(c)
    cond_c = jax.nn.silu(c_ctx)
    xl, xc = x, ctx
    for i in range(DEPTH):
        last = i == DEPTH - 1
        kind, j = i % N_MIXERS, i // N_MIXERS
        mod_l = (cond_l @ mod_w[i] + mod_b[i]).reshape(-1, 6, 1, D_MODEL)
        mod_c = (cond_c @ mod_w[i] + mod_b[i]).reshape(6, D_MODEL)
        hl = modulate(xl, mod_l[:, 0], mod_l[:, 1])
        hc = modulate(xc, mod_c[0], mod_c[1])
        if kind == 0:
            yc, yl = global_gqa(hc, hl, a_w_qkv[j], a_w_o[j], a_q_norm[j], a_k_norm[j], cos_a, sin_a, last)
        elif kind == 1:
            yc, yl = window_gqa_sink(hc, hl, b_w_qkv[j], b_w_o[j], b_sink[j], cos_b, sin_b, last)
        else:
            yc, yl = rwkv7_bidir(hc, hl, c_mu[j], c_w_rkv[j], c_w_o[j], c_w0[j], c_w1[j], c_w2[j],
                                 c_a0[j], c_a1[j], c_a2[j], c_g1[j], c_g2[j], c_k_k[j], c_k_a[j],
                                 c_r_k[j], c_ln_w[j], c_ln_b[j], last)
        xl = xl + mod_l[:, 2] * yl
        hl = modulate(xl, mod_l[:, 3], mod_l[:, 4])
        xl = xl + mod_l[:, 5] * expert_choice_ffn(hl, router_w[i], ffn_w1[i], ffn_w3[i], ffn_w2[i])
        if not last:
            xc = xc + mod_c[2] * yc
            hc = modulate(xc, mod_c[3], mod_c[4])
            xc = xc + mod_c[5] * expert_choice_ffn(hc, router_w[i], ffn_w1[i], ffn_w3[i], ffn_w2[i])
    return rms_norm(xl) * final_norm
```

```python
import functools
import math

import jax
import jax.numpy as jnp
from jax import lax
from jax.experimental import pallas as pl
from jax.experimental.pallas import tpu as pltpu

F32 = jnp.float32
BF16 = jnp.bfloat16
I32 = jnp.int32

D_MODEL = 1024
GRID_W = 64
Q_BLOCK = 128
ROPE_THETA = 10000.0
NORM_EPS = 1e-6
NEG_INF = -1e30
A_HEADS, A_KV_HEADS, A_HEAD_DIM = 8, 2, 128
B_HEADS, B_KV_HEADS, B_HEAD_DIM = 16, 4, 64
WINDOW = 128
C_HEAD_DIM = 64
C_HEADS = D_MODEL // C_HEAD_DIM
C_GN_EPS = C_HEAD_DIM * 1e-5
N_EXPERTS = 16
EXPERT_FF = 2 * D_MODEL
CAPACITY_FACTOR = 2
N_MIXERS = 3

ROW_TILE = 256
SLOT_TILE = 128
SCAN_CHUNK = 16
VMEM_LIMIT = 56 * 1024 * 1024


def _params(*sem):
    return pltpu.CompilerParams(dimension_semantics=sem, vmem_limit_bytes=VMEM_LIMIT)


def _rms(x):
    return x * lax.rsqrt(jnp.mean(x * x, axis=-1, keepdims=True) + NORM_EPS)


def _normmod(x, mod_ref, shift_row, scale_row):
    return _rms(x) * (1.0 + mod_ref[scale_row:scale_row + 1, :]) + mod_ref[shift_row:shift_row + 1, :]


def _bdot(a, b):
    return jnp.dot(a.astype(BF16), b.astype(BF16), preferred_element_type=F32)


def _dot_nt(a, b):
    return lax.dot_general(a.astype(BF16), b.astype(BF16), (((1,), (1,)), ((), ())),
                           preferred_element_type=F32)


def _sigmoid(x):
    return 1.0 / (1.0 + jnp.exp(-x))


def _mod_kernel(cond_ref, w_ref, b_ref, o_ref):
    c = cond_ref[...]
    a = c * _sigmoid(c)
    o_ref[0] = _bdot(a, w_ref[0]) + b_ref[0]


def _mod_tables(cond, mod_w, mod_b):
    depth, d, n = mod_w.shape
    rows = cond.shape[0]
    return pl.pallas_call(
        _mod_kernel,
        grid=(depth, n // d),
        in_specs=[pl.BlockSpec((rows, d), lambda i, j: (0, 0)),
                  pl.BlockSpec((1, d, d), lambda i, j: (i, 0, j)),
                  pl.BlockSpec((1, 1, d), lambda i, j: (i, 0, j))],
        out_specs=pl.BlockSpec((1, rows, d), lambda i, j: (i, 0, j)),
        out_shape=jax.ShapeDtypeStruct((depth, rows, n), F32),
        compiler_params=_params("arbitrary", "arbitrary"),
        name="mod_tables",
    )(cond, mod_w, mod_b.reshape(depth, 1, n))


def _mod_spec(n_ctx_tiles):
    return pl.BlockSpec((None, None, 6, D_MODEL),
                        lambda b, t: (b, jnp.where(t >= n_ctx_tiles, 1, 0), 0, 0))


def _rope128(x, cos, sin_signed, half):
    if half == 64:
        rot = pltpu.roll(x, 64, axis=1)
    else:
        lane = lax.broadcasted_iota(I32, x.shape, 1)
        rot = jnp.where((lane % 64) < 32, pltpu.roll(x, 96, axis=1), pltpu.roll(x, 32, axis=1))
    return x * cos + rot * sin_signed


def _qkv_kernel(x_ref, mod_ref, w_ref, gq_ref, gk_ref, cos_ref, sin_ref, q_ref, k_ref, v_ref, *, kind):
    h = _normmod(x_ref[...], mod_ref, 0, 1)
    y = _bdot(h, w_ref[...])
    cos, sin = cos_ref[...], sin_ref[...]
    nq = D_MODEL
    nkv = (y.shape[1] - nq) // 2
    half = 64 if kind == 0 else 32
    for j in range((nq + nkv) // 128):
        s = y[:, j * 128:(j + 1) * 128]
        if kind == 0:
            gain = gq_ref[...] if j < nq // 128 else gk_ref[...]
            s = _rms(s) * gain
        s = _rope128(s, cos, sin, half).astype(BF16)
        if j < nq // 128:
            q_ref[:, j * 128:(j + 1) * 128] = s
        elif kind == 0:
            k_ref[:, (j - nq // 128) * 128:(j - nq // 128 + 1) * 128] = s
        else:
            jj = (j - nq // 128) * 2
            k_ref[jj] = s[:, :64]
            k_ref[jj + 1] = s[:, 64:]
    v = y[:, nq + nkv:].astype(BF16)
    if kind == 0:
        v_ref[...] = v
    else:
        for jj in range(nkv // 64):
            v_ref[jj] = v[:, jj * 64:(jj + 1) * 64]


def _qkv_proj(x, modtab, w, gq, gk, cos, sin, *, kind, n_ctx):
    b, t_all, d = x.shape
    tm = ROW_TILE
    nt = t_all // tm
    n = w.shape[1]
    nkv = (n - d) // 2
    if kind == 0:
        kv_shape = jax.ShapeDtypeStruct((b, t_all, nkv), BF16)
        kv_spec = pl.BlockSpec((None, tm, nkv), lambda i, t: (i, t, 0))
    else:
        kv_shape = jax.ShapeDtypeStruct((b, nkv // 64, t_all, 64), BF16)
        kv_spec = pl.BlockSpec((None, nkv // 64, tm, 64), lambda i, t: (i, 0, t, 0))
    return pl.pallas_call(
        functools.partial(_qkv_kernel, kind=kind),
        grid=(b, nt),
        in_specs=[pl.BlockSpec((None, tm, d), lambda i, t: (i, t, 0)),
                  _mod_spec(n_ctx // tm),
                  pl.BlockSpec((d, n), lambda i, t: (0, 0)),
                  pl.BlockSpec((1, 128), lambda i, t: (0, 0)),
                  pl.BlockSpec((1, 128), lambda i, t: (0, 0)),
                  pl.BlockSpec((tm, 128), lambda i, t: (t, 0)),
                  pl.BlockSpec((tm, 128), lambda i, t: (t, 0))],
        out_specs=[pl.BlockSpec((None, tm, d), lambda i, t: (i, t, 0)), kv_spec, kv_spec],
        out_shape=[jax.ShapeDtypeStruct((b, t_all, d), BF16), kv_shape, kv_shape],
        compiler_params=_params("parallel", "parallel"),
        name=f"qkv_proj_{kind}",
    )(x, modtab, w, gq, gk, cos, sin)


def _flash_chunk(carry, q, kc, vc, scale):
    m, l, acc = carry
    s = _dot_nt(q, kc) * scale
    m_new = jnp.maximum(m, jnp.max(s, axis=-1, keepdims=True))
    alpha = jnp.exp(m - m_new)
    p = jnp.exp(s - m_new)
    l = alpha * l + jnp.sum(p, axis=-1, keepdims=True)
    acc = alpha * acc + jnp.dot(p.astype(BF16), vc, preferred_element_type=F32)
    return m_new, l, acc


def _gattn_kernel(q_ref, k_ref, v_ref, o_ref, *, n_ctx, n_ctx_tiles, kchunk):
    hd = A_HEAD_DIM
    g = A_HEADS // A_KV_HEADS
    tq = q_ref.shape[0]
    t_all = k_ref.shape[0]
    scale = hd ** -0.5
    q = jnp.concatenate([q_ref[:, i * hd:(i + 1) * hd] for i in range(g)], axis=0)
    init = (jnp.full((g * tq, 1), NEG_INF, F32), jnp.zeros((g * tq, 1), F32), jnp.zeros((g * tq, hd), F32))

    def finish(carry):
        _, l, acc = carry
        o = (acc / l).astype(BF16)
        for i in range(g):
            o_ref[:, i * hd:(i + 1) * hd] = o[i * tq:(i + 1) * tq]

    ctx_carry = _flash_chunk(init, q, k_ref[0:n_ctx, :], v_ref[0:n_ctx, :], scale)

    @pl.when(pl.program_id(2) < n_ctx_tiles)
    def _():
        finish(ctx_carry)

    @pl.when(pl.program_id(2) >= n_ctx_tiles)
    def _():
        def body(c, carry):
            start = pl.multiple_of(n_ctx + c * kchunk, 128)
            return _flash_chunk(carry, q, k_ref[pl.ds(start, kchunk), :], v_ref[pl.ds(start, kchunk), :], scale)
        finish(lax.fori_loop(0, (t_all - n_ctx) // kchunk, body, ctx_carry))


def _global_attention(q, k, v, *, n_ctx):
    b, t_all, d = q.shape
    hd, g = A_HEAD_DIM, A_HEADS // A_KV_HEADS
    tq = 128
    return pl.pallas_call(
        functools.partial(_gattn_kernel, n_ctx=n_ctx, n_ctx_tiles=n_ctx // tq, kchunk=512),
        grid=(b, A_KV_HEADS, t_all // tq),
        in_specs=[pl.BlockSpec((None, tq, g * hd), lambda i, h, t: (i, t, h)),
                  pl.BlockSpec((None, t_all, hd), lambda i, h, t: (i, 0, h)),
                  pl.BlockSpec((None, t_all, hd), lambda i, h, t: (i, 0, h))],
        out_specs=pl.BlockSpec((None, tq, g * hd), lambda i, h, t: (i, t, h)),
        out_shape=jax.ShapeDtypeStruct((b, t_all, d), BF16),
        compiler_params=_params("parallel", "parallel", "arbitrary"),
        name="global_attention",
    )(q, k, v)


def _wattn_kernel(q_ref, k_ref, v_ref, sink_ref, o_ref, *, n_ctx, n_ctx_tiles):
    hd = B_HEAD_DIM
    g = B_HEADS // B_KV_HEADS
    tq = q_ref.shape[0]
    t_all = k_ref.shape[0]
    n_lat = t_all - n_ctx
    scale = hd ** -0.5
    t = pl.program_id(2)
    q = jnp.concatenate([q_ref[:, i * hd:(i + 1) * hd] for i in range(g)], axis=0)
    sink = jnp.concatenate([jnp.broadcast_to(sink_ref[:, i:i + 1], (tq, 1)) for i in range(g)], axis=0)
    kc, vc = k_ref[0:n_ctx, :], v_ref[0:n_ctx, :]
    s_ctx = _dot_nt(q, kc) * scale

    def finish(o):
        o = o.astype(BF16)
        o_ref[...] = jnp.concatenate([o[i * tq:(i + 1) * tq] for i in range(g)], axis=1)

    @pl.when(t < n_ctx_tiles)
    def _():
        m = jnp.maximum(jnp.max(s_ctx, axis=-1, keepdims=True), sink)
        p = jnp.exp(s_ctx - m)
        l = jnp.sum(p, axis=-1, keepdims=True) + jnp.exp(sink - m)
        finish(jnp.dot(p.astype(BF16), vc, preferred_element_type=F32) / l)

    @pl.when(t >= n_ctx_tiles)
    def _():
        blk = t - n_ctx_tiles
        wlen = 3 * Q_BLOCK
        ws = jnp.clip((blk - 1) * Q_BLOCK, 0, n_lat - wlen)
        start = pl.multiple_of(n_ctx + ws, 128)
        kw, vw = k_ref[pl.ds(start, wlen), :], v_ref[pl.ds(start, wlen), :]
        row = lax.broadcasted_iota(I32, (g * tq, wlen), 0)
        col = lax.broadcasted_iota(I32, (g * tq, wlen), 1)
        q_pos = blk * Q_BLOCK + row % tq
        k_pos = ws + col
        band = jnp.abs(k_pos - q_pos) <= WINDOW
        s_win = jnp.where(band, _dot_nt(q, kw) * scale, NEG_INF)
        m = jnp.maximum(jnp.maximum(jnp.max(s_win, axis=-1, keepdims=True),
                                    jnp.max(s_ctx, axis=-1, keepdims=True)), sink)
        p_win = jnp.exp(s_win - m)
        p_ctx = jnp.exp(s_ctx - m)
        l = (jnp.sum(p_win, axis=-1, keepdims=True) + jnp.sum(p_ctx, axis=-1, keepdims=True)
             + jnp.exp(sink - m))
        o = (jnp.dot(p_win.astype(BF16), vw, preferred_element_type=F32)
             + jnp.dot(p_ctx.astype(BF16), vc, preferred_element_type=F32))
        finish(o / l)


def _window_attention(q, k, v, sink, *, n_ctx):
    b, t_all, d = q.shape
    hd, g = B_HEAD_DIM, B_HEADS // B_KV_HEADS
    tq = Q_BLOCK
    return pl.pallas_call(
        functools.partial(_wattn_kernel, n_ctx=n_ctx, n_ctx_tiles=n_ctx // tq),
        grid=(b, B_KV_HEADS, t_all // tq),
        in_specs=[pl.BlockSpec((None, tq, g * hd), lambda i, h, t: (i, t, h)),
                  pl.BlockSpec((None, None, t_all, hd), lambda i, h, t: (i, h, 0, 0)),
                  pl.BlockSpec((None, None, t_all, hd), lambda i, h, t: (i, h, 0, 0)),
                  pl.BlockSpec((None, 1, g), lambda i, h, t: (h, 0, 0))],
        out_specs=pl.BlockSpec((None, tq, g * hd), lambda i, h, t: (i, t, h)),
        out_shape=jax.ShapeDtypeStruct((b, t_all, d), BF16),
        compiler_params=_params("parallel", "parallel", "arbitrary"),
        name="window_attention",
    )(q, k, v, sink.reshape(B_KV_HEADS, 1, g))


def _proj_res_kernel(a_ref, w_ref, x_ref, mod_ref, o_ref):
    o_ref[...] = x_ref[...] + mod_ref[2:3, :] * _bdot(a_ref[...], w_ref[...])


def _proj_residual(a, w, x, modtab, *, n_ctx):
    b, t_all, d = x.shape
    tm = ROW_TILE
    return pl.pallas_call(
        _proj_res_kernel,
        grid=(b, t_all // tm),
        in_specs=[pl.BlockSpec((None, tm, d), lambda i, t: (i, t, 0)),
                  pl.BlockSpec((d, d), lambda i, t: (0, 0)),
                  pl.BlockSpec((None, tm, d), lambda i, t: (i, t, 0)),
                  _mod_spec(n_ctx // tm)],
        out_specs=pl.BlockSpec((None, tm, d), lambda i, t: (i, t, 0)),
        out_shape=jax.ShapeDtypeStruct((b, t_all, d), F32),
        compiler_params=_params("parallel", "parallel"),
        name="proj_residual",
    )(a, w, x, modtab)


def _router_kernel(x_ref, mod_ref, rwt_ref, h_ref, aff_ref, afft_ref):
    h = _normmod(x_ref[...], mod_ref, 3, 4)
    h_ref[...] = h.astype(BF16)
    logits_t = lax.dot_general(rwt_ref[...], h, (((1,), (1,)), ((), ())),
                               precision=lax.Precision.HIGHEST, preferred_element_type=F32)
    e = jnp.exp(logits_t - jnp.max(logits_t, axis=0, keepdims=True))
    aff_t = e / jnp.sum(e, axis=0, keepdims=True)
    afft_ref[...] = aff_t
    aff_ref[...] = aff_t.T


def _router(x, modtab, router_w_t, *, n_ctx):
    b, t_all, d = x.shape
    tm = ROW_TILE
    ne = router_w_t.shape[0]
    return pl.pallas_call(
        _router_kernel,
        grid=(b, t_all // tm),
        in_specs=[pl.BlockSpec((None, tm, d), lambda i, t: (i, t, 0)),
                  _mod_spec(n_ctx // tm),
                  pl.BlockSpec((ne, d), lambda i, t: (0, 0))],
        out_specs=[pl.BlockSpec((None, tm, d), lambda i, t: (i, t, 0)),
                   pl.BlockSpec((None, tm, ne), lambda i, t: (i, t, 0)),
                   pl.BlockSpec((None, ne, tm), lambda i, t: (i, 0, t))],
        out_shape=[jax.ShapeDtypeStruct((b, t_all, d), BF16),
                   jax.ShapeDtypeStruct((b, t_all, ne), F32),
                   jax.ShapeDtypeStruct((b, ne, t_all), F32)],
        compiler_params=_params("parallel", "parallel"),
        name="router",
    )(x, modtab, router_w_t)


def _lane_prefix(mask_f, tri):
    n = mask_f.shape[1]
    run = jnp.zeros((mask_f.shape[0], 1), F32)
    out = []
    for j in range(n // 128):
        blk = mask_f[:, j * 128:(j + 1) * 128]
        incl = jnp.dot(blk.astype(BF16), tri, preferred_element_type=F32)
        out.append(incl - blk + run)
        run = run + incl[:, 127:128]
    return jnp.concatenate(out, axis=1)


def _select_kernel(afft_ref, slot_ref, slot_tm_ref, *, segments):
    r = lax.broadcasted_iota(I32, (128, 128), 0)
    c = lax.broadcasted_iota(I32, (128, 128), 1)
    tri = (r <= c).astype(BF16)
    for off, n, cap, base in segments:
        bits = lax.bitcast_convert_type(afft_ref[:, off:off + n], I32)
        cap_f = jnp.float32(cap)

        def body(i, thr):
            cand = thr | jnp.left_shift(jnp.int32(1), 30 - i)
            cnt = jnp.sum((bits >= cand).astype(F32), axis=1, keepdims=True)
            return jnp.where(cnt >= cap_f, cand, thr)

        thr = lax.fori_loop(0, 31, body, jnp.zeros((bits.shape[0], 1), I32))
        gt = (bits > thr).astype(F32)
        eq = (bits == thr).astype(F32)
        need = cap_f - jnp.sum(gt, axis=1, keepdims=True)
        sel = gt + eq * (_lane_prefix(eq, tri) < need).astype(F32)
        slot = _lane_prefix(sel, tri).astype(I32) + base
        slot = jnp.where(sel > 0.5, slot, -1)
        slot_ref[:, off:off + n] = slot
        slot_tm_ref[off:off + n, :] = slot.astype(F32).T


def _select(aff_t, segments):
    b, ne, t_all = aff_t.shape
    return pl.pallas_call(
        functools.partial(_select_kernel, segments=segments),
        grid=(b,),
        in_specs=[pl.BlockSpec((None, ne, t_all), lambda i: (i, 0, 0))],
        out_specs=[pl.BlockSpec((None, ne, t_all), lambda i: (i, 0, 0)),
                   pl.BlockSpec((None, t_all, ne), lambda i: (i, 0, 0))],
        out_shape=[jax.ShapeDtypeStruct((b, ne, t_all), I32), jax.ShapeDtypeStruct((b, t_all, ne), F32)],
        compiler_params=_params("parallel"),
        name="expert_select",
    )(aff_t)


def _gather_kernel(slot_ref, h_ref, xs_ref, *, segments):
    for off, n, cap, base in segments:
        slot = slot_ref[:, off:off + n]
        hseg = h_ref[off:off + n, :]
        for j in range(-(-cap // SLOT_TILE)):
            ids = base + j * SLOT_TILE + lax.broadcasted_iota(I32, (SLOT_TILE, 1), 0)
            onehot = jnp.where(ids == slot, 1.0, 0.0).astype(BF16)
            rows = jnp.dot(onehot, hseg, preferred_element_type=F32)
            xs_ref[base + j * SLOT_TILE:base + (j + 1) * SLOT_TILE, :] = rows.astype(BF16)


def _gather(slot, h, segments, n_slots):
    b, ne, t_all = slot.shape
    d = h.shape[2]
    return pl.pallas_call(
        functools.partial(_gather_kernel, segments=segments),
        grid=(b, ne),
        in_specs=[pl.BlockSpec((None, None, 1, t_all), lambda i, e: (i, e, 0, 0)),
                  pl.BlockSpec((None, t_all, d), lambda i, e: (i, 0, 0))],
        out_specs=pl.BlockSpec((None, None, n_slots, d), lambda i, e: (e, i, 0, 0)),
        out_shape=jax.ShapeDtypeStruct((ne, b, n_slots, d), BF16),
        compiler_params=_params("parallel", "arbitrary"),
        name="expert_gather",
    )(slot.reshape(b, ne, 1, t_all), h)


def _ffn_kernel(xs_ref, w1_ref, w3_ref, w2_ref, y_ref):
    xs = xs_ref[...]
    a = jnp.dot(xs, w1_ref[...], preferred_element_type=F32)
    g = jnp.dot(xs, w3_ref[...], preferred_element_type=F32)
    hid = (a * _sigmoid(a)) * g
    y = _bdot(hid, w2_ref[...])
    hi = y.astype(BF16)
    d = y.shape[1]
    y_ref[:, :d] = hi
    y_ref[:, d:] = (y - hi.astype(F32)).astype(BF16)


def _expert_ffn(xs, w1, w3, w2):
    ne, m, d = xs.shape
    ff = w1.shape[2]
    tm = 640 if m % 640 == 0 else 128
    return pl.pallas_call(
        _ffn_kernel,
        grid=(ne, m // tm),
        in_specs=[pl.BlockSpec((None, tm, d), lambda e, i: (e, i, 0)),
                  pl.BlockSpec((None, d, ff), lambda e, i: (e, 0, 0)),
                  pl.BlockSpec((None, d, ff), lambda e, i: (e, 0, 0)),
                  pl.BlockSpec((None, ff, d), lambda e, i: (e, 0, 0))],
        out_specs=pl.BlockSpec((None, tm, 2 * d), lambda e, i: (e, i, 0)),
        out_shape=jax.ShapeDtypeStruct((ne, m, 2 * d), BF16),
        compiler_params=_params("parallel", "arbitrary"),
        name="expert_ffn",
    )(xs, w1, w3, w2)


def _combine_kernel(slot_ref, aff_ref, y_ref, x_ref, mod_ref, o_ref, acc_ref, *, tile_slots):
    t, e = pl.program_id(1), pl.program_id(2)
    ne = pl.num_programs(2)
    d = x_ref.shape[1]

    @pl.when(e == 0)
    def _():
        acc_ref[...] = jnp.zeros_like(acc_ref)

    lane_e = lax.broadcasted_iota(I32, aff_ref.shape, 1)
    gate = jnp.sum(jnp.where(lane_e == e, aff_ref[...], 0.0), axis=1, keepdims=True)
    slot = jnp.sum(jnp.where(lane_e == e, slot_ref[...], 0.0), axis=1, keepdims=True).astype(I32)
    for ti, (base, width) in enumerate(tile_slots):
        @pl.when(t == ti)
        def _():
            ids = base + lax.broadcasted_iota(I32, (1, width), 1)
            onehot = jnp.where(slot == ids, 1.0, 0.0).astype(BF16)
            z = jnp.dot(onehot, y_ref[base:base + width, :], preferred_element_type=F32)
            acc_ref[...] += gate * (z[:, :d] + z[:, d:])

    @pl.when(e == ne - 1)
    def _():
        o_ref[...] = x_ref[...] + mod_ref[5:6, :] * acc_ref[...]


def _combine(slot_tm, aff, y, x, modtab, tile_slots, *, n_ctx, tile0):
    b, t_all, d = x.shape
    ne = aff.shape[2]
    tm = ROW_TILE
    n_slots = y.shape[2]
    nct = n_ctx // tm
    return pl.pallas_call(
        functools.partial(_combine_kernel, tile_slots=tile_slots),
        grid=(b, len(tile_slots), ne),
        in_specs=[pl.BlockSpec((None, tm, ne), lambda i, t, e: (i, t + tile0, 0)),
                  pl.BlockSpec((None, tm, ne), lambda i, t, e: (i, t + tile0, 0)),
                  pl.BlockSpec((None, None, n_slots, 2 * d), lambda i, t, e: (e, i, 0, 0)),
                  pl.BlockSpec((None, tm, d), lambda i, t, e: (i, t + tile0, 0)),
                  pl.BlockSpec((None, None, 6, d),
                               lambda i, t, e: (i, jnp.where(t + tile0 >= nct, 1, 0), 0, 0))],
        out_specs=pl.BlockSpec((None, tm, d), lambda i, t, e: (i, t, 0)),
        out_shape=jax.ShapeDtypeStruct((b, len(tile_slots) * tm, d), F32),
        scratch_shapes=[pltpu.VMEM((tm, d), F32)],
        compiler_params=_params("parallel", "parallel", "arbitrary"),
        name="expert_combine",
    )(slot_tm, aff, y, x, modtab)


def _moe(x, modtab, router_w_t, w1, w3, w2, *, n_ctx, with_ctx):
    b, t_all, d = x.shape
    n_lat = t_all - n_ctx
    ne = N_EXPERTS
    tm = ROW_TILE
    cap_c = CAPACITY_FACTOR * n_ctx // ne
    cap_l = CAPACITY_FACTOR * n_lat // ne
    ctx_slots = -(-cap_c // SLOT_TILE) * SLOT_TILE
    lat_slots = -(-cap_l // SLOT_TILE) * SLOT_TILE
    h, aff, aff_t = _router(x, modtab, router_w_t, n_ctx=n_ctx)
    if with_ctx:
        segments = ((0, n_ctx, cap_c, 0), (n_ctx, n_lat, cap_l, ctx_slots))
        n_slots = ctx_slots + lat_slots
        tile_slots = ((0, ctx_slots),) * (n_ctx // tm) + ((ctx_slots, lat_slots),) * (n_lat // tm)
        tile0 = 0
    else:
        segments = ((n_ctx, n_lat, cap_l, 0),)
        n_slots = lat_slots
        tile_slots = ((0, lat_slots),) * (n_lat // tm)
        tile0 = n_ctx // tm
    slot, slot_tm = _select(aff_t, segments)
    xs = _gather(slot, h, segments, n_slots)
    y = _expert_ffn(xs.reshape(ne, b * n_slots, d), w1, w3, w2).reshape(ne, b, n_slots, 2 * d)
    return _combine(slot_tm, aff, y, x, modtab, tile_slots, n_ctx=n_ctx, tile0=tile0)


def _segsum64(x, bd):
    hi = x.astype(BF16)
    lo = (x - hi.astype(F32)).astype(BF16)
    out = []
    for j in range(x.shape[1] // 256):
        sl = slice(j * 256, (j + 1) * 256)
        out.append(jnp.dot(hi[:, sl], bd, preferred_element_type=F32)
                   + jnp.dot(lo[:, sl], bd, preferred_element_type=F32))
    return jnp.concatenate(out, axis=1)


def _block_diag_ones():
    r = lax.broadcasted_iota(I32, (256, 256), 0)
    c = lax.broadcasted_iota(I32, (256, 256), 1)
    return (r // 64 == c // 64).astype(BF16)


def _softplus(x):
    return jnp.maximum(x, 0.0) + jnp.log(1.0 + jnp.exp(-jnp.abs(x)))


def _rwkv_feat_kernel(x_ref, xp_ref, xn_ref, mod_ref, mu_ref, wrkv_ref, w0_ref, w1_ref, w2_ref,
                      a0_ref, a1_ref, a2_ref, g1_ref, g2_ref, kk_ref, ka_ref,
                      r_out, v_out, nkk_out, g_out, w_out, k_out, b_out, *, n_ctx_tiles):
    t = pl.program_id(1)
    nt = pl.num_programs(1)
    tm = x_ref.shape[0]
    h = _normmod(x_ref[...], mod_ref, 0, 1)
    hp = _normmod(xp_ref[7:8, :], mod_ref, 0, 1)
    hn = _normmod(xn_ref[0:1, :], mod_ref, 0, 1)
    has_left = jnp.logical_and(t != 0, t != n_ctx_tiles)
    has_right = jnp.logical_and(t != n_ctx_tiles - 1, t != nt - 1)
    hp = jnp.where(has_left, hp, 0.0)
    hn = jnp.where(has_right, hn, 0.0)
    row = lax.broadcasted_iota(I32, h.shape, 0)
    left = jnp.where(row == 0, hp, pltpu.roll(h, 1, axis=0))
    right = jnp.where(row == tm - 1, hn, pltpu.roll(h, tm - 1, axis=0))
    xx = 0.5 * (left + right) - h
    xr, xw, xk, xv, xa, xg = (h + xx * mu_ref[i:i + 1, :] for i in range(6))
    r = _bdot(xr, wrkv_ref[0])
    k = _bdot(xk, wrkv_ref[1])
    v = _bdot(xv, wrkv_ref[2])
    g = _bdot(_sigmoid(_bdot(xg, g1_ref[...])), g2_ref[...])
    bd = _block_diag_ones()
    kk = k * kk_ref[...]
    kk = kk * lax.rsqrt(jnp.maximum(_segsum64(kk * kk, bd), 1e-24))
    r_out[...] = r
    v_out[...] = v
    nkk_out[...] = -kk
    g_out[...] = g
    for d in range(2):
        w_lora = _bdot(jnp.tanh(_bdot(xw, w1_ref[d])), w2_ref[d])
        log_w = -_softplus(-(w0_ref[d:d + 1, :] + w_lora)) - 0.5
        w_out[d] = jnp.exp(-jnp.exp(log_w))
        a = _sigmoid(a0_ref[d:d + 1, :] + _bdot(_bdot(xa, a1_ref[d]), a2_ref[d]))
        k_out[d] = k * (1.0 + (a - 1.0) * ka_ref[...])
        b_out[d] = kk * a


def _rwkv_features(x, modtab, p, *, n_ctx):
    b, t_all, d = x.shape
    tm = ROW_TILE
    nt = t_all // tm
    tb = tm // 8
    full = lambda shape: pl.BlockSpec(shape, lambda i, t: (0,) * len(shape))
    tok = pl.BlockSpec((None, tm, d), lambda i, t: (i, t, 0))
    tok2 = pl.BlockSpec((2, None, tm, d), lambda i, t: (0, i, t, 0))
    one = jax.ShapeDtypeStruct((b, t_all, d), F32)
    two = jax.ShapeDtypeStruct((2, b, t_all, d), F32)
    return pl.pallas_call(
        functools.partial(_rwkv_feat_kernel, n_ctx_tiles=n_ctx // tm),
        grid=(b, nt),
        in_specs=[tok,
                  pl.BlockSpec((None, 8, d), lambda i, t: (i, jnp.maximum(t * tb - 1, 0), 0)),
                  pl.BlockSpec((None, 8, d), lambda i, t: (i, jnp.minimum((t + 1) * tb, nt * tb - 1), 0)),
                  _mod_spec(n_ctx // tm),
                  full((6, d)), full((3, d, d)), full((2, d)), full(p["w1"].shape), full(p["w2"].shape),
                  full((2, d)), full(p["a1"].shape), full(p["a2"].shape), full(p["g1"].shape),
                  full(p["g2"].shape), full((1, d)), full((1, d))],
        out_specs=[tok, tok, tok, tok, tok2, tok2, tok2],
        out_shape=[one, one, one, one, two, two, two],
        compiler_params=_params("parallel", "parallel"),
        name="rwkv_features",
    )(x, x, x, modtab, p["mu"], p["w_rkv"], p["w0"], p["w1"], p["w2"], p["a0"], p["a1"], p["a2"],
      p["g1"], p["g2"], p["k_k"], p["k_a"])


def _scan_kernel(r_ref, w_ref, k_ref, v_ref, a_ref, b_ref, o_ref, s_ref, *, reverse):
    n = s_ref.shape[0]
    steps = r_ref.shape[0]

    @pl.when(pl.program_id(0) == 0)
    def _():
        s_ref[...] = jnp.zeros_like(s_ref)

    def step(i, carry):
        j = steps - 1 - i if reverse else i
        sa0 = jnp.zeros(s_ref.shape[1:], F32)
        sa1 = jnp.zeros(s_ref.shape[1:], F32)
        for kk in range(0, n, 2):
            sa0 = sa0 + s_ref[kk] * a_ref[j, kk:kk + 1, :]
            sa1 = sa1 + s_ref[kk + 1] * a_ref[j, kk + 1:kk + 2, :]
        sa = sa0 + sa1
        v = v_ref[j]
        o0 = jnp.zeros(s_ref.shape[1:], F32)
        o1 = jnp.zeros(s_ref.shape[1:], F32)
        for kk in range(n):
            s_new = (s_ref[kk] * w_ref[j, kk:kk + 1, :] + sa * b_ref[j, kk:kk + 1, :]
                     + v * k_ref[j, kk:kk + 1, :])
            s_ref[kk] = s_new
            if kk % 2 == 0:
                o0 = o0 + s_new * r_ref[j, kk:kk + 1, :]
            else:
                o1 = o1 + s_new * r_ref[j, kk:kk + 1, :]
        o_ref[j] = o0 + o1
        return carry

    lax.fori_loop(0, steps, step, 0)


def _wkv_scan(r, w, k, v, a, b, *, n_ctx, reverse):
    t_all, n, chains = r.shape
    tc = SCAN_CHUNK
    ncc, nch = n_ctx // tc, t_all // tc
    if reverse:
        idx = lambda c: (jnp.where(c < ncc, ncc - 1 - c, nch - 1 - (c - ncc)), 0, 0)
    else:
        idx = lambda c: (c, 0, 0)
    spec = pl.BlockSpec((tc, n, chains), idx)
    return pl.pallas_call(
        functools.partial(_scan_kernel, reverse=reverse),
        grid=(nch,),
        in_specs=[spec] * 6,
        out_specs=spec,
        out_shape=jax.ShapeDtypeStruct((t_all, n, chains), F32),
        scratch_shapes=[pltpu.VMEM((n, n, chains), F32)],
        compiler_params=_params("arbitrary"),
        name="wkv_scan_bwd" if reverse else "wkv_scan_fwd",
    )(r, w, k, v, a, b)


def _rwkv_out_kernel(o_ref, r_ref, k_ref, v_ref, g_ref, rk_ref, lnw_ref, lnb_ref, wo_ref, x_ref, mod_ref,
                     out_ref):
    bd = _block_diag_ones()
    inv_n = 1.0 / C_HEAD_DIM
    o = o_ref[...]
    o = o - _segsum64(o, bd) * inv_n
    o = o * lax.rsqrt(_segsum64(o * o, bd) * inv_n + C_GN_EPS)
    o = o * lnw_ref[...] + lnb_ref[...]
    r = r_ref[...]
    bonus = _segsum64(r * k_ref[0] * rk_ref[0:1, :] + r * k_ref[1] * rk_ref[1:2, :], bd) * v_ref[...]
    y = _bdot((o + bonus) * g_ref[...], wo_ref[...])
    out_ref[...] = x_ref[...] + mod_ref[2:3, :] * y


def _rwkv_readout(o, r, k2, v, g, p, x, modtab, *, n_ctx):
    b, t_all, d = x.shape
    tm = ROW_TILE
    tok = pl.BlockSpec((None, tm, d), lambda i, t: (i, t, 0))
    full = lambda shape: pl.BlockSpec(shape, lambda i, t: (0,) * len(shape))
    return pl.pallas_call(
        _rwkv_out_kernel,
        grid=(b, t_all // tm),
        in_specs=[tok, tok, pl.BlockSpec((2, None, tm, d), lambda i, t: (0, i, t, 0)), tok, tok,
                  full((2, d)), full((1, d)), full((1, d)), full((d, d)), tok, _mod_spec(n_ctx // tm)],
        out_specs=tok,
        out_shape=jax.ShapeDtypeStruct((b, t_all, d), F32),
        compiler_params=_params("parallel", "parallel"),
        name="rwkv_readout",
    )(o, r, k2, v, g, p["r_k"], p["ln_w"], p["ln_b"], p["w_o"], x, modtab)


def _to_scan_layout(a):
    b, t, d = a.shape
    return jnp.transpose(a.reshape(b, t, C_HEADS, C_HEAD_DIM), (1, 3, 0, 2)).reshape(t, C_HEAD_DIM, b * C_HEADS)


def _from_scan_layout(o, b):
    t = o.shape[0]
    return jnp.transpose(o.reshape(t, C_HEAD_DIM, b, C_HEADS), (2, 0, 3, 1)).reshape(b, t, D_MODEL)


def _rwkv_mixer(x, modtab, p, *, n_ctx):
    b = x.shape[0]
    r, v, nkk, g, w2, k2, b2 = _rwkv_features(x, modtab, p, n_ctx=n_ctx)
    rs, vs, as_ = _to_scan_layout(r), _to_scan_layout(v), _to_scan_layout(nkk)
    o = None
    for d in range(2):
        od = _wkv_scan(rs, _to_scan_layout(w2[d]), _to_scan_layout(k2[d]), vs, as_, _to_scan_layout(b2[d]),
                       n_ctx=n_ctx, reverse=(d == 1))
        o = od if o is None else o + od
    return _rwkv_readout(_from_scan_layout(o, b), r, k2, v, g, p, x, modtab, n_ctx=n_ctx)


def _final_kernel(x_ref, w_ref, o_ref):
    o_ref[...] = _rms(x_ref[...]) * w_ref[...]


def _final_norm(x, w):
    b, t, d = x.shape
    tm = ROW_TILE
    nt = t // tm
    return pl.pallas_call(
        _final_kernel,
        grid=(b, nt),
        in_specs=[pl.BlockSpec((None, tm, d), lambda i, t: (i, t, 0)),
                  pl.BlockSpec((1, d), lambda i, t: (0, 0))],
        out_specs=pl.BlockSpec((None, tm, d), lambda i, t: (i, t, 0)),
        out_shape=jax.ShapeDtypeStruct((b, nt * tm, d), F32),
        compiler_params=_params("parallel", "parallel"),
        name="final_norm",
    )(x, w.reshape(1, d))


def _rope_tables(n_ctx, n_lat, head_dim):
    rows = jnp.repeat(jnp.arange(n_lat // GRID_W, dtype=I32), GRID_W).astype(F32)
    cols = jnp.tile(jnp.arange(GRID_W, dtype=I32), n_lat // GRID_W).astype(F32)
    n_freq = head_dim // 4
    inv_freq = ROPE_THETA ** (-jnp.arange(n_freq, dtype=F32) / n_freq)
    ang = jnp.concatenate([rows[:, None] * inv_freq, cols[:, None] * inv_freq], axis=-1)
    cos, sin = jnp.cos(ang), jnp.sin(ang)
    reps = 128 // head_dim
    cos = jnp.tile(jnp.concatenate([cos, cos], axis=-1), (1, reps))
    sin = jnp.tile(jnp.concatenate([-sin, sin], axis=-1), (1, reps))
    cos = jnp.concatenate([jnp.ones((n_ctx, 128), F32), cos], axis=0)
    sin = jnp.concatenate([jnp.zeros((n_ctx, 128), F32), sin], axis=0)
    return cos, sin


def kernel(x, c, ctx, c_ctx, mod_w, mod_b, a_w_qkv, a_w_o, a_q_norm, a_k_norm, b_w_qkv, b_w_o, b_sink,
           c_mu, c_w_rkv, c_w_o, c_w0, c_w1, c_w2, c_a0, c_a1, c_a2, c_g1, c_g2, c_k_k, c_k_a, c_r_k,
           c_ln_w, c_ln_b, router_w, ffn_w1, ffn_w3, ffn_w2, final_norm):
    b, n_lat, d = x.shape
    n_ctx = ctx.shape[1]
    depth = mod_w.shape[0]
    assert d == D_MODEL and n_ctx % ROW_TILE == 0 and n_lat % ROW_TILE == 0

    cond_rows = -(-(b + 1) // 8) * 8
    cond = jnp.zeros((cond_rows, d), F32).at[:b].set(c).at[b].set(c_ctx)
    mods = _mod_tables(cond, mod_w, mod_b).reshape(depth, cond_rows, 6, d)
    cos_a, sin_a = _rope_tables(n_ctx, n_lat, A_HEAD_DIM)
    cos_b, sin_b = _rope_tables(n_ctx, n_lat, B_HEAD_DIM)

    xs = jnp.concatenate([ctx, x], axis=1)
    for i in range(depth):
        last = i == depth - 1
        kind, j = i % N_MIXERS, i // N_MIXERS
        modtab = jnp.stack([jnp.broadcast_to(mods[i, b], (b, 6, d)), mods[i, :b]], axis=1)
        if kind == 0:
            q, k, v = _qkv_proj(xs, modtab, a_w_qkv[j].astype(BF16), a_q_norm[j].reshape(1, -1),
                                a_k_norm[j].reshape(1, -1), cos_a, sin_a, kind=0, n_ctx=n_ctx)
            o = _global_attention(q, k, v, n_ctx=n_ctx)
            xs = _proj_residual(o, a_w_o[j].astype(BF16), xs, modtab, n_ctx=n_ctx)
        elif kind == 1:
            ones = jnp.ones((1, 128), F32)
            q, k, v = _qkv_proj(xs, modtab, b_w_qkv[j].astype(BF16), ones, ones, cos_b, sin_b,
                                kind=1, n_ctx=n_ctx)
            o = _window_attention(q, k, v, b_sink[j], n_ctx=n_ctx)
            xs = _proj_residual(o, b_w_o[j].astype(BF16), xs, modtab, n_ctx=n_ctx)
        else:
            p = dict(mu=c_mu[j], w_rkv=c_w_rkv[j].astype(BF16), w_o=c_w_o[j].astype(BF16), w0=c_w0[j],
                     w1=c_w1[j].astype(BF16), w2=c_w2[j].astype(BF16), a0=c_a0[j],
                     a1=c_a1[j].astype(BF16), a2=c_a2[j].astype(BF16), g1=c_g1[j].astype(BF16),
                     g2=c_g2[j].astype(BF16), k_k=c_k_k[j].reshape(1, d), k_a=c_k_a[j].reshape(1, d),
                     r_k=c_r_k[j].reshape(2, d), ln_w=c_ln_w[j].reshape(1, d), ln_b=c_ln_b[j].reshape(1, d))
            xs = _rwkv_mixer(xs, modtab, p, n_ctx=n_ctx)
        xs = _moe(xs, modtab, router_w[i].T, ffn_w1[i].astype(BF16), ffn_w3[i].astype(BF16),
                  ffn_w2[i].astype(BF16), n_ctx=n_ctx, with_ctx=not last)
    return _final_norm(xs, final_norm)
```

```python
import functools
import math

import jax
import jax.numpy as jnp
from jax import lax
from jax.experimental import pallas as pl
from jax.experimental.pallas import tpu as pltpu

F32 = jnp.float32
BF16 = jnp.bfloat16
I32 = jnp.int32

D_MODEL = 1024
GRID_W = 64
Q_BLOCK = 128
ROPE_THETA = 10000.0
NORM_EPS = 1e-6
NEG_INF = -1e30
A_HEADS, A_KV_HEADS, A_HEAD_DIM = 8, 2, 128
B_HEADS, B_KV_HEADS, B_HEAD_DIM = 16, 4, 64
WINDOW = 128
C_HEAD_DIM = 64
C_HEADS = D_MODEL // C_HEAD_DIM
C_GN_EPS = C_HEAD_DIM * 1e-5
N_EXPERTS = 16
EXPERT_FF = 2 * D_MODEL
CAPACITY_FACTOR = 2
N_MIXERS = 3
LOG2E = math.log2(math.e)

ROW_TILE = 256
COMBINE_TILE = 1024
SLOT_TILE = 128
SCAN_CHUNK = 16
VMEM_LIMIT = 56 * 1024 * 1024


def _params(*sem):
    return pltpu.CompilerParams(dimension_semantics=sem, vmem_limit_bytes=VMEM_LIMIT)


def _rms(x):
    return x * lax.rsqrt(jnp.mean(x * x, axis=-1, keepdims=True) + NORM_EPS)


def _normmod(x, mod_ref, shift_row, scale_row):
    return _rms(x) * (1.0 + mod_ref[scale_row:scale_row + 1, :]) + mod_ref[shift_row:shift_row + 1, :]


def _bdot(a, b):
    return jnp.dot(a.astype(BF16), b.astype(BF16), preferred_element_type=F32)


def _dot_nt(a, b):
    return lax.dot_general(a.astype(BF16), b.astype(BF16), (((1,), (1,)), ((), ())),
                           preferred_element_type=F32)


def _sigmoid(x):
    return 1.0 / (1.0 + jnp.exp(-x))


def _mod_spec(n_lat_tiles):
    return pl.BlockSpec((None, None, 6, D_MODEL),
                        lambda b, t: (b, jnp.where(t < n_lat_tiles, 1, 0), 0, 0))


def _cast_kernel(x_ref, o_ref):
    o_ref[...] = x_ref[...].astype(BF16)


def _cast_bf16(w):
    shape = w.shape
    w2 = w.reshape(-1, shape[-1])
    rows, cols = w2.shape
    tr = 512
    out = pl.pallas_call(
        _cast_kernel,
        grid=(rows // tr,),
        in_specs=[pl.BlockSpec((tr, cols), lambda i: (i, 0))],
        out_specs=pl.BlockSpec((tr, cols), lambda i: (i, 0)),
        out_shape=jax.ShapeDtypeStruct((rows, cols), BF16),
        compiler_params=_params("parallel"),
        name="cast_bf16",
    )(w2)
    return out.reshape(shape)


def _mod_kernel(cond_ref, w_ref, b_ref, o_ref):
    c = cond_ref[...]
    a = c * _sigmoid(c)
    o_ref[0] = _bdot(a, w_ref[0]) + b_ref[0]


def _mod_tables(cond, mod_w, mod_b):
    depth, d, n = mod_w.shape
    rows = cond.shape[0]
    return pl.pallas_call(
        _mod_kernel,
        grid=(depth, n // d),
        in_specs=[pl.BlockSpec((rows, d), lambda i, j: (0, 0)),
                  pl.BlockSpec((1, d, d), lambda i, j: (i, 0, j)),
                  pl.BlockSpec((1, 1, d), lambda i, j: (i, 0, j))],
        out_specs=pl.BlockSpec((1, rows, d), lambda i, j: (i, 0, j)),
        out_shape=jax.ShapeDtypeStruct((depth, rows, n), F32),
        compiler_params=_params("arbitrary", "arbitrary"),
        name="mod_tables",
    )(cond, mod_w, mod_b.reshape(depth, 1, n))


def _rope128(x, cos, sin_signed, half):
    if half == 64:
        rot = pltpu.roll(x, 64, axis=1)
    else:
        lane = lax.broadcasted_iota(I32, x.shape, 1)
        rot = jnp.where((lane % 64) < 32, pltpu.roll(x, 96, axis=1), pltpu.roll(x, 32, axis=1))
    return x * cos + rot * sin_signed


def _qkv_kernel(x_ref, mod_ref, w_ref, gq_ref, gk_ref, cos_ref, sin_ref, q_ref, k_ref, v_ref, *, kind):
    h = _normmod(x_ref[...], mod_ref, 0, 1)
    y = _bdot(h, w_ref[...])
    cos, sin = cos_ref[...], sin_ref[...]
    nq = D_MODEL
    nkv = (y.shape[1] - nq) // 2
    half = 64 if kind == 0 else 32
    q_scale = (2 * half) ** -0.5 * LOG2E
    for j in range((nq + nkv) // 128):
        s = y[:, j * 128:(j + 1) * 128]
        if kind == 0:
            gain = gq_ref[...] if j < nq // 128 else gk_ref[...]
            s = _rms(s) * gain
        s = _rope128(s, cos, sin, half)
        if j < nq // 128:
            q_ref[:, j * 128:(j + 1) * 128] = (s * q_scale).astype(BF16)
        elif kind == 0:
            k_ref[:, (j - nq // 128) * 128:(j - nq // 128 + 1) * 128] = s.astype(BF16)
        else:
            jj = (j - nq // 128) * 2
            k_ref[jj] = s[:, :64].astype(BF16)
            k_ref[jj + 1] = s[:, 64:].astype(BF16)
    v = y[:, nq + nkv:].astype(BF16)
    if kind == 0:
        v_ref[...] = v
    else:
        for jj in range(nkv // 64):
            v_ref[jj] = v[:, jj * 64:(jj + 1) * 64]


def _qkv_proj(x, modtab, w, gq, gk, cos, sin, *, kind, n_lat):
    b, t_all, d = x.shape
    tm = ROW_TILE
    nt = t_all // tm
    n = w.shape[1]
    nkv = (n - d) // 2
    if kind == 0:
        kv_shape = jax.ShapeDtypeStruct((b, t_all, nkv), BF16)
        kv_spec = pl.BlockSpec((None, tm, nkv), lambda i, t: (i, t, 0))
    else:
        kv_shape = jax.ShapeDtypeStruct((b, nkv // 64, t_all, 64), BF16)
        kv_spec = pl.BlockSpec((None, nkv // 64, tm, 64), lambda i, t: (i, 0, t, 0))
    return pl.pallas_call(
        functools.partial(_qkv_kernel, kind=kind),
        grid=(b, nt),
        in_specs=[pl.BlockSpec((None, tm, d), lambda i, t: (i, t, 0)),
                  _mod_spec(n_lat // tm),
                  pl.BlockSpec((d, n), lambda i, t: (0, 0)),
                  pl.BlockSpec((1, 128), lambda i, t: (0, 0)),
                  pl.BlockSpec((1, 128), lambda i, t: (0, 0)),
                  pl.BlockSpec((tm, 128), lambda i, t: (t, 0)),
                  pl.BlockSpec((tm, 128), lambda i, t: (t, 0))],
        out_specs=[pl.BlockSpec((None, tm, d), lambda i, t: (i, t, 0)), kv_spec, kv_spec],
        out_shape=[jax.ShapeDtypeStruct((b, t_all, d), BF16), kv_shape, kv_shape],
        compiler_params=_params("parallel", "parallel"),
        name=f"qkv_proj_{kind}",
    )(x, modtab, w, gq, gk, cos, sin)


def _flash_chunk(carry, q, kc, vc):
    m, l, acc = carry
    s = _dot_nt(q, kc)
    m_new = jnp.maximum(m, jnp.max(s, axis=-1, keepdims=True))
    alpha = jnp.exp2(m - m_new)
    p = jnp.exp2(s - m_new)
    l = alpha * l + jnp.sum(p, axis=-1, keepdims=True)
    acc = alpha * acc + jnp.dot(p.astype(BF16), vc, preferred_element_type=F32)
    return m_new, l, acc


def _gattn_kernel(q_ref, k_ref, v_ref, o_ref, *, n_lat, kchunk):
    hd = A_HEAD_DIM
    g = A_HEADS // A_KV_HEADS
    tq = q_ref.shape[0]
    t_all = k_ref.shape[0]
    qs = jnp.concatenate([q_ref[:, i * hd:(i + 1) * hd] for i in range(g)], axis=0)
    init = (jnp.full((g * tq, 1), NEG_INF, F32), jnp.zeros((g * tq, 1), F32), jnp.zeros((g * tq, hd), F32))

    def finish(carry):
        _, l, acc = carry
        o = (acc / l).astype(BF16)
        for i in range(g):
            o_ref[:, i * hd:(i + 1) * hd] = o[i * tq:(i + 1) * tq]

    ctx_carry = _flash_chunk(init, qs, k_ref[n_lat:t_all, :], v_ref[n_lat:t_all, :])
    is_ctx = pl.program_id(2) >= n_lat // tq

    @pl.when(is_ctx)
    def _():
        finish(ctx_carry)

    @pl.when(jnp.logical_not(is_ctx))
    def _():
        def body(c, carry):
            start = pl.multiple_of(c * kchunk, kchunk)
            return _flash_chunk(carry, qs, k_ref[pl.ds(start, kchunk), :], v_ref[pl.ds(start, kchunk), :])
        finish(lax.fori_loop(0, n_lat // kchunk, body, ctx_carry))


def _global_attention(q, k, v, *, n_lat, rows):
    b, t_all, d = q.shape
    hd, g = A_HEAD_DIM, A_HEADS // A_KV_HEADS
    tq = ROW_TILE
    return pl.pallas_call(
        functools.partial(_gattn_kernel, n_lat=n_lat, kchunk=min(2048, n_lat)),
        grid=(b, A_KV_HEADS, rows // tq),
        in_specs=[pl.BlockSpec((None, tq, g * hd), lambda i, h, t: (i, t, h)),
                  pl.BlockSpec((None, t_all, hd), lambda i, h, t: (i, 0, h)),
                  pl.BlockSpec((None, t_all, hd), lambda i, h, t: (i, 0, h))],
        out_specs=pl.BlockSpec((None, tq, g * hd), lambda i, h, t: (i, t, h)),
        out_shape=jax.ShapeDtypeStruct((b, t_all, d), BF16),
        compiler_params=_params("parallel", "parallel", "arbitrary"),
        name="global_attention",
    )(q, k, v)


def _wattn_kernel(q_ref, k_ref, v_ref, sink_ref, o_ref, *, n_lat):
    hd = B_HEAD_DIM
    g = B_HEADS // B_KV_HEADS
    tq = q_ref.shape[0]
    t_all = k_ref.shape[0]
    t = pl.program_id(2)
    is_ctx = t >= n_lat // tq
    kc, vc = k_ref[n_lat:t_all, :], v_ref[n_lat:t_all, :]
    q = jnp.concatenate([q_ref[:, i * hd:(i + 1) * hd] for i in range(g)], axis=0)
    sink = jnp.concatenate([jnp.broadcast_to(sink_ref[:, i:i + 1], (tq, 1)) for i in range(g)],
                           axis=0) * LOG2E
    s_ctx = _dot_nt(q, kc)

    def finish(o):
        o = o.astype(BF16)
        o_ref[...] = jnp.concatenate([o[i * tq:(i + 1) * tq] for i in range(g)], axis=1)

    @pl.when(is_ctx)
    def _():
        m = jnp.maximum(jnp.max(s_ctx, axis=-1, keepdims=True), sink)
        p = jnp.exp2(s_ctx - m)
        l = jnp.sum(p, axis=-1, keepdims=True) + jnp.exp2(sink - m)
        finish(jnp.dot(p.astype(BF16), vc, preferred_element_type=F32) / l)

    @pl.when(jnp.logical_not(is_ctx))
    def _():
        wlen = 3 * Q_BLOCK
        ws = jnp.clip((t - 1) * Q_BLOCK, 0, n_lat - wlen)
        start = pl.multiple_of(ws, 128)
        kw, vw = k_ref[pl.ds(start, wlen), :], v_ref[pl.ds(start, wlen), :]
        q_pos = t * Q_BLOCK + lax.broadcasted_iota(I32, (g * tq, wlen), 0) % tq
        k_pos = ws + lax.broadcasted_iota(I32, (g * tq, wlen), 1)
        band = jnp.abs(k_pos - q_pos) <= WINDOW
        s_win = jnp.where(band, _dot_nt(q, kw), NEG_INF)
        m = jnp.maximum(jnp.maximum(jnp.max(s_win, axis=-1, keepdims=True),
                                    jnp.max(s_ctx, axis=-1, keepdims=True)), sink)
        p_win = jnp.exp2(s_win - m)
        p_ctx = jnp.exp2(s_ctx - m)
        l = (jnp.sum(p_win, axis=-1, keepdims=True) + jnp.sum(p_ctx, axis=-1, keepdims=True)
             + jnp.exp2(sink - m))
        o = (jnp.dot(p_win.astype(BF16), vw, preferred_element_type=F32)
             + jnp.dot(p_ctx.astype(BF16), vc, preferred_element_type=F32))
        finish(o / l)


def _window_attention(q, k, v, sink, *, n_lat, rows):
    b, t_all, d = q.shape
    hd, g = B_HEAD_DIM, B_HEADS // B_KV_HEADS
    tq = Q_BLOCK
    return pl.pallas_call(
        functools.partial(_wattn_kernel, n_lat=n_lat),
        grid=(b, B_KV_HEADS, rows // tq),
        in_specs=[pl.BlockSpec((None, tq, g * hd), lambda i, h, t: (i, t, h)),
                  pl.BlockSpec((None, None, t_all, hd), lambda i, h, t: (i, h, 0, 0)),
                  pl.BlockSpec((None, None, t_all, hd), lambda i, h, t: (i, h, 0, 0)),
                  pl.BlockSpec((None, 1, g), lambda i, h, t: (h, 0, 0))],
        out_specs=pl.BlockSpec((None, tq, g * hd), lambda i, h, t: (i, t, h)),
        out_shape=jax.ShapeDtypeStruct((b, t_all, d), BF16),
        compiler_params=_params("parallel", "parallel", "arbitrary"),
        name="window_attention",
    )(q, k, v, sink.reshape(B_KV_HEADS, 1, g))


def _proj_res_kernel(a_ref, w_ref, x_ref, mod_ref, o_ref):
    o_ref[...] = x_ref[...] + mod_ref[2:3, :] * _bdot(a_ref[...], w_ref[...])


def _proj_residual(a, w, x, modtab, *, n_lat, rows):
    b, t_all, d = x.shape
    tm = ROW_TILE
    return pl.pallas_call(
        _proj_res_kernel,
        grid=(b, rows // tm),
        in_specs=[pl.BlockSpec((None, tm, d), lambda i, t: (i, t, 0)),
                  pl.BlockSpec((d, d), lambda i, t: (0, 0)),
                  pl.BlockSpec((None, tm, d), lambda i, t: (i, t, 0)),
                  _mod_spec(n_lat // tm)],
        out_specs=pl.BlockSpec((None, tm, d), lambda i, t: (i, t, 0)),
        out_shape=jax.ShapeDtypeStruct((b, rows, d), F32),
        compiler_params=_params("parallel", "parallel"),
        name="proj_residual",
    )(a, w, x, modtab)


def _router_kernel(x_ref, mod_ref, rwt_ref, h_ref, aff_ref, afft_ref):
    h = _normmod(x_ref[...], mod_ref, 3, 4)
    h_ref[...] = h.astype(BF16)
    logits_t = lax.dot_general(rwt_ref[...], h, (((1,), (1,)), ((), ())),
                               precision=lax.Precision.HIGHEST, preferred_element_type=F32)
    e = jnp.exp(logits_t - jnp.max(logits_t, axis=0, keepdims=True))
    aff_t = e / jnp.sum(e, axis=0, keepdims=True)
    afft_ref[...] = aff_t
    aff_ref[...] = aff_t.T


def _router(x, modtab, router_w_t, *, n_lat):
    b, rows, d = x.shape
    tm = ROW_TILE
    ne = router_w_t.shape[0]
    return pl.pallas_call(
        _router_kernel,
        grid=(b, rows // tm),
        in_specs=[pl.BlockSpec((None, tm, d), lambda i, t: (i, t, 0)),
                  _mod_spec(n_lat // tm),
                  pl.BlockSpec((ne, d), lambda i, t: (0, 0))],
        out_specs=[pl.BlockSpec((None, tm, d), lambda i, t: (i, t, 0)),
                   pl.BlockSpec((None, tm, ne), lambda i, t: (i, t, 0)),
                   pl.BlockSpec((None, ne, tm), lambda i, t: (i, 0, t))],
        out_shape=[jax.ShapeDtypeStruct((b, rows, d), BF16),
                   jax.ShapeDtypeStruct((b, rows, ne), F32),
                   jax.ShapeDtypeStruct((b, ne, rows), F32)],
        compiler_params=_params("parallel", "parallel"),
        name="router",
    )(x, modtab, router_w_t)


def _lane_prefix(mask_f, tri):
    n = mask_f.shape[1]
    run = jnp.zeros((mask_f.shape[0], 1), F32)
    out = []
    for j in range(n // 128):
        blk = mask_f[:, j * 128:(j + 1) * 128]
        incl = jnp.dot(blk.astype(BF16), tri, preferred_element_type=F32)
        out.append(incl - blk + run)
        run = run + incl[:, 127:128]
    return jnp.concatenate(out, axis=1)


def _select_kernel(afft_ref, slot_ref, slot_tm_ref, *, segments):
    r = lax.broadcasted_iota(I32, (128, 128), 0)
    c = lax.broadcasted_iota(I32, (128, 128), 1)
    tri = (r <= c).astype(BF16)
    for off, n, cap, base in segments:
        bits = lax.bitcast_convert_type(afft_ref[:, off:off + n], I32)
        cap_f = jnp.float32(cap)

        def body(i, thr):
            cand = thr | jnp.left_shift(jnp.int32(1), 30 - i)
            cnt = jnp.sum((bits >= cand).astype(F32), axis=1, keepdims=True)
            return jnp.where(cnt >= cap_f, cand, thr)

        thr = lax.fori_loop(0, 31, body, jnp.zeros((bits.shape[0], 1), I32))
        gt = (bits > thr).astype(F32)
        eq = (bits == thr).astype(F32)
        need = cap_f - jnp.sum(gt, axis=1, keepdims=True)
        sel = gt + eq * (_lane_prefix(eq, tri) < need).astype(F32)
        slot = _lane_prefix(sel, tri).astype(I32) + base
        slot = jnp.where(sel > 0.5, slot, -1)
        slot_ref[:, off:off + n] = slot
        slot_tm_ref[off:off + n, :] = slot.astype(F32).T


def _select(aff_t, segments):
    b, ne, rows = aff_t.shape
    return pl.pallas_call(
        functools.partial(_select_kernel, segments=segments),
        grid=(b,),
        in_specs=[pl.BlockSpec((None, ne, rows), lambda i: (i, 0, 0))],
        out_specs=[pl.BlockSpec((None, ne, rows), lambda i: (i, 0, 0)),
                   pl.BlockSpec((None, rows, ne), lambda i: (i, 0, 0))],
        out_shape=[jax.ShapeDtypeStruct((b, ne, rows), I32), jax.ShapeDtypeStruct((b, rows, ne), F32)],
        compiler_params=_params("parallel"),
        name="expert_select",
    )(aff_t)


def _gather_kernel(slot_ref, h_ref, xs_ref, *, segments):
    for off, n, cap, base in segments:
        slot = slot_ref[:, off:off + n]
        hseg = h_ref[off:off + n, :]
        st = min(SLOT_TILE, cap)
        for j in range(cap // st):
            ids = base + j * st + lax.broadcasted_iota(I32, (st, 1), 0)
            onehot = jnp.where(ids == slot, 1.0, 0.0).astype(BF16)
            rows = jnp.dot(onehot, hseg, preferred_element_type=F32)
            xs_ref[base + j * st:base + (j + 1) * st, :] = rows.astype(BF16)


def _gather(slot, h, segments, n_slots):
    b, ne, rows = slot.shape
    d = h.shape[2]
    return pl.pallas_call(
        functools.partial(_gather_kernel, segments=segments),
        grid=(b, ne),
        in_specs=[pl.BlockSpec((None, None, 1, rows), lambda i, e: (i, e, 0, 0)),
                  pl.BlockSpec((None, rows, d), lambda i, e: (i, 0, 0))],
        out_specs=pl.BlockSpec((None, None, n_slots, d), lambda i, e: (e, i, 0, 0)),
        out_shape=jax.ShapeDtypeStruct((ne, b, n_slots, d), BF16),
        compiler_params=_params("parallel", "arbitrary"),
        name="expert_gather",
    )(slot.reshape(b, ne, 1, rows), h)


def _ffn_kernel(xs_ref, w1_ref, w3_ref, w2_ref, y_ref):
    xs = xs_ref[...]
    a = jnp.dot(xs, w1_ref[...], preferred_element_type=F32)
    g = jnp.dot(xs, w3_ref[...], preferred_element_type=F32)
    hid = (a * _sigmoid(a)) * g
    y = _bdot(hid, w2_ref[...])
    hi = y.astype(BF16)
    d = y.shape[1]
    y_ref[:, :d] = hi
    y_ref[:, d:] = (y - hi.astype(F32)).astype(BF16)


def _expert_ffn(xs, w1, w3, w2, layer):
    ne, b, n_slots, d = xs.shape
    ff = w1.shape[3]
    return pl.pallas_call(
        _ffn_kernel,
        grid=(ne, b),
        in_specs=[pl.BlockSpec((None, None, n_slots, d), lambda e, i: (e, i, 0, 0)),
                  pl.BlockSpec((None, None, d, ff), lambda e, i: (layer, e, 0, 0)),
                  pl.BlockSpec((None, None, d, ff), lambda e, i: (layer, e, 0, 0)),
                  pl.BlockSpec((None, None, ff, d), lambda e, i: (layer, e, 0, 0))],
        out_specs=pl.BlockSpec((None, None, n_slots, 2 * d), lambda e, i: (e, i, 0, 0)),
        out_shape=jax.ShapeDtypeStruct((ne, b, n_slots, 2 * d), BF16),
        compiler_params=_params("parallel", "arbitrary"),
        name="expert_ffn",
    )(xs, w1, w3, w2)


def _combine_kernel(slot_ref, aff_ref, y_ref, x_ref, mod_ref, o_ref, acc_ref, *, slot0):
    e = pl.program_id(2)
    ne = pl.num_programs(2)
    d = x_ref.shape[1]
    width = y_ref.shape[0]

    @pl.when(e == 0)
    def _():
        acc_ref[...] = jnp.zeros_like(acc_ref)

    lane_e = lax.broadcasted_iota(I32, aff_ref.shape, 1)
    gate = jnp.sum(jnp.where(lane_e == e, aff_ref[...], 0.0), axis=1, keepdims=True)
    slot = jnp.sum(jnp.where(lane_e == e, slot_ref[...], 0.0), axis=1, keepdims=True).astype(I32)
    ids = slot0 + lax.broadcasted_iota(I32, (1, width), 1)
    onehot = jnp.where(slot == ids, 1.0, 0.0).astype(BF16)
    z = jnp.dot(onehot, y_ref[...], preferred_element_type=F32)
    acc_ref[...] += gate * (z[:, :d] + z[:, d:])

    @pl.when(e == ne - 1)
    def _():
        o_ref[...] = x_ref[...] + mod_ref[5:6, :] * acc_ref[...]


def _combine(slot_tm, aff, y, x, modtab, *, row0, rows, tm, slot0, width, seg, in_place):
    b, _, d = x.shape
    ne = aff.shape[2]
    t0, s0 = row0 // tm, slot0 // width
    out_rows = x.shape[1] if in_place else rows
    return pl.pallas_call(
        functools.partial(_combine_kernel, slot0=slot0),
        grid=(b, rows // tm, ne),
        in_specs=[pl.BlockSpec((None, tm, ne), lambda i, t, e: (i, t + t0, 0)),
                  pl.BlockSpec((None, tm, ne), lambda i, t, e: (i, t + t0, 0)),
                  pl.BlockSpec((None, None, width, 2 * d), lambda i, t, e: (e, i, s0, 0)),
                  pl.BlockSpec((None, tm, d), lambda i, t, e: (i, t + t0, 0)),
                  pl.BlockSpec((None, None, 6, d), lambda i, t, e: (i, seg, 0, 0))],
        out_specs=pl.BlockSpec((None, tm, d), lambda i, t, e: (i, t + (t0 if in_place else 0), 0)),
        out_shape=jax.ShapeDtypeStruct((b, out_rows, d), F32),
        scratch_shapes=[pltpu.VMEM((tm, d), F32)],
        input_output_aliases={3: 0} if in_place else {},
        compiler_params=_params("parallel", "parallel", "arbitrary"),
        name="expert_combine",
    )(slot_tm, aff, y, x, modtab)


def _moe(x, modtab, router_w_t, w1, w3, w2, layer, *, n_lat, with_ctx):
    b, rows, d = x.shape
    ne = N_EXPERTS
    cap_l = CAPACITY_FACTOR * n_lat // ne
    h, aff, aff_t = _router(x, modtab, router_w_t, n_lat=n_lat)
    segments = ((0, n_lat, cap_l, 0),)
    n_slots = cap_l
    if with_ctx:
        n_ctx = rows - n_lat
        cap_c = CAPACITY_FACTOR * n_ctx // ne
        segments += ((n_lat, n_ctx, cap_c, cap_l),)
        n_slots += cap_c
    slot, slot_tm = _select(aff_t, segments)
    xs = _gather(slot, h, segments, n_slots)
    y = _expert_ffn(xs, w1, w3, w2, layer)
    out = _combine(slot_tm, aff, y, x, modtab, row0=0, rows=n_lat, tm=min(COMBINE_TILE, n_lat), slot0=0,
                   width=cap_l, seg=1, in_place=with_ctx)
    if with_ctx:
        out = _combine(slot_tm, aff, y, out, modtab, row0=n_lat, rows=n_ctx, tm=n_ctx, slot0=cap_l,
                       width=cap_c, seg=0, in_place=True)
    return out


def _segsum64(x, bd):
    hi = x.astype(BF16)
    lo = (x - hi.astype(F32)).astype(BF16)
    out = []
    for j in range(x.shape[1] // 256):
        sl = slice(j * 256, (j + 1) * 256)
        out.append(jnp.dot(hi[:, sl], bd, preferred_element_type=F32)
                   + jnp.dot(lo[:, sl], bd, preferred_element_type=F32))
    return jnp.concatenate(out, axis=1)


def _block_diag_ones():
    r = lax.broadcasted_iota(I32, (256, 256), 0)
    c = lax.broadcasted_iota(I32, (256, 256), 1)
    return (r // 64 == c // 64).astype(BF16)


def _softplus(x):
    return jnp.maximum(x, 0.0) + jnp.log(1.0 + jnp.exp(-jnp.abs(x)))


def _rwkv_feat_kernel(x_ref, xp_ref, xn_ref, mod_ref, mu_ref, wrkv_ref, w0_ref, w1_ref, w2_ref,
                      a0_ref, a1_ref, a2_ref, g1_ref, g2_ref, kk_ref, ka_ref,
                      r_out, v_out, nkk_out, g_out, w_out, k_out, b_out, *, n_lat_tiles):
    t = pl.program_id(1)
    nt = pl.num_programs(1)
    tm = x_ref.shape[0]
    h = _normmod(x_ref[...], mod_ref, 0, 1)
    hp = _normmod(xp_ref[7:8, :], mod_ref, 0, 1)
    hn = _normmod(xn_ref[0:1, :], mod_ref, 0, 1)
    has_left = jnp.logical_and(t != 0, t != n_lat_tiles)
    has_right = jnp.logical_and(t != n_lat_tiles - 1, t != nt - 1)
    hp = jnp.where(has_left, hp, 0.0)
    hn = jnp.where(has_right, hn, 0.0)
    row = lax.broadcasted_iota(I32, h.shape, 0)
    left = jnp.where(row == 0, hp, pltpu.roll(h, 1, axis=0))
    right = jnp.where(row == tm - 1, hn, pltpu.roll(h, tm - 1, axis=0))
    xx = 0.5 * (left + right) - h
    xr, xw, xk, xv, xa, xg = (h + xx * mu_ref[i:i + 1, :] for i in range(6))
    r = _bdot(xr, wrkv_ref[0])
    k = _bdot(xk, wrkv_ref[1])
    v = _bdot(xv, wrkv_ref[2])
    g = _bdot(_sigmoid(_bdot(xg, g1_ref[...])), g2_ref[...])
    bd = _block_diag_ones()
    kk = k * kk_ref[...]
    kk = kk * lax.rsqrt(jnp.maximum(_segsum64(kk * kk, bd), 1e-24))
    r_out[...] = r
    v_out[...] = v
    nkk_out[...] = -kk
    g_out[...] = g
    for d in range(2):
        w_lora = _bdot(jnp.tanh(_bdot(xw, w1_ref[d])), w2_ref[d])
        log_w = -_softplus(-(w0_ref[d:d + 1, :] + w_lora)) - 0.5
        w_out[d] = jnp.exp(-jnp.exp(log_w))
        a = _sigmoid(a0_ref[d:d + 1, :] + _bdot(_bdot(xa, a1_ref[d]), a2_ref[d]))
        k_out[d] = k * (1.0 + (a - 1.0) * ka_ref[...])
        b_out[d] = kk * a


def _rwkv_features(x, modtab, p, *, n_lat):
    b, t_all, d = x.shape
    tm = ROW_TILE
    nt = t_all // tm
    tb = tm // 8
    full = lambda shape: pl.BlockSpec(shape, lambda i, t: (0,) * len(shape))
    tok = pl.BlockSpec((None, tm, d), lambda i, t: (i, t, 0))
    tok2 = pl.BlockSpec((2, None, tm, d), lambda i, t: (0, i, t, 0))
    one = jax.ShapeDtypeStruct((b, t_all, d), F32)
    two = jax.ShapeDtypeStruct((2, b, t_all, d), F32)
    return pl.pallas_call(
        functools.partial(_rwkv_feat_kernel, n_lat_tiles=n_lat // tm),
        grid=(b, nt),
        in_specs=[tok,
                  pl.BlockSpec((None, 8, d), lambda i, t: (i, jnp.maximum(t * tb - 1, 0), 0)),
                  pl.BlockSpec((None, 8, d), lambda i, t: (i, jnp.minimum((t + 1) * tb, nt * tb - 1), 0)),
                  _mod_spec(n_lat // tm),
                  full((6, d)), full((3, d, d)), full((2, d)), full(p["w1"].shape), full(p["w2"].shape),
                  full((2, d)), full(p["a1"].shape), full(p["a2"].shape), full(p["g1"].shape),
                  full(p["g2"].shape), full((1, d)), full((1, d))],
        out_specs=[tok, tok, tok, tok, tok2, tok2, tok2],
        out_shape=[one, one, one, one, two, two, two],
        compiler_params=_params("parallel", "parallel"),
        name="rwkv_features",
    )(x, x, x, modtab, p["mu"], p["w_rkv"], p["w0"], p["w1"], p["w2"], p["a0"], p["a1"], p["a2"],
      p["g1"], p["g2"], p["k_k"], p["k_a"])


def _scan_kernel(r_ref, w_ref, k_ref, v_ref, a_ref, b_ref, o_ref, s_ref, *, reverse):
    n = s_ref.shape[0]
    steps = r_ref.shape[0]

    @pl.when(pl.program_id(0) == 0)
    def _():
        s_ref[...] = jnp.zeros_like(s_ref)

    def step(i, carry):
        j = steps - 1 - i if reverse else i
        sa0 = jnp.zeros(s_ref.shape[1:], F32)
        sa1 = jnp.zeros(s_ref.shape[1:], F32)
        for kk in range(0, n, 2):
            sa0 = sa0 + s_ref[kk] * a_ref[j, kk:kk + 1, :]
            sa1 = sa1 + s_ref[kk + 1] * a_ref[j, kk + 1:kk + 2, :]
        sa = sa0 + sa1
        v = v_ref[j]
        o0 = jnp.zeros(s_ref.shape[1:], F32)
        o1 = jnp.zeros(s_ref.shape[1:], F32)
        for kk in range(n):
            s_new = (s_ref[kk] * w_ref[j, kk:kk + 1, :] + sa * b_ref[j, kk:kk + 1, :]
                     + v * k_ref[j, kk:kk + 1, :])
            s_ref[kk] = s_new
            if kk % 2 == 0:
                o0 = o0 + s_new * r_ref[j, kk:kk + 1, :]
            else:
                o1 = o1 + s_new * r_ref[j, kk:kk + 1, :]
        o_ref[j] = o0 + o1
        return carry

    lax.fori_loop(0, steps, step, 0)


def _wkv_scan(r, w, k, v, a, b, *, n_lat, reverse):
    t_all, n, chains = r.shape
    tc = SCAN_CHUNK
    nlc, nch = n_lat // tc, t_all // tc
    ncc = nch - nlc
    if reverse:
        idx = lambda c: (jnp.where(c < ncc, nch - 1 - c, nlc - 1 - (c - ncc)), 0, 0)
    else:
        idx = lambda c: (jnp.where(c < ncc, nlc + c, c - ncc), 0, 0)
    spec = pl.BlockSpec((tc, n, chains), idx)
    return pl.pallas_call(
        functools.partial(_scan_kernel, reverse=reverse),
        grid=(nch,),
        in_specs=[spec] * 6,
        out_specs=spec,
        out_shape=jax.ShapeDtypeStruct((t_all, n, chains), F32),
        scratch_shapes=[pltpu.VMEM((n, n, chains), F32)],
        compiler_params=_params("arbitrary"),
        name="wkv_scan_bwd" if reverse else "wkv_scan_fwd",
    )(r, w, k, v, a, b)


def _rwkv_out_kernel(o_ref, r_ref, k_ref, v_ref, g_ref, rk_ref, lnw_ref, lnb_ref, wo_ref, x_ref, mod_ref,
                     out_ref):
    bd = _block_diag_ones()
    inv_n = 1.0 / C_HEAD_DIM
    o = o_ref[...]
    o = o - _segsum64(o, bd) * inv_n
    o = o * lax.rsqrt(_segsum64(o * o, bd) * inv_n + C_GN_EPS)
    o = o * lnw_ref[...] + lnb_ref[...]
    r = r_ref[...]
    bonus = _segsum64(r * k_ref[0] * rk_ref[0:1, :] + r * k_ref[1] * rk_ref[1:2, :], bd) * v_ref[...]
    y = _bdot((o + bonus) * g_ref[...], wo_ref[...])
    out_ref[...] = x_ref[...] + mod_ref[2:3, :] * y


def _rwkv_readout(o, r, k2, v, g, p, x, modtab, *, n_lat, rows):
    b, t_all, d = x.shape
    tm = ROW_TILE
    tok = pl.BlockSpec((None, tm, d), lambda i, t: (i, t, 0))
    full = lambda shape: pl.BlockSpec(shape, lambda i, t: (0,) * len(shape))
    return pl.pallas_call(
        _rwkv_out_kernel,
        grid=(b, rows // tm),
        in_specs=[tok, tok, pl.BlockSpec((2, None, tm, d), lambda i, t: (0, i, t, 0)), tok, tok,
                  full((2, d)), full((1, d)), full((1, d)), full((d, d)), tok, _mod_spec(n_lat // tm)],
        out_specs=tok,
        out_shape=jax.ShapeDtypeStruct((b, rows, d), F32),
        compiler_params=_params("parallel", "parallel"),
        name="rwkv_readout",
    )(o, r, k2, v, g, p["r_k"], p["ln_w"], p["ln_b"], p["w_o"], x, modtab)


def _to_scan_layout(a):
    b, t, d = a.shape
    return jnp.transpose(a.reshape(b, t, C_HEADS, C_HEAD_DIM), (1, 3, 0, 2)).reshape(t, C_HEAD_DIM, b * C_HEADS)


def _from_scan_layout(o, b):
    t = o.shape[0]
    return jnp.transpose(o.reshape(t, C_HEAD_DIM, b, C_HEADS), (2, 0, 3, 1)).reshape(b, t, D_MODEL)


def _rwkv_mixer(x, modtab, p, *, n_lat, rows):
    b = x.shape[0]
    r, v, nkk, g, w2, k2, b2 = _rwkv_features(x, modtab, p, n_lat=n_lat)
    rs, vs, as_ = _to_scan_layout(r), _to_scan_layout(v), _to_scan_layout(nkk)
    o = None
    for d in range(2):
        od = _wkv_scan(rs, _to_scan_layout(w2[d]), _to_scan_layout(k2[d]), vs, as_, _to_scan_layout(b2[d]),
                       n_lat=n_lat, reverse=(d == 1))
        o = od if o is None else o + od
    return _rwkv_readout(_from_scan_layout(o, b), r, k2, v, g, p, x, modtab, n_lat=n_lat, rows=rows)


def _final_kernel(x_ref, w_ref, o_ref):
    o_ref[...] = _rms(x_ref[...]) * w_ref[...]


def _final_norm(x, w):
    b, t, d = x.shape
    tm = ROW_TILE
    return pl.pallas_call(
        _final_kernel,
        grid=(b, t // tm),
        in_specs=[pl.BlockSpec((None, tm, d), lambda i, t: (i, t, 0)),
                  pl.BlockSpec((1, d), lambda i, t: (0, 0))],
        out_specs=pl.BlockSpec((None, tm, d), lambda i, t: (i, t, 0)),
        out_shape=jax.ShapeDtypeStruct((b, t, d), F32),
        compiler_params=_params("parallel", "parallel"),
        name="final_norm",
    )(x, w.reshape(1, d))


def _rope_tables(n_lat, n_ctx, head_dim):
    rows = jnp.repeat(jnp.arange(n_lat // GRID_W, dtype=I32), GRID_W).astype(F32)
    cols = jnp.tile(jnp.arange(GRID_W, dtype=I32), n_lat // GRID_W).astype(F32)
    n_freq = head_dim // 4
    inv_freq = ROPE_THETA ** (-jnp.arange(n_freq, dtype=F32) / n_freq)
    ang = jnp.concatenate([rows[:, None] * inv_freq, cols[:, None] * inv_freq], axis=-1)
    cos, sin = jnp.cos(ang), jnp.sin(ang)
    reps = 128 // head_dim
    cos = jnp.tile(jnp.concatenate([cos, cos], axis=-1), (1, reps))
    sin = jnp.tile(jnp.concatenate([-sin, sin], axis=-1), (1, reps))
    cos = jnp.concatenate([cos, jnp.ones((n_ctx, 128), F32)], axis=0)
    sin = jnp.concatenate([sin, jnp.zeros((n_ctx, 128), F32)], axis=0)
    return cos, sin


def kernel(x, c, ctx, c_ctx, mod_w, mod_b, a_w_qkv, a_w_o, a_q_norm, a_k_norm, b_w_qkv, b_w_o, b_sink,
           c_mu, c_w_rkv, c_w_o, c_w0, c_w1, c_w2, c_a0, c_a1, c_a2, c_g1, c_g2, c_k_k, c_k_a, c_r_k,
           c_ln_w, c_ln_b, router_w, ffn_w1, ffn_w3, ffn_w2, final_norm):
    b, n_lat, d = x.shape
    n_ctx = ctx.shape[1]
    t_all = n_lat + n_ctx
    depth = mod_w.shape[0]
    assert d == D_MODEL and n_ctx % ROW_TILE == 0 and n_lat % min(COMBINE_TILE, n_lat) == 0

    cond_rows = -(-(b + 1) // 8) * 8
    cond = jnp.zeros((cond_rows, d), F32).at[:b].set(c).at[b].set(c_ctx)
    mods = _mod_tables(cond, mod_w, mod_b).reshape(depth, cond_rows, 6, d)
    cos_a, sin_a = _rope_tables(n_lat, n_ctx, A_HEAD_DIM)
    cos_b, sin_b = _rope_tables(n_lat, n_ctx, B_HEAD_DIM)
    w1_all, w3_all, w2_all = _cast_bf16(ffn_w1), _cast_bf16(ffn_w3), _cast_bf16(ffn_w2)

    xs = jnp.concatenate([x, ctx], axis=1)
    for i in range(depth):
        last = i == depth - 1
        rows = n_lat if last else t_all
        kind, j = i % N_MIXERS, i // N_MIXERS
        modtab = jnp.stack([jnp.broadcast_to(mods[i, b], (b, 6, d)), mods[i, :b]], axis=1)
        if kind == 0:
            q, k, v = _qkv_proj(xs, modtab, a_w_qkv[j].astype(BF16), a_q_norm[j].reshape(1, -1),
                                a_k_norm[j].reshape(1, -1), cos_a, sin_a, kind=0, n_lat=n_lat)
            o = _global_attention(q, k, v, n_lat=n_lat, rows=rows)
            xs = _proj_residual(o, a_w_o[j].astype(BF16), xs, modtab, n_lat=n_lat, rows=rows)
        elif kind == 1:
            ones = jnp.ones((1, 128), F32)
            q, k, v = _qkv_proj(xs, modtab, b_w_qkv[j].astype(BF16), ones, ones, cos_b, sin_b,
                                kind=1, n_lat=n_lat)
            o = _window_attention(q, k, v, b_sink[j], n_lat=n_lat, rows=rows)
            xs = _proj_residual(o, b_w_o[j].astype(BF16), xs, modtab, n_lat=n_lat, rows=rows)
        else:
            p = dict(mu=c_mu[j], w_rkv=c_w_rkv[j].astype(BF16), w_o=c_w_o[j].astype(BF16), w0=c_w0[j],
                     w1=c_w1[j].astype(BF16), w2=c_w2[j].astype(BF16), a0=c_a0[j],
                     a1=c_a1[j].astype(BF16), a2=c_a2[j].astype(BF16), g1=c_g1[j].astype(BF16),
                     g2=c_g2[j].astype(BF16), k_k=c_k_k[j].reshape(1, d), k_a=c_k_a[j].reshape(1, d),
                     r_k=c_r_k[j].reshape(2, d), ln_w=c_ln_w[j].reshape(1, d), ln_b=c_ln_b[j].reshape(1, d))
            xs = _rwkv_mixer(xs, modtab, p, n_lat=n_lat, rows=rows)
        xs = _moe(xs, modtab, router_w[i].T, w1_all, w3_all, w2_all, i, n_lat=n_lat, with_ctx=not last)
    return _final_norm(xs, final_norm)
```

```python
import functools
import math

import jax
import jax.numpy as jnp
from jax import lax
from jax.experimental import pallas as pl
from jax.experimental.pallas import tpu as pltpu

F32 = jnp.float32
BF16 = jnp.bfloat16
I32 = jnp.int32

D_MODEL = 1024
GRID_W = 64
Q_BLOCK = 128
ROPE_THETA = 10000.0
NORM_EPS = 1e-6
NEG_INF = -1e30
A_HEADS, A_KV_HEADS, A_HEAD_DIM = 8, 2, 128
B_HEADS, B_KV_HEADS, B_HEAD_DIM = 16, 4, 64
WINDOW = 128
C_HEAD_DIM = 64
C_HEADS = D_MODEL // C_HEAD_DIM
C_GN_EPS = C_HEAD_DIM * 1e-5
N_EXPERTS = 16
EXPERT_FF = 2 * D_MODEL
CAPACITY_FACTOR = 2
N_MIXERS = 3
LOG2E = math.log2(math.e)

ROW_TILE = 256
COMBINE_TILE = 1024
SLOT_TILE = 128
GATHER_BLOCK = 256
SCAN_CHUNK = 16
VMEM_LIMIT = 56 * 1024 * 1024


def _params(*sem):
    return pltpu.CompilerParams(dimension_semantics=sem, vmem_limit_bytes=VMEM_LIMIT)


def _rms(x):
    return x * lax.rsqrt(jnp.mean(x * x, axis=-1, keepdims=True) + NORM_EPS)


def _normmod(x, mod_ref, shift_row, scale_row):
    return _rms(x) * (1.0 + mod_ref[scale_row:scale_row + 1, :]) + mod_ref[shift_row:shift_row + 1, :]


def _bdot(a, b):
    return jnp.dot(a.astype(BF16), b.astype(BF16), preferred_element_type=F32)


def _dot_nt(a, b):
    return lax.dot_general(a.astype(BF16), b.astype(BF16), (((1,), (1,)), ((), ())),
                           preferred_element_type=F32)


def _sigmoid(x):
    return 1.0 / (1.0 + jnp.exp(-x))


def _mod_spec(n_lat_tiles):
    return pl.BlockSpec((None, None, 6, D_MODEL),
                        lambda b, t: (b, jnp.where(t < n_lat_tiles, 1, 0), 0, 0))


def _cast_kernel(x_ref, o_ref):
    o_ref[...] = x_ref[...].astype(BF16)


def _cast_bf16(w):
    shape = w.shape
    w2 = w.reshape(-1, shape[-1])
    rows, cols = w2.shape
    tr = 512
    out = pl.pallas_call(
        _cast_kernel,
        grid=(rows // tr,),
        in_specs=[pl.BlockSpec((tr, cols), lambda i: (i, 0))],
        out_specs=pl.BlockSpec((tr, cols), lambda i: (i, 0)),
        out_shape=jax.ShapeDtypeStruct((rows, cols), BF16),
        compiler_params=_params("parallel"),
        name="cast_bf16",
    )(w2)
    return out.reshape(shape)


def _mod_kernel(cond_ref, w_ref, b_ref, o_ref):
    c = cond_ref[...]
    a = c * _sigmoid(c)
    o_ref[0] = _bdot(a, w_ref[0]) + b_ref[0]


def _mod_tables(cond, mod_w, mod_b):
    depth, d, n = mod_w.shape
    rows = cond.shape[0]
    return pl.pallas_call(
        _mod_kernel,
        grid=(depth, n // d),
        in_specs=[pl.BlockSpec((rows, d), lambda i, j: (0, 0)),
                  pl.BlockSpec((1, d, d), lambda i, j: (i, 0, j)),
                  pl.BlockSpec((1, 1, d), lambda i, j: (i, 0, j))],
        out_specs=pl.BlockSpec((1, rows, d), lambda i, j: (i, 0, j)),
        out_shape=jax.ShapeDtypeStruct((depth, rows, n), F32),
        compiler_params=_params("arbitrary", "arbitrary"),
        name="mod_tables",
    )(cond, mod_w, mod_b.reshape(depth, 1, n))


def _rope128(x, cos, sin_signed, half):
    if half == 64:
        rot = pltpu.roll(x, 64, axis=1)
    else:
        lane = lax.broadcasted_iota(I32, x.shape, 1)
        rot = jnp.where((lane % 64) < 32, pltpu.roll(x, 96, axis=1), pltpu.roll(x, 32, axis=1))
    return x * cos + rot * sin_signed


def _qkv_kernel(x_ref, mod_ref, w_ref, gq_ref, gk_ref, cos_ref, sin_ref, q_ref, k_ref, v_ref, *, kind):
    h = _normmod(x_ref[...], mod_ref, 0, 1)
    y = _bdot(h, w_ref[...])
    cos, sin = cos_ref[...], sin_ref[...]
    nq = D_MODEL
    nkv = (y.shape[1] - nq) // 2
    half = 64 if kind == 0 else 32
    q_scale = (2 * half) ** -0.5 * LOG2E
    for j in range((nq + nkv) // 128):
        s = y[:, j * 128:(j + 1) * 128]
        if kind == 0:
            gain = gq_ref[...] if j < nq // 128 else gk_ref[...]
            s = _rms(s) * gain
        s = _rope128(s, cos, sin, half)
        if j < nq // 128:
            q_ref[:, j * 128:(j + 1) * 128] = (s * q_scale).astype(BF16)
        elif kind == 0:
            k_ref[:, (j - nq // 128) * 128:(j - nq // 128 + 1) * 128] = s.astype(BF16)
        else:
            jj = (j - nq // 128) * 2
            k_ref[jj] = s[:, :64].astype(BF16)
            k_ref[jj + 1] = s[:, 64:].astype(BF16)
    v = y[:, nq + nkv:].astype(BF16)
    if kind == 0:
        v_ref[...] = v
    else:
        for jj in range(nkv // 64):
            v_ref[jj] = v[:, jj * 64:(jj + 1) * 64]


def _qkv_proj(x, modtab, w, gq, gk, cos, sin, *, kind, n_lat):
    b, t_all, d = x.shape
    tm = ROW_TILE
    nt = t_all // tm
    n = w.shape[1]
    nkv = (n - d) // 2
    if kind == 0:
        kv_shape = jax.ShapeDtypeStruct((b, t_all, nkv), BF16)
        kv_spec = pl.BlockSpec((None, tm, nkv), lambda i, t: (i, t, 0))
    else:
        kv_shape = jax.ShapeDtypeStruct((b, nkv // 64, t_all, 64), BF16)
        kv_spec = pl.BlockSpec((None, nkv // 64, tm, 64), lambda i, t: (i, 0, t, 0))
    return pl.pallas_call(
        functools.partial(_qkv_kernel, kind=kind),
        grid=(b, nt),
        in_specs=[pl.BlockSpec((None, tm, d), lambda i, t: (i, t, 0)),
                  _mod_spec(n_lat // tm),
                  pl.BlockSpec((d, n), lambda i, t: (0, 0)),
                  pl.BlockSpec((1, 128), lambda i, t: (0, 0)),
                  pl.BlockSpec((1, 128), lambda i, t: (0, 0)),
                  pl.BlockSpec((tm, 128), lambda i, t: (t, 0)),
                  pl.BlockSpec((tm, 128), lambda i, t: (t, 0))],
        out_specs=[pl.BlockSpec((None, tm, d), lambda i, t: (i, t, 0)), kv_spec, kv_spec],
        out_shape=[jax.ShapeDtypeStruct((b, t_all, d), BF16), kv_shape, kv_shape],
        compiler_params=_params("parallel", "parallel"),
        name=f"qkv_proj_{kind}",
    )(x, modtab, w, gq, gk, cos, sin)


def _flash_chunk(carry, q, kc, vc):
    m, l, acc = carry
    s = _dot_nt(q, kc)
    m_new = jnp.maximum(m, jnp.max(s, axis=-1, keepdims=True))
    alpha = jnp.exp2(m - m_new)
    p = jnp.exp2(s - m_new)
    l = alpha * l + jnp.sum(p, axis=-1, keepdims=True)
    acc = alpha * acc + jnp.dot(p.astype(BF16), vc, preferred_element_type=F32)
    return m_new, l, acc


def _gattn_kernel(q_ref, k_ref, v_ref, o_ref, *, n_lat, kchunk):
    hd = A_HEAD_DIM
    g = A_HEADS // A_KV_HEADS
    tq = q_ref.shape[0]
    t_all = k_ref.shape[0]
    qs = jnp.concatenate([q_ref[:, i * hd:(i + 1) * hd] for i in range(g)], axis=0)
    init = (jnp.full((g * tq, 1), NEG_INF, F32), jnp.zeros((g * tq, 1), F32), jnp.zeros((g * tq, hd), F32))

    def finish(carry):
        _, l, acc = carry
        o = (acc / l).astype(BF16)
        for i in range(g):
            o_ref[:, i * hd:(i + 1) * hd] = o[i * tq:(i + 1) * tq]

    ctx_carry = _flash_chunk(init, qs, k_ref[n_lat:t_all, :], v_ref[n_lat:t_all, :])
    is_ctx = pl.program_id(2) >= n_lat // tq

    @pl.when(is_ctx)
    def _():
        finish(ctx_carry)

    @pl.when(jnp.logical_not(is_ctx))
    def _():
        def body(c, carry):
            start = pl.multiple_of(c * kchunk, kchunk)
            return _flash_chunk(carry, qs, k_ref[pl.ds(start, kchunk), :], v_ref[pl.ds(start, kchunk), :])
        finish(lax.fori_loop(0, n_lat // kchunk, body, ctx_carry))


def _global_attention(q, k, v, *, n_lat, rows):
    b, t_all, d = q.shape
    hd, g = A_HEAD_DIM, A_HEADS // A_KV_HEADS
    tq = ROW_TILE
    return pl.pallas_call(
        functools.partial(_gattn_kernel, n_lat=n_lat, kchunk=min(2048, n_lat)),
        grid=(b, A_KV_HEADS, rows // tq),
        in_specs=[pl.BlockSpec((None, tq, g * hd), lambda i, h, t: (i, t, h)),
                  pl.BlockSpec((None, t_all, hd), lambda i, h, t: (i, 0, h)),
                  pl.BlockSpec((None, t_all, hd), lambda i, h, t: (i, 0, h))],
        out_specs=pl.BlockSpec((None, tq, g * hd), lambda i, h, t: (i, t, h)),
        out_shape=jax.ShapeDtypeStruct((b, t_all, d), BF16),
        compiler_params=_params("parallel", "parallel", "arbitrary"),
        name="global_attention",
    )(q, k, v)


def _wattn_kernel(q_ref, k_ref, v_ref, sink_ref, o_ref, *, n_lat):
    hd = B_HEAD_DIM
    g = B_HEADS // B_KV_HEADS
    tq = q_ref.shape[0]
    t_all = k_ref.shape[0]
    t = pl.program_id(2)
    is_ctx = t >= n_lat // tq
    kc, vc = k_ref[n_lat:t_all, :], v_ref[n_lat:t_all, :]
    q = jnp.concatenate([q_ref[:, i * hd:(i + 1) * hd] for i in range(g)], axis=0)
    sink = jnp.concatenate([jnp.broadcast_to(sink_ref[:, i:i + 1], (tq, 1)) for i in range(g)],
                           axis=0) * LOG2E
    s_ctx = _dot_nt(q, kc)

    def finish(o):
        o = o.astype(BF16)
        o_ref[...] = jnp.concatenate([o[i * tq:(i + 1) * tq] for i in range(g)], axis=1)

    @pl.when(is_ctx)
    def _():
        m = jnp.maximum(jnp.max(s_ctx, axis=-1, keepdims=True), sink)
        p = jnp.exp2(s_ctx - m)
        l = jnp.sum(p, axis=-1, keepdims=True) + jnp.exp2(sink - m)
        finish(jnp.dot(p.astype(BF16), vc, preferred_element_type=F32) / l)

    @pl.when(jnp.logical_not(is_ctx))
    def _():
        wlen = tq + 2 * WINDOW
        ws = jnp.clip(t * tq - WINDOW, 0, n_lat - wlen)
        start = pl.multiple_of(ws, 128)
        kw, vw = k_ref[pl.ds(start, wlen), :], v_ref[pl.ds(start, wlen), :]
        q_pos = t * tq + lax.broadcasted_iota(I32, (g * tq, wlen), 0) % tq
        k_pos = ws + lax.broadcasted_iota(I32, (g * tq, wlen), 1)
        band = jnp.abs(k_pos - q_pos) <= WINDOW
        s_win = jnp.where(band, _dot_nt(q, kw), NEG_INF)
        m = jnp.maximum(jnp.maximum(jnp.max(s_win, axis=-1, keepdims=True),
                                    jnp.max(s_ctx, axis=-1, keepdims=True)), sink)
        p_win = jnp.exp2(s_win - m)
        p_ctx = jnp.exp2(s_ctx - m)
        l = (jnp.sum(p_win, axis=-1, keepdims=True) + jnp.sum(p_ctx, axis=-1, keepdims=True)
             + jnp.exp2(sink - m))
        o = (jnp.dot(p_win.astype(BF16), vw, preferred_element_type=F32)
             + jnp.dot(p_ctx.astype(BF16), vc, preferred_element_type=F32))
        finish(o / l)


def _window_attention(q, k, v, sink, *, n_lat, rows):
    b, t_all, d = q.shape
    hd, g = B_HEAD_DIM, B_HEADS // B_KV_HEADS
    tq = ROW_TILE
    return pl.pallas_call(
        functools.partial(_wattn_kernel, n_lat=n_lat),
        grid=(b, B_KV_HEADS, rows // tq),
        in_specs=[pl.BlockSpec((None, tq, g * hd), lambda i, h, t: (i, t, h)),
                  pl.BlockSpec((None, None, t_all, hd), lambda i, h, t: (i, h, 0, 0)),
                  pl.BlockSpec((None, None, t_all, hd), lambda i, h, t: (i, h, 0, 0)),
                  pl.BlockSpec((None, 1, g), lambda i, h, t: (h, 0, 0))],
        out_specs=pl.BlockSpec((None, tq, g * hd), lambda i, h, t: (i, t, h)),
        out_shape=jax.ShapeDtypeStruct((b, t_all, d), BF16),
        compiler_params=_params("parallel", "parallel", "arbitrary"),
        name="window_attention",
    )(q, k, v, sink.reshape(B_KV_HEADS, 1, g))


def _proj_res_kernel(a_ref, w_ref, x_ref, mod_ref, o_ref):
    o_ref[...] = x_ref[...] + mod_ref[2:3, :] * _bdot(a_ref[...], w_ref[...])


def _proj_residual(a, w, x, modtab, *, n_lat, rows):
    b, t_all, d = x.shape
    tm = ROW_TILE
    return pl.pallas_call(
        _proj_res_kernel,
        grid=(b, rows // tm),
        in_specs=[pl.BlockSpec((None, tm, d), lambda i, t: (i, t, 0)),
                  pl.BlockSpec((d, d), lambda i, t: (0, 0)),
                  pl.BlockSpec((None, tm, d), lambda i, t: (i, t, 0)),
                  _mod_spec(n_lat // tm)],
        out_specs=pl.BlockSpec((None, tm, d), lambda i, t: (i, t, 0)),
        out_shape=jax.ShapeDtypeStruct((b, rows, d), F32),
        compiler_params=_params("parallel", "parallel"),
        name="proj_residual",
    )(a, w, x, modtab)


def _router_kernel(x_ref, mod_ref, rwt_ref, h_ref, aff_ref, afft_ref):
    h = _normmod(x_ref[...], mod_ref, 3, 4)
    h_ref[...] = h.astype(BF16)
    logits_t = lax.dot_general(rwt_ref[...], h, (((1,), (1,)), ((), ())),
                               precision=lax.Precision.HIGHEST, preferred_element_type=F32)
    e = jnp.exp(logits_t - jnp.max(logits_t, axis=0, keepdims=True))
    aff_t = e / jnp.sum(e, axis=0, keepdims=True)
    afft_ref[...] = aff_t
    aff_ref[...] = aff_t.T


def _router(x, modtab, router_w_t, *, n_lat):
    b, rows, d = x.shape
    tm = ROW_TILE
    ne = router_w_t.shape[0]
    return pl.pallas_call(
        _router_kernel,
        grid=(b, rows // tm),
        in_specs=[pl.BlockSpec((None, tm, d), lambda i, t: (i, t, 0)),
                  _mod_spec(n_lat // tm),
                  pl.BlockSpec((ne, d), lambda i, t: (0, 0))],
        out_specs=[pl.BlockSpec((None, tm, d), lambda i, t: (i, t, 0)),
                   pl.BlockSpec((None, tm, ne), lambda i, t: (i, t, 0)),
                   pl.BlockSpec((None, ne, tm), lambda i, t: (i, 0, t))],
        out_shape=[jax.ShapeDtypeStruct((b, rows, d), BF16),
                   jax.ShapeDtypeStruct((b, rows, ne), F32),
                   jax.ShapeDtypeStruct((b, ne, rows), F32)],
        compiler_params=_params("parallel", "parallel"),
        name="router",
    )(x, modtab, router_w_t)


def _lane_prefix(mask_f, tri):
    n = mask_f.shape[1]
    run = jnp.zeros((mask_f.shape[0], 1), F32)
    out = []
    for j in range(n // 128):
        blk = mask_f[:, j * 128:(j + 1) * 128]
        incl = jnp.dot(blk.astype(BF16), tri, preferred_element_type=F32)
        out.append(incl - blk + run)
        run = run + incl[:, 127:128]
    return jnp.concatenate(out, axis=1)


def _select_kernel(afft_ref, slot_ref, slot_tm_ref, bounds_ref, *, segments):
    r = lax.broadcasted_iota(I32, (128, 128), 0)
    c = lax.broadcasted_iota(I32, (128, 128), 1)
    tri = (r <= c).astype(BF16)
    for off, n, cap, base in segments:
        bits = lax.bitcast_convert_type(afft_ref[:, off:off + n], I32)
        cap_f = jnp.float32(cap)

        def body(i, thr):
            cand = thr | jnp.left_shift(jnp.int32(1), 30 - i)
            cnt = jnp.sum((bits >= cand).astype(F32), axis=1, keepdims=True)
            return jnp.where(cnt >= cap_f, cand, thr)

        thr = lax.fori_loop(0, 31, body, jnp.zeros((bits.shape[0], 1), I32))
        gt = (bits > thr).astype(F32)
        eq = (bits == thr).astype(F32)
        need = cap_f - jnp.sum(gt, axis=1, keepdims=True)
        sel = gt + eq * (_lane_prefix(eq, tri) < need).astype(F32)
        pref = _lane_prefix(sel, tri)
        slot = jnp.where(sel > 0.5, pref.astype(I32) + base, -1)
        slot_ref[:, off:off + n] = slot
        slot_tm_ref[off:off + n, :] = slot.astype(F32).T
        if off == 0:
            incl = pref + sel
            st = min(SLOT_TILE, cap)
            lane = lax.broadcasted_iota(I32, bounds_ref.shape, 1)
            bounds = jnp.zeros(bounds_ref.shape, F32)
            for j in range(cap // st):
                for q, s_id in enumerate((j * st, j * st + st - 1)):
                    tok = jnp.sum((incl <= s_id).astype(F32), axis=1, keepdims=True)
                    bounds = jnp.where(lane == q * (cap // st) + j, jnp.floor(tok * (1.0 / GATHER_BLOCK)), bounds)
            bounds_ref[...] = bounds.astype(I32)


def _select(aff_t, segments):
    b, ne, rows = aff_t.shape
    return pl.pallas_call(
        functools.partial(_select_kernel, segments=segments),
        grid=(b,),
        in_specs=[pl.BlockSpec((None, ne, rows), lambda i: (i, 0, 0))],
        out_specs=[pl.BlockSpec((None, ne, rows), lambda i: (i, 0, 0)),
                   pl.BlockSpec((None, rows, ne), lambda i: (i, 0, 0)),
                   pl.BlockSpec((None, ne, 128), lambda i: (i, 0, 0))],
        out_shape=[jax.ShapeDtypeStruct((b, ne, rows), I32), jax.ShapeDtypeStruct((b, rows, ne), F32),
                   jax.ShapeDtypeStruct((b, ne, 128), I32)],
        compiler_params=_params("parallel"),
        name="expert_select",
    )(aff_t)


def _gather_kernel(bounds_ref, slot_ref, h_ref, xs_ref, *, segments):
    i, e = pl.program_id(0), pl.program_id(1)
    tb = GATHER_BLOCK
    d = h_ref.shape[1]
    for off, n, cap, base in segments:
        st = min(SLOT_TILE, cap)
        nt = cap // st
        for j in range(nt):
            ids = base + j * st + lax.broadcasted_iota(I32, (st, 1), 0)

            def body(blk, acc):
                start = pl.multiple_of(blk * tb, tb)
                onehot = jnp.where(ids == slot_ref[pl.ds(blk, 1), :], 1.0, 0.0).astype(BF16)
                return acc + jnp.dot(onehot, h_ref[pl.ds(start, tb), :], preferred_element_type=F32)

            if off == 0:
                lo, hi = bounds_ref[i, e, j], bounds_ref[i, e, nt + j] + 1
            else:
                lo, hi = off // tb, (off + n) // tb
            rows = lax.fori_loop(lo, hi, body, jnp.zeros((st, d), F32))
            xs_ref[base + j * st:base + (j + 1) * st, :] = rows.astype(BF16)


def _gather(slot, bounds, h, segments, n_slots):
    b, ne, rows = slot.shape
    d = h.shape[2]
    tb = GATHER_BLOCK
    return pl.pallas_call(
        functools.partial(_gather_kernel, segments=segments),
        grid_spec=pltpu.PrefetchScalarGridSpec(
            num_scalar_prefetch=1,
            grid=(b, ne),
            in_specs=[pl.BlockSpec((None, None, rows // tb, tb), lambda i, e, bnd: (i, e, 0, 0)),
                      pl.BlockSpec((None, rows, d), lambda i, e, bnd: (i, 0, 0))],
            out_specs=pl.BlockSpec((None, None, n_slots, d), lambda i, e, bnd: (e, i, 0, 0))),
        out_shape=jax.ShapeDtypeStruct((ne, b, n_slots, d), BF16),
        compiler_params=_params("parallel", "arbitrary"),
        name="expert_gather",
    )(bounds, slot.reshape(b, ne, rows // tb, tb), h)


def _ffn_kernel(xs_ref, w1_ref, w3_ref, w2_ref, y_ref):
    xs = xs_ref[...]
    a = jnp.dot(xs, w1_ref[...], preferred_element_type=F32)
    g = jnp.dot(xs, w3_ref[...], preferred_element_type=F32)
    hid = (a * _sigmoid(a)) * g
    y_ref[...] = _bdot(hid, w2_ref[...]).astype(BF16)


def _expert_ffn(xs, w1, w3, w2, layer):
    ne, b, n_slots, d = xs.shape
    ff = w1.shape[3]
    return pl.pallas_call(
        _ffn_kernel,
        grid=(ne, b),
        in_specs=[pl.BlockSpec((None, None, n_slots, d), lambda e, i: (e, i, 0, 0)),
                  pl.BlockSpec((None, None, d, ff), lambda e, i: (layer, e, 0, 0)),
                  pl.BlockSpec((None, None, d, ff), lambda e, i: (layer, e, 0, 0)),
                  pl.BlockSpec((None, None, ff, d), lambda e, i: (layer, e, 0, 0))],
        out_specs=pl.BlockSpec((None, None, n_slots, d), lambda e, i: (e, i, 0, 0)),
        out_shape=jax.ShapeDtypeStruct((ne, b, n_slots, d), BF16),
        compiler_params=_params("parallel", "arbitrary"),
        name="expert_ffn",
    )(xs, w1, w3, w2)


def _combine_kernel(slot_ref, aff_ref, y_ref, x_ref, mod_ref, o_ref, acc_ref, *, slot0):
    e = pl.program_id(2)
    ne = pl.num_programs(2)
    d = x_ref.shape[1]
    width = y_ref.shape[0]

    @pl.when(e == 0)
    def _():
        acc_ref[...] = jnp.zeros_like(acc_ref)

    lane_e = lax.broadcasted_iota(I32, aff_ref.shape, 1)
    gate = jnp.sum(jnp.where(lane_e == e, aff_ref[...], 0.0), axis=1, keepdims=True)
    slot = jnp.sum(jnp.where(lane_e == e, slot_ref[...], 0.0), axis=1, keepdims=True).astype(I32)
    ids = slot0 + lax.broadcasted_iota(I32, (1, width), 1)
    onehot = jnp.where(slot == ids, 1.0, 0.0).astype(BF16)
    acc_ref[...] += gate * jnp.dot(onehot, y_ref[...], preferred_element_type=F32)

    @pl.when(e == ne - 1)
    def _():
        o_ref[...] = x_ref[...] + mod_ref[5:6, :] * acc_ref[...]


def _combine(slot_tm, aff, y, x, modtab, *, row0, rows, tm, slot0, width, seg, in_place):
    b, _, d = x.shape
    ne = aff.shape[2]
    t0, s0 = row0 // tm, slot0 // width
    out_rows = x.shape[1] if in_place else rows
    return pl.pallas_call(
        functools.partial(_combine_kernel, slot0=slot0),
        grid=(b, rows // tm, ne),
        in_specs=[pl.BlockSpec((None, tm, ne), lambda i, t, e: (i, t + t0, 0)),
                  pl.BlockSpec((None, tm, ne), lambda i, t, e: (i, t + t0, 0)),
                  pl.BlockSpec((None, None, width, d), lambda i, t, e: (e, i, s0, 0)),
                  pl.BlockSpec((None, tm, d), lambda i, t, e: (i, t + t0, 0)),
                  pl.BlockSpec((None, None, 6, d), lambda i, t, e: (i, seg, 0, 0))],
        out_specs=pl.BlockSpec((None, tm, d), lambda i, t, e: (i, t + (t0 if in_place else 0), 0)),
        out_shape=jax.ShapeDtypeStruct((b, out_rows, d), F32),
        scratch_shapes=[pltpu.VMEM((tm, d), F32)],
        input_output_aliases={3: 0} if in_place else {},
        compiler_params=_params("parallel", "parallel", "arbitrary"),
        name="expert_combine",
    )(slot_tm, aff, y, x, modtab)


def _moe(x, modtab, router_w_t, w1, w3, w2, layer, *, n_lat, with_ctx):
    b, rows, d = x.shape
    ne = N_EXPERTS
    cap_l = CAPACITY_FACTOR * n_lat // ne
    h, aff, aff_t = _router(x, modtab, router_w_t, n_lat=n_lat)
    segments = ((0, n_lat, cap_l, 0),)
    n_slots = cap_l
    if with_ctx:
        n_ctx = rows - n_lat
        cap_c = CAPACITY_FACTOR * n_ctx // ne
        segments += ((n_lat, n_ctx, cap_c, cap_l),)
        n_slots += cap_c
    slot, slot_tm, bounds = _select(aff_t, segments)
    xs = _gather(slot, bounds[:, :, :2 * (cap_l // min(SLOT_TILE, cap_l))], h, segments, n_slots)
    y = _expert_ffn(xs, w1, w3, w2, layer)
    out = _combine(slot_tm, aff, y, x, modtab, row0=0, rows=n_lat, tm=min(COMBINE_TILE, n_lat), slot0=0,
                   width=cap_l, seg=1, in_place=with_ctx)
    if with_ctx:
        out = _combine(slot_tm, aff, y, out, modtab, row0=n_lat, rows=n_ctx, tm=n_ctx, slot0=cap_l,
                       width=cap_c, seg=0, in_place=True)
    return out


def _segsum64(x, bd):
    hi = x.astype(BF16)
    lo = (x - hi.astype(F32)).astype(BF16)
    out = []
    for j in range(x.shape[1] // 256):
        sl = slice(j * 256, (j + 1) * 256)
        out.append(jnp.dot(hi[:, sl], bd, preferred_element_type=F32)
                   + jnp.dot(lo[:, sl], bd, preferred_element_type=F32))
    return jnp.concatenate(out, axis=1)


def _block_diag_ones():
    r = lax.broadcasted_iota(I32, (256, 256), 0)
    c = lax.broadcasted_iota(I32, (256, 256), 1)
    return (r // 64 == c // 64).astype(BF16)


def _softplus(x):
    return jnp.maximum(x, 0.0) + jnp.log(1.0 + jnp.exp(-jnp.abs(x)))


def _rwkv_feat_kernel(x_ref, xp_ref, xn_ref, mod_ref, mu_ref, wrkv_ref, w0_ref, w1_ref, w2_ref,
                      a0_ref, a1_ref, a2_ref, g1_ref, g2_ref, kk_ref, ka_ref,
                      r_out, v_out, nkk_out, g_out, w_out, k_out, b_out, *, n_lat_tiles):
    t = pl.program_id(1)
    nt = pl.num_programs(1)
    tm = x_ref.shape[0]
    h = _normmod(x_ref[...], mod_ref, 0, 1)
    hp = _normmod(xp_ref[7:8, :], mod_ref, 0, 1)
    hn = _normmod(xn_ref[0:1, :], mod_ref, 0, 1)
    has_left = jnp.logical_and(t != 0, t != n_lat_tiles)
    has_right = jnp.logical_and(t != n_lat_tiles - 1, t != nt - 1)
    hp = jnp.where(has_left, hp, 0.0)
    hn = jnp.where(has_right, hn, 0.0)
    row = lax.broadcasted_iota(I32, h.shape, 0)
    left = jnp.where(row == 0, hp, pltpu.roll(h, 1, axis=0))
    right = jnp.where(row == tm - 1, hn, pltpu.roll(h, tm - 1, axis=0))
    xx = 0.5 * (left + right) - h
    xr, xw, xk, xv, xa, xg = (h + xx * mu_ref[i:i + 1, :] for i in range(6))
    r = _bdot(xr, wrkv_ref[0])
    k = _bdot(xk, wrkv_ref[1])
    v = _bdot(xv, wrkv_ref[2])
    g = _bdot(_sigmoid(_bdot(xg, g1_ref[...])), g2_ref[...])
    bd = _block_diag_ones()
    kk = k * kk_ref[...]
    kk = kk * lax.rsqrt(jnp.maximum(_segsum64(kk * kk, bd), 1e-24))
    r_out[...] = r
    v_out[...] = v
    nkk_out[...] = -kk
    g_out[...] = g
    for d in range(2):
        w_lora = _bdot(jnp.tanh(_bdot(xw, w1_ref[d])), w2_ref[d])
        log_w = -_softplus(-(w0_ref[d:d + 1, :] + w_lora)) - 0.5
        w_out[d] = jnp.exp(-jnp.exp(log_w))
        a = _sigmoid(a0_ref[d:d + 1, :] + _bdot(_bdot(xa, a1_ref[d]), a2_ref[d]))
        k_out[d] = k * (1.0 + (a - 1.0) * ka_ref[...])
        b_out[d] = kk * a


def _rwkv_features(x, modtab, p, *, n_lat):
    b, t_all, d = x.shape
    tm = ROW_TILE
    nt = t_all // tm
    tb = tm // 8
    full = lambda shape: pl.BlockSpec(shape, lambda i, t: (0,) * len(shape))
    tok = pl.BlockSpec((None, tm, d), lambda i, t: (i, t, 0))
    tok2 = pl.BlockSpec((2, None, tm, d), lambda i, t: (0, i, t, 0))
    one = jax.ShapeDtypeStruct((b, t_all, d), F32)
    two = jax.ShapeDtypeStruct((2, b, t_all, d), F32)
    return pl.pallas_call(
        functools.partial(_rwkv_feat_kernel, n_lat_tiles=n_lat // tm),
        grid=(b, nt),
        in_specs=[tok,
                  pl.BlockSpec((None, 8, d), lambda i, t: (i, jnp.maximum(t * tb - 1, 0), 0)),
                  pl.BlockSpec((None, 8, d), lambda i, t: (i, jnp.minimum((t + 1) * tb, nt * tb - 1), 0)),
                  _mod_spec(n_lat // tm),
                  full((6, d)), full((3, d, d)), full((2, d)), full(p["w1"].shape), full(p["w2"].shape),
                  full((2, d)), full(p["a1"].shape), full(p["a2"].shape), full(p["g1"].shape),
                  full(p["g2"].shape), full((1, d)), full((1, d))],
        out_specs=[tok, tok, tok, tok, tok2, tok2, tok2],
        out_shape=[one, one, one, one, two, two, two],
        compiler_params=_params("parallel", "parallel"),
        name="rwkv_features",
    )(x, x, x, modtab, p["mu"], p["w_rkv"], p["w0"], p["w1"], p["w2"], p["a0"], p["a1"], p["a2"],
      p["g1"], p["g2"], p["k_k"], p["k_a"])


def _scan_kernel(r_ref, w_ref, k_ref, v_ref, a_ref, b_ref, o_ref, s_ref, *, reverse):
    n = s_ref.shape[0]
    steps = r_ref.shape[0]

    @pl.when(pl.program_id(0) == 0)
    def _():
        s_ref[...] = jnp.zeros_like(s_ref)

    def step(i, carry):
        j = steps - 1 - i if reverse else i
        r = r_ref[j]
        wr = w_ref[j] * r
        br = jnp.sum(b_ref[j] * r, axis=0, keepdims=True)
        kr = jnp.sum(k_ref[j] * r, axis=0, keepdims=True)
        acc = [jnp.zeros(s_ref.shape[1:], F32) for _ in range(4)]
        for kk in range(n):
            s = s_ref[kk]
            acc[kk % 2] = acc[kk % 2] + s * a_ref[j, kk:kk + 1, :]
            acc[2 + kk % 2] = acc[2 + kk % 2] + s * wr[kk:kk + 1, :]
        sa = acc[0] + acc[1]
        v = v_ref[j]
        o_ref[j] = acc[2] + acc[3] + sa * br + v * kr
        for kk in range(n):
            s_ref[kk] = (s_ref[kk] * w_ref[j, kk:kk + 1, :] + sa * b_ref[j, kk:kk + 1, :]
                         + v * k_ref[j, kk:kk + 1, :])
        return carry

    lax.fori_loop(0, steps, step, 0)


def _wkv_scan(r, w, k, v, a, b, *, n_lat, reverse):
    t_all, n, chains = r.shape
    tc = SCAN_CHUNK
    nlc, nch = n_lat // tc, t_all // tc
    ncc = nch - nlc
    if reverse:
        idx = lambda c: (jnp.where(c < ncc, nch - 1 - c, nlc - 1 - (c - ncc)), 0, 0)
    else:
        idx = lambda c: (jnp.where(c < ncc, nlc + c, c - ncc), 0, 0)
    spec = pl.BlockSpec((tc, n, chains), idx)
    return pl.pallas_call(
        functools.partial(_scan_kernel, reverse=reverse),
        grid=(nch,),
        in_specs=[spec] * 6,
        out_specs=spec,
        out_shape=jax.ShapeDtypeStruct((t_all, n, chains), F32),
        scratch_shapes=[pltpu.VMEM((n, n, chains), F32)],
        compiler_params=_params("arbitrary"),
        name="wkv_scan_bwd" if reverse else "wkv_scan_fwd",
    )(r, w, k, v, a, b)


def _rwkv_out_kernel(o_ref, r_ref, k_ref, v_ref, g_ref, rk_ref, lnw_ref, lnb_ref, wo_ref, x_ref, mod_ref,
                     out_ref):
    bd = _block_diag_ones()
    inv_n = 1.0 / C_HEAD_DIM
    o = o_ref[...]
    o = o - _segsum64(o, bd) * inv_n
    o = o * lax.rsqrt(_segsum64(o * o, bd) * inv_n + C_GN_EPS)
    o = o * lnw_ref[...] + lnb_ref[...]
    r = r_ref[...]
    bonus = _segsum64(r * k_ref[0] * rk_ref[0:1, :] + r * k_ref[1] * rk_ref[1:2, :], bd) * v_ref[...]
    y = _bdot((o + bonus) * g_ref[...], wo_ref[...])
    out_ref[...] = x_ref[...] + mod_ref[2:3, :] * y


def _rwkv_readout(o, r, k2, v, g, p, x, modtab, *, n_lat, rows):
    b, t_all, d = x.shape
    tm = ROW_TILE
    tok = pl.BlockSpec((None, tm, d), lambda i, t: (i, t, 0))
    full = lambda shape: pl.BlockSpec(shape, lambda i, t: (0,) * len(shape))
    return pl.pallas_call(
        _rwkv_out_kernel,
        grid=(b, rows // tm),
        in_specs=[tok, tok, pl.BlockSpec((2, None, tm, d), lambda i, t: (0, i, t, 0)), tok, tok,
                  full((2, d)), full((1, d)), full((1, d)), full((d, d)), tok, _mod_spec(n_lat // tm)],
        out_specs=tok,
        out_shape=jax.ShapeDtypeStruct((b, rows, d), F32),
        compiler_params=_params("parallel", "parallel"),
        name="rwkv_readout",
    )(o, r, k2, v, g, p["r_k"], p["ln_w"], p["ln_b"], p["w_o"], x, modtab)


def _to_scan_layout(a):
    b, t, d = a.shape
    return jnp.transpose(a.reshape(b, t, C_HEADS, C_HEAD_DIM), (1, 3, 0, 2)).reshape(t, C_HEAD_DIM, b * C_HEADS)


def _from_scan_layout(o, b):
    t = o.shape[0]
    return jnp.transpose(o.reshape(t, C_HEAD_DIM, b, C_HEADS), (2, 0, 3, 1)).reshape(b, t, D_MODEL)


def _rwkv_mixer(x, modtab, p, *, n_lat, rows):
    b = x.shape[0]
    r, v, nkk, g, w2, k2, b2 = _rwkv_features(x, modtab, p, n_lat=n_lat)
    rs, vs, as_ = _to_scan_layout(r), _to_scan_layout(v), _to_scan_layout(nkk)
    o = None
    for d in range(2):
        od = _wkv_scan(rs, _to_scan_layout(w2[d]), _to_scan_layout(k2[d]), vs, as_, _to_scan_layout(b2[d]),
                       n_lat=n_lat, reverse=(d == 1))
        o = od if o is None else o + od
    return _rwkv_readout(_from_scan_layout(o, b), r, k2, v, g, p, x, modtab, n_lat=n_lat, rows=rows)


def _final_kernel(x_ref, w_ref, o_ref):
    o_ref[...] = _rms(x_ref[...]) * w_ref[...]


def _final_norm(x, w):
    b, t, d = x.shape
    tm = ROW_TILE
    return pl.pallas_call(
        _final_kernel,
        grid=(b, t // tm),
        in_specs=[pl.BlockSpec((None, tm, d), lambda i, t: (i, t, 0)),
                  pl.BlockSpec((1, d), lambda i, t: (0, 0))],
        out_specs=pl.BlockSpec((None, tm, d), lambda i, t: (i, t, 0)),
        out_shape=jax.ShapeDtypeStruct((b, t, d), F32),
        compiler_params=_params("parallel", "parallel"),
        name="final_norm",
    )(x, w.reshape(1, d))


def _rope_tables(n_lat, n_ctx, head_dim):
    rows = jnp.repeat(jnp.arange(n_lat // GRID_W, dtype=I32), GRID_W).astype(F32)
    cols = jnp.tile(jnp.arange(GRID_W, dtype=I32), n_lat // GRID_W).astype(F32)
    n_freq = head_dim // 4
    inv_freq = ROPE_THETA ** (-jnp.arange(n_freq, dtype=F32) / n_freq)
    ang = jnp.concatenate([rows[:, None] * inv_freq, cols[:, None] * inv_freq], axis=-1)
    cos, sin = jnp.cos(ang), jnp.sin(ang)
    reps = 128 // head_dim
    cos = jnp.tile(jnp.concatenate([cos, cos], axis=-1), (1, reps))
    sin = jnp.tile(jnp.concatenate([-sin, sin], axis=-1), (1, reps))
    cos = jnp.concatenate([cos, jnp.ones((n_ctx, 128), F32)], axis=0)
    sin = jnp.concatenate([sin, jnp.zeros((n_ctx, 128), F32)], axis=0)
    return cos, sin


def kernel(x, c, ctx, c_ctx, mod_w, mod_b, a_w_qkv, a_w_o, a_q_norm, a_k_norm, b_w_qkv, b_w_o, b_sink,
           c_mu, c_w_rkv, c_w_o, c_w0, c_w1, c_w2, c_a0, c_a1, c_a2, c_g1, c_g2, c_k_k, c_k_a, c_r_k,
           c_ln_w, c_ln_b, router_w, ffn_w1, ffn_w3, ffn_w2, final_norm):
    b, n_lat, d = x.shape
    n_ctx = ctx.shape[1]
    t_all = n_lat + n_ctx
    depth = mod_w.shape[0]
    assert d == D_MODEL and n_ctx % ROW_TILE == 0 and n_lat % min(COMBINE_TILE, n_lat) == 0

    cond_rows = -(-(b + 1) // 8) * 8
    cond = jnp.zeros((cond_rows, d), F32).at[:b].set(c).at[b].set(c_ctx)
    mods = _mod_tables(cond, mod_w, mod_b).reshape(depth, cond_rows, 6, d)
    cos_a, sin_a = _rope_tables(n_lat, n_ctx, A_HEAD_DIM)
    cos_b, sin_b = _rope_tables(n_lat, n_ctx, B_HEAD_DIM)
    w1_all, w3_all, w2_all = _cast_bf16(ffn_w1), _cast_bf16(ffn_w3), _cast_bf16(ffn_w2)

    xs = jnp.concatenate([x, ctx], axis=1)
    for i in range(depth):
        last = i == depth - 1
        rows = n_lat if last else t_all
        kind, j = i % N_MIXERS, i // N_MIXERS
        modtab = jnp.stack([jnp.broadcast_to(mods[i, b], (b, 6, d)), mods[i, :b]], axis=1)
        if kind == 0:
            q, k, v = _qkv_proj(xs, modtab, a_w_qkv[j].astype(BF16), a_q_norm[j].reshape(1, -1),
                                a_k_norm[j].reshape(1, -1), cos_a, sin_a, kind=0, n_lat=n_lat)
            o = _global_attention(q, k, v, n_lat=n_lat, rows=rows)
            xs = _proj_residual(o, a_w_o[j].astype(BF16), xs, modtab, n_lat=n_lat, rows=rows)
        elif kind == 1:
            ones = jnp.ones((1, 128), F32)
            q, k, v = _qkv_proj(xs, modtab, b_w_qkv[j].astype(BF16), ones, ones, cos_b, sin_b,
                                kind=1, n_lat=n_lat)
            o = _window_attention(q, k, v, b_sink[j], n_lat=n_lat, rows=rows)
            xs = _proj_residual(o, b_w_o[j].astype(BF16), xs, modtab, n_lat=n_lat, rows=rows)
        else:
            p = dict(mu=c_mu[j], w_rkv=c_w_rkv[j].astype(BF16), w_o=c_w_o[j].astype(BF16), w0=c_w0[j],
                     w1=c_w1[j].astype(BF16), w2=c_w2[j].astype(BF16), a0=c_a0[j],
                     a1=c_a1[j].astype(BF16), a2=c_a2[j].astype(BF16), g1=c_g1[j].astype(BF16),
                     g2=c_g2[j].astype(BF16), k_k=c_k_k[j].reshape(1, d), k_a=c_k_a[j].reshape(1, d),
                     r_k=c_r_k[j].reshape(2, d), ln_w=c_ln_w[j].reshape(1, d), ln_b=c_ln_b[j].reshape(1, d))
            xs = _rwkv_mixer(xs, modtab, p, n_lat=n_lat, rows=rows)
        xs = _moe(xs, modtab, router_w[i].T, w1_all, w3_all, w2_all, i, n_lat=n_lat, with_ctx=not last)
    return _final_norm(xs, final_norm)
```

```python
import functools
import math

import jax
import jax.numpy as jnp
from jax import lax
from jax.experimental import pallas as pl
from jax.experimental.pallas import tpu as pltpu

F32 = jnp.float32
BF16 = jnp.bfloat16
I32 = jnp.int32

D_MODEL = 1024
GRID_W = 64
Q_BLOCK = 128
ROPE_THETA = 10000.0
NORM_EPS = 1e-6
NEG_INF = -1e30
A_HEADS, A_KV_HEADS, A_HEAD_DIM = 8, 2, 128
B_HEADS, B_KV_HEADS, B_HEAD_DIM = 16, 4, 64
WINDOW = 128
C_HEAD_DIM = 64
C_HEADS = D_MODEL // C_HEAD_DIM
C_GN_EPS = C_HEAD_DIM * 1e-5
N_EXPERTS = 16
EXPERT_FF = 2 * D_MODEL
CAPACITY_FACTOR = 2
N_MIXERS = 3
LOG2E = math.log2(math.e)

ROW_TILE = 256
COMBINE_TILE = 1024
SLOT_TILE = 128
SCAN_CHUNK = 16
VMEM_LIMIT = 56 * 1024 * 1024


def _params(*sem):
    return pltpu.CompilerParams(dimension_semantics=sem, vmem_limit_bytes=VMEM_LIMIT)


def _rms(x):
    return x * lax.rsqrt(jnp.mean(x * x, axis=-1, keepdims=True) + NORM_EPS)


def _normmod(x, mod_ref, shift_row, scale_row):
    return _rms(x) * (1.0 + mod_ref[scale_row:scale_row + 1, :]) + mod_ref[shift_row:shift_row + 1, :]


def _bdot(a, b):
    return jnp.dot(a.astype(BF16), b.astype(BF16), preferred_element_type=F32)


def _dot_nt(a, b):
    return lax.dot_general(a.astype(BF16), b.astype(BF16), (((1,), (1,)), ((), ())),
                           preferred_element_type=F32)


def _sigmoid(x):
    return 1.0 / (1.0 + jnp.exp(-x))


def _mod_spec(n_lat_tiles):
    return pl.BlockSpec((None, None, 6, D_MODEL),
                        lambda b, t: (b, jnp.where(t < n_lat_tiles, 1, 0), 0, 0))


def _cast_kernel(x_ref, o_ref):
    o_ref[...] = x_ref[...].astype(BF16)


def _cast_bf16(w):
    shape = w.shape
    w2 = w.reshape(-1, shape[-1])
    rows, cols = w2.shape
    tr = 512
    out = pl.pallas_call(
        _cast_kernel,
        grid=(rows // tr,),
        in_specs=[pl.BlockSpec((tr, cols), lambda i: (i, 0))],
        out_specs=pl.BlockSpec((tr, cols), lambda i: (i, 0)),
        out_shape=jax.ShapeDtypeStruct((rows, cols), BF16),
        compiler_params=_params("parallel"),
        name="cast_bf16",
    )(w2)
    return out.reshape(shape)


def _mod_kernel(cond_ref, w_ref, b_ref, o_ref):
    c = cond_ref[...]
    a = c * _sigmoid(c)
    o_ref[0] = _bdot(a, w_ref[0]) + b_ref[0]


def _mod_tables(cond, mod_w, mod_b):
    depth, d, n = mod_w.shape
    rows = cond.shape[0]
    return pl.pallas_call(
        _mod_kernel,
        grid=(depth, n // d),
        in_specs=[pl.BlockSpec((rows, d), lambda i, j: (0, 0)),
                  pl.BlockSpec((1, d, d), lambda i, j: (i, 0, j)),
                  pl.BlockSpec((1, 1, d), lambda i, j: (i, 0, j))],
        out_specs=pl.BlockSpec((1, rows, d), lambda i, j: (i, 0, j)),
        out_shape=jax.ShapeDtypeStruct((depth, rows, n), F32),
        compiler_params=_params("arbitrary", "arbitrary"),
        name="mod_tables",
    )(cond, mod_w, mod_b.reshape(depth, 1, n))


def _rope128(x, cos, sin_signed, half):
    if half == 64:
        rot = pltpu.roll(x, 64, axis=1)
    else:
        lane = lax.broadcasted_iota(I32, x.shape, 1)
        rot = jnp.where((lane % 64) < 32, pltpu.roll(x, 96, axis=1), pltpu.roll(x, 32, axis=1))
    return x * cos + rot * sin_signed


def _qkv_kernel(x_ref, mod_ref, w_ref, gq_ref, gk_ref, cos_ref, sin_ref, q_ref, k_ref, v_ref, *, kind):
    h = _normmod(x_ref[...], mod_ref, 0, 1)
    y = _bdot(h, w_ref[...])
    cos, sin = cos_ref[...], sin_ref[...]
    nq = D_MODEL
    nkv = (y.shape[1] - nq) // 2
    half = 64 if kind == 0 else 32
    q_scale = (2 * half) ** -0.5 * LOG2E
    for j in range((nq + nkv) // 128):
        s = y[:, j * 128:(j + 1) * 128]
        if kind == 0:
            gain = gq_ref[...] if j < nq // 128 else gk_ref[...]
            s = _rms(s) * gain
        s = _rope128(s, cos, sin, half)
        if j < nq // 128:
            q_ref[:, j * 128:(j + 1) * 128] = (s * q_scale).astype(BF16)
        elif kind == 0:
            k_ref[:, (j - nq // 128) * 128:(j - nq // 128 + 1) * 128] = s.astype(BF16)
        else:
            jj = (j - nq // 128) * 2
            k_ref[jj] = s[:, :64].astype(BF16)
            k_ref[jj + 1] = s[:, 64:].astype(BF16)
    v = y[:, nq + nkv:].astype(BF16)
    if kind == 0:
        v_ref[...] = v
    else:
        for jj in range(nkv // 64):
            v_ref[jj] = v[:, jj * 64:(jj + 1) * 64]


def _qkv_proj(x, modtab, w, gq, gk, cos, sin, *, kind, n_lat):
    b, t_all, d = x.shape
    tm = ROW_TILE
    nt = t_all // tm
    n = w.shape[1]
    nkv = (n - d) // 2
    if kind == 0:
        kv_shape = jax.ShapeDtypeStruct((b, t_all, nkv), BF16)
        kv_spec = pl.BlockSpec((None, tm, nkv), lambda i, t: (i, t, 0))
    else:
        kv_shape = jax.ShapeDtypeStruct((b, nkv // 64, t_all, 64), BF16)
        kv_spec = pl.BlockSpec((None, nkv // 64, tm, 64), lambda i, t: (i, 0, t, 0))
    return pl.pallas_call(
        functools.partial(_qkv_kernel, kind=kind),
        grid=(b, nt),
        in_specs=[pl.BlockSpec((None, tm, d), lambda i, t: (i, t, 0)),
                  _mod_spec(n_lat // tm),
                  pl.BlockSpec((d, n), lambda i, t: (0, 0)),
                  pl.BlockSpec((1, 128), lambda i, t: (0, 0)),
                  pl.BlockSpec((1, 128), lambda i, t: (0, 0)),
                  pl.BlockSpec((tm, 128), lambda i, t: (t, 0)),
                  pl.BlockSpec((tm, 128), lambda i, t: (t, 0))],
        out_specs=[pl.BlockSpec((None, tm, d), lambda i, t: (i, t, 0)), kv_spec, kv_spec],
        out_shape=[jax.ShapeDtypeStruct((b, t_all, d), BF16), kv_shape, kv_shape],
        compiler_params=_params("parallel", "parallel"),
        name=f"qkv_proj_{kind}",
    )(x, modtab, w, gq, gk, cos, sin)


def _flash_chunk(carry, q, kc, vc):
    m, l, acc = carry
    s = _dot_nt(q, kc)
    m_new = jnp.maximum(m, jnp.max(s, axis=-1, keepdims=True))
    alpha = jnp.exp2(m - m_new)
    p = jnp.exp2(s - m_new)
    l = alpha * l + jnp.sum(p, axis=-1, keepdims=True)
    acc = alpha * acc + jnp.dot(p.astype(BF16), vc, preferred_element_type=F32)
    return m_new, l, acc


def _gattn_kernel(q_ref, k_ref, v_ref, o_ref, *, n_lat, kchunk):
    hd = A_HEAD_DIM
    g = A_HEADS // A_KV_HEADS
    tq = q_ref.shape[0]
    t_all = k_ref.shape[0]
    qs = jnp.concatenate([q_ref[:, i * hd:(i + 1) * hd] for i in range(g)], axis=0)
    init = (jnp.full((g * tq, 1), NEG_INF, F32), jnp.zeros((g * tq, 1), F32), jnp.zeros((g * tq, hd), F32))

    def finish(carry):
        _, l, acc = carry
        o = (acc / l).astype(BF16)
        for i in range(g):
            o_ref[:, i * hd:(i + 1) * hd] = o[i * tq:(i + 1) * tq]

    ctx_carry = _flash_chunk(init, qs, k_ref[n_lat:t_all, :], v_ref[n_lat:t_all, :])
    is_ctx = pl.program_id(2) >= n_lat // tq

    @pl.when(is_ctx)
    def _():
        finish(ctx_carry)

    @pl.when(jnp.logical_not(is_ctx))
    def _():
        def body(c, carry):
            start = pl.multiple_of(c * kchunk, kchunk)
            return _flash_chunk(carry, qs, k_ref[pl.ds(start, kchunk), :], v_ref[pl.ds(start, kchunk), :])
        finish(lax.fori_loop(0, n_lat // kchunk, body, ctx_carry))


def _global_attention(q, k, v, *, n_lat, rows):
    b, t_all, d = q.shape
    hd, g = A_HEAD_DIM, A_HEADS // A_KV_HEADS
    tq = ROW_TILE
    return pl.pallas_call(
        functools.partial(_gattn_kernel, n_lat=n_lat, kchunk=min(2048, n_lat)),
        grid=(b, A_KV_HEADS, rows // tq),
        in_specs=[pl.BlockSpec((None, tq, g * hd), lambda i, h, t: (i, t, h)),
                  pl.BlockSpec((None, t_all, hd), lambda i, h, t: (i, 0, h)),
                  pl.BlockSpec((None, t_all, hd), lambda i, h, t: (i, 0, h))],
        out_specs=pl.BlockSpec((None, tq, g * hd), lambda i, h, t: (i, t, h)),
        out_shape=jax.ShapeDtypeStruct((b, t_all, d), BF16),
        compiler_params=_params("parallel", "parallel", "arbitrary"),
        name="global_attention",
    )(q, k, v)


def _wattn_kernel(q_ref, k_ref, v_ref, sink_ref, bias_ref, o_ref, *, n_lat):
    hd = B_HEAD_DIM
    g = B_HEADS // B_KV_HEADS
    tq = q_ref.shape[0]
    t_all = k_ref.shape[1]
    t = pl.program_id(1)
    n_tiles = n_lat // tq
    is_ctx = t >= n_tiles

    def group(j):
        q = jnp.concatenate([q_ref[:, (j * g + i) * hd:(j * g + i + 1) * hd] for i in range(g)], axis=0)
        sink = jnp.concatenate([jnp.broadcast_to(sink_ref[j, :, i:i + 1], (tq, 1)) for i in range(g)],
                               axis=0) * LOG2E
        kc, vc = k_ref[j, n_lat:t_all, :], v_ref[j, n_lat:t_all, :]
        return q, sink, _dot_nt(q, kc), vc

    def finish(j, o):
        o = o.astype(BF16)
        o_ref[:, j * g * hd:(j + 1) * g * hd] = jnp.concatenate([o[i * tq:(i + 1) * tq] for i in range(g)], axis=1)

    @pl.when(is_ctx)
    def _():
        for j in range(B_KV_HEADS):
            q, sink, s_ctx, vc = group(j)
            m = jnp.maximum(jnp.max(s_ctx, axis=-1, keepdims=True), sink)
            p = jnp.exp2(s_ctx - m)
            l = jnp.sum(p, axis=-1, keepdims=True) + jnp.exp2(sink - m)
            finish(j, jnp.dot(p.astype(BF16), vc, preferred_element_type=F32) / l)

    @pl.when(jnp.logical_not(is_ctx))
    def _():
        wlen = tq + 2 * WINDOW
        ws = jnp.clip(t * tq - WINDOW, 0, n_lat - wlen)
        start = pl.multiple_of(ws, 128)
        bias = bias_ref[jnp.where(t == 0, 0, jnp.where(t == n_tiles - 1, 2, 1))]
        bias = jnp.concatenate([bias] * g, axis=0)
        for j in range(B_KV_HEADS):
            q, sink, _, _ = group(j)
            keys = jnp.concatenate([k_ref[j, pl.ds(start, wlen), :], k_ref[j, n_lat:t_all, :]], axis=0)
            vals = jnp.concatenate([v_ref[j, pl.ds(start, wlen), :], v_ref[j, n_lat:t_all, :]], axis=0)
            s = _dot_nt(q, keys) + bias
            m = jnp.maximum(jnp.max(s, axis=-1, keepdims=True), sink)
            p = jnp.exp2(s - m)
            l = jnp.sum(p, axis=-1, keepdims=True) + jnp.exp2(sink - m)
            finish(j, jnp.dot(p.astype(BF16), vals, preferred_element_type=F32) / l)


def _band_bias(tq, n_ctx):
    row = jnp.arange(tq, dtype=I32)[:, None]
    col = jnp.arange(tq + 2 * WINDOW, dtype=I32)[None, :]
    band = jnp.stack([jnp.where(jnp.abs(col - off - row) <= WINDOW, 0.0, NEG_INF).astype(F32)
                      for off in (0, WINDOW, 2 * WINDOW)])
    return jnp.concatenate([band, jnp.zeros((3, tq, n_ctx), F32)], axis=2)


def _window_attention(q, k, v, sink, *, n_lat, rows):
    b, t_all, d = q.shape
    hd, g = B_HEAD_DIM, B_HEADS // B_KV_HEADS
    tq = Q_BLOCK
    wlen = tq + 2 * WINDOW
    return pl.pallas_call(
        functools.partial(_wattn_kernel, n_lat=n_lat),
        grid=(b, rows // tq),
        in_specs=[pl.BlockSpec((None, tq, d), lambda i, t: (i, t, 0)),
                  pl.BlockSpec((None, B_KV_HEADS, t_all, hd), lambda i, t: (i, 0, 0, 0)),
                  pl.BlockSpec((None, B_KV_HEADS, t_all, hd), lambda i, t: (i, 0, 0, 0)),
                  pl.BlockSpec((B_KV_HEADS, 1, g), lambda i, t: (0, 0, 0)),
                  pl.BlockSpec((3, tq, wlen + t_all - n_lat), lambda i, t: (0, 0, 0))],
        out_specs=pl.BlockSpec((None, tq, d), lambda i, t: (i, t, 0)),
        out_shape=jax.ShapeDtypeStruct((b, t_all, d), BF16),
        compiler_params=_params("parallel", "arbitrary"),
        name="window_attention",
    )(q, k, v, sink.reshape(B_KV_HEADS, 1, g), _band_bias(tq, t_all - n_lat))


def _proj_res_kernel(a_ref, w_ref, x_ref, mod_ref, o_ref):
    o_ref[...] = x_ref[...] + mod_ref[2:3, :] * _bdot(a_ref[...], w_ref[...])


def _proj_residual(a, w, x, modtab, *, n_lat, rows):
    b, t_all, d = x.shape
    tm = ROW_TILE
    return pl.pallas_call(
        _proj_res_kernel,
        grid=(b, rows // tm),
        in_specs=[pl.BlockSpec((None, tm, d), lambda i, t: (i, t, 0)),
                  pl.BlockSpec((d, d), lambda i, t: (0, 0)),
                  pl.BlockSpec((None, tm, d), lambda i, t: (i, t, 0)),
                  _mod_spec(n_lat // tm)],
        out_specs=pl.BlockSpec((None, tm, d), lambda i, t: (i, t, 0)),
        out_shape=jax.ShapeDtypeStruct((b, rows, d), F32),
        compiler_params=_params("parallel", "parallel"),
        name="proj_residual",
    )(a, w, x, modtab)


def _router_kernel(x_ref, mod_ref, rwt_ref, h_ref, aff_ref, afft_ref):
    h = _normmod(x_ref[...], mod_ref, 3, 4)
    h_ref[...] = h.astype(BF16)
    logits_t = lax.dot_general(rwt_ref[...], h, (((1,), (1,)), ((), ())),
                               precision=lax.Precision.HIGHEST, preferred_element_type=F32)
    e = jnp.exp(logits_t - jnp.max(logits_t, axis=0, keepdims=True))
    aff_t = e / jnp.sum(e, axis=0, keepdims=True)
    afft_ref[...] = aff_t
    aff_ref[...] = aff_t.T


def _router(x, modtab, router_w_t, *, n_lat):
    b, rows, d = x.shape
    tm = ROW_TILE
    ne = router_w_t.shape[0]
    return pl.pallas_call(
        _router_kernel,
        grid=(b, rows // tm),
        in_specs=[pl.BlockSpec((None, tm, d), lambda i, t: (i, t, 0)),
                  _mod_spec(n_lat // tm),
                  pl.BlockSpec((ne, d), lambda i, t: (0, 0))],
        out_specs=[pl.BlockSpec((None, tm, d), lambda i, t: (i, t, 0)),
                   pl.BlockSpec((None, tm, ne), lambda i, t: (i, t, 0)),
                   pl.BlockSpec((None, ne, tm), lambda i, t: (i, 0, t))],
        out_shape=[jax.ShapeDtypeStruct((b, rows, d), BF16),
                   jax.ShapeDtypeStruct((b, rows, ne), F32),
                   jax.ShapeDtypeStruct((b, ne, rows), F32)],
        compiler_params=_params("parallel", "parallel"),
        name="router",
    )(x, modtab, router_w_t)


def _lane_prefix(mask_f, tri):
    n = mask_f.shape[1]
    run = jnp.zeros((mask_f.shape[0], 1), F32)
    out = []
    for j in range(n // 128):
        blk = mask_f[:, j * 128:(j + 1) * 128]
        incl = jnp.dot(blk.astype(BF16), tri, preferred_element_type=F32)
        out.append(incl - blk + run)
        run = run + incl[:, 127:128]
    return jnp.concatenate(out, axis=1)


def _select_kernel(afft_ref, slot_ref, slot_tm_ref, *, segments):
    r = lax.broadcasted_iota(I32, (128, 128), 0)
    c = lax.broadcasted_iota(I32, (128, 128), 1)
    tri = (r <= c).astype(BF16)
    for off, n, cap, base in segments:
        bits = lax.bitcast_convert_type(afft_ref[:, off:off + n], I32)
        cap_f = jnp.float32(cap)

        def body(i, thr):
            cand = thr | jnp.left_shift(jnp.int32(1), 30 - i)
            cnt = jnp.sum((bits >= cand).astype(F32), axis=1, keepdims=True)
            return jnp.where(cnt >= cap_f, cand, thr)

        thr = lax.fori_loop(0, 31, body, jnp.zeros((bits.shape[0], 1), I32))
        gt = (bits > thr).astype(F32)
        eq = (bits == thr).astype(F32)
        need = cap_f - jnp.sum(gt, axis=1, keepdims=True)
        sel = gt + eq * (_lane_prefix(eq, tri) < need).astype(F32)
        slot = _lane_prefix(sel, tri).astype(I32) + base
        slot = jnp.where(sel > 0.5, slot, -1)
        slot_ref[:, off:off + n] = slot
        slot_tm_ref[off:off + n, :] = slot.astype(F32).T


def _select(aff_t, segments):
    b, ne, rows = aff_t.shape
    return pl.pallas_call(
        functools.partial(_select_kernel, segments=segments),
        grid=(b,),
        in_specs=[pl.BlockSpec((None, ne, rows), lambda i: (i, 0, 0))],
        out_specs=[pl.BlockSpec((None, ne, rows), lambda i: (i, 0, 0)),
                   pl.BlockSpec((None, rows, ne), lambda i: (i, 0, 0))],
        out_shape=[jax.ShapeDtypeStruct((b, ne, rows), I32), jax.ShapeDtypeStruct((b, rows, ne), F32)],
        compiler_params=_params("parallel"),
        name="expert_select",
    )(aff_t)


def _gather_kernel(slot_ref, h_ref, xs_ref, *, segments):
    for off, n, cap, base in segments:
        slot = slot_ref[:, off:off + n]
        hseg = h_ref[off:off + n, :]
        st = min(SLOT_TILE, cap)
        for j in range(cap // st):
            ids = base + j * st + lax.broadcasted_iota(I32, (st, 1), 0)
            onehot = jnp.where(ids == slot, 1.0, 0.0).astype(BF16)
            rows = jnp.dot(onehot, hseg, preferred_element_type=F32)
            xs_ref[base + j * st:base + (j + 1) * st, :] = rows.astype(BF16)


def _gather(slot, h, segments, n_slots):
    b, ne, rows = slot.shape
    d = h.shape[2]
    return pl.pallas_call(
        functools.partial(_gather_kernel, segments=segments),
        grid=(b, ne),
        in_specs=[pl.BlockSpec((None, None, 1, rows), lambda i, e: (i, e, 0, 0)),
                  pl.BlockSpec((None, rows, d), lambda i, e: (i, 0, 0))],
        out_specs=pl.BlockSpec((None, None, n_slots, d), lambda i, e: (e, i, 0, 0)),
        out_shape=jax.ShapeDtypeStruct((ne, b, n_slots, d), BF16),
        compiler_params=_params("parallel", "arbitrary"),
        name="expert_gather",
    )(slot.reshape(b, ne, 1, rows), h)


def _ffn_kernel(xs_ref, w1_ref, w3_ref, w2_ref, y_ref):
    xs = xs_ref[...]
    a = jnp.dot(xs, w1_ref[...], preferred_element_type=F32)
    g = jnp.dot(xs, w3_ref[...], preferred_element_type=F32)
    hid = (a * _sigmoid(a)) * g
    y_ref[...] = _bdot(hid, w2_ref[...]).astype(BF16)


def _expert_ffn(xs, w1, w3, w2, layer):
    ne, b, n_slots, d = xs.shape
    ff = w1.shape[3]
    return pl.pallas_call(
        _ffn_kernel,
        grid=(ne, b),
        in_specs=[pl.BlockSpec((None, None, n_slots, d), lambda e, i: (e, i, 0, 0)),
                  pl.BlockSpec((None, None, d, ff), lambda e, i: (layer, e, 0, 0)),
                  pl.BlockSpec((None, None, d, ff), lambda e, i: (layer, e, 0, 0)),
                  pl.BlockSpec((None, None, ff, d), lambda e, i: (layer, e, 0, 0))],
        out_specs=pl.BlockSpec((None, None, n_slots, d), lambda e, i: (e, i, 0, 0)),
        out_shape=jax.ShapeDtypeStruct((ne, b, n_slots, d), BF16),
        compiler_params=_params("parallel", "arbitrary"),
        name="expert_ffn",
    )(xs, w1, w3, w2)


def _combine_kernel(slot_ref, aff_ref, y_ref, x_ref, mod_ref, o_ref, acc_ref, *, slot0):
    e = pl.program_id(2)
    ne = pl.num_programs(2)
    d = x_ref.shape[1]
    width = y_ref.shape[0]

    @pl.when(e == 0)
    def _():
        acc_ref[...] = jnp.zeros_like(acc_ref)

    lane_e = lax.broadcasted_iota(I32, aff_ref.shape, 1)
    gate = jnp.sum(jnp.where(lane_e == e, aff_ref[...], 0.0), axis=1, keepdims=True)
    slot = jnp.sum(jnp.where(lane_e == e, slot_ref[...], 0.0), axis=1, keepdims=True).astype(I32)
    ids = slot0 + lax.broadcasted_iota(I32, (1, width), 1)
    onehot = jnp.where(slot == ids, 1.0, 0.0).astype(BF16)
    acc_ref[...] += gate * jnp.dot(onehot, y_ref[...], preferred_element_type=F32)

    @pl.when(e == ne - 1)
    def _():
        o_ref[...] = x_ref[...] + mod_ref[5:6, :] * acc_ref[...]


def _combine(slot_tm, aff, y, x, modtab, *, row0, rows, tm, slot0, width, seg, in_place):
    b, _, d = x.shape
    ne = aff.shape[2]
    t0, s0 = row0 // tm, slot0 // width
    out_rows = x.shape[1] if in_place else rows
    return pl.pallas_call(
        functools.partial(_combine_kernel, slot0=slot0),
        grid=(b, rows // tm, ne),
        in_specs=[pl.BlockSpec((None, tm, ne), lambda i, t, e: (i, t + t0, 0)),
                  pl.BlockSpec((None, tm, ne), lambda i, t, e: (i, t + t0, 0)),
                  pl.BlockSpec((None, None, width, d), lambda i, t, e: (e, i, s0, 0)),
                  pl.BlockSpec((None, tm, d), lambda i, t, e: (i, t + t0, 0)),
                  pl.BlockSpec((None, None, 6, d), lambda i, t, e: (i, seg, 0, 0))],
        out_specs=pl.BlockSpec((None, tm, d), lambda i, t, e: (i, t + (t0 if in_place else 0), 0)),
        out_shape=jax.ShapeDtypeStruct((b, out_rows, d), F32),
        scratch_shapes=[pltpu.VMEM((tm, d), F32)],
        input_output_aliases={3: 0} if in_place else {},
        compiler_params=_params("parallel", "parallel", "arbitrary"),
        name="expert_combine",
    )(slot_tm, aff, y, x, modtab)


def _moe(x, modtab, router_w_t, w1, w3, w2, layer, *, n_lat, with_ctx):
    b, rows, d = x.shape
    ne = N_EXPERTS
    cap_l = CAPACITY_FACTOR * n_lat // ne
    h, aff, aff_t = _router(x, modtab, router_w_t, n_lat=n_lat)
    segments = ((0, n_lat, cap_l, 0),)
    n_slots = cap_l
    if with_ctx:
        n_ctx = rows - n_lat
        cap_c = CAPACITY_FACTOR * n_ctx // ne
        segments += ((n_lat, n_ctx, cap_c, cap_l),)
        n_slots += cap_c
    slot, slot_tm = _select(aff_t, segments)
    xs = _gather(slot, h, segments, n_slots)
    y = _expert_ffn(xs, w1, w3, w2, layer)
    out = _combine(slot_tm, aff, y, x, modtab, row0=0, rows=n_lat, tm=min(COMBINE_TILE, n_lat), slot0=0,
                   width=cap_l, seg=1, in_place=with_ctx)
    if with_ctx:
        out = _combine(slot_tm, aff, y, out, modtab, row0=n_lat, rows=n_ctx, tm=n_ctx, slot0=cap_l,
                       width=cap_c, seg=0, in_place=True)
    return out


def _segsum64(x, bd):
    hi = x.astype(BF16)
    lo = (x - hi.astype(F32)).astype(BF16)
    out = []
    for j in range(x.shape[1] // 256):
        sl = slice(j * 256, (j + 1) * 256)
        out.append(jnp.dot(hi[:, sl], bd, preferred_element_type=F32)
                   + jnp.dot(lo[:, sl], bd, preferred_element_type=F32))
    return jnp.concatenate(out, axis=1)


def _block_diag_ones():
    r = lax.broadcasted_iota(I32, (256, 256), 0)
    c = lax.broadcasted_iota(I32, (256, 256), 1)
    return (r // 64 == c // 64).astype(BF16)


def _softplus(x):
    return jnp.maximum(x, 0.0) + jnp.log(1.0 + jnp.exp(-jnp.abs(x)))


def _rwkv_feat_kernel(x_ref, xp_ref, xn_ref, mod_ref, mu_ref, wrkv_ref, w0_ref, w1_ref, w2_ref,
                      a0_ref, a1_ref, a2_ref, g1_ref, g2_ref, kk_ref, ka_ref,
                      r_out, v_out, nkk_out, g_out, w_out, k_out, b_out, *, n_lat_tiles):
    t = pl.program_id(1)
    nt = pl.num_programs(1)
    tm = x_ref.shape[0]
    h = _normmod(x_ref[...], mod_ref, 0, 1)
    hp = _normmod(xp_ref[7:8, :], mod_ref, 0, 1)
    hn = _normmod(xn_ref[0:1, :], mod_ref, 0, 1)
    has_left = jnp.logical_and(t != 0, t != n_lat_tiles)
    has_right = jnp.logical_and(t != n_lat_tiles - 1, t != nt - 1)
    hp = jnp.where(has_left, hp, 0.0)
    hn = jnp.where(has_right, hn, 0.0)
    row = lax.broadcasted_iota(I32, h.shape, 0)
    left = jnp.where(row == 0, hp, pltpu.roll(h, 1, axis=0))
    right = jnp.where(row == tm - 1, hn, pltpu.roll(h, tm - 1, axis=0))
    xx = 0.5 * (left + right) - h
    xr, xw, xk, xv, xa, xg = (h + xx * mu_ref[i:i + 1, :] for i in range(6))
    r = _bdot(xr, wrkv_ref[0])
    k = _bdot(xk, wrkv_ref[1])
    v = _bdot(xv, wrkv_ref[2])
    g = _bdot(_sigmoid(_bdot(xg, g1_ref[...])), g2_ref[...])
    bd = _block_diag_ones()
    kk = k * kk_ref[...]
    kk = kk * lax.rsqrt(jnp.maximum(_segsum64(kk * kk, bd), 1e-24))
    r_out[...] = r
    v_out[...] = v
    nkk_out[...] = -kk
    g_out[...] = g
    for d in range(2):
        w_lora = _bdot(jnp.tanh(_bdot(xw, w1_ref[d])), w2_ref[d])
        log_w = -_softplus(-(w0_ref[d:d + 1, :] + w_lora)) - 0.5
        w_out[d] = jnp.exp(-jnp.exp(log_w))
        a = _sigmoid(a0_ref[d:d + 1, :] + _bdot(_bdot(xa, a1_ref[d]), a2_ref[d]))
        k_out[d] = k * (1.0 + (a - 1.0) * ka_ref[...])
        b_out[d] = kk * a


def _rwkv_features(x, modtab, p, *, n_lat):
    b, t_all, d = x.shape
    tm = ROW_TILE
    nt = t_all // tm
    tb = tm // 8
    full = lambda shape: pl.BlockSpec(shape, lambda i, t: (0,) * len(shape))
    tok = pl.BlockSpec((None, tm, d), lambda i, t: (i, t, 0))
    tok2 = pl.BlockSpec((2, None, tm, d), lambda i, t: (0, i, t, 0))
    one = jax.ShapeDtypeStruct((b, t_all, d), F32)
    two = jax.ShapeDtypeStruct((2, b, t_all, d), F32)
    return pl.pallas_call(
        functools.partial(_rwkv_feat_kernel, n_lat_tiles=n_lat // tm),
        grid=(b, nt),
        in_specs=[tok,
                  pl.BlockSpec((None, 8, d), lambda i, t: (i, jnp.maximum(t * tb - 1, 0), 0)),
                  pl.BlockSpec((None, 8, d), lambda i, t: (i, jnp.minimum((t + 1) * tb, nt * tb - 1), 0)),
                  _mod_spec(n_lat // tm),
                  full((6, d)), full((3, d, d)), full((2, d)), full(p["w1"].shape), full(p["w2"].shape),
                  full((2, d)), full(p["a1"].shape), full(p["a2"].shape), full(p["g1"].shape),
                  full(p["g2"].shape), full((1, d)), full((1, d))],
        out_specs=[tok, tok, tok, tok, tok2, tok2, tok2],
        out_shape=[one, one, one, one, two, two, two],
        compiler_params=_params("parallel", "parallel"),
        name="rwkv_features",
    )(x, x, x, modtab, p["mu"], p["w_rkv"], p["w0"], p["w1"], p["w2"], p["a0"], p["a1"], p["a2"],
      p["g1"], p["g2"], p["k_k"], p["k_a"])


def _scan_kernel(r_ref, w_ref, k_ref, v_ref, a_ref, b_ref, o_ref, s_ref, wr_ref, *, reverse):
    n = s_ref.shape[0]
    steps = r_ref.shape[0]

    @pl.when(pl.program_id(0) == 0)
    def _():
        s_ref[...] = jnp.zeros_like(s_ref)

    def step(i, carry):
        j = steps - 1 - i if reverse else i
        r = r_ref[j]
        wr_ref[...] = w_ref[j] * r
        br = jnp.sum(b_ref[j] * r, axis=0, keepdims=True)
        kr = jnp.sum(k_ref[j] * r, axis=0, keepdims=True)
        acc = [jnp.zeros(s_ref.shape[1:], F32) for _ in range(4)]
        for kk in range(n):
            s = s_ref[kk]
            acc[kk % 2] = acc[kk % 2] + s * a_ref[j, kk:kk + 1, :]
            acc[2 + kk % 2] = acc[2 + kk % 2] + s * wr_ref[kk:kk + 1, :]
        sa = acc[0] + acc[1]
        v = v_ref[j]
        o_ref[j] = acc[2] + acc[3] + sa * br + v * kr
        for kk in range(n):
            s_ref[kk] = (s_ref[kk] * w_ref[j, kk:kk + 1, :] + sa * b_ref[j, kk:kk + 1, :]
                         + v * k_ref[j, kk:kk + 1, :])
        return carry

    lax.fori_loop(0, steps, step, 0)


def _wkv_scan(r, w, k, v, a, b, *, n_lat, reverse):
    t_all, n, chains = r.shape
    tc = SCAN_CHUNK
    nlc, nch = n_lat // tc, t_all // tc
    ncc = nch - nlc
    if reverse:
        idx = lambda c: (jnp.where(c < ncc, nch - 1 - c, nlc - 1 - (c - ncc)), 0, 0)
    else:
        idx = lambda c: (jnp.where(c < ncc, nlc + c, c - ncc), 0, 0)
    spec = pl.BlockSpec((tc, n, chains), idx)
    return pl.pallas_call(
        functools.partial(_scan_kernel, reverse=reverse),
        grid=(nch,),
        in_specs=[spec] * 6,
        out_specs=spec,
        out_shape=jax.ShapeDtypeStruct((t_all, n, chains), F32),
        scratch_shapes=[pltpu.VMEM((n, n, chains), F32), pltpu.VMEM((n, chains), F32)],
        compiler_params=_params("arbitrary"),
        name="wkv_scan_bwd" if reverse else "wkv_scan_fwd",
    )(r, w, k, v, a, b)


def _rwkv_out_kernel(o_ref, r_ref, k_ref, v_ref, g_ref, rk_ref, lnw_ref, lnb_ref, wo_ref, x_ref, mod_ref,
                     out_ref):
    bd = _block_diag_ones()
    inv_n = 1.0 / C_HEAD_DIM
    o = o_ref[...]
    o = o - _segsum64(o, bd) * inv_n
    o = o * lax.rsqrt(_segsum64(o * o, bd) * inv_n + C_GN_EPS)
    o = o * lnw_ref[...] + lnb_ref[...]
    r = r_ref[...]
    bonus = _segsum64(r * k_ref[0] * rk_ref[0:1, :] + r * k_ref[1] * rk_ref[1:2, :], bd) * v_ref[...]
    y = _bdot((o + bonus) * g_ref[...], wo_ref[...])
    out_ref[...] = x_ref[...] + mod_ref[2:3, :] * y


def _rwkv_readout(o, r, k2, v, g, p, x, modtab, *, n_lat, rows):
    b, t_all, d = x.shape
    tm = ROW_TILE
    tok = pl.BlockSpec((None, tm, d), lambda i, t: (i, t, 0))
    full = lambda shape: pl.BlockSpec(shape, lambda i, t: (0,) * len(shape))
    return pl.pallas_call(
        _rwkv_out_kernel,
        grid=(b, rows // tm),
        in_specs=[tok, tok, pl.BlockSpec((2, None, tm, d), lambda i, t: (0, i, t, 0)), tok, tok,
                  full((2, d)), full((1, d)), full((1, d)), full((d, d)), tok, _mod_spec(n_lat // tm)],
        out_specs=tok,
        out_shape=jax.ShapeDtypeStruct((b, rows, d), F32),
        compiler_params=_params("parallel", "parallel"),
        name="rwkv_readout",
    )(o, r, k2, v, g, p["r_k"], p["ln_w"], p["ln_b"], p["w_o"], x, modtab)


def _to_scan_kernel(x_ref, o_ref, y_ref):
    nb, tt, d = x_ref.shape
    n, chains = o_ref.shape[1:]
    for b in range(nb):
        for p in range(d // 128):
            row = (b * (d // 128) + p) * 128
            y_ref[row:row + 128, :] = x_ref[b, :, p * 128:(p + 1) * 128].T
    for k in range(n):
        o_ref[:, k, :] = y_ref[pl.ds(k, chains, stride=n), :].T


def _to_scan_layout(a, d=None):
    b, t, dm = a.shape[-3:]
    tt = 128
    chains = b * dm // C_HEAD_DIM
    if d is None:
        spec = pl.BlockSpec((b, tt, dm), lambda i: (0, i, 0))
    else:
        spec = pl.BlockSpec((None, b, tt, dm), lambda i: (d, 0, i, 0))
    return pl.pallas_call(
        _to_scan_kernel,
        grid=(t // tt,),
        in_specs=[spec],
        out_specs=pl.BlockSpec((tt, C_HEAD_DIM, chains), lambda i: (i, 0, 0)),
        out_shape=jax.ShapeDtypeStruct((t, C_HEAD_DIM, chains), F32),
        scratch_shapes=[pltpu.VMEM((b * dm, tt), F32)],
        compiler_params=_params("parallel"),
        name="to_scan_layout",
    )(a)


def _from_scan_kernel(a_ref, b_ref, o_ref, y_ref):
    nb, tt, d = o_ref.shape
    n, chains = a_ref.shape[1:]
    for k in range(n):
        y_ref[pl.ds(k, chains, stride=n), :] = (a_ref[:, k, :] + b_ref[:, k, :]).T
    for b in range(nb):
        for p in range(d // 128):
            row = (b * (d // 128) + p) * 128
            o_ref[b, :, p * 128:(p + 1) * 128] = y_ref[row:row + 128, :].T


def _from_scan_layout(o_f, o_b, b):
    t, n, chains = o_f.shape
    tt = 128
    dm = n * chains // b
    spec = pl.BlockSpec((tt, n, chains), lambda i: (i, 0, 0))
    return pl.pallas_call(
        _from_scan_kernel,
        grid=(t // tt,),
        in_specs=[spec, spec],
        out_specs=pl.BlockSpec((b, tt, dm), lambda i: (0, i, 0)),
        out_shape=jax.ShapeDtypeStruct((b, t, dm), F32),
        scratch_shapes=[pltpu.VMEM((b * dm, tt), F32)],
        compiler_params=_params("parallel"),
        name="from_scan_layout",
    )(o_f, o_b)


def _rwkv_mixer(x, modtab, p, *, n_lat, rows):
    b = x.shape[0]
    r, v, nkk, g, w2, k2, b2 = _rwkv_features(x, modtab, p, n_lat=n_lat)
    rs, vs, as_ = _to_scan_layout(r), _to_scan_layout(v), _to_scan_layout(nkk)
    o_f, o_b = (_wkv_scan(rs, _to_scan_layout(w2, d), _to_scan_layout(k2, d), vs, as_, _to_scan_layout(b2, d),
                          n_lat=n_lat, reverse=(d == 1)) for d in range(2))
    return _rwkv_readout(_from_scan_layout(o_f, o_b, b), r, k2, v, g, p, x, modtab, n_lat=n_lat, rows=rows)


def _final_kernel(x_ref, w_ref, o_ref):
    o_ref[...] = _rms(x_ref[...]) * w_ref[...]


def _final_norm(x, w):
    b, t, d = x.shape
    tm = ROW_TILE
    return pl.pallas_call(
        _final_kernel,
        grid=(b, t // tm),
        in_specs=[pl.BlockSpec((None, tm, d), lambda i, t: (i, t, 0)),
                  pl.BlockSpec((1, d), lambda i, t: (0, 0))],
        out_specs=pl.BlockSpec((None, tm, d), lambda i, t: (i, t, 0)),
        out_shape=jax.ShapeDtypeStruct((b, t, d), F32),
        compiler_params=_params("parallel", "parallel"),
        name="final_norm",
    )(x, w.reshape(1, d))


def _rope_tables(n_lat, n_ctx, head_dim):
    rows = jnp.repeat(jnp.arange(n_lat // GRID_W, dtype=I32), GRID_W).astype(F32)
    cols = jnp.tile(jnp.arange(GRID_W, dtype=I32), n_lat // GRID_W).astype(F32)
    n_freq = head_dim // 4
    inv_freq = ROPE_THETA ** (-jnp.arange(n_freq, dtype=F32) / n_freq)
    ang = jnp.concatenate([rows[:, None] * inv_freq, cols[:, None] * inv_freq], axis=-1)
    cos, sin = jnp.cos(ang), jnp.sin(ang)
    reps = 128 // head_dim
    cos = jnp.tile(jnp.concatenate([cos, cos], axis=-1), (1, reps))
    sin = jnp.tile(jnp.concatenate([-sin, sin], axis=-1), (1, reps))
    cos = jnp.concatenate([cos, jnp.ones((n_ctx, 128), F32)], axis=0)
    sin = jnp.concatenate([sin, jnp.zeros((n_ctx, 128), F32)], axis=0)
    return cos, sin


def kernel(x, c, ctx, c_ctx, mod_w, mod_b, a_w_qkv, a_w_o, a_q_norm, a_k_norm, b_w_qkv, b_w_o, b_sink,
           c_mu, c_w_rkv, c_w_o, c_w0, c_w1, c_w2, c_a0, c_a1, c_a2, c_g1, c_g2, c_k_k, c_k_a, c_r_k,
           c_ln_w, c_ln_b, router_w, ffn_w1, ffn_w3, ffn_w2, final_norm):
    b, n_lat, d = x.shape
    n_ctx = ctx.shape[1]
    t_all = n_lat + n_ctx
    depth = mod_w.shape[0]
    assert d == D_MODEL and n_ctx % ROW_TILE == 0 and n_lat % min(COMBINE_TILE, n_lat) == 0

    cond_rows = -(-(b + 1) // 8) * 8
    cond = jnp.zeros((cond_rows, d), F32).at[:b].set(c).at[b].set(c_ctx)
    mods = _mod_tables(cond, mod_w, mod_b).reshape(depth, cond_rows, 6, d)
    cos_a, sin_a = _rope_tables(n_lat, n_ctx, A_HEAD_DIM)
    cos_b, sin_b = _rope_tables(n_lat, n_ctx, B_HEAD_DIM)
    w1_all, w3_all, w2_all = _cast_bf16(ffn_w1), _cast_bf16(ffn_w3), _cast_bf16(ffn_w2)

    xs = jnp.concatenate([x, ctx], axis=1)
    for i in range(depth):
        last = i == depth - 1
        rows = n_lat if last else t_all
        kind, j = i % N_MIXERS, i // N_MIXERS
        modtab = jnp.stack([jnp.broadcast_to(mods[i, b], (b, 6, d)), mods[i, :b]], axis=1)
        if kind == 0:
            q, k, v = _qkv_proj(xs, modtab, a_w_qkv[j].astype(BF16), a_q_norm[j].reshape(1, -1),
                                a_k_norm[j].reshape(1, -1), cos_a, sin_a, kind=0, n_lat=n_lat)
            o = _global_attention(q, k, v, n_lat=n_lat, rows=rows)
            xs = _proj_residual(o, a_w_o[j].astype(BF16), xs, modtab, n_lat=n_lat, rows=rows)
        elif kind == 1:
            ones = jnp.ones((1, 128), F32)
            q, k, v = _qkv_proj(xs, modtab, b_w_qkv[j].astype(BF16), ones, ones, cos_b, sin_b,
                                kind=1, n_lat=n_lat)
            o = _window_attention(q, k, v, b_sink[j], n_lat=n_lat, rows=rows)
            xs = _proj_residual(o, b_w_o[j].astype(BF16), xs, modtab, n_lat=n_lat, rows=rows)
        else:
            p = dict(mu=c_mu[j], w_rkv=c_w_rkv[j].astype(BF16), w_o=c_w_o[j].astype(BF16), w0=c_w0[j],
                     w1=c_w1[j].astype(BF16), w2=c_w2[j].astype(BF16), a0=c_a0[j],
                     a1=c_a1[j].astype(BF16), a2=c_a2[j].astype(BF16), g1=c_g1[j].astype(BF16),
                     g2=c_g2[j].astype(BF16), k_k=c_k_k[j].reshape(1, d), k_a=c_k_a[j].reshape(1, d),
                     r_k=c_r_k[j].reshape(2, d), ln_w=c_ln_w[j].reshape(1, d), ln_b=c_ln_b[j].reshape(1, d))
            xs = _rwkv_mixer(xs, modtab, p, n_lat=n_lat, rows=rows)
        xs = _moe(xs, modtab, router_w[i].T, w1_all, w3_all, w2_all, i, n_lat=n_lat, with_ctx=not last)
    return _final_norm(xs, final_norm)
```

```python
import functools
import math

import jax
import jax.numpy as jnp
from jax import lax
from jax.experimental import pallas as pl
from jax.experimental.pallas import tpu as pltpu

F32 = jnp.float32
BF16 = jnp.bfloat16
I32 = jnp.int32

D_MODEL = 1024
GRID_W = 64
Q_BLOCK = 128
ROPE_THETA = 10000.0
NORM_EPS = 1e-6
NEG_INF = -1e30
A_HEADS, A_KV_HEADS, A_HEAD_DIM = 8, 2, 128
B_HEADS, B_KV_HEADS, B_HEAD_DIM = 16, 4, 64
WINDOW = 128
C_HEAD_DIM = 64
C_HEADS = D_MODEL // C_HEAD_DIM
C_GN_EPS = C_HEAD_DIM * 1e-5
N_EXPERTS = 16
EXPERT_FF = 2 * D_MODEL
CAPACITY_FACTOR = 2
N_MIXERS = 3
LOG2E = math.log2(math.e)

ROW_TILE = 256
COMBINE_TILE = 1024
COMBINE_EXPERTS = 4
SLOT_TILE = 128
SCAN_CHUNK = 16
VMEM_LIMIT = 56 * 1024 * 1024


def _params(*sem):
    return pltpu.CompilerParams(dimension_semantics=sem, vmem_limit_bytes=VMEM_LIMIT)


def _rms(x):
    return x * lax.rsqrt(jnp.mean(x * x, axis=-1, keepdims=True) + NORM_EPS)


def _normmod(x, mod_ref, shift_row, scale_row):
    return _rms(x) * (1.0 + mod_ref[scale_row:scale_row + 1, :]) + mod_ref[shift_row:shift_row + 1, :]


def _bdot(a, b):
    return jnp.dot(a.astype(BF16), b.astype(BF16), preferred_element_type=F32)


def _dot_nt(a, b):
    return lax.dot_general(a.astype(BF16), b.astype(BF16), (((1,), (1,)), ((), ())),
                           preferred_element_type=F32)


def _sigmoid(x):
    return 1.0 / (1.0 + jnp.exp(-x))


def _mod_spec(n_lat_tiles):
    return pl.BlockSpec((None, None, 6, D_MODEL),
                        lambda b, t: (b, jnp.where(t < n_lat_tiles, 1, 0), 0, 0))


def _cast_kernel(x_ref, o_ref):
    o_ref[...] = x_ref[...].astype(BF16)


def _cast_bf16(w):
    shape = w.shape
    w2 = w.reshape(-1, shape[-1])
    rows, cols = w2.shape
    tr = 512
    out = pl.pallas_call(
        _cast_kernel,
        grid=(rows // tr,),
        in_specs=[pl.BlockSpec((tr, cols), lambda i: (i, 0))],
        out_specs=pl.BlockSpec((tr, cols), lambda i: (i, 0)),
        out_shape=jax.ShapeDtypeStruct((rows, cols), BF16),
        compiler_params=_params("parallel"),
        name="cast_bf16",
    )(w2)
    return out.reshape(shape)


def _mod_kernel(cond_ref, w_ref, b_ref, o_ref):
    c = cond_ref[...]
    a = c * _sigmoid(c)
    o_ref[0] = _bdot(a, w_ref[0]) + b_ref[0]


def _mod_tables(cond, mod_w, mod_b):
    depth, d, n = mod_w.shape
    rows = cond.shape[0]
    return pl.pallas_call(
        _mod_kernel,
        grid=(depth, n // d),
        in_specs=[pl.BlockSpec((rows, d), lambda i, j: (0, 0)),
                  pl.BlockSpec((1, d, d), lambda i, j: (i, 0, j)),
                  pl.BlockSpec((1, 1, d), lambda i, j: (i, 0, j))],
        out_specs=pl.BlockSpec((1, rows, d), lambda i, j: (i, 0, j)),
        out_shape=jax.ShapeDtypeStruct((depth, rows, n), F32),
        compiler_params=_params("arbitrary", "arbitrary"),
        name="mod_tables",
    )(cond, mod_w, mod_b.reshape(depth, 1, n))


def _rope128(x, cos, sin_signed, half):
    if half == 64:
        rot = pltpu.roll(x, 64, axis=1)
    else:
        lane = lax.broadcasted_iota(I32, x.shape, 1)
        rot = jnp.where((lane % 64) < 32, pltpu.roll(x, 96, axis=1), pltpu.roll(x, 32, axis=1))
    return x * cos + rot * sin_signed


def _qkv_kernel(x_ref, mod_ref, w_ref, gq_ref, gk_ref, cos_ref, sin_ref, q_ref, k_ref, v_ref, *, kind):
    h = _normmod(x_ref[...], mod_ref, 0, 1)
    y = _bdot(h, w_ref[...])
    cos, sin = cos_ref[...], sin_ref[...]
    nq = D_MODEL
    nkv = (y.shape[1] - nq) // 2
    half = 64 if kind == 0 else 32
    q_scale = (2 * half) ** -0.5 * LOG2E
    for j in range((nq + nkv) // 128):
        s = y[:, j * 128:(j + 1) * 128]
        if kind == 0:
            gain = gq_ref[...] if j < nq // 128 else gk_ref[...]
            s = _rms(s) * gain
        s = _rope128(s, cos, sin, half)
        if j < nq // 128:
            q_ref[:, j * 128:(j + 1) * 128] = (s * q_scale).astype(BF16)
        elif kind == 0:
            k_ref[:, (j - nq // 128) * 128:(j - nq // 128 + 1) * 128] = s.astype(BF16)
        else:
            jj = (j - nq // 128) * 2
            k_ref[jj] = s[:, :64].astype(BF16)
            k_ref[jj + 1] = s[:, 64:].astype(BF16)
    v = y[:, nq + nkv:].astype(BF16)
    if kind == 0:
        v_ref[...] = v
    else:
        for jj in range(nkv // 64):
            v_ref[jj] = v[:, jj * 64:(jj + 1) * 64]


def _qkv_proj(x, modtab, w, gq, gk, cos, sin, *, kind, n_lat):
    b, t_all, d = x.shape
    tm = ROW_TILE
    nt = t_all // tm
    n = w.shape[1]
    nkv = (n - d) // 2
    if kind == 0:
        kv_shape = jax.ShapeDtypeStruct((b, t_all, nkv), BF16)
        kv_spec = pl.BlockSpec((None, tm, nkv), lambda i, t: (i, t, 0))
    else:
        kv_shape = jax.ShapeDtypeStruct((b, nkv // 64, t_all, 64), BF16)
        kv_spec = pl.BlockSpec((None, nkv // 64, tm, 64), lambda i, t: (i, 0, t, 0))
    return pl.pallas_call(
        functools.partial(_qkv_kernel, kind=kind),
        grid=(b, nt),
        in_specs=[pl.BlockSpec((None, tm, d), lambda i, t: (i, t, 0)),
                  _mod_spec(n_lat // tm),
                  pl.BlockSpec((d, n), lambda i, t: (0, 0)),
                  pl.BlockSpec((1, 128), lambda i, t: (0, 0)),
                  pl.BlockSpec((1, 128), lambda i, t: (0, 0)),
                  pl.BlockSpec((tm, 128), lambda i, t: (t, 0)),
                  pl.BlockSpec((tm, 128), lambda i, t: (t, 0))],
        out_specs=[pl.BlockSpec((None, tm, d), lambda i, t: (i, t, 0)), kv_spec, kv_spec],
        out_shape=[jax.ShapeDtypeStruct((b, t_all, d), BF16), kv_shape, kv_shape],
        compiler_params=_params("parallel", "parallel"),
        name=f"qkv_proj_{kind}",
    )(x, modtab, w, gq, gk, cos, sin)


def _flash_chunk(carry, q, kc, vc):
    m, l, acc = carry
    s = _dot_nt(q, kc)
    m_new = jnp.maximum(m, jnp.max(s, axis=-1, keepdims=True))
    alpha = jnp.exp2(m - m_new)
    p = jnp.exp2(s - m_new)
    l = alpha * l + jnp.sum(p, axis=-1, keepdims=True)
    acc = alpha * acc + jnp.dot(p.astype(BF16), vc, preferred_element_type=F32)
    return m_new, l, acc


def _gattn_kernel(q_ref, k_ref, v_ref, o_ref, *, n_lat, kchunk):
    hd = A_HEAD_DIM
    g = A_HEADS // A_KV_HEADS
    tq = q_ref.shape[0]
    t_all = k_ref.shape[0]
    qs = jnp.concatenate([q_ref[:, i * hd:(i + 1) * hd] for i in range(g)], axis=0)
    init = (jnp.full((g * tq, 1), NEG_INF, F32), jnp.zeros((g * tq, 1), F32), jnp.zeros((g * tq, hd), F32))

    def finish(carry):
        _, l, acc = carry
        o = (acc / l).astype(BF16)
        for i in range(g):
            o_ref[:, i * hd:(i + 1) * hd] = o[i * tq:(i + 1) * tq]

    ctx_carry = _flash_chunk(init, qs, k_ref[n_lat:t_all, :], v_ref[n_lat:t_all, :])
    is_ctx = pl.program_id(2) >= n_lat // tq

    @pl.when(is_ctx)
    def _():
        finish(ctx_carry)

    @pl.when(jnp.logical_not(is_ctx))
    def _():
        def body(c, carry):
            start = pl.multiple_of(c * kchunk, kchunk)
            return _flash_chunk(carry, qs, k_ref[pl.ds(start, kchunk), :], v_ref[pl.ds(start, kchunk), :])
        finish(lax.fori_loop(0, n_lat // kchunk, body, ctx_carry))


def _global_attention(q, k, v, *, n_lat, rows):
    b, t_all, d = q.shape
    hd, g = A_HEAD_DIM, A_HEADS // A_KV_HEADS
    tq = ROW_TILE
    return pl.pallas_call(
        functools.partial(_gattn_kernel, n_lat=n_lat, kchunk=min(2048, n_lat)),
        grid=(b, A_KV_HEADS, rows // tq),
        in_specs=[pl.BlockSpec((None, tq, g * hd), lambda i, h, t: (i, t, h)),
                  pl.BlockSpec((None, t_all, hd), lambda i, h, t: (i, 0, h)),
                  pl.BlockSpec((None, t_all, hd), lambda i, h, t: (i, 0, h))],
        out_specs=pl.BlockSpec((None, tq, g * hd), lambda i, h, t: (i, t, h)),
        out_shape=jax.ShapeDtypeStruct((b, t_all, d), BF16),
        compiler_params=_params("parallel", "parallel", "arbitrary"),
        name="global_attention",
    )(q, k, v)


def _wattn_kernel(q_ref, k_ref, v_ref, sink_ref, bias_ref, o_ref, *, n_lat):
    hd = B_HEAD_DIM
    g = B_HEADS // B_KV_HEADS
    tq = q_ref.shape[0]
    t_all = k_ref.shape[1]
    t = pl.program_id(1)
    n_tiles = n_lat // tq
    is_ctx = t >= n_tiles

    def group(j):
        q = jnp.concatenate([q_ref[:, (j * g + i) * hd:(j * g + i + 1) * hd] for i in range(g)], axis=0)
        sink = jnp.concatenate([jnp.broadcast_to(sink_ref[j, :, i:i + 1], (tq, 1)) for i in range(g)],
                               axis=0) * LOG2E
        kc, vc = k_ref[j, n_lat:t_all, :], v_ref[j, n_lat:t_all, :]
        return q, sink, _dot_nt(q, kc), vc

    def finish(j, o):
        o = o.astype(BF16)
        o_ref[:, j * g * hd:(j + 1) * g * hd] = jnp.concatenate([o[i * tq:(i + 1) * tq] for i in range(g)], axis=1)

    @pl.when(is_ctx)
    def _():
        for j in range(B_KV_HEADS):
            q, sink, s_ctx, vc = group(j)
            m = jnp.maximum(jnp.max(s_ctx, axis=-1, keepdims=True), sink)
            p = jnp.exp2(s_ctx - m)
            l = jnp.sum(p, axis=-1, keepdims=True) + jnp.exp2(sink - m)
            finish(j, jnp.dot(p.astype(BF16), vc, preferred_element_type=F32) / l)

    @pl.when(jnp.logical_not(is_ctx))
    def _():
        wlen = tq + 2 * WINDOW
        ws = jnp.clip(t * tq - WINDOW, 0, n_lat - wlen)
        start = pl.multiple_of(ws, 128)
        bias = bias_ref[jnp.where(t == 0, 0, jnp.where(t == n_tiles - 1, 2, 1))]
        bias = jnp.concatenate([bias] * g, axis=0)
        for j in range(B_KV_HEADS):
            q, sink, _, _ = group(j)
            keys = jnp.concatenate([k_ref[j, pl.ds(start, wlen), :], k_ref[j, n_lat:t_all, :]], axis=0)
            vals = jnp.concatenate([v_ref[j, pl.ds(start, wlen), :], v_ref[j, n_lat:t_all, :]], axis=0)
            s = _dot_nt(q, keys) + bias
            m = jnp.maximum(jnp.max(s, axis=-1, keepdims=True), sink)
            p = jnp.exp2(s - m)
            l = jnp.sum(p, axis=-1, keepdims=True) + jnp.exp2(sink - m)
            finish(j, jnp.dot(p.astype(BF16), vals, preferred_element_type=F32) / l)


def _band_bias(tq, n_ctx):
    row = jnp.arange(tq, dtype=I32)[:, None]
    col = jnp.arange(tq + 2 * WINDOW, dtype=I32)[None, :]
    band = jnp.stack([jnp.where(jnp.abs(col - off - row) <= WINDOW, 0.0, NEG_INF).astype(F32)
                      for off in (0, WINDOW, 2 * WINDOW)])
    return jnp.concatenate([band, jnp.zeros((3, tq, n_ctx), F32)], axis=2)


def _window_attention(q, k, v, sink, *, n_lat, rows):
    b, t_all, d = q.shape
    hd, g = B_HEAD_DIM, B_HEADS // B_KV_HEADS
    tq = Q_BLOCK
    wlen = tq + 2 * WINDOW
    return pl.pallas_call(
        functools.partial(_wattn_kernel, n_lat=n_lat),
        grid=(b, rows // tq),
        in_specs=[pl.BlockSpec((None, tq, d), lambda i, t: (i, t, 0)),
                  pl.BlockSpec((None, B_KV_HEADS, t_all, hd), lambda i, t: (i, 0, 0, 0)),
                  pl.BlockSpec((None, B_KV_HEADS, t_all, hd), lambda i, t: (i, 0, 0, 0)),
                  pl.BlockSpec((B_KV_HEADS, 1, g), lambda i, t: (0, 0, 0)),
                  pl.BlockSpec((3, tq, wlen + t_all - n_lat), lambda i, t: (0, 0, 0))],
        out_specs=pl.BlockSpec((None, tq, d), lambda i, t: (i, t, 0)),
        out_shape=jax.ShapeDtypeStruct((b, t_all, d), BF16),
        compiler_params=_params("parallel", "arbitrary"),
        name="window_attention",
    )(q, k, v, sink.reshape(B_KV_HEADS, 1, g), _band_bias(tq, t_all - n_lat))


def _proj_res_kernel(a_ref, w_ref, x_ref, mod_ref, o_ref):
    o_ref[...] = x_ref[...] + mod_ref[2:3, :] * _bdot(a_ref[...], w_ref[...])


def _proj_residual(a, w, x, modtab, *, n_lat, rows):
    b, t_all, d = x.shape
    tm = ROW_TILE
    return pl.pallas_call(
        _proj_res_kernel,
        grid=(b, rows // tm),
        in_specs=[pl.BlockSpec((None, tm, d), lambda i, t: (i, t, 0)),
                  pl.BlockSpec((d, d), lambda i, t: (0, 0)),
                  pl.BlockSpec((None, tm, d), lambda i, t: (i, t, 0)),
                  _mod_spec(n_lat // tm)],
        out_specs=pl.BlockSpec((None, tm, d), lambda i, t: (i, t, 0)),
        out_shape=jax.ShapeDtypeStruct((b, rows, d), F32),
        compiler_params=_params("parallel", "parallel"),
        name="proj_residual",
    )(a, w, x, modtab)


def _router_kernel(x_ref, mod_ref, rwt_ref, h_ref, aff_ref, afft_ref):
    h = _normmod(x_ref[...], mod_ref, 3, 4)
    h_ref[...] = h.astype(BF16)
    logits_t = lax.dot_general(rwt_ref[...], h, (((1,), (1,)), ((), ())),
                               precision=lax.Precision.HIGHEST, preferred_element_type=F32)
    e = jnp.exp(logits_t - jnp.max(logits_t, axis=0, keepdims=True))
    aff_t = e / jnp.sum(e, axis=0, keepdims=True)
    afft_ref[...] = aff_t
    aff_ref[...] = aff_t.T


def _router(x, modtab, router_w_t, *, n_lat):
    b, rows, d = x.shape
    tm = ROW_TILE
    ne = router_w_t.shape[0]
    return pl.pallas_call(
        _router_kernel,
        grid=(b, rows // tm),
        in_specs=[pl.BlockSpec((None, tm, d), lambda i, t: (i, t, 0)),
                  _mod_spec(n_lat // tm),
                  pl.BlockSpec((ne, d), lambda i, t: (0, 0))],
        out_specs=[pl.BlockSpec((None, tm, d), lambda i, t: (i, t, 0)),
                   pl.BlockSpec((None, tm, ne), lambda i, t: (i, t, 0)),
                   pl.BlockSpec((None, ne, tm), lambda i, t: (i, 0, t))],
        out_shape=[jax.ShapeDtypeStruct((b, rows, d), BF16),
                   jax.ShapeDtypeStruct((b, rows, ne), F32),
                   jax.ShapeDtypeStruct((b, ne, rows), F32)],
        compiler_params=_params("parallel", "parallel"),
        name="router",
    )(x, modtab, router_w_t)


def _lane_prefix(mask_f, tri):
    n = mask_f.shape[1]
    run = jnp.zeros((mask_f.shape[0], 1), F32)
    out = []
    for j in range(n // 128):
        blk = mask_f[:, j * 128:(j + 1) * 128]
        incl = jnp.dot(blk.astype(BF16), tri, preferred_element_type=F32)
        out.append(incl - blk + run)
        run = run + incl[:, 127:128]
    return jnp.concatenate(out, axis=1)


def _select_kernel(afft_ref, slot_ref, slot_tm_ref, *, segments):
    r = lax.broadcasted_iota(I32, (128, 128), 0)
    c = lax.broadcasted_iota(I32, (128, 128), 1)
    tri = (r <= c).astype(BF16)
    for off, n, cap, base in segments:
        bits = lax.bitcast_convert_type(afft_ref[:, off:off + n], I32)
        cap_f = jnp.float32(cap)

        def body(i, thr):
            cand = thr | jnp.left_shift(jnp.int32(1), 30 - i)
            cnt = jnp.sum((bits >= cand).astype(F32), axis=1, keepdims=True)
            return jnp.where(cnt >= cap_f, cand, thr)

        thr = lax.fori_loop(0, 31, body, jnp.zeros((bits.shape[0], 1), I32))
        gt = (bits > thr).astype(F32)
        eq = (bits == thr).astype(F32)
        need = cap_f - jnp.sum(gt, axis=1, keepdims=True)
        sel = gt + eq * (_lane_prefix(eq, tri) < need).astype(F32)
        slot = _lane_prefix(sel, tri).astype(I32) + base
        slot = jnp.where(sel > 0.5, slot, -1)
        slot_ref[:, off:off + n] = slot
        slot_tm_ref[off:off + n, :] = slot.astype(F32).T


def _select(aff_t, segments):
    b, ne, rows = aff_t.shape
    return pl.pallas_call(
        functools.partial(_select_kernel, segments=segments),
        grid=(b,),
        in_specs=[pl.BlockSpec((None, ne, rows), lambda i: (i, 0, 0))],
        out_specs=[pl.BlockSpec((None, ne, rows), lambda i: (i, 0, 0)),
                   pl.BlockSpec((None, rows, ne), lambda i: (i, 0, 0))],
        out_shape=[jax.ShapeDtypeStruct((b, ne, rows), I32), jax.ShapeDtypeStruct((b, rows, ne), F32)],
        compiler_params=_params("parallel"),
        name="expert_select",
    )(aff_t)


def _gather_kernel(slot_ref, h_ref, xs_ref, *, segments):
    for off, n, cap, base in segments:
        slot = slot_ref[:, off:off + n]
        hseg = h_ref[off:off + n, :]
        st = min(SLOT_TILE, cap)
        for j in range(cap // st):
            ids = base + j * st + lax.broadcasted_iota(I32, (st, 1), 0)
            onehot = jnp.where(ids == slot, 1.0, 0.0).astype(BF16)
            rows = jnp.dot(onehot, hseg, preferred_element_type=F32)
            xs_ref[base + j * st:base + (j + 1) * st, :] = rows.astype(BF16)


def _gather(slot, h, segments, n_slots):
    b, ne, rows = slot.shape
    d = h.shape[2]
    return pl.pallas_call(
        functools.partial(_gather_kernel, segments=segments),
        grid=(b, ne),
        in_specs=[pl.BlockSpec((None, None, 1, rows), lambda i, e: (i, e, 0, 0)),
                  pl.BlockSpec((None, rows, d), lambda i, e: (i, 0, 0))],
        out_specs=pl.BlockSpec((None, None, n_slots, d), lambda i, e: (e, i, 0, 0)),
        out_shape=jax.ShapeDtypeStruct((ne, b, n_slots, d), BF16),
        compiler_params=_params("parallel", "arbitrary"),
        name="expert_gather",
    )(slot.reshape(b, ne, 1, rows), h)


def _ffn_kernel(xs_ref, w1_ref, w3_ref, w2_ref, y_ref):
    xs = xs_ref[...]
    a = jnp.dot(xs, w1_ref[...], preferred_element_type=F32)
    g = jnp.dot(xs, w3_ref[...], preferred_element_type=F32)
    hid = (a * _sigmoid(a)) * g
    y_ref[...] = _bdot(hid, w2_ref[...]).astype(BF16)


def _expert_ffn(xs, w1, w3, w2, layer):
    ne, b, n_slots, d = xs.shape
    ff = w1.shape[3]
    return pl.pallas_call(
        _ffn_kernel,
        grid=(ne, b),
        in_specs=[pl.BlockSpec((None, None, n_slots, d), lambda e, i: (e, i, 0, 0)),
                  pl.BlockSpec((None, None, d, ff), lambda e, i: (layer, e, 0, 0)),
                  pl.BlockSpec((None, None, d, ff), lambda e, i: (layer, e, 0, 0)),
                  pl.BlockSpec((None, None, ff, d), lambda e, i: (layer, e, 0, 0))],
        out_specs=pl.BlockSpec((None, None, n_slots, d), lambda e, i: (e, i, 0, 0)),
        out_shape=jax.ShapeDtypeStruct((ne, b, n_slots, d), BF16),
        compiler_params=_params("parallel", "arbitrary"),
        name="expert_ffn",
    )(xs, w1, w3, w2)


def _combine_kernel(slot_ref, aff_ref, y_ref, x_ref, mod_ref, o_ref, acc_ref, *, slot0):
    eg = pl.program_id(2)
    n_groups = pl.num_programs(2)
    group, width = y_ref.shape[:2]

    @pl.when(eg == 0)
    def _():
        acc_ref[...] = jnp.zeros_like(acc_ref)

    lane_e = lax.broadcasted_iota(I32, aff_ref.shape, 1)
    ids = slot0 + lax.broadcasted_iota(I32, (1, width), 1)
    total = None
    for j in range(group):
        e = eg * group + j
        gate = jnp.sum(jnp.where(lane_e == e, aff_ref[...], 0.0), axis=1, keepdims=True)
        slot = jnp.sum(jnp.where(lane_e == e, slot_ref[...], 0.0), axis=1, keepdims=True).astype(I32)
        onehot = jnp.where(slot == ids, 1.0, 0.0).astype(BF16)
        part = gate * jnp.dot(onehot, y_ref[j], preferred_element_type=F32)
        total = part if total is None else total + part
    acc_ref[...] += total

    @pl.when(eg == n_groups - 1)
    def _():
        o_ref[...] = x_ref[...] + mod_ref[5:6, :] * acc_ref[...]


def _combine(slot_tm, aff, y, x, modtab, *, row0, rows, tm, slot0, width, seg, in_place):
    b, _, d = x.shape
    ne = aff.shape[2]
    group = COMBINE_EXPERTS
    t0, s0 = row0 // tm, slot0 // width
    out_rows = x.shape[1] if in_place else rows
    return pl.pallas_call(
        functools.partial(_combine_kernel, slot0=slot0),
        grid=(b, rows // tm, ne // group),
        in_specs=[pl.BlockSpec((None, tm, ne), lambda i, t, e: (i, t + t0, 0)),
                  pl.BlockSpec((None, tm, ne), lambda i, t, e: (i, t + t0, 0)),
                  pl.BlockSpec((group, None, width, d), lambda i, t, e: (e, i, s0, 0)),
                  pl.BlockSpec((None, tm, d), lambda i, t, e: (i, t + t0, 0)),
                  pl.BlockSpec((None, None, 6, d), lambda i, t, e: (i, seg, 0, 0))],
        out_specs=pl.BlockSpec((None, tm, d), lambda i, t, e: (i, t + (t0 if in_place else 0), 0)),
        out_shape=jax.ShapeDtypeStruct((b, out_rows, d), F32),
        scratch_shapes=[pltpu.VMEM((tm, d), F32)],
        input_output_aliases={3: 0} if in_place else {},
        compiler_params=_params("parallel", "parallel", "arbitrary"),
        name="expert_combine",
    )(slot_tm, aff, y, x, modtab)


def _moe(x, modtab, router_w_t, w1, w3, w2, layer, *, n_lat, with_ctx):
    b, rows, d = x.shape
    ne = N_EXPERTS
    cap_l = CAPACITY_FACTOR * n_lat // ne
    h, aff, aff_t = _router(x, modtab, router_w_t, n_lat=n_lat)
    segments = ((0, n_lat, cap_l, 0),)
    n_slots = cap_l
    if with_ctx:
        n_ctx = rows - n_lat
        cap_c = CAPACITY_FACTOR * n_ctx // ne
        segments += ((n_lat, n_ctx, cap_c, cap_l),)
        n_slots += cap_c
    slot, slot_tm = _select(aff_t, segments)
    xs = _gather(slot, h, segments, n_slots)
    y = _expert_ffn(xs, w1, w3, w2, layer)
    out = _combine(slot_tm, aff, y, x, modtab, row0=0, rows=n_lat, tm=min(COMBINE_TILE, n_lat), slot0=0,
                   width=cap_l, seg=1, in_place=with_ctx)
    if with_ctx:
        out = _combine(slot_tm, aff, y, out, modtab, row0=n_lat, rows=n_ctx, tm=n_ctx, slot0=cap_l,
                       width=cap_c, seg=0, in_place=True)
    return out


def _segsum64(x, bd):
    hi = x.astype(BF16)
    lo = (x - hi.astype(F32)).astype(BF16)
    out = []
    for j in range(x.shape[1] // 256):
        sl = slice(j * 256, (j + 1) * 256)
        out.append(jnp.dot(hi[:, sl], bd, preferred_element_type=F32)
                   + jnp.dot(lo[:, sl], bd, preferred_element_type=F32))
    return jnp.concatenate(out, axis=1)


def _block_diag_ones():
    r = lax.broadcasted_iota(I32, (256, 256), 0)
    c = lax.broadcasted_iota(I32, (256, 256), 1)
    return (r // 64 == c // 64).astype(BF16)


def _softplus(x):
    return jnp.maximum(x, 0.0) + jnp.log(1.0 + jnp.exp(-jnp.abs(x)))


def _rwkv_feat_kernel(x_ref, xp_ref, xn_ref, mod_ref, mu_ref, wrkv_ref, w0_ref, w1_ref, w2_ref,
                      a0_ref, a1_ref, a2_ref, g1_ref, g2_ref, kk_ref, ka_ref,
                      r_out, v_out, nkk_out, g_out, w_out, k_out, b_out, *, n_lat_tiles):
    t = pl.program_id(1)
    nt = pl.num_programs(1)
    tm = x_ref.shape[0]
    h = _normmod(x_ref[...], mod_ref, 0, 1)
    hp = _normmod(xp_ref[7:8, :], mod_ref, 0, 1)
    hn = _normmod(xn_ref[0:1, :], mod_ref, 0, 1)
    has_left = jnp.logical_and(t != 0, t != n_lat_tiles)
    has_right = jnp.logical_and(t != n_lat_tiles - 1, t != nt - 1)
    hp = jnp.where(has_left, hp, 0.0)
    hn = jnp.where(has_right, hn, 0.0)
    row = lax.broadcasted_iota(I32, h.shape, 0)
    left = jnp.where(row == 0, hp, pltpu.roll(h, 1, axis=0))
    right = jnp.where(row == tm - 1, hn, pltpu.roll(h, tm - 1, axis=0))
    xx = 0.5 * (left + right) - h
    xr, xw, xk, xv, xa, xg = (h + xx * mu_ref[i:i + 1, :] for i in range(6))
    r = _bdot(xr, wrkv_ref[0])
    k = _bdot(xk, wrkv_ref[1])
    v = _bdot(xv, wrkv_ref[2])
    g = _bdot(_sigmoid(_bdot(xg, g1_ref[...])), g2_ref[...])
    bd = _block_diag_ones()
    kk = k * kk_ref[...]
    kk = kk * lax.rsqrt(jnp.maximum(_segsum64(kk * kk, bd), 1e-24))
    r_out[...] = r
    v_out[...] = v
    nkk_out[...] = -kk
    g_out[...] = g
    for d in range(2):
        w_lora = _bdot(jnp.tanh(_bdot(xw, w1_ref[d])), w2_ref[d])
        log_w = -_softplus(-(w0_ref[d:d + 1, :] + w_lora)) - 0.5
        w_out[d] = jnp.exp(-jnp.exp(log_w))
        a = _sigmoid(a0_ref[d:d + 1, :] + _bdot(_bdot(xa, a1_ref[d]), a2_ref[d]))
        k_out[d] = k * (1.0 + (a - 1.0) * ka_ref[...])
        b_out[d] = kk * a


def _rwkv_features(x, modtab, p, *, n_lat):
    b, t_all, d = x.shape
    tm = ROW_TILE
    nt = t_all // tm
    tb = tm // 8
    full = lambda shape: pl.BlockSpec(shape, lambda i, t: (0,) * len(shape))
    tok = pl.BlockSpec((None, tm, d), lambda i, t: (i, t, 0))
    tok2 = pl.BlockSpec((2, None, tm, d), lambda i, t: (0, i, t, 0))
    one = jax.ShapeDtypeStruct((b, t_all, d), F32)
    two = jax.ShapeDtypeStruct((2, b, t_all, d), F32)
    return pl.pallas_call(
        functools.partial(_rwkv_feat_kernel, n_lat_tiles=n_lat // tm),
        grid=(b, nt),
        in_specs=[tok,
                  pl.BlockSpec((None, 8, d), lambda i, t: (i, jnp.maximum(t * tb - 1, 0), 0)),
                  pl.BlockSpec((None, 8, d), lambda i, t: (i, jnp.minimum((t + 1) * tb, nt * tb - 1), 0)),
                  _mod_spec(n_lat // tm),
                  full((6, d)), full((3, d, d)), full((2, d)), full(p["w1"].shape), full(p["w2"].shape),
                  full((2, d)), full(p["a1"].shape), full(p["a2"].shape), full(p["g1"].shape),
                  full(p["g2"].shape), full((1, d)), full((1, d))],
        out_specs=[tok, tok, tok, tok, tok2, tok2, tok2],
        out_shape=[one, one, one, one, two, two, two],
        compiler_params=_params("parallel", "parallel"),
        name="rwkv_features",
    )(x, x, x, modtab, p["mu"], p["w_rkv"], p["w0"], p["w1"], p["w2"], p["a0"], p["a1"], p["a2"],
      p["g1"], p["g2"], p["k_k"], p["k_a"])


def _scan_kernel(r_ref, w_ref, k_ref, v_ref, a_ref, b_ref, o_ref, s_ref, wr_ref, *, reverse):
    n = s_ref.shape[0]
    steps = r_ref.shape[0]

    @pl.when(pl.program_id(0) == 0)
    def _():
        s_ref[...] = jnp.zeros_like(s_ref)

    def step(i, carry):
        j = steps - 1 - i if reverse else i
        r = r_ref[j]
        wr_ref[...] = w_ref[j] * r
        br = jnp.sum(b_ref[j] * r, axis=0, keepdims=True)
        kr = jnp.sum(k_ref[j] * r, axis=0, keepdims=True)
        acc = [jnp.zeros(s_ref.shape[1:], F32) for _ in range(4)]
        for kk in range(n):
            s = s_ref[kk]
            acc[kk % 2] = acc[kk % 2] + s * a_ref[j, kk:kk + 1, :]
            acc[2 + kk % 2] = acc[2 + kk % 2] + s * wr_ref[kk:kk + 1, :]
        sa = acc[0] + acc[1]
        v = v_ref[j]
        o_ref[j] = acc[2] + acc[3] + sa * br + v * kr
        for kk in range(n):
            s_ref[kk] = (s_ref[kk] * w_ref[j, kk:kk + 1, :] + sa * b_ref[j, kk:kk + 1, :]
                         + v * k_ref[j, kk:kk + 1, :])
        return carry

    lax.fori_loop(0, steps, step, 0)


def _wkv_scan(r, w, k, v, a, b, *, n_lat, reverse):
    t_all, n, chains = r.shape
    tc = SCAN_CHUNK
    nlc, nch = n_lat // tc, t_all // tc
    ncc = nch - nlc
    if reverse:
        idx = lambda c: (jnp.where(c < ncc, nch - 1 - c, nlc - 1 - (c - ncc)), 0, 0)
    else:
        idx = lambda c: (jnp.where(c < ncc, nlc + c, c - ncc), 0, 0)
    spec = pl.BlockSpec((tc, n, chains), idx)
    return pl.pallas_call(
        functools.partial(_scan_kernel, reverse=reverse),
        grid=(nch,),
        in_specs=[spec] * 6,
        out_specs=spec,
        out_shape=jax.ShapeDtypeStruct((t_all, n, chains), F32),
        scratch_shapes=[pltpu.VMEM((n, n, chains), F32), pltpu.VMEM((n, chains), F32)],
        compiler_params=_params("arbitrary"),
        name="wkv_scan_bwd" if reverse else "wkv_scan_fwd",
    )(r, w, k, v, a, b)


def _rwkv_out_kernel(o_ref, r_ref, k_ref, v_ref, g_ref, rk_ref, lnw_ref, lnb_ref, wo_ref, x_ref, mod_ref,
                     out_ref):
    bd = _block_diag_ones()
    inv_n = 1.0 / C_HEAD_DIM
    o = o_ref[...]
    o = o - _segsum64(o, bd) * inv_n
    o = o * lax.rsqrt(_segsum64(o * o, bd) * inv_n + C_GN_EPS)
    o = o * lnw_ref[...] + lnb_ref[...]
    r = r_ref[...]
    bonus = _segsum64(r * k_ref[0] * rk_ref[0:1, :] + r * k_ref[1] * rk_ref[1:2, :], bd) * v_ref[...]
    y = _bdot((o + bonus) * g_ref[...], wo_ref[...])
    out_ref[...] = x_ref[...] + mod_ref[2:3, :] * y


def _rwkv_readout(o, r, k2, v, g, p, x, modtab, *, n_lat, rows):
    b, t_all, d = x.shape
    tm = ROW_TILE
    tok = pl.BlockSpec((None, tm, d), lambda i, t: (i, t, 0))
    full = lambda shape: pl.BlockSpec(shape, lambda i, t: (0,) * len(shape))
    return pl.pallas_call(
        _rwkv_out_kernel,
        grid=(b, rows // tm),
        in_specs=[tok, tok, pl.BlockSpec((2, None, tm, d), lambda i, t: (0, i, t, 0)), tok, tok,
                  full((2, d)), full((1, d)), full((1, d)), full((d, d)), tok, _mod_spec(n_lat // tm)],
        out_specs=tok,
        out_shape=jax.ShapeDtypeStruct((b, rows, d), F32),
        compiler_params=_params("parallel", "parallel"),
        name="rwkv_readout",
    )(o, r, k2, v, g, p["r_k"], p["ln_w"], p["ln_b"], p["w_o"], x, modtab)


def _to_scan_kernel(x_ref, o_ref, y_ref):
    nb, tt, d = x_ref.shape
    n, chains = o_ref.shape[1:]
    for b in range(nb):
        for p in range(d // 128):
            row = (b * (d // 128) + p) * 128
            y_ref[row:row + 128, :] = x_ref[b, :, p * 128:(p + 1) * 128].T
    for k in range(n):
        o_ref[:, k, :] = y_ref[pl.ds(k, chains, stride=n), :].T


def _to_scan_layout(a, d=None):
    b, t, dm = a.shape[-3:]
    tt = 128
    chains = b * dm // C_HEAD_DIM
    if d is None:
        spec = pl.BlockSpec((b, tt, dm), lambda i: (0, i, 0))
    else:
        spec = pl.BlockSpec((None, b, tt, dm), lambda i: (d, 0, i, 0))
    return pl.pallas_call(
        _to_scan_kernel,
        grid=(t // tt,),
        in_specs=[spec],
        out_specs=pl.BlockSpec((tt, C_HEAD_DIM, chains), lambda i: (i, 0, 0)),
        out_shape=jax.ShapeDtypeStruct((t, C_HEAD_DIM, chains), F32),
        scratch_shapes=[pltpu.VMEM((b * dm, tt), F32)],
        compiler_params=_params("parallel"),
        name="to_scan_layout",
    )(a)


def _from_scan_kernel(a_ref, b_ref, o_ref, y_ref):
    nb, tt, d = o_ref.shape
    n, chains = a_ref.shape[1:]
    for k in range(n):
        y_ref[pl.ds(k, chains, stride=n), :] = (a_ref[:, k, :] + b_ref[:, k, :]).T
    for b in range(nb):
        for p in range(d // 128):
            row = (b * (d // 128) + p) * 128
            o_ref[b, :, p * 128:(p + 1) * 128] = y_ref[row:row + 128, :].T


def _from_scan_layout(o_f, o_b, b):
    t, n, chains = o_f.shape
    tt = 128
    dm = n * chains // b
    spec = pl.BlockSpec((tt, n, chains), lambda i: (i, 0, 0))
    return pl.pallas_call(
        _from_scan_kernel,
        grid=(t // tt,),
        in_specs=[spec, spec],
        out_specs=pl.BlockSpec((b, tt, dm), lambda i: (0, i, 0)),
        out_shape=jax.ShapeDtypeStruct((b, t, dm), F32),
        scratch_shapes=[pltpu.VMEM((b * dm, tt), F32)],
        compiler_params=_params("parallel"),
        name="from_scan_layout",
    )(o_f, o_b)


def _rwkv_mixer(x, modtab, p, *, n_lat, rows):
    b = x.shape[0]
    r, v, nkk, g, w2, k2, b2 = _rwkv_features(x, modtab, p, n_lat=n_lat)
    rs, vs, as_ = _to_scan_layout(r), _to_scan_layout(v), _to_scan_layout(nkk)
    o_f, o_b = (_wkv_scan(rs, _to_scan_layout(w2, d), _to_scan_layout(k2, d), vs, as_, _to_scan_layout(b2, d),
                          n_lat=n_lat, reverse=(d == 1)) for d in range(2))
    return _rwkv_readout(_from_scan_layout(o_f, o_b, b), r, k2, v, g, p, x, modtab, n_lat=n_lat, rows=rows)


def _final_kernel(x_ref, w_ref, o_ref):
    o_ref[...] = _rms(x_ref[...]) * w_ref[...]


def _final_norm(x, w):
    b, t, d = x.shape
    tm = ROW_TILE
    return pl.pallas_call(
        _final_kernel,
        grid=(b, t // tm),
        in_specs=[pl.BlockSpec((None, tm, d), lambda i, t: (i, t, 0)),
                  pl.BlockSpec((1, d), lambda i, t: (0, 0))],
        out_specs=pl.BlockSpec((None, tm, d), lambda i, t: (i, t, 0)),
        out_shape=jax.ShapeDtypeStruct((b, t, d), F32),
        compiler_params=_params("parallel", "parallel"),
        name="final_norm",
    )(x, w.reshape(1, d))


def _rope_tables(n_lat, n_ctx, head_dim):
    rows = jnp.repeat(jnp.arange(n_lat // GRID_W, dtype=I32), GRID_W).astype(F32)
    cols = jnp.tile(jnp.arange(GRID_W, dtype=I32), n_lat // GRID_W).astype(F32)
    n_freq = head_dim // 4
    inv_freq = ROPE_THETA ** (-jnp.arange(n_freq, dtype=F32) / n_freq)
    ang = jnp.concatenate([rows[:, None] * inv_freq, cols[:, None] * inv_freq], axis=-1)
    cos, sin = jnp.cos(ang), jnp.sin(ang)
    reps = 128 // head_dim
    cos = jnp.tile(jnp.concatenate([cos, cos], axis=-1), (1, reps))
    sin = jnp.tile(jnp.concatenate([-sin, sin], axis=-1), (1, reps))
    cos = jnp.concatenate([cos, jnp.ones((n_ctx, 128), F32)], axis=0)
    sin = jnp.concatenate([sin, jnp.zeros((n_ctx, 128), F32)], axis=0)
    return cos, sin


def kernel(x, c, ctx, c_ctx, mod_w, mod_b, a_w_qkv, a_w_o, a_q_norm, a_k_norm, b_w_qkv, b_w_o, b_sink,
           c_mu, c_w_rkv, c_w_o, c_w0, c_w1, c_w2, c_a0, c_a1, c_a2, c_g1, c_g2, c_k_k, c_k_a, c_r_k,
           c_ln_w, c_ln_b, router_w, ffn_w1, ffn_w3, ffn_w2, final_norm):
    b, n_lat, d = x.shape
    n_ctx = ctx.shape[1]
    t_all = n_lat + n_ctx
    depth = mod_w.shape[0]
    assert d == D_MODEL and n_ctx % ROW_TILE == 0 and n_lat % min(COMBINE_TILE, n_lat) == 0

    cond_rows = -(-(b + 1) // 8) * 8
    cond = jnp.zeros((cond_rows, d), F32).at[:b].set(c).at[b].set(c_ctx)
    mods = _mod_tables(cond, mod_w, mod_b).reshape(depth, cond_rows, 6, d)
    cos_a, sin_a = _rope_tables(n_lat, n_ctx, A_HEAD_DIM)
    cos_b, sin_b = _rope_tables(n_lat, n_ctx, B_HEAD_DIM)
    w1_all, w3_all, w2_all = _cast_bf16(ffn_w1), _cast_bf16(ffn_w3), _cast_bf16(ffn_w2)

    xs = jnp.concatenate([x, ctx], axis=1)
    for i in range(depth):
        last = i == depth - 1
        rows = n_lat if last else t_all
        kind, j = i % N_MIXERS, i // N_MIXERS
        modtab = jnp.stack([jnp.broadcast_to(mods[i, b], (b, 6, d)), mods[i, :b]], axis=1)
        if kind == 0:
            q, k, v = _qkv_proj(xs, modtab, a_w_qkv[j].astype(BF16), a_q_norm[j].reshape(1, -1),
                                a_k_norm[j].reshape(1, -1), cos_a, sin_a, kind=0, n_lat=n_lat)
            o = _global_attention(q, k, v, n_lat=n_lat, rows=rows)
            xs = _proj_residual(o, a_w_o[j].astype(BF16), xs, modtab, n_lat=n_lat, rows=rows)
        elif kind == 1:
            ones = jnp.ones((1, 128), F32)
            q, k, v = _qkv_proj(xs, modtab, b_w_qkv[j].astype(BF16), ones, ones, cos_b, sin_b,
                                kind=1, n_lat=n_lat)
            o = _window_attention(q, k, v, b_sink[j], n_lat=n_lat, rows=rows)
            xs = _proj_residual(o, b_w_o[j].astype(BF16), xs, modtab, n_lat=n_lat, rows=rows)
        else:
            p = dict(mu=c_mu[j], w_rkv=c_w_rkv[j].astype(BF16), w_o=c_w_o[j].astype(BF16), w0=c_w0[j],
                     w1=c_w1[j].astype(BF16), w2=c_w2[j].astype(BF16), a0=c_a0[j],
                     a1=c_a1[j].astype(BF16), a2=c_a2[j].astype(BF16), g1=c_g1[j].astype(BF16),
                     g2=c_g2[j].astype(BF16), k_k=c_k_k[j].reshape(1, d), k_a=c_k_a[j].reshape(1, d),
                     r_k=c_r_k[j].reshape(2, d), ln_w=c_ln_w[j].reshape(1, d), ln_b=c_ln_b[j].reshape(1, d))
            xs = _rwkv_mixer(xs, modtab, p, n_lat=n_lat, rows=rows)
        xs = _moe(xs, modtab, router_w[i].T, w1_all, w3_all, w2_all, i, n_lat=n_lat, with_ctx=not last)
    return _final_norm(xs, final_norm)
```

```python
import functools
import math

import jax
import jax.numpy as jnp
from jax import lax
from jax.experimental import pallas as pl
from jax.experimental.pallas import tpu as pltpu

F32 = jnp.float32
BF16 = jnp.bfloat16
I32 = jnp.int32

D_MODEL = 1024
GRID_W = 64
Q_BLOCK = 128
ROPE_THETA = 10000.0
NORM_EPS = 1e-6
NEG_INF = -1e30
A_HEADS, A_KV_HEADS, A_HEAD_DIM = 8, 2, 128
B_HEADS, B_KV_HEADS, B_HEAD_DIM = 16, 4, 64
WINDOW = 128
C_HEAD_DIM = 64
C_HEADS = D_MODEL // C_HEAD_DIM
C_GN_EPS = C_HEAD_DIM * 1e-5
N_EXPERTS = 16
EXPERT_FF = 2 * D_MODEL
CAPACITY_FACTOR = 2
N_MIXERS = 3
LOG2E = math.log2(math.e)

ROW_TILE = 256
COMBINE_TILE = 1024
COMBINE_EXPERTS = 8
SLOT_TILE = 128
SCAN_CHUNK = 16
VMEM_LIMIT = 56 * 1024 * 1024


def _params(*sem):
    return pltpu.CompilerParams(dimension_semantics=sem, vmem_limit_bytes=VMEM_LIMIT)


def _rms(x):
    return x * lax.rsqrt(jnp.mean(x * x, axis=-1, keepdims=True) + NORM_EPS)


def _normmod(x, mod_ref, shift_row, scale_row):
    return _rms(x) * (1.0 + mod_ref[scale_row:scale_row + 1, :]) + mod_ref[shift_row:shift_row + 1, :]


def _bdot(a, b):
    return jnp.dot(a.astype(BF16), b.astype(BF16), preferred_element_type=F32)


def _dot_nt(a, b):
    return lax.dot_general(a.astype(BF16), b.astype(BF16), (((1,), (1,)), ((), ())),
                           preferred_element_type=F32)


def _sigmoid(x):
    return 1.0 / (1.0 + jnp.exp(-x))


def _mod_spec(n_lat_tiles):
    return pl.BlockSpec((None, None, 6, D_MODEL),
                        lambda b, t: (b, jnp.where(t < n_lat_tiles, 1, 0), 0, 0))


def _cast_kernel(x_ref, o_ref):
    o_ref[...] = x_ref[...].astype(BF16)


def _cast_bf16(w):
    shape = w.shape
    w2 = w.reshape(-1, shape[-1])
    rows, cols = w2.shape
    tr = 512
    out = pl.pallas_call(
        _cast_kernel,
        grid=(rows // tr,),
        in_specs=[pl.BlockSpec((tr, cols), lambda i: (i, 0))],
        out_specs=pl.BlockSpec((tr, cols), lambda i: (i, 0)),
        out_shape=jax.ShapeDtypeStruct((rows, cols), BF16),
        compiler_params=_params("parallel"),
        name="cast_bf16",
    )(w2)
    return out.reshape(shape)


def _mod_kernel(cond_ref, w_ref, b_ref, o_ref):
    c = cond_ref[...]
    a = c * _sigmoid(c)
    o_ref[0] = _bdot(a, w_ref[0]) + b_ref[0]


def _mod_tables(cond, mod_w, mod_b):
    depth, d, n = mod_w.shape
    rows = cond.shape[0]
    return pl.pallas_call(
        _mod_kernel,
        grid=(depth, n // d),
        in_specs=[pl.BlockSpec((rows, d), lambda i, j: (0, 0)),
                  pl.BlockSpec((1, d, d), lambda i, j: (i, 0, j)),
                  pl.BlockSpec((1, 1, d), lambda i, j: (i, 0, j))],
        out_specs=pl.BlockSpec((1, rows, d), lambda i, j: (i, 0, j)),
        out_shape=jax.ShapeDtypeStruct((depth, rows, n), F32),
        compiler_params=_params("arbitrary", "arbitrary"),
        name="mod_tables",
    )(cond, mod_w, mod_b.reshape(depth, 1, n))


def _rope128(x, cos, sin_signed, half):
    if half == 64:
        rot = pltpu.roll(x, 64, axis=1)
    else:
        lane = lax.broadcasted_iota(I32, x.shape, 1)
        rot = jnp.where((lane % 64) < 32, pltpu.roll(x, 96, axis=1), pltpu.roll(x, 32, axis=1))
    return x * cos + rot * sin_signed


def _qkv_kernel(x_ref, mod_ref, w_ref, gq_ref, gk_ref, cos_ref, sin_ref, q_ref, k_ref, v_ref, *, kind):
    h = _normmod(x_ref[...], mod_ref, 0, 1)
    y = _bdot(h, w_ref[...])
    cos, sin = cos_ref[...], sin_ref[...]
    nq = D_MODEL
    nkv = (y.shape[1] - nq) // 2
    half = 64 if kind == 0 else 32
    q_scale = (2 * half) ** -0.5 * LOG2E
    for j in range((nq + nkv) // 128):
        s = y[:, j * 128:(j + 1) * 128]
        if kind == 0:
            gain = gq_ref[...] if j < nq // 128 else gk_ref[...]
            s = _rms(s) * gain
        s = _rope128(s, cos, sin, half)
        if j < nq // 128:
            q_ref[:, j * 128:(j + 1) * 128] = (s * q_scale).astype(BF16)
        elif kind == 0:
            k_ref[:, (j - nq // 128) * 128:(j - nq // 128 + 1) * 128] = s.astype(BF16)
        else:
            jj = (j - nq // 128) * 2
            k_ref[jj] = s[:, :64].astype(BF16)
            k_ref[jj + 1] = s[:, 64:].astype(BF16)
    v = y[:, nq + nkv:].astype(BF16)
    if kind == 0:
        v_ref[...] = v
    else:
        for jj in range(nkv // 64):
            v_ref[jj] = v[:, jj * 64:(jj + 1) * 64]


def _qkv_proj(x, modtab, w, gq, gk, cos, sin, *, kind, n_lat):
    b, t_all, d = x.shape
    tm = ROW_TILE
    nt = t_all // tm
    n = w.shape[1]
    nkv = (n - d) // 2
    if kind == 0:
        kv_shape = jax.ShapeDtypeStruct((b, t_all, nkv), BF16)
        kv_spec = pl.BlockSpec((None, tm, nkv), lambda i, t: (i, t, 0))
    else:
        kv_shape = jax.ShapeDtypeStruct((b, nkv // 64, t_all, 64), BF16)
        kv_spec = pl.BlockSpec((None, nkv // 64, tm, 64), lambda i, t: (i, 0, t, 0))
    return pl.pallas_call(
        functools.partial(_qkv_kernel, kind=kind),
        grid=(b, nt),
        in_specs=[pl.BlockSpec((None, tm, d), lambda i, t: (i, t, 0)),
                  _mod_spec(n_lat // tm),
                  pl.BlockSpec((d, n), lambda i, t: (0, 0)),
                  pl.BlockSpec((1, 128), lambda i, t: (0, 0)),
                  pl.BlockSpec((1, 128), lambda i, t: (0, 0)),
                  pl.BlockSpec((tm, 128), lambda i, t: (t, 0)),
                  pl.BlockSpec((tm, 128), lambda i, t: (t, 0))],
        out_specs=[pl.BlockSpec((None, tm, d), lambda i, t: (i, t, 0)), kv_spec, kv_spec],
        out_shape=[jax.ShapeDtypeStruct((b, t_all, d), BF16), kv_shape, kv_shape],
        compiler_params=_params("parallel", "parallel"),
        name=f"qkv_proj_{kind}",
    )(x, modtab, w, gq, gk, cos, sin)


def _flash_chunk(carry, q, kc, vc):
    m, l, acc = carry
    s = _dot_nt(q, kc)
    m_new = jnp.maximum(m, jnp.max(s, axis=-1, keepdims=True))
    alpha = jnp.exp2(m - m_new)
    p = jnp.exp2(s - m_new)
    l = alpha * l + jnp.sum(p, axis=-1, keepdims=True)
    acc = alpha * acc + jnp.dot(p.astype(BF16), vc, preferred_element_type=F32)
    return m_new, l, acc


def _gattn_kernel(q_ref, k_ref, v_ref, o_ref, *, n_lat, kchunk):
    hd = A_HEAD_DIM
    g = A_HEADS // A_KV_HEADS
    tq = q_ref.shape[0]
    t_all = k_ref.shape[0]
    qs = jnp.concatenate([q_ref[:, i * hd:(i + 1) * hd] for i in range(g)], axis=0)
    init = (jnp.full((g * tq, 1), NEG_INF, F32), jnp.zeros((g * tq, 1), F32), jnp.zeros((g * tq, hd), F32))

    def finish(carry):
        _, l, acc = carry
        o = (acc / l).astype(BF16)
        for i in range(g):
            o_ref[:, i * hd:(i + 1) * hd] = o[i * tq:(i + 1) * tq]

    ctx_carry = _flash_chunk(init, qs, k_ref[n_lat:t_all, :], v_ref[n_lat:t_all, :])
    is_ctx = pl.program_id(2) >= n_lat // tq

    @pl.when(is_ctx)
    def _():
        finish(ctx_carry)

    @pl.when(jnp.logical_not(is_ctx))
    def _():
        def body(c, carry):
            start = pl.multiple_of(c * kchunk, kchunk)
            return _flash_chunk(carry, qs, k_ref[pl.ds(start, kchunk), :], v_ref[pl.ds(start, kchunk), :])
        finish(lax.fori_loop(0, n_lat // kchunk, body, ctx_carry))


def _global_attention(q, k, v, *, n_lat, rows):
    b, t_all, d = q.shape
    hd, g = A_HEAD_DIM, A_HEADS // A_KV_HEADS
    tq = ROW_TILE
    return pl.pallas_call(
        functools.partial(_gattn_kernel, n_lat=n_lat, kchunk=min(2048, n_lat)),
        grid=(b, A_KV_HEADS, rows // tq),
        in_specs=[pl.BlockSpec((None, tq, g * hd), lambda i, h, t: (i, t, h)),
                  pl.BlockSpec((None, t_all, hd), lambda i, h, t: (i, 0, h)),
                  pl.BlockSpec((None, t_all, hd), lambda i, h, t: (i, 0, h))],
        out_specs=pl.BlockSpec((None, tq, g * hd), lambda i, h, t: (i, t, h)),
        out_shape=jax.ShapeDtypeStruct((b, t_all, d), BF16),
        compiler_params=_params("parallel", "parallel", "arbitrary"),
        name="global_attention",
    )(q, k, v)


def _wattn_kernel(q_ref, k_ref, v_ref, sink_ref, bias_ref, o_ref, *, n_lat):
    hd = B_HEAD_DIM
    g = B_HEADS // B_KV_HEADS
    tq = q_ref.shape[0]
    t_all = k_ref.shape[1]
    t = pl.program_id(1)
    n_tiles = n_lat // tq
    is_ctx = t >= n_tiles

    def group(j):
        q = jnp.concatenate([q_ref[:, (j * g + i) * hd:(j * g + i + 1) * hd] for i in range(g)], axis=0)
        sink = jnp.concatenate([jnp.broadcast_to(sink_ref[j, :, i:i + 1], (tq, 1)) for i in range(g)],
                               axis=0) * LOG2E
        kc, vc = k_ref[j, n_lat:t_all, :], v_ref[j, n_lat:t_all, :]
        return q, sink, _dot_nt(q, kc), vc

    def finish(j, o):
        o = o.astype(BF16)
        o_ref[:, j * g * hd:(j + 1) * g * hd] = jnp.concatenate([o[i * tq:(i + 1) * tq] for i in range(g)], axis=1)

    @pl.when(is_ctx)
    def _():
        for j in range(B_KV_HEADS):
            q, sink, s_ctx, vc = group(j)
            m = jnp.maximum(jnp.max(s_ctx, axis=-1, keepdims=True), sink)
            p = jnp.exp2(s_ctx - m)
            l = jnp.sum(p, axis=-1, keepdims=True) + jnp.exp2(sink - m)
            finish(j, jnp.dot(p.astype(BF16), vc, preferred_element_type=F32) / l)

    @pl.when(jnp.logical_not(is_ctx))
    def _():
        wlen = tq + 2 * WINDOW
        ws = jnp.clip(t * tq - WINDOW, 0, n_lat - wlen)
        start = pl.multiple_of(ws, 128)
        bias = bias_ref[jnp.where(t == 0, 0, jnp.where(t == n_tiles - 1, 2, 1))]
        bias = jnp.concatenate([bias] * g, axis=0)
        for j in range(B_KV_HEADS):
            q, sink, _, _ = group(j)
            keys = jnp.concatenate([k_ref[j, pl.ds(start, wlen), :], k_ref[j, n_lat:t_all, :]], axis=0)
            vals = jnp.concatenate([v_ref[j, pl.ds(start, wlen), :], v_ref[j, n_lat:t_all, :]], axis=0)
            s = _dot_nt(q, keys) + bias
            m = jnp.maximum(jnp.max(s, axis=-1, keepdims=True), sink)
            p = jnp.exp2(s - m)
            l = jnp.sum(p, axis=-1, keepdims=True) + jnp.exp2(sink - m)
            finish(j, jnp.dot(p.astype(BF16), vals, preferred_element_type=F32) / l)


def _band_bias(tq, n_ctx):
    row = jnp.arange(tq, dtype=I32)[:, None]
    col = jnp.arange(tq + 2 * WINDOW, dtype=I32)[None, :]
    band = jnp.stack([jnp.where(jnp.abs(col - off - row) <= WINDOW, 0.0, NEG_INF).astype(F32)
                      for off in (0, WINDOW, 2 * WINDOW)])
    return jnp.concatenate([band, jnp.zeros((3, tq, n_ctx), F32)], axis=2)


def _window_attention(q, k, v, sink, *, n_lat, rows):
    b, t_all, d = q.shape
    hd, g = B_HEAD_DIM, B_HEADS // B_KV_HEADS
    tq = Q_BLOCK
    wlen = tq + 2 * WINDOW
    return pl.pallas_call(
        functools.partial(_wattn_kernel, n_lat=n_lat),
        grid=(b, rows // tq),
        in_specs=[pl.BlockSpec((None, tq, d), lambda i, t: (i, t, 0)),
                  pl.BlockSpec((None, B_KV_HEADS, t_all, hd), lambda i, t: (i, 0, 0, 0)),
                  pl.BlockSpec((None, B_KV_HEADS, t_all, hd), lambda i, t: (i, 0, 0, 0)),
                  pl.BlockSpec((B_KV_HEADS, 1, g), lambda i, t: (0, 0, 0)),
                  pl.BlockSpec((3, tq, wlen + t_all - n_lat), lambda i, t: (0, 0, 0))],
        out_specs=pl.BlockSpec((None, tq, d), lambda i, t: (i, t, 0)),
        out_shape=jax.ShapeDtypeStruct((b, t_all, d), BF16),
        compiler_params=_params("parallel", "arbitrary"),
        name="window_attention",
    )(q, k, v, sink.reshape(B_KV_HEADS, 1, g), _band_bias(tq, t_all - n_lat))


def _proj_res_kernel(a_ref, w_ref, x_ref, mod_ref, o_ref):
    o_ref[...] = x_ref[...] + mod_ref[2:3, :] * _bdot(a_ref[...], w_ref[...])


def _proj_residual(a, w, x, modtab, *, n_lat, rows):
    b, t_all, d = x.shape
    tm = ROW_TILE
    return pl.pallas_call(
        _proj_res_kernel,
        grid=(b, rows // tm),
        in_specs=[pl.BlockSpec((None, tm, d), lambda i, t: (i, t, 0)),
                  pl.BlockSpec((d, d), lambda i, t: (0, 0)),
                  pl.BlockSpec((None, tm, d), lambda i, t: (i, t, 0)),
                  _mod_spec(n_lat // tm)],
        out_specs=pl.BlockSpec((None, tm, d), lambda i, t: (i, t, 0)),
        out_shape=jax.ShapeDtypeStruct((b, rows, d), F32),
        compiler_params=_params("parallel", "parallel"),
        name="proj_residual",
    )(a, w, x, modtab)


def _router_kernel(x_ref, mod_ref, rwt_ref, h_ref, aff_ref, afft_ref):
    h = _normmod(x_ref[...], mod_ref, 3, 4)
    h_ref[...] = h.astype(BF16)
    logits_t = lax.dot_general(rwt_ref[...], h, (((1,), (1,)), ((), ())),
                               precision=lax.Precision.HIGHEST, preferred_element_type=F32)
    e = jnp.exp(logits_t - jnp.max(logits_t, axis=0, keepdims=True))
    aff_t = e / jnp.sum(e, axis=0, keepdims=True)
    afft_ref[...] = aff_t
    aff_ref[...] = aff_t.T


def _router(x, modtab, router_w_t, *, n_lat):
    b, rows, d = x.shape
    tm = ROW_TILE
    ne = router_w_t.shape[0]
    return pl.pallas_call(
        _router_kernel,
        grid=(b, rows // tm),
        in_specs=[pl.BlockSpec((None, tm, d), lambda i, t: (i, t, 0)),
                  _mod_spec(n_lat // tm),
                  pl.BlockSpec((ne, d), lambda i, t: (0, 0))],
        out_specs=[pl.BlockSpec((None, tm, d), lambda i, t: (i, t, 0)),
                   pl.BlockSpec((None, tm, ne), lambda i, t: (i, t, 0)),
                   pl.BlockSpec((None, ne, tm), lambda i, t: (i, 0, t))],
        out_shape=[jax.ShapeDtypeStruct((b, rows, d), BF16),
                   jax.ShapeDtypeStruct((b, rows, ne), F32),
                   jax.ShapeDtypeStruct((b, ne, rows), F32)],
        compiler_params=_params("parallel", "parallel"),
        name="router",
    )(x, modtab, router_w_t)


def _lane_prefix(mask_f, tri):
    n = mask_f.shape[1]
    run = jnp.zeros((mask_f.shape[0], 1), F32)
    out = []
    for j in range(n // 128):
        blk = mask_f[:, j * 128:(j + 1) * 128]
        incl = jnp.dot(blk.astype(BF16), tri, preferred_element_type=F32)
        out.append(incl - blk + run)
        run = run + incl[:, 127:128]
    return jnp.concatenate(out, axis=1)


def _select_kernel(afft_ref, slot_ref, slot_tm_ref, *, segments):
    r = lax.broadcasted_iota(I32, (128, 128), 0)
    c = lax.broadcasted_iota(I32, (128, 128), 1)
    tri = (r <= c).astype(BF16)
    for off, n, cap, base in segments:
        bits = lax.bitcast_convert_type(afft_ref[:, off:off + n], I32)
        cap_f = jnp.float32(cap)

        def body(i, thr):
            cand = thr | jnp.left_shift(jnp.int32(1), 30 - i)
            cnt = jnp.sum((bits >= cand).astype(F32), axis=1, keepdims=True)
            return jnp.where(cnt >= cap_f, cand, thr)

        thr = lax.fori_loop(0, 31, body, jnp.zeros((bits.shape[0], 1), I32))
        gt = (bits > thr).astype(F32)
        eq = (bits == thr).astype(F32)
        need = cap_f - jnp.sum(gt, axis=1, keepdims=True)
        sel = gt + eq * (_lane_prefix(eq, tri) < need).astype(F32)
        slot = _lane_prefix(sel, tri).astype(I32) + base
        slot = jnp.where(sel > 0.5, slot, -1)
        slot_ref[:, off:off + n] = slot
        slot_tm_ref[off:off + n, :] = slot.astype(F32).T


def _select(aff_t, segments):
    b, ne, rows = aff_t.shape
    return pl.pallas_call(
        functools.partial(_select_kernel, segments=segments),
        grid=(b,),
        in_specs=[pl.BlockSpec((None, ne, rows), lambda i: (i, 0, 0))],
        out_specs=[pl.BlockSpec((None, ne, rows), lambda i: (i, 0, 0)),
                   pl.BlockSpec((None, rows, ne), lambda i: (i, 0, 0))],
        out_shape=[jax.ShapeDtypeStruct((b, ne, rows), I32), jax.ShapeDtypeStruct((b, rows, ne), F32)],
        compiler_params=_params("parallel"),
        name="expert_select",
    )(aff_t)


def _gather_kernel(slot_ref, h_ref, xs_ref, *, segments):
    for off, n, cap, base in segments:
        slot = slot_ref[:, off:off + n]
        hseg = h_ref[off:off + n, :]
        st = min(SLOT_TILE, cap)
        for j in range(cap // st):
            ids = base + j * st + lax.broadcasted_iota(I32, (st, 1), 0)
            onehot = jnp.where(ids == slot, 1.0, 0.0).astype(BF16)
            rows = jnp.dot(onehot, hseg, preferred_element_type=F32)
            xs_ref[base + j * st:base + (j + 1) * st, :] = rows.astype(BF16)


def _gather(slot, h, segments, n_slots):
    b, ne, rows = slot.shape
    d = h.shape[2]
    return pl.pallas_call(
        functools.partial(_gather_kernel, segments=segments),
        grid=(b, ne),
        in_specs=[pl.BlockSpec((None, None, 1, rows), lambda i, e: (i, e, 0, 0)),
                  pl.BlockSpec((None, rows, d), lambda i, e: (i, 0, 0))],
        out_specs=pl.BlockSpec((None, None, n_slots, d), lambda i, e: (e, i, 0, 0)),
        out_shape=jax.ShapeDtypeStruct((ne, b, n_slots, d), BF16),
        compiler_params=_params("parallel", "arbitrary"),
        name="expert_gather",
    )(slot.reshape(b, ne, 1, rows), h)


def _ffn_kernel(xs_ref, w1_ref, w3_ref, w2_ref, y_ref):
    xs = xs_ref[...]
    a = jnp.dot(xs, w1_ref[...], preferred_element_type=F32)
    g = jnp.dot(xs, w3_ref[...], preferred_element_type=F32)
    hid = (a * _sigmoid(a)) * g
    y_ref[...] = _bdot(hid, w2_ref[...]).astype(BF16)


def _expert_ffn(xs, w1, w3, w2, layer):
    ne, b, n_slots, d = xs.shape
    ff = w1.shape[3]
    return pl.pallas_call(
        _ffn_kernel,
        grid=(ne, b),
        in_specs=[pl.BlockSpec((None, None, n_slots, d), lambda e, i: (e, i, 0, 0)),
                  pl.BlockSpec((None, None, d, ff), lambda e, i: (layer, e, 0, 0)),
                  pl.BlockSpec((None, None, d, ff), lambda e, i: (layer, e, 0, 0)),
                  pl.BlockSpec((None, None, ff, d), lambda e, i: (layer, e, 0, 0))],
        out_specs=pl.BlockSpec((None, None, n_slots, d), lambda e, i: (e, i, 0, 0)),
        out_shape=jax.ShapeDtypeStruct((ne, b, n_slots, d), BF16),
        compiler_params=_params("parallel", "arbitrary"),
        name="expert_ffn",
    )(xs, w1, w3, w2)


def _combine_kernel(slot_ref, aff_ref, y_ref, x_ref, mod_ref, fw_ref, o_ref, acc_ref, *, slot0, final):
    eg = pl.program_id(2)
    n_groups = pl.num_programs(2)
    group, width = y_ref.shape[:2]

    @pl.when(eg == 0)
    def _():
        acc_ref[...] = jnp.zeros_like(acc_ref)

    lane_e = lax.broadcasted_iota(I32, aff_ref.shape, 1)
    ids = slot0 + lax.broadcasted_iota(I32, (1, width), 1)
    total = None
    for j in range(group):
        e = eg * group + j
        gate = jnp.sum(jnp.where(lane_e == e, aff_ref[...], 0.0), axis=1, keepdims=True)
        slot = jnp.sum(jnp.where(lane_e == e, slot_ref[...], 0.0), axis=1, keepdims=True).astype(I32)
        onehot = jnp.where(slot == ids, 1.0, 0.0).astype(BF16)
        part = gate * jnp.dot(onehot, y_ref[j], preferred_element_type=F32)
        total = part if total is None else total + part
    acc_ref[...] += total

    @pl.when(eg == n_groups - 1)
    def _():
        out = x_ref[...] + mod_ref[5:6, :] * acc_ref[...]
        o_ref[...] = _rms(out) * fw_ref[...] if final else out


def _combine(slot_tm, aff, y, x, modtab, final_w, *, row0, rows, tm, slot0, width, seg, in_place, final=False):
    b, _, d = x.shape
    ne = aff.shape[2]
    group = COMBINE_EXPERTS
    t0, s0 = row0 // tm, slot0 // width
    out_rows = x.shape[1] if in_place else rows
    return pl.pallas_call(
        functools.partial(_combine_kernel, slot0=slot0, final=final),
        grid=(b, rows // tm, ne // group),
        in_specs=[pl.BlockSpec((None, tm, ne), lambda i, t, e: (i, t + t0, 0)),
                  pl.BlockSpec((None, tm, ne), lambda i, t, e: (i, t + t0, 0)),
                  pl.BlockSpec((group, None, width, d), lambda i, t, e: (e, i, s0, 0)),
                  pl.BlockSpec((None, tm, d), lambda i, t, e: (i, t + t0, 0)),
                  pl.BlockSpec((None, None, 6, d), lambda i, t, e: (i, seg, 0, 0)),
                  pl.BlockSpec((1, d), lambda i, t, e: (0, 0))],
        out_specs=pl.BlockSpec((None, tm, d), lambda i, t, e: (i, t + (t0 if in_place else 0), 0)),
        out_shape=jax.ShapeDtypeStruct((b, out_rows, d), F32),
        scratch_shapes=[pltpu.VMEM((tm, d), F32)],
        input_output_aliases={3: 0} if in_place else {},
        compiler_params=_params("parallel", "parallel", "arbitrary"),
        name="expert_combine",
    )(slot_tm, aff, y, x, modtab, final_w.reshape(1, d))


def _moe(x, modtab, router_w_t, w1, w3, w2, layer, final_w, *, n_lat, with_ctx):
    b, rows, d = x.shape
    ne = N_EXPERTS
    cap_l = CAPACITY_FACTOR * n_lat // ne
    h, aff, aff_t = _router(x, modtab, router_w_t, n_lat=n_lat)
    segments = ((0, n_lat, cap_l, 0),)
    n_slots = cap_l
    if with_ctx:
        n_ctx = rows - n_lat
        cap_c = CAPACITY_FACTOR * n_ctx // ne
        segments += ((n_lat, n_ctx, cap_c, cap_l),)
        n_slots += cap_c
    slot, slot_tm = _select(aff_t, segments)
    xs = _gather(slot, h, segments, n_slots)
    y = _expert_ffn(xs, w1, w3, w2, layer)
    out = _combine(slot_tm, aff, y, x, modtab, final_w, row0=0, rows=n_lat, tm=min(COMBINE_TILE, n_lat),
                   slot0=0, width=cap_l, seg=1, in_place=with_ctx, final=not with_ctx)
    if with_ctx:
        out = _combine(slot_tm, aff, y, out, modtab, final_w, row0=n_lat, rows=n_ctx, tm=n_ctx, slot0=cap_l,
                       width=cap_c, seg=0, in_place=True)
    return out


def _segsum64(x, bd):
    hi = x.astype(BF16)
    lo = (x - hi.astype(F32)).astype(BF16)
    out = []
    for j in range(x.shape[1] // 256):
        sl = slice(j * 256, (j + 1) * 256)
        out.append(jnp.dot(hi[:, sl], bd, preferred_element_type=F32)
                   + jnp.dot(lo[:, sl], bd, preferred_element_type=F32))
    return jnp.concatenate(out, axis=1)


def _block_diag_ones():
    r = lax.broadcasted_iota(I32, (256, 256), 0)
    c = lax.broadcasted_iota(I32, (256, 256), 1)
    return (r // 64 == c // 64).astype(BF16)


def _softplus(x):
    return jnp.maximum(x, 0.0) + jnp.log(1.0 + jnp.exp(-jnp.abs(x)))


def _rwkv_feat_kernel(x_ref, xp_ref, xn_ref, mod_ref, mu_ref, wrkv_ref, w0_ref, w1_ref, w2_ref,
                      a0_ref, a1_ref, a2_ref, g1_ref, g2_ref, kk_ref, ka_ref,
                      r_out, v_out, nkk_out, g_out, w_out, k_out, b_out, *, n_lat_tiles):
    t = pl.program_id(1)
    nt = pl.num_programs(1)
    tm = x_ref.shape[0]
    h = _normmod(x_ref[...], mod_ref, 0, 1)
    hp = _normmod(xp_ref[7:8, :], mod_ref, 0, 1)
    hn = _normmod(xn_ref[0:1, :], mod_ref, 0, 1)
    has_left = jnp.logical_and(t != 0, t != n_lat_tiles)
    has_right = jnp.logical_and(t != n_lat_tiles - 1, t != nt - 1)
    hp = jnp.where(has_left, hp, 0.0)
    hn = jnp.where(has_right, hn, 0.0)
    row = lax.broadcasted_iota(I32, h.shape, 0)
    left = jnp.where(row == 0, hp, pltpu.roll(h, 1, axis=0))
    right = jnp.where(row == tm - 1, hn, pltpu.roll(h, tm - 1, axis=0))
    xx = 0.5 * (left + right) - h
    xr, xw, xk, xv, xa, xg = (h + xx * mu_ref[i:i + 1, :] for i in range(6))
    r = _bdot(xr, wrkv_ref[0])
    k = _bdot(xk, wrkv_ref[1])
    v = _bdot(xv, wrkv_ref[2])
    g = _bdot(_sigmoid(_bdot(xg, g1_ref[...])), g2_ref[...])
    bd = _block_diag_ones()
    kk = k * kk_ref[...]
    kk = kk * lax.rsqrt(jnp.maximum(_segsum64(kk * kk, bd), 1e-24))
    r_out[...] = r
    v_out[...] = v
    nkk_out[...] = -kk
    g_out[...] = g
    for d in range(2):
        w_lora = _bdot(jnp.tanh(_bdot(xw, w1_ref[d])), w2_ref[d])
        log_w = -_softplus(-(w0_ref[d:d + 1, :] + w_lora)) - 0.5
        w_out[d] = jnp.exp(-jnp.exp(log_w))
        a = _sigmoid(a0_ref[d:d + 1, :] + _bdot(_bdot(xa, a1_ref[d]), a2_ref[d]))
        k_out[d] = k * (1.0 + (a - 1.0) * ka_ref[...])
        b_out[d] = kk * a


def _rwkv_features(x, modtab, p, *, n_lat):
    b, t_all, d = x.shape
    tm = ROW_TILE
    nt = t_all // tm
    tb = tm // 8
    full = lambda shape: pl.BlockSpec(shape, lambda i, t: (0,) * len(shape))
    tok = pl.BlockSpec((None, tm, d), lambda i, t: (i, t, 0))
    tok2 = pl.BlockSpec((2, None, tm, d), lambda i, t: (0, i, t, 0))
    one = jax.ShapeDtypeStruct((b, t_all, d), F32)
    two = jax.ShapeDtypeStruct((2, b, t_all, d), F32)
    return pl.pallas_call(
        functools.partial(_rwkv_feat_kernel, n_lat_tiles=n_lat // tm),
        grid=(b, nt),
        in_specs=[tok,
                  pl.BlockSpec((None, 8, d), lambda i, t: (i, jnp.maximum(t * tb - 1, 0), 0)),
                  pl.BlockSpec((None, 8, d), lambda i, t: (i, jnp.minimum((t + 1) * tb, nt * tb - 1), 0)),
                  _mod_spec(n_lat // tm),
                  full((6, d)), full((3, d, d)), full((2, d)), full(p["w1"].shape), full(p["w2"].shape),
                  full((2, d)), full(p["a1"].shape), full(p["a2"].shape), full(p["g1"].shape),
                  full(p["g2"].shape), full((1, d)), full((1, d))],
        out_specs=[tok, tok, tok, tok, tok2, tok2, tok2],
        out_shape=[one, one, one, one, two, two, two],
        compiler_params=_params("parallel", "parallel"),
        name="rwkv_features",
    )(x, x, x, modtab, p["mu"], p["w_rkv"], p["w0"], p["w1"], p["w2"], p["a0"], p["a1"], p["a2"],
      p["g1"], p["g2"], p["k_k"], p["k_a"])


def _scan_kernel(r_ref, w_ref, k_ref, v_ref, a_ref, b_ref, o_ref, s_ref, wr_ref, *, reverse):
    n = s_ref.shape[0]
    steps = r_ref.shape[0]

    @pl.when(pl.program_id(0) == 0)
    def _():
        s_ref[...] = jnp.zeros_like(s_ref)

    def step(i, carry):
        j = steps - 1 - i if reverse else i
        r = r_ref[j]
        wr_ref[...] = w_ref[j] * r
        br = jnp.sum(b_ref[j] * r, axis=0, keepdims=True)
        kr = jnp.sum(k_ref[j] * r, axis=0, keepdims=True)
        acc = [jnp.zeros(s_ref.shape[1:], F32) for _ in range(4)]
        for kk in range(n):
            s = s_ref[kk]
            acc[kk % 2] = acc[kk % 2] + s * a_ref[j, kk:kk + 1, :]
            acc[2 + kk % 2] = acc[2 + kk % 2] + s * wr_ref[kk:kk + 1, :]
        sa = acc[0] + acc[1]
        v = v_ref[j]
        o_ref[j] = acc[2] + acc[3] + sa * br + v * kr
        for kk in range(n):
            s_ref[kk] = (s_ref[kk] * w_ref[j, kk:kk + 1, :] + sa * b_ref[j, kk:kk + 1, :]
                         + v * k_ref[j, kk:kk + 1, :])
        return carry

    lax.fori_loop(0, steps, step, 0)


def _wkv_scan(r, w, k, v, a, b, *, n_lat, reverse):
    t_all, n, chains = r.shape
    tc = SCAN_CHUNK
    nlc, nch = n_lat // tc, t_all // tc
    ncc = nch - nlc
    if reverse:
        idx = lambda c: (jnp.where(c < ncc, nch - 1 - c, nlc - 1 - (c - ncc)), 0, 0)
    else:
        idx = lambda c: (jnp.where(c < ncc, nlc + c, c - ncc), 0, 0)
    spec = pl.BlockSpec((tc, n, chains), idx)
    return pl.pallas_call(
        functools.partial(_scan_kernel, reverse=reverse),
        grid=(nch,),
        in_specs=[spec] * 6,
        out_specs=spec,
        out_shape=jax.ShapeDtypeStruct((t_all, n, chains), F32),
        scratch_shapes=[pltpu.VMEM((n, n, chains), F32), pltpu.VMEM((n, chains), F32)],
        compiler_params=_params("arbitrary"),
        name="wkv_scan_bwd" if reverse else "wkv_scan_fwd",
    )(r, w, k, v, a, b)


def _rwkv_out_kernel(o_ref, r_ref, k_ref, v_ref, g_ref, rk_ref, lnw_ref, lnb_ref, wo_ref, x_ref, mod_ref,
                     out_ref):
    bd = _block_diag_ones()
    inv_n = 1.0 / C_HEAD_DIM
    o = o_ref[...]
    o = o - _segsum64(o, bd) * inv_n
    o = o * lax.rsqrt(_segsum64(o * o, bd) * inv_n + C_GN_EPS)
    o = o * lnw_ref[...] + lnb_ref[...]
    r = r_ref[...]
    bonus = _segsum64(r * k_ref[0] * rk_ref[0:1, :] + r * k_ref[1] * rk_ref[1:2, :], bd) * v_ref[...]
    y = _bdot((o + bonus) * g_ref[...], wo_ref[...])
    out_ref[...] = x_ref[...] + mod_ref[2:3, :] * y


def _rwkv_readout(o, r, k2, v, g, p, x, modtab, *, n_lat, rows):
    b, t_all, d = x.shape
    tm = ROW_TILE
    tok = pl.BlockSpec((None, tm, d), lambda i, t: (i, t, 0))
    full = lambda shape: pl.BlockSpec(shape, lambda i, t: (0,) * len(shape))
    return pl.pallas_call(
        _rwkv_out_kernel,
        grid=(b, rows // tm),
        in_specs=[tok, tok, pl.BlockSpec((2, None, tm, d), lambda i, t: (0, i, t, 0)), tok, tok,
                  full((2, d)), full((1, d)), full((1, d)), full((d, d)), tok, _mod_spec(n_lat // tm)],
        out_specs=tok,
        out_shape=jax.ShapeDtypeStruct((b, rows, d), F32),
        compiler_params=_params("parallel", "parallel"),
        name="rwkv_readout",
    )(o, r, k2, v, g, p["r_k"], p["ln_w"], p["ln_b"], p["w_o"], x, modtab)


def _to_scan_kernel(x_ref, o_ref, y_ref):
    nb, tt, d = x_ref.shape
    n, chains = o_ref.shape[1:]
    for b in range(nb):
        for p in range(d // 128):
            row = (b * (d // 128) + p) * 128
            y_ref[row:row + 128, :] = x_ref[b, :, p * 128:(p + 1) * 128].T
    for k in range(n):
        o_ref[:, k, :] = y_ref[pl.ds(k, chains, stride=n), :].T


def _to_scan_layout(a, d=None):
    b, t, dm = a.shape[-3:]
    tt = 128
    chains = b * dm // C_HEAD_DIM
    if d is None:
        spec = pl.BlockSpec((b, tt, dm), lambda i: (0, i, 0))
    else:
        spec = pl.BlockSpec((None, b, tt, dm), lambda i: (d, 0, i, 0))
    return pl.pallas_call(
        _to_scan_kernel,
        grid=(t // tt,),
        in_specs=[spec],
        out_specs=pl.BlockSpec((tt, C_HEAD_DIM, chains), lambda i: (i, 0, 0)),
        out_shape=jax.ShapeDtypeStruct((t, C_HEAD_DIM, chains), F32),
        scratch_shapes=[pltpu.VMEM((b * dm, tt), F32)],
        compiler_params=_params("parallel"),
        name="to_scan_layout",
    )(a)


def _from_scan_kernel(a_ref, b_ref, o_ref, y_ref):
    nb, tt, d = o_ref.shape
    n, chains = a_ref.shape[1:]
    for k in range(n):
        y_ref[pl.ds(k, chains, stride=n), :] = (a_ref[:, k, :] + b_ref[:, k, :]).T
    for b in range(nb):
        for p in range(d // 128):
            row = (b * (d // 128) + p) * 128
            o_ref[b, :, p * 128:(p + 1) * 128] = y_ref[row:row + 128, :].T


def _from_scan_layout(o_f, o_b, b):
    t, n, chains = o_f.shape
    tt = 128
    dm = n * chains // b
    spec = pl.BlockSpec((tt, n, chains), lambda i: (i, 0, 0))
    return pl.pallas_call(
        _from_scan_kernel,
        grid=(t // tt,),
        in_specs=[spec, spec],
        out_specs=pl.BlockSpec((b, tt, dm), lambda i: (0, i, 0)),
        out_shape=jax.ShapeDtypeStruct((b, t, dm), F32),
        scratch_shapes=[pltpu.VMEM((b * dm, tt), F32)],
        compiler_params=_params("parallel"),
        name="from_scan_layout",
    )(o_f, o_b)


def _rwkv_mixer(x, modtab, p, *, n_lat, rows):
    b = x.shape[0]
    r, v, nkk, g, w2, k2, b2 = _rwkv_features(x, modtab, p, n_lat=n_lat)
    rs, vs, as_ = _to_scan_layout(r), _to_scan_layout(v), _to_scan_layout(nkk)
    o_f, o_b = (_wkv_scan(rs, _to_scan_layout(w2, d), _to_scan_layout(k2, d), vs, as_, _to_scan_layout(b2, d),
                          n_lat=n_lat, reverse=(d == 1)) for d in range(2))
    return _rwkv_readout(_from_scan_layout(o_f, o_b, b), r, k2, v, g, p, x, modtab, n_lat=n_lat, rows=rows)


def _rope_tables(n_lat, n_ctx, head_dim):
    rows = jnp.repeat(jnp.arange(n_lat // GRID_W, dtype=I32), GRID_W).astype(F32)
    cols = jnp.tile(jnp.arange(GRID_W, dtype=I32), n_lat // GRID_W).astype(F32)
    n_freq = head_dim // 4
    inv_freq = ROPE_THETA ** (-jnp.arange(n_freq, dtype=F32) / n_freq)
    ang = jnp.concatenate([rows[:, None] * inv_freq, cols[:, None] * inv_freq], axis=-1)
    cos, sin = jnp.cos(ang), jnp.sin(ang)
    reps = 128 // head_dim
    cos = jnp.tile(jnp.concatenate([cos, cos], axis=-1), (1, reps))
    sin = jnp.tile(jnp.concatenate([-sin, sin], axis=-1), (1, reps))
    cos = jnp.concatenate([cos, jnp.ones((n_ctx, 128), F32)], axis=0)
    sin = jnp.concatenate([sin, jnp.zeros((n_ctx, 128), F32)], axis=0)
    return cos, sin


def kernel(x, c, ctx, c_ctx, mod_w, mod_b, a_w_qkv, a_w_o, a_q_norm, a_k_norm, b_w_qkv, b_w_o, b_sink,
           c_mu, c_w_rkv, c_w_o, c_w0, c_w1, c_w2, c_a0, c_a1, c_a2, c_g1, c_g2, c_k_k, c_k_a, c_r_k,
           c_ln_w, c_ln_b, router_w, ffn_w1, ffn_w3, ffn_w2, final_norm):
    b, n_lat, d = x.shape
    n_ctx = ctx.shape[1]
    t_all = n_lat + n_ctx
    depth = mod_w.shape[0]
    assert d == D_MODEL and n_ctx % ROW_TILE == 0 and n_lat % min(COMBINE_TILE, n_lat) == 0

    cond_rows = -(-(b + 1) // 8) * 8
    cond = jnp.zeros((cond_rows, d), F32).at[:b].set(c).at[b].set(c_ctx)
    mods = _mod_tables(cond, mod_w, mod_b).reshape(depth, cond_rows, 6, d)
    cos_a, sin_a = _rope_tables(n_lat, n_ctx, A_HEAD_DIM)
    cos_b, sin_b = _rope_tables(n_lat, n_ctx, B_HEAD_DIM)
    w1_all, w3_all, w2_all = _cast_bf16(ffn_w1), _cast_bf16(ffn_w3), _cast_bf16(ffn_w2)

    xs = jnp.concatenate([x, ctx], axis=1)
    for i in range(depth):
        last = i == depth - 1
        rows = n_lat if last else t_all
        kind, j = i % N_MIXERS, i // N_MIXERS
        modtab = jnp.stack([jnp.broadcast_to(mods[i, b], (b, 6, d)), mods[i, :b]], axis=1)
        if kind == 0:
            q, k, v = _qkv_proj(xs, modtab, a_w_qkv[j].astype(BF16), a_q_norm[j].reshape(1, -1),
                                a_k_norm[j].reshape(1, -1), cos_a, sin_a, kind=0, n_lat=n_lat)
            o = _global_attention(q, k, v, n_lat=n_lat, rows=rows)
            xs = _proj_residual(o, a_w_o[j].astype(BF16), xs, modtab, n_lat=n_lat, rows=rows)
        elif kind == 1:
            ones = jnp.ones((1, 128), F32)
            q, k, v = _qkv_proj(xs, modtab, b_w_qkv[j].astype(BF16), ones, ones, cos_b, sin_b,
                                kind=1, n_lat=n_lat)
            o = _window_attention(q, k, v, b_sink[j], n_lat=n_lat, rows=rows)
            xs = _proj_residual(o, b_w_o[j].astype(BF16), xs, modtab, n_lat=n_lat, rows=rows)
        else:
            p = dict(mu=c_mu[j], w_rkv=c_w_rkv[j].astype(BF16), w_o=c_w_o[j].astype(BF16), w0=c_w0[j],
                     w1=c_w1[j].astype(BF16), w2=c_w2[j].astype(BF16), a0=c_a0[j],
                     a1=c_a1[j].astype(BF16), a2=c_a2[j].astype(BF16), g1=c_g1[j].astype(BF16),
                     g2=c_g2[j].astype(BF16), k_k=c_k_k[j].reshape(1, d), k_a=c_k_a[j].reshape(1, d),
                     r_k=c_r_k[j].reshape(2, d), ln_w=c_ln_w[j].reshape(1, d), ln_b=c_ln_b[j].reshape(1, d))
            xs = _rwkv_mixer(xs, modtab, p, n_lat=n_lat, rows=rows)
        xs = _moe(xs, modtab, router_w[i].T, w1_all, w3_all, w2_all, i, final_norm, n_lat=n_lat,
                  with_ctx=not last)
    return xs
```

```python
import functools
import math

import jax
import jax.numpy as jnp
from jax import lax
from jax.experimental import pallas as pl
from jax.experimental.pallas import tpu as pltpu

F32 = jnp.float32
BF16 = jnp.bfloat16
I32 = jnp.int32

D_MODEL = 1024
GRID_W = 64
Q_BLOCK = 128
ROPE_THETA = 10000.0
NORM_EPS = 1e-6
NEG_INF = -1e30
A_HEADS, A_KV_HEADS, A_HEAD_DIM = 8, 2, 128
B_HEADS, B_KV_HEADS, B_HEAD_DIM = 16, 4, 64
WINDOW = 128
C_HEAD_DIM = 64
C_HEADS = D_MODEL // C_HEAD_DIM
C_GN_EPS = C_HEAD_DIM * 1e-5
N_EXPERTS = 16
EXPERT_FF = 2 * D_MODEL
CAPACITY_FACTOR = 2
N_MIXERS = 3
LOG2E = math.log2(math.e)

ROW_TILE = 256
COMBINE_TILE = 1024
COMBINE_EXPERTS = 8
SLOT_TILE = 128
SCAN_CHUNK = 32
VMEM_LIMIT = 56 * 1024 * 1024
FFN_VMEM_LIMIT = 60 * 1024 * 1024


def _params(*sem, vmem_limit=VMEM_LIMIT):
    return pltpu.CompilerParams(dimension_semantics=sem, vmem_limit_bytes=vmem_limit)


def _rms(x):
    return x * lax.rsqrt(jnp.mean(x * x, axis=-1, keepdims=True) + NORM_EPS)


def _normmod(x, mod_ref, shift_row, scale_row):
    return _rms(x) * (1.0 + mod_ref[scale_row:scale_row + 1, :]) + mod_ref[shift_row:shift_row + 1, :]


def _bdot(a, b):
    return jnp.dot(a.astype(BF16), b.astype(BF16), preferred_element_type=F32)


def _dot_nt(a, b):
    return lax.dot_general(a.astype(BF16), b.astype(BF16), (((1,), (1,)), ((), ())),
                           preferred_element_type=F32)


def _sigmoid(x):
    return 1.0 / (1.0 + jnp.exp(-x))


def _mod_spec(n_lat_tiles):
    return pl.BlockSpec((None, None, 6, D_MODEL),
                        lambda b, t: (b, jnp.where(t < n_lat_tiles, 1, 0), 0, 0))


def _mod_kernel(cond_ref, w_ref, b_ref, o_ref):
    c = cond_ref[...]
    a = c * _sigmoid(c)
    o_ref[0] = _bdot(a, w_ref[0]) + b_ref[0]


def _mod_tables(cond, mod_w, mod_b):
    depth, d, n = mod_w.shape
    rows = cond.shape[0]
    return pl.pallas_call(
        _mod_kernel,
        grid=(depth, n // d),
        in_specs=[pl.BlockSpec((rows, d), lambda i, j: (0, 0)),
                  pl.BlockSpec((1, d, d), lambda i, j: (i, 0, j)),
                  pl.BlockSpec((1, 1, d), lambda i, j: (i, 0, j))],
        out_specs=pl.BlockSpec((1, rows, d), lambda i, j: (i, 0, j)),
        out_shape=jax.ShapeDtypeStruct((depth, rows, n), F32),
        compiler_params=_params("arbitrary", "arbitrary"),
        name="mod_tables",
    )(cond, mod_w, mod_b.reshape(depth, 1, n))


def _rope128(x, cos, sin_signed, half):
    if half == 64:
        rot = pltpu.roll(x, 64, axis=1)
    else:
        lane = lax.broadcasted_iota(I32, x.shape, 1)
        rot = jnp.where((lane % 64) < 32, pltpu.roll(x, 96, axis=1), pltpu.roll(x, 32, axis=1))
    return x * cos + rot * sin_signed


def _qkv_kernel(x_ref, mod_ref, w_ref, gq_ref, gk_ref, cos_ref, sin_ref, q_ref, k_ref, v_ref, *, kind):
    h = _normmod(x_ref[...], mod_ref, 0, 1)
    y = _bdot(h, w_ref[...])
    cos, sin = cos_ref[...], sin_ref[...]
    nq = D_MODEL
    nkv = (y.shape[1] - nq) // 2
    half = 64 if kind == 0 else 32
    q_scale = (2 * half) ** -0.5 * LOG2E
    for j in range((nq + nkv) // 128):
        s = y[:, j * 128:(j + 1) * 128]
        if kind == 0:
            gain = gq_ref[...] if j < nq // 128 else gk_ref[...]
            s = _rms(s) * gain
        s = _rope128(s, cos, sin, half)
        if j < nq // 128:
            q_ref[:, j * 128:(j + 1) * 128] = (s * q_scale).astype(BF16)
        elif kind == 0:
            k_ref[:, (j - nq // 128) * 128:(j - nq // 128 + 1) * 128] = s.astype(BF16)
        else:
            jj = (j - nq // 128) * 2
            k_ref[jj] = s[:, :64].astype(BF16)
            k_ref[jj + 1] = s[:, 64:].astype(BF16)
    v = y[:, nq + nkv:].astype(BF16)
    if kind == 0:
        v_ref[...] = v
    else:
        for jj in range(nkv // 64):
            v_ref[jj] = v[:, jj * 64:(jj + 1) * 64]


def _qkv_proj(x, modtab, w, gq, gk, cos, sin, *, kind, n_lat):
    b, t_all, d = x.shape
    tm = ROW_TILE
    nt = t_all // tm
    n = w.shape[1]
    nkv = (n - d) // 2
    if kind == 0:
        kv_shape = jax.ShapeDtypeStruct((b, t_all, nkv), BF16)
        kv_spec = pl.BlockSpec((None, tm, nkv), lambda i, t: (i, t, 0))
    else:
        kv_shape = jax.ShapeDtypeStruct((b, nkv // 64, t_all, 64), BF16)
        kv_spec = pl.BlockSpec((None, nkv // 64, tm, 64), lambda i, t: (i, 0, t, 0))
    return pl.pallas_call(
        functools.partial(_qkv_kernel, kind=kind),
        grid=(b, nt),
        in_specs=[pl.BlockSpec((None, tm, d), lambda i, t: (i, t, 0)),
                  _mod_spec(n_lat // tm),
                  pl.BlockSpec((d, n), lambda i, t: (0, 0)),
                  pl.BlockSpec((1, 128), lambda i, t: (0, 0)),
                  pl.BlockSpec((1, 128), lambda i, t: (0, 0)),
                  pl.BlockSpec((tm, 128), lambda i, t: (t, 0)),
                  pl.BlockSpec((tm, 128), lambda i, t: (t, 0))],
        out_specs=[pl.BlockSpec((None, tm, d), lambda i, t: (i, t, 0)), kv_spec, kv_spec],
        out_shape=[jax.ShapeDtypeStruct((b, t_all, d), BF16), kv_shape, kv_shape],
        compiler_params=_params("parallel", "parallel"),
        name=f"qkv_proj_{kind}",
    )(x, modtab, w, gq, gk, cos, sin)


def _flash_chunk(carry, q, kc, vc):
    m, l, acc = carry
    s = _dot_nt(q, kc)
    m_new = jnp.maximum(m, jnp.max(s, axis=-1, keepdims=True))
    alpha = jnp.exp2(m - m_new)
    p = jnp.exp2(s - m_new)
    l = alpha * l + jnp.sum(p, axis=-1, keepdims=True)
    acc = alpha * acc + jnp.dot(p.astype(BF16), vc, preferred_element_type=F32)
    return m_new, l, acc


def _gattn_kernel(q_ref, k_ref, v_ref, o_ref, *, n_lat, kchunk):
    hd = A_HEAD_DIM
    g = A_HEADS // A_KV_HEADS
    tq = q_ref.shape[0]
    t_all = k_ref.shape[0]
    qs = jnp.concatenate([q_ref[:, i * hd:(i + 1) * hd] for i in range(g)], axis=0)
    init = (jnp.full((g * tq, 1), NEG_INF, F32), jnp.zeros((g * tq, 1), F32), jnp.zeros((g * tq, hd), F32))

    def finish(carry):
        _, l, acc = carry
        o = (acc / l).astype(BF16)
        for i in range(g):
            o_ref[:, i * hd:(i + 1) * hd] = o[i * tq:(i + 1) * tq]

    ctx_carry = _flash_chunk(init, qs, k_ref[n_lat:t_all, :], v_ref[n_lat:t_all, :])
    is_ctx = pl.program_id(2) >= n_lat // tq

    @pl.when(is_ctx)
    def _():
        finish(ctx_carry)

    @pl.when(jnp.logical_not(is_ctx))
    def _():
        def body(c, carry):
            start = pl.multiple_of(c * kchunk, kchunk)
            return _flash_chunk(carry, qs, k_ref[pl.ds(start, kchunk), :], v_ref[pl.ds(start, kchunk), :])
        finish(lax.fori_loop(0, n_lat // kchunk, body, ctx_carry))


def _global_attention(q, k, v, *, n_lat, rows):
    b, t_all, d = q.shape
    hd, g = A_HEAD_DIM, A_HEADS // A_KV_HEADS
    tq = ROW_TILE
    return pl.pallas_call(
        functools.partial(_gattn_kernel, n_lat=n_lat, kchunk=min(2048, n_lat)),
        grid=(b, A_KV_HEADS, rows // tq),
        in_specs=[pl.BlockSpec((None, tq, g * hd), lambda i, h, t: (i, t, h)),
                  pl.BlockSpec((None, t_all, hd), lambda i, h, t: (i, 0, h)),
                  pl.BlockSpec((None, t_all, hd), lambda i, h, t: (i, 0, h))],
        out_specs=pl.BlockSpec((None, tq, g * hd), lambda i, h, t: (i, t, h)),
        out_shape=jax.ShapeDtypeStruct((b, t_all, d), BF16),
        compiler_params=_params("parallel", "parallel", "arbitrary"),
        name="global_attention",
    )(q, k, v)


def _wattn_kernel(q_ref, k_ref, v_ref, sink_ref, bias_ref, o_ref, *, n_lat):
    hd = B_HEAD_DIM
    g = B_HEADS // B_KV_HEADS
    tq = q_ref.shape[0]
    t_all = k_ref.shape[1]
    t = pl.program_id(1)
    n_tiles = n_lat // tq
    is_ctx = t >= n_tiles

    def group(j):
        q = jnp.concatenate([q_ref[:, (j * g + i) * hd:(j * g + i + 1) * hd] for i in range(g)], axis=0)
        sink = jnp.concatenate([jnp.broadcast_to(sink_ref[j, :, i:i + 1], (tq, 1)) for i in range(g)],
                               axis=0) * LOG2E
        kc, vc = k_ref[j, n_lat:t_all, :], v_ref[j, n_lat:t_all, :]
        return q, sink, _dot_nt(q, kc), vc

    def finish(j, o):
        o = o.astype(BF16)
        o_ref[:, j * g * hd:(j + 1) * g * hd] = jnp.concatenate([o[i * tq:(i + 1) * tq] for i in range(g)], axis=1)

    @pl.when(is_ctx)
    def _():
        for j in range(B_KV_HEADS):
            q, sink, s_ctx, vc = group(j)
            m = jnp.maximum(jnp.max(s_ctx, axis=-1, keepdims=True), sink)
            p = jnp.exp2(s_ctx - m)
            l = jnp.sum(p, axis=-1, keepdims=True) + jnp.exp2(sink - m)
            finish(j, jnp.dot(p.astype(BF16), vc, preferred_element_type=F32) / l)

    @pl.when(jnp.logical_not(is_ctx))
    def _():
        wlen = tq + 2 * WINDOW
        ws = jnp.clip(t * tq - WINDOW, 0, n_lat - wlen)
        start = pl.multiple_of(ws, 128)
        bias = bias_ref[jnp.where(t == 0, 0, jnp.where(t == n_tiles - 1, 2, 1))]
        bias = jnp.concatenate([bias] * g, axis=0)
        for j in range(B_KV_HEADS):
            q, sink, _, _ = group(j)
            keys = jnp.concatenate([k_ref[j, pl.ds(start, wlen), :], k_ref[j, n_lat:t_all, :]], axis=0)
            vals = jnp.concatenate([v_ref[j, pl.ds(start, wlen), :], v_ref[j, n_lat:t_all, :]], axis=0)
            s = _dot_nt(q, keys) + bias
            m = jnp.maximum(jnp.max(s, axis=-1, keepdims=True), sink)
            p = jnp.exp2(s - m)
            l = jnp.sum(p, axis=-1, keepdims=True) + jnp.exp2(sink - m)
            finish(j, jnp.dot(p.astype(BF16), vals, preferred_element_type=F32) / l)


def _band_bias(tq, n_ctx):
    row = jnp.arange(tq, dtype=I32)[:, None]
    col = jnp.arange(tq + 2 * WINDOW, dtype=I32)[None, :]
    band = jnp.stack([jnp.where(jnp.abs(col - off - row) <= WINDOW, 0.0, NEG_INF).astype(F32)
                      for off in (0, WINDOW, 2 * WINDOW)])
    return jnp.concatenate([band, jnp.zeros((3, tq, n_ctx), F32)], axis=2)


def _window_attention(q, k, v, sink, *, n_lat, rows):
    b, t_all, d = q.shape
    hd, g = B_HEAD_DIM, B_HEADS // B_KV_HEADS
    tq = Q_BLOCK
    wlen = tq + 2 * WINDOW
    return pl.pallas_call(
        functools.partial(_wattn_kernel, n_lat=n_lat),
        grid=(b, rows // tq),
        in_specs=[pl.BlockSpec((None, tq, d), lambda i, t: (i, t, 0)),
                  pl.BlockSpec((None, B_KV_HEADS, t_all, hd), lambda i, t: (i, 0, 0, 0)),
                  pl.BlockSpec((None, B_KV_HEADS, t_all, hd), lambda i, t: (i, 0, 0, 0)),
                  pl.BlockSpec((B_KV_HEADS, 1, g), lambda i, t: (0, 0, 0)),
                  pl.BlockSpec((3, tq, wlen + t_all - n_lat), lambda i, t: (0, 0, 0))],
        out_specs=pl.BlockSpec((None, tq, d), lambda i, t: (i, t, 0)),
        out_shape=jax.ShapeDtypeStruct((b, t_all, d), BF16),
        compiler_params=_params("parallel", "arbitrary"),
        name="window_attention",
    )(q, k, v, sink.reshape(B_KV_HEADS, 1, g), _band_bias(tq, t_all - n_lat))


def _proj_res_kernel(a_ref, w_ref, x_ref, mod_ref, o_ref):
    o_ref[...] = x_ref[...] + mod_ref[2:3, :] * _bdot(a_ref[...], w_ref[...])


def _proj_residual(a, w, x, modtab, *, n_lat, rows):
    b, t_all, d = x.shape
    tm = ROW_TILE
    return pl.pallas_call(
        _proj_res_kernel,
        grid=(b, rows // tm),
        in_specs=[pl.BlockSpec((None, tm, d), lambda i, t: (i, t, 0)),
                  pl.BlockSpec((d, d), lambda i, t: (0, 0)),
                  pl.BlockSpec((None, tm, d), lambda i, t: (i, t, 0)),
                  _mod_spec(n_lat // tm)],
        out_specs=pl.BlockSpec((None, tm, d), lambda i, t: (i, t, 0)),
        out_shape=jax.ShapeDtypeStruct((b, rows, d), F32),
        compiler_params=_params("parallel", "parallel"),
        name="proj_residual",
    )(a, w, x, modtab)


def _router_kernel(x_ref, mod_ref, rwt_ref, h_ref, aff_ref, afft_ref):
    h = _normmod(x_ref[...], mod_ref, 3, 4)
    h_ref[...] = h.astype(BF16)
    logits_t = lax.dot_general(rwt_ref[...], h, (((1,), (1,)), ((), ())),
                               precision=lax.Precision.HIGHEST, preferred_element_type=F32)
    e = jnp.exp(logits_t - jnp.max(logits_t, axis=0, keepdims=True))
    aff_t = e / jnp.sum(e, axis=0, keepdims=True)
    afft_ref[...] = aff_t
    aff_ref[...] = aff_t.T


def _router(x, modtab, router_w_t, *, n_lat):
    b, rows, d = x.shape
    tm = ROW_TILE
    ne = router_w_t.shape[0]
    return pl.pallas_call(
        _router_kernel,
        grid=(b, rows // tm),
        in_specs=[pl.BlockSpec((None, tm, d), lambda i, t: (i, t, 0)),
                  _mod_spec(n_lat // tm),
                  pl.BlockSpec((ne, d), lambda i, t: (0, 0))],
        out_specs=[pl.BlockSpec((None, tm, d), lambda i, t: (i, t, 0)),
                   pl.BlockSpec((None, tm, ne), lambda i, t: (i, t, 0)),
                   pl.BlockSpec((None, ne, tm), lambda i, t: (i, 0, t))],
        out_shape=[jax.ShapeDtypeStruct((b, rows, d), BF16),
                   jax.ShapeDtypeStruct((b, rows, ne), F32),
                   jax.ShapeDtypeStruct((b, ne, rows), F32)],
        compiler_params=_params("parallel", "parallel"),
        name="router",
    )(x, modtab, router_w_t)


def _lane_prefix(mask_f, tri):
    n = mask_f.shape[1]
    run = jnp.zeros((mask_f.shape[0], 1), F32)
    out = []
    for j in range(n // 128):
        blk = mask_f[:, j * 128:(j + 1) * 128]
        incl = jnp.dot(blk.astype(BF16), tri, preferred_element_type=F32)
        out.append(incl - blk + run)
        run = run + incl[:, 127:128]
    return jnp.concatenate(out, axis=1)


def _select_kernel(afft_ref, slot_ref, slot_tm_ref, *, segments):
    r = lax.broadcasted_iota(I32, (128, 128), 0)
    c = lax.broadcasted_iota(I32, (128, 128), 1)
    tri = (r <= c).astype(BF16)
    for off, n, cap, base in segments:
        bits = lax.bitcast_convert_type(afft_ref[:, off:off + n], I32)
        cap_f = jnp.float32(cap)

        def body(i, thr):
            cand = thr | jnp.left_shift(jnp.int32(1), 30 - i)
            cnt = jnp.sum((bits >= cand).astype(F32), axis=1, keepdims=True)
            return jnp.where(cnt >= cap_f, cand, thr)

        thr = lax.fori_loop(0, 31, body, jnp.zeros((bits.shape[0], 1), I32))
        gt = (bits > thr).astype(F32)
        eq = (bits == thr).astype(F32)
        need = cap_f - jnp.sum(gt, axis=1, keepdims=True)
        sel = gt + eq * (_lane_prefix(eq, tri) < need).astype(F32)
        slot = _lane_prefix(sel, tri).astype(I32) + base
        slot = jnp.where(sel > 0.5, slot, -1)
        slot_ref[:, off:off + n] = slot
        slot_tm_ref[off:off + n, :] = slot.astype(F32).T


def _select(aff_t, segments):
    b, ne, rows = aff_t.shape
    return pl.pallas_call(
        functools.partial(_select_kernel, segments=segments),
        grid=(b,),
        in_specs=[pl.BlockSpec((None, ne, rows), lambda i: (i, 0, 0))],
        out_specs=[pl.BlockSpec((None, ne, rows), lambda i: (i, 0, 0)),
                   pl.BlockSpec((None, rows, ne), lambda i: (i, 0, 0))],
        out_shape=[jax.ShapeDtypeStruct((b, ne, rows), I32), jax.ShapeDtypeStruct((b, rows, ne), F32)],
        compiler_params=_params("parallel"),
        name="expert_select",
    )(aff_t)


def _gather_kernel(slot_ref, h_ref, xs_ref, *, segments):
    for off, n, cap, base in segments:
        slot = slot_ref[:, off:off + n]
        hseg = h_ref[off:off + n, :]
        st = min(SLOT_TILE, cap)
        for j in range(cap // st):
            ids = base + j * st + lax.broadcasted_iota(I32, (st, 1), 0)
            onehot = jnp.where(ids == slot, 1.0, 0.0).astype(BF16)
            rows = jnp.dot(onehot, hseg, preferred_element_type=F32)
            xs_ref[base + j * st:base + (j + 1) * st, :] = rows.astype(BF16)


def _gather(slot, h, segments, n_slots):
    b, ne, rows = slot.shape
    d = h.shape[2]
    return pl.pallas_call(
        functools.partial(_gather_kernel, segments=segments),
        grid=(b, ne),
        in_specs=[pl.BlockSpec((None, None, 1, rows), lambda i, e: (i, e, 0, 0)),
                  pl.BlockSpec((None, rows, d), lambda i, e: (i, 0, 0))],
        out_specs=pl.BlockSpec((None, None, n_slots, d), lambda i, e: (e, i, 0, 0)),
        out_shape=jax.ShapeDtypeStruct((ne, b, n_slots, d), BF16),
        compiler_params=_params("parallel", "arbitrary"),
        name="expert_gather",
    )(slot.reshape(b, ne, 1, rows), h)


def _ffn_kernel(xs_ref, w1_ref, w3_ref, w2_ref, y_ref):
    xs = xs_ref[...]
    a = _bdot(xs, w1_ref[...])
    g = _bdot(xs, w3_ref[...])
    hid = (a * _sigmoid(a)) * g
    y_ref[...] = _bdot(hid, w2_ref[...]).astype(BF16)


def _expert_ffn(xs, w1, w3, w2, layer):
    ne, b, n_slots, d = xs.shape
    ff = w1.shape[3]
    return pl.pallas_call(
        _ffn_kernel,
        grid=(ne, b),
        in_specs=[pl.BlockSpec((None, None, n_slots, d), lambda e, i: (e, i, 0, 0)),
                  pl.BlockSpec((None, None, d, ff), lambda e, i: (layer, e, 0, 0)),
                  pl.BlockSpec((None, None, d, ff), lambda e, i: (layer, e, 0, 0)),
                  pl.BlockSpec((None, None, ff, d), lambda e, i: (layer, e, 0, 0))],
        out_specs=pl.BlockSpec((None, None, n_slots, d), lambda e, i: (e, i, 0, 0)),
        out_shape=jax.ShapeDtypeStruct((ne, b, n_slots, d), BF16),
        compiler_params=_params("parallel", "arbitrary", vmem_limit=FFN_VMEM_LIMIT),
        name="expert_ffn",
    )(xs, w1, w3, w2)


def _combine_kernel(slot_ref, aff_ref, y_ref, x_ref, mod_ref, fw_ref, o_ref, acc_ref, *, slot0, final):
    eg = pl.program_id(2)
    n_groups = pl.num_programs(2)
    group, width = y_ref.shape[:2]

    @pl.when(eg == 0)
    def _():
        acc_ref[...] = jnp.zeros_like(acc_ref)

    lane_e = lax.broadcasted_iota(I32, aff_ref.shape, 1)
    ids = slot0 + lax.broadcasted_iota(I32, (1, width), 1)
    total = None
    for j in range(group):
        e = eg * group + j
        gate = jnp.sum(jnp.where(lane_e == e, aff_ref[...], 0.0), axis=1, keepdims=True)
        slot = jnp.sum(jnp.where(lane_e == e, slot_ref[...], 0.0), axis=1, keepdims=True).astype(I32)
        onehot = jnp.where(slot == ids, 1.0, 0.0).astype(BF16)
        part = gate * jnp.dot(onehot, y_ref[j], preferred_element_type=F32)
        total = part if total is None else total + part
    acc_ref[...] += total

    @pl.when(eg == n_groups - 1)
    def _():
        out = x_ref[...] + mod_ref[5:6, :] * acc_ref[...]
        o_ref[...] = _rms(out) * fw_ref[...] if final else out


def _combine(slot_tm, aff, y, x, modtab, final_w, *, row0, rows, tm, slot0, width, seg, in_place, final=False):
    b, _, d = x.shape
    ne = aff.shape[2]
    group = COMBINE_EXPERTS
    t0, s0 = row0 // tm, slot0 // width
    out_rows = x.shape[1] if in_place else rows
    return pl.pallas_call(
        functools.partial(_combine_kernel, slot0=slot0, final=final),
        grid=(b, rows // tm, ne // group),
        in_specs=[pl.BlockSpec((None, tm, ne), lambda i, t, e: (i, t + t0, 0)),
                  pl.BlockSpec((None, tm, ne), lambda i, t, e: (i, t + t0, 0)),
                  pl.BlockSpec((group, None, width, d), lambda i, t, e: (e, i, s0, 0)),
                  pl.BlockSpec((None, tm, d), lambda i, t, e: (i, t + t0, 0)),
                  pl.BlockSpec((None, None, 6, d), lambda i, t, e: (i, seg, 0, 0)),
                  pl.BlockSpec((1, d), lambda i, t, e: (0, 0))],
        out_specs=pl.BlockSpec((None, tm, d), lambda i, t, e: (i, t + (t0 if in_place else 0), 0)),
        out_shape=jax.ShapeDtypeStruct((b, out_rows, d), F32),
        scratch_shapes=[pltpu.VMEM((tm, d), F32)],
        input_output_aliases={3: 0} if in_place else {},
        compiler_params=_params("parallel", "parallel", "arbitrary"),
        name="expert_combine",
    )(slot_tm, aff, y, x, modtab, final_w.reshape(1, d))


def _moe(x, modtab, router_w_t, w1, w3, w2, layer, final_w, *, n_lat, with_ctx):
    b, rows, d = x.shape
    ne = N_EXPERTS
    cap_l = CAPACITY_FACTOR * n_lat // ne
    h, aff, aff_t = _router(x, modtab, router_w_t, n_lat=n_lat)
    segments = ((0, n_lat, cap_l, 0),)
    n_slots = cap_l
    if with_ctx:
        n_ctx = rows - n_lat
        cap_c = CAPACITY_FACTOR * n_ctx // ne
        segments += ((n_lat, n_ctx, cap_c, cap_l),)
        n_slots += cap_c
    slot, slot_tm = _select(aff_t, segments)
    xs = _gather(slot, h, segments, n_slots)
    y = _expert_ffn(xs, w1, w3, w2, layer)
    out = _combine(slot_tm, aff, y, x, modtab, final_w, row0=0, rows=n_lat, tm=min(COMBINE_TILE, n_lat),
                   slot0=0, width=cap_l, seg=1, in_place=with_ctx, final=not with_ctx)
    if with_ctx:
        out = _combine(slot_tm, aff, y, out, modtab, final_w, row0=n_lat, rows=n_ctx, tm=n_ctx, slot0=cap_l,
                       width=cap_c, seg=0, in_place=True)
    return out


def _segsum64(x, bd):
    hi = x.astype(BF16)
    lo = (x - hi.astype(F32)).astype(BF16)
    out = []
    for j in range(x.shape[1] // 256):
        sl = slice(j * 256, (j + 1) * 256)
        out.append(jnp.dot(hi[:, sl], bd, preferred_element_type=F32)
                   + jnp.dot(lo[:, sl], bd, preferred_element_type=F32))
    return jnp.concatenate(out, axis=1)


def _block_diag_ones():
    r = lax.broadcasted_iota(I32, (256, 256), 0)
    c = lax.broadcasted_iota(I32, (256, 256), 1)
    return (r // 64 == c // 64).astype(BF16)


def _softplus(x):
    return jnp.maximum(x, 0.0) + jnp.log(1.0 + jnp.exp(-jnp.abs(x)))


def _rwkv_feat_kernel(x_ref, xp_ref, xn_ref, mod_ref, mu_ref, wrkv_ref, w0_ref, w1_ref, w2_ref,
                      a0_ref, a1_ref, a2_ref, g1_ref, g2_ref, kk_ref, ka_ref,
                      r_out, v_out, nkk_out, g_out, w_out, k_out, b_out, *, n_lat_tiles):
    t = pl.program_id(1)
    nt = pl.num_programs(1)
    tm = x_ref.shape[0]
    h = _normmod(x_ref[...], mod_ref, 0, 1)
    hp = _normmod(xp_ref[7:8, :], mod_ref, 0, 1)
    hn = _normmod(xn_ref[0:1, :], mod_ref, 0, 1)
    has_left = jnp.logical_and(t != 0, t != n_lat_tiles)
    has_right = jnp.logical_and(t != n_lat_tiles - 1, t != nt - 1)
    hp = jnp.where(has_left, hp, 0.0)
    hn = jnp.where(has_right, hn, 0.0)
    row = lax.broadcasted_iota(I32, h.shape, 0)
    left = jnp.where(row == 0, hp, pltpu.roll(h, 1, axis=0))
    right = jnp.where(row == tm - 1, hn, pltpu.roll(h, tm - 1, axis=0))
    xx = 0.5 * (left + right) - h
    xr, xw, xk, xv, xa, xg = (h + xx * mu_ref[i:i + 1, :] for i in range(6))
    r = _bdot(xr, wrkv_ref[0])
    k = _bdot(xk, wrkv_ref[1])
    v = _bdot(xv, wrkv_ref[2])
    g = _bdot(_sigmoid(_bdot(xg, g1_ref[...])), g2_ref[...])
    bd = _block_diag_ones()
    kk = k * kk_ref[...]
    kk = kk * lax.rsqrt(jnp.maximum(_segsum64(kk * kk, bd), 1e-24))
    r_out[...] = r
    v_out[...] = v
    nkk_out[...] = -kk
    g_out[...] = g
    for d in range(2):
        w_lora = _bdot(jnp.tanh(_bdot(xw, w1_ref[d])), w2_ref[d])
        log_w = -_softplus(-(w0_ref[d:d + 1, :] + w_lora)) - 0.5
        w_out[d] = jnp.exp(-jnp.exp(log_w))
        a = _sigmoid(a0_ref[d:d + 1, :] + _bdot(_bdot(xa, a1_ref[d]), a2_ref[d]))
        k_out[d] = k * (1.0 + (a - 1.0) * ka_ref[...])
        b_out[d] = kk * a


def _rwkv_features(x, modtab, p, *, n_lat):
    b, t_all, d = x.shape
    tm = ROW_TILE
    nt = t_all // tm
    tb = tm // 8
    full = lambda shape: pl.BlockSpec(shape, lambda i, t: (0,) * len(shape))
    tok = pl.BlockSpec((None, tm, d), lambda i, t: (i, t, 0))
    tok2 = pl.BlockSpec((2, None, tm, d), lambda i, t: (0, i, t, 0))
    one = jax.ShapeDtypeStruct((b, t_all, d), F32)
    two = jax.ShapeDtypeStruct((2, b, t_all, d), F32)
    return pl.pallas_call(
        functools.partial(_rwkv_feat_kernel, n_lat_tiles=n_lat // tm),
        grid=(b, nt),
        in_specs=[tok,
                  pl.BlockSpec((None, 8, d), lambda i, t: (i, jnp.maximum(t * tb - 1, 0), 0)),
                  pl.BlockSpec((None, 8, d), lambda i, t: (i, jnp.minimum((t + 1) * tb, nt * tb - 1), 0)),
                  _mod_spec(n_lat // tm),
                  full((6, d)), full((3, d, d)), full((2, d)), full(p["w1"].shape), full(p["w2"].shape),
                  full((2, d)), full(p["a1"].shape), full(p["a2"].shape), full(p["g1"].shape),
                  full(p["g2"].shape), full((1, d)), full((1, d))],
        out_specs=[tok, tok, tok, tok, tok2, tok2, tok2],
        out_shape=[one, one, one, one, two, two, two],
        compiler_params=_params("parallel", "parallel"),
        name="rwkv_features",
    )(x, x, x, modtab, p["mu"], p["w_rkv"], p["w0"], p["w1"], p["w2"], p["a0"], p["a1"], p["a2"],
      p["g1"], p["g2"], p["k_k"], p["k_a"])


def _scan_kernel(r_ref, w_ref, k_ref, v_ref, a_ref, b_ref, o_ref, s_ref, wr_ref, *, reverse):
    n = s_ref.shape[0]
    steps = r_ref.shape[0]

    @pl.when(pl.program_id(0) == 0)
    def _():
        s_ref[...] = jnp.zeros_like(s_ref)

    def step(i, carry):
        j = steps - 1 - i if reverse else i
        r = r_ref[j]
        wr_ref[...] = w_ref[j] * r
        br = jnp.sum(b_ref[j] * r, axis=0, keepdims=True)
        kr = jnp.sum(k_ref[j] * r, axis=0, keepdims=True)
        acc = [jnp.zeros(s_ref.shape[1:], F32) for _ in range(4)]
        for kk in range(n):
            s = s_ref[kk]
            acc[kk % 2] = acc[kk % 2] + s * a_ref[j, kk:kk + 1, :]
            acc[2 + kk % 2] = acc[2 + kk % 2] + s * wr_ref[kk:kk + 1, :]
        sa = acc[0] + acc[1]
        v = v_ref[j]
        o_ref[j] = acc[2] + acc[3] + sa * br + v * kr
        for kk in range(n):
            s_ref[kk] = (s_ref[kk] * w_ref[j, kk:kk + 1, :] + sa * b_ref[j, kk:kk + 1, :]
                         + v * k_ref[j, kk:kk + 1, :])
        return carry

    lax.fori_loop(0, steps, step, 0)


def _wkv_scan(r, w, k, v, a, b, *, n_lat, reverse):
    t_all, n, chains = r.shape
    tc = SCAN_CHUNK
    nlc, nch = n_lat // tc, t_all // tc
    ncc = nch - nlc
    if reverse:
        idx = lambda c: (jnp.where(c < ncc, nch - 1 - c, nlc - 1 - (c - ncc)), 0, 0)
    else:
        idx = lambda c: (jnp.where(c < ncc, nlc + c, c - ncc), 0, 0)
    spec = pl.BlockSpec((tc, n, chains), idx)
    return pl.pallas_call(
        functools.partial(_scan_kernel, reverse=reverse),
        grid=(nch,),
        in_specs=[spec] * 6,
        out_specs=spec,
        out_shape=jax.ShapeDtypeStruct((t_all, n, chains), F32),
        scratch_shapes=[pltpu.VMEM((n, n, chains), F32), pltpu.VMEM((n, chains), F32)],
        compiler_params=_params("arbitrary"),
        name="wkv_scan_bwd" if reverse else "wkv_scan_fwd",
    )(r, w, k, v, a, b)


def _rwkv_out_kernel(o_ref, r_ref, k_ref, v_ref, g_ref, rk_ref, lnw_ref, lnb_ref, wo_ref, x_ref, mod_ref,
                     out_ref):
    bd = _block_diag_ones()
    inv_n = 1.0 / C_HEAD_DIM
    o = o_ref[...]
    o = o - _segsum64(o, bd) * inv_n
    o = o * lax.rsqrt(_segsum64(o * o, bd) * inv_n + C_GN_EPS)
    o = o * lnw_ref[...] + lnb_ref[...]
    r = r_ref[...]
    bonus = _segsum64(r * k_ref[0] * rk_ref[0:1, :] + r * k_ref[1] * rk_ref[1:2, :], bd) * v_ref[...]
    y = _bdot((o + bonus) * g_ref[...], wo_ref[...])
    out_ref[...] = x_ref[...] + mod_ref[2:3, :] * y


def _rwkv_readout(o, r, k2, v, g, p, x, modtab, *, n_lat, rows):
    b, t_all, d = x.shape
    tm = ROW_TILE
    tok = pl.BlockSpec((None, tm, d), lambda i, t: (i, t, 0))
    full = lambda shape: pl.BlockSpec(shape, lambda i, t: (0,) * len(shape))
    return pl.pallas_call(
        _rwkv_out_kernel,
        grid=(b, rows // tm),
        in_specs=[tok, tok, pl.BlockSpec((2, None, tm, d), lambda i, t: (0, i, t, 0)), tok, tok,
                  full((2, d)), full((1, d)), full((1, d)), full((d, d)), tok, _mod_spec(n_lat // tm)],
        out_specs=tok,
        out_shape=jax.ShapeDtypeStruct((b, rows, d), F32),
        compiler_params=_params("parallel", "parallel"),
        name="rwkv_readout",
    )(o, r, k2, v, g, p["r_k"], p["ln_w"], p["ln_b"], p["w_o"], x, modtab)


def _to_scan_kernel(x_ref, o_ref, y_ref):
    nb, tt, d = x_ref.shape
    n, chains = o_ref.shape[1:]
    for b in range(nb):
        for p in range(d // 128):
            row = (b * (d // 128) + p) * 128
            y_ref[row:row + 128, :] = x_ref[b, :, p * 128:(p + 1) * 128].T
    for k in range(n):
        o_ref[:, k, :] = y_ref[pl.ds(k, chains, stride=n), :].T


def _to_scan_layout(a, d=None):
    b, t, dm = a.shape[-3:]
    tt = 128
    chains = b * dm // C_HEAD_DIM
    if d is None:
        spec = pl.BlockSpec((b, tt, dm), lambda i: (0, i, 0))
    else:
        spec = pl.BlockSpec((None, b, tt, dm), lambda i: (d, 0, i, 0))
    return pl.pallas_call(
        _to_scan_kernel,
        grid=(t // tt,),
        in_specs=[spec],
        out_specs=pl.BlockSpec((tt, C_HEAD_DIM, chains), lambda i: (i, 0, 0)),
        out_shape=jax.ShapeDtypeStruct((t, C_HEAD_DIM, chains), F32),
        scratch_shapes=[pltpu.VMEM((b * dm, tt), F32)],
        compiler_params=_params("parallel"),
        name="to_scan_layout",
    )(a)


def _from_scan_kernel(a_ref, b_ref, o_ref, y_ref):
    nb, tt, d = o_ref.shape
    n, chains = a_ref.shape[1:]
    for k in range(n):
        y_ref[pl.ds(k, chains, stride=n), :] = (a_ref[:, k, :] + b_ref[:, k, :]).T
    for b in range(nb):
        for p in range(d // 128):
            row = (b * (d // 128) + p) * 128
            o_ref[b, :, p * 128:(p + 1) * 128] = y_ref[row:row + 128, :].T


def _from_scan_layout(o_f, o_b, b):
    t, n, chains = o_f.shape
    tt = 128
    dm = n * chains // b
    spec = pl.BlockSpec((tt, n, chains), lambda i: (i, 0, 0))
    return pl.pallas_call(
        _from_scan_kernel,
        grid=(t // tt,),
        in_specs=[spec, spec],
        out_specs=pl.BlockSpec((b, tt, dm), lambda i: (0, i, 0)),
        out_shape=jax.ShapeDtypeStruct((b, t, dm), F32),
        scratch_shapes=[pltpu.VMEM((b * dm, tt), F32)],
        compiler_params=_params("parallel"),
        name="from_scan_layout",
    )(o_f, o_b)


def _rwkv_mixer(x, modtab, p, *, n_lat, rows):
    b = x.shape[0]
    r, v, nkk, g, w2, k2, b2 = _rwkv_features(x, modtab, p, n_lat=n_lat)
    rs, vs, as_ = _to_scan_layout(r), _to_scan_layout(v), _to_scan_layout(nkk)
    o_f, o_b = (_wkv_scan(rs, _to_scan_layout(w2, d), _to_scan_layout(k2, d), vs, as_, _to_scan_layout(b2, d),
                          n_lat=n_lat, reverse=(d == 1)) for d in range(2))
    return _rwkv_readout(_from_scan_layout(o_f, o_b, b), r, k2, v, g, p, x, modtab, n_lat=n_lat, rows=rows)


def _rope_tables(n_lat, n_ctx, head_dim):
    rows = jnp.repeat(jnp.arange(n_lat // GRID_W, dtype=I32), GRID_W).astype(F32)
    cols = jnp.tile(jnp.arange(GRID_W, dtype=I32), n_lat // GRID_W).astype(F32)
    n_freq = head_dim // 4
    inv_freq = ROPE_THETA ** (-jnp.arange(n_freq, dtype=F32) / n_freq)
    ang = jnp.concatenate([rows[:, None] * inv_freq, cols[:, None] * inv_freq], axis=-1)
    cos, sin = jnp.cos(ang), jnp.sin(ang)
    reps = 128 // head_dim
    cos = jnp.tile(jnp.concatenate([cos, cos], axis=-1), (1, reps))
    sin = jnp.tile(jnp.concatenate([-sin, sin], axis=-1), (1, reps))
    cos = jnp.concatenate([cos, jnp.ones((n_ctx, 128), F32)], axis=0)
    sin = jnp.concatenate([sin, jnp.zeros((n_ctx, 128), F32)], axis=0)
    return cos, sin


def kernel(x, c, ctx, c_ctx, mod_w, mod_b, a_w_qkv, a_w_o, a_q_norm, a_k_norm, b_w_qkv, b_w_o, b_sink,
           c_mu, c_w_rkv, c_w_o, c_w0, c_w1, c_w2, c_a0, c_a1, c_a2, c_g1, c_g2, c_k_k, c_k_a, c_r_k,
           c_ln_w, c_ln_b, router_w, ffn_w1, ffn_w3, ffn_w2, final_norm):
    b, n_lat, d = x.shape
    n_ctx = ctx.shape[1]
    t_all = n_lat + n_ctx
    depth = mod_w.shape[0]
    assert d == D_MODEL and n_ctx % ROW_TILE == 0 and n_lat % min(COMBINE_TILE, n_lat) == 0

    cond_rows = -(-(b + 1) // 8) * 8
    cond = jnp.zeros((cond_rows, d), F32).at[:b].set(c).at[b].set(c_ctx)
    mods = _mod_tables(cond, mod_w, mod_b).reshape(depth, cond_rows, 6, d)
    cos_a, sin_a = _rope_tables(n_lat, n_ctx, A_HEAD_DIM)
    cos_b, sin_b = _rope_tables(n_lat, n_ctx, B_HEAD_DIM)
    xs = jnp.concatenate([x, ctx], axis=1)
    for i in range(depth):
        last = i == depth - 1
        rows = n_lat if last else t_all
        kind, j = i % N_MIXERS, i // N_MIXERS
        modtab = jnp.stack([jnp.broadcast_to(mods[i, b], (b, 6, d)), mods[i, :b]], axis=1)
        if kind == 0:
            q, k, v = _qkv_proj(xs, modtab, a_w_qkv[j].astype(BF16), a_q_norm[j].reshape(1, -1),
                                a_k_norm[j].reshape(1, -1), cos_a, sin_a, kind=0, n_lat=n_lat)
            o = _global_attention(q, k, v, n_lat=n_lat, rows=rows)
            xs = _proj_residual(o, a_w_o[j].astype(BF16), xs, modtab, n_lat=n_lat, rows=rows)
        elif kind == 1:
            ones = jnp.ones((1, 128), F32)
            q, k, v = _qkv_proj(xs, modtab, b_w_qkv[j].astype(BF16), ones, ones, cos_b, sin_b,
                                kind=1, n_lat=n_lat)
            o = _window_attention(q, k, v, b_sink[j], n_lat=n_lat, rows=rows)
            xs = _proj_residual(o, b_w_o[j].astype(BF16), xs, modtab, n_lat=n_lat, rows=rows)
        else:
            p = dict(mu=c_mu[j], w_rkv=c_w_rkv[j].astype(BF16), w_o=c_w_o[j].astype(BF16), w0=c_w0[j],
                     w1=c_w1[j].astype(BF16), w2=c_w2[j].astype(BF16), a0=c_a0[j],
                     a1=c_a1[j].astype(BF16), a2=c_a2[j].astype(BF16), g1=c_g1[j].astype(BF16),
                     g2=c_g2[j].astype(BF16), k_k=c_k_k[j].reshape(1, d), k_a=c_k_a[j].reshape(1, d),
                     r_k=c_r_k[j].reshape(2, d), ln_w=c_ln_w[j].reshape(1, d), ln_b=c_ln_b[j].reshape(1, d))
            xs = _rwkv_mixer(xs, modtab, p, n_lat=n_lat, rows=rows)
        xs = _moe(xs, modtab, router_w[i].T, ffn_w1, ffn_w3, ffn_w2, i, final_norm, n_lat=n_lat,
                  with_ctx=not last)
    return xs
```

```python
import functools
import math

import jax
import jax.numpy as jnp
from jax import lax
from jax.experimental import pallas as pl
from jax.experimental.pallas import tpu as pltpu

F32 = jnp.float32
BF16 = jnp.bfloat16
I32 = jnp.int32

D_MODEL = 1024
GRID_W = 64
Q_BLOCK = 128
ROPE_THETA = 10000.0
NORM_EPS = 1e-6
NEG_INF = -1e30
A_HEADS, A_KV_HEADS, A_HEAD_DIM = 8, 2, 128
B_HEADS, B_KV_HEADS, B_HEAD_DIM = 16, 4, 64
WINDOW = 128
C_HEAD_DIM = 64
C_HEADS = D_MODEL // C_HEAD_DIM
C_GN_EPS = C_HEAD_DIM * 1e-5
N_EXPERTS = 16
EXPERT_FF = 2 * D_MODEL
CAPACITY_FACTOR = 2
N_MIXERS = 3
LOG2E = math.log2(math.e)

ROW_TILE = 256
COMBINE_TILE = 1024
COMBINE_EXPERTS = 8
SLOT_TILE = 128
SCAN_CHUNK = 32
VMEM_LIMIT = 56 * 1024 * 1024
FFN_VMEM_LIMIT = 60 * 1024 * 1024


def _params(*sem, vmem_limit=VMEM_LIMIT):
    return pltpu.CompilerParams(dimension_semantics=sem, vmem_limit_bytes=vmem_limit)


def _rms(x):
    return x * lax.rsqrt(jnp.mean(x * x, axis=-1, keepdims=True) + NORM_EPS)


def _normmod(x, mod_ref, shift_row, scale_row):
    return _rms(x) * (1.0 + mod_ref[scale_row:scale_row + 1, :]) + mod_ref[shift_row:shift_row + 1, :]


def _bdot(a, b):
    return jnp.dot(a.astype(BF16), b.astype(BF16), preferred_element_type=F32)


def _dot_nt(a, b):
    return lax.dot_general(a.astype(BF16), b.astype(BF16), (((1,), (1,)), ((), ())),
                           preferred_element_type=F32)


def _sigmoid(x):
    return 1.0 / (1.0 + jnp.exp(-x))


def _mod_spec(n_lat_tiles):
    return pl.BlockSpec((None, None, 6, D_MODEL),
                        lambda b, t: (b, jnp.where(t < n_lat_tiles, 1, 0), 0, 0))


def _mod_kernel(cond_ref, w_ref, b_ref, o_ref):
    c = cond_ref[...]
    a = c * _sigmoid(c)
    o_ref[0] = _bdot(a, w_ref[0]) + b_ref[0]


def _mod_tables(cond, mod_w, mod_b):
    depth, d, n = mod_w.shape
    rows = cond.shape[0]
    return pl.pallas_call(
        _mod_kernel,
        grid=(depth, n // d),
        in_specs=[pl.BlockSpec((rows, d), lambda i, j: (0, 0)),
                  pl.BlockSpec((1, d, d), lambda i, j: (i, 0, j)),
                  pl.BlockSpec((1, 1, d), lambda i, j: (i, 0, j))],
        out_specs=pl.BlockSpec((1, rows, d), lambda i, j: (i, 0, j)),
        out_shape=jax.ShapeDtypeStruct((depth, rows, n), F32),
        compiler_params=_params("arbitrary", "arbitrary"),
        name="mod_tables",
    )(cond, mod_w, mod_b.reshape(depth, 1, n))


def _rope128(x, cos, sin_signed, half):
    if half == 64:
        rot = pltpu.roll(x, 64, axis=1)
    else:
        lane = lax.broadcasted_iota(I32, x.shape, 1)
        rot = jnp.where((lane % 64) < 32, pltpu.roll(x, 96, axis=1), pltpu.roll(x, 32, axis=1))
    return x * cos + rot * sin_signed


def _qkv_kernel(x_ref, mod_ref, w_ref, gq_ref, gk_ref, cos_ref, sin_ref, q_ref, k_ref, v_ref, *, kind):
    h = _normmod(x_ref[...], mod_ref, 0, 1)
    y = _bdot(h, w_ref[...])
    cos, sin = cos_ref[...], sin_ref[...]
    nq = D_MODEL
    nkv = (y.shape[1] - nq) // 2
    half = 64 if kind == 0 else 32
    q_scale = (2 * half) ** -0.5 * LOG2E
    for j in range((nq + nkv) // 128):
        s = y[:, j * 128:(j + 1) * 128]
        if kind == 0:
            gain = gq_ref[...] if j < nq // 128 else gk_ref[...]
            s = _rms(s) * gain
        s = _rope128(s, cos, sin, half)
        if j < nq // 128:
            q_ref[:, j * 128:(j + 1) * 128] = (s * q_scale).astype(BF16)
        elif kind == 0:
            k_ref[:, (j - nq // 128) * 128:(j - nq // 128 + 1) * 128] = s.astype(BF16)
        else:
            jj = (j - nq // 128) * 2
            k_ref[jj] = s[:, :64].astype(BF16)
            k_ref[jj + 1] = s[:, 64:].astype(BF16)
    v = y[:, nq + nkv:].astype(BF16)
    if kind == 0:
        v_ref[...] = v
    else:
        for jj in range(nkv // 64):
            v_ref[jj] = v[:, jj * 64:(jj + 1) * 64]


def _qkv_proj(x, modtab, w, gq, gk, cos, sin, *, kind, n_lat):
    b, t_all, d = x.shape
    tm = ROW_TILE
    nt = t_all // tm
    n = w.shape[1]
    nkv = (n - d) // 2
    if kind == 0:
        kv_shape = jax.ShapeDtypeStruct((b, t_all, nkv), BF16)
        kv_spec = pl.BlockSpec((None, tm, nkv), lambda i, t: (i, t, 0))
    else:
        kv_shape = jax.ShapeDtypeStruct((b, nkv // 64, t_all, 64), BF16)
        kv_spec = pl.BlockSpec((None, nkv // 64, tm, 64), lambda i, t: (i, 0, t, 0))
    return pl.pallas_call(
        functools.partial(_qkv_kernel, kind=kind),
        grid=(b, nt),
        in_specs=[pl.BlockSpec((None, tm, d), lambda i, t: (i, t, 0)),
                  _mod_spec(n_lat // tm),
                  pl.BlockSpec((d, n), lambda i, t: (0, 0)),
                  pl.BlockSpec((1, 128), lambda i, t: (0, 0)),
                  pl.BlockSpec((1, 128), lambda i, t: (0, 0)),
                  pl.BlockSpec((tm, 128), lambda i, t: (t, 0)),
                  pl.BlockSpec((tm, 128), lambda i, t: (t, 0))],
        out_specs=[pl.BlockSpec((None, tm, d), lambda i, t: (i, t, 0)), kv_spec, kv_spec],
        out_shape=[jax.ShapeDtypeStruct((b, t_all, d), BF16), kv_shape, kv_shape],
        compiler_params=_params("parallel", "parallel"),
        name=f"qkv_proj_{kind}",
    )(x, modtab, w, gq, gk, cos, sin)


def _flash_chunk(carry, q, kc, vc):
    m, l, acc = carry
    s = _dot_nt(q, kc)
    m_new = jnp.maximum(m, jnp.max(s, axis=-1, keepdims=True))
    alpha = jnp.exp2(m - m_new)
    p = jnp.exp2(s - m_new)
    l = alpha * l + jnp.sum(p, axis=-1, keepdims=True)
    acc = alpha * acc + jnp.dot(p.astype(BF16), vc, preferred_element_type=F32)
    return m_new, l, acc


def _gattn_kernel(q_ref, k_ref, v_ref, o_ref, *, n_lat, kchunk):
    hd = A_HEAD_DIM
    g = A_HEADS // A_KV_HEADS
    tq = q_ref.shape[0]
    t_all = k_ref.shape[0]
    groups = range(A_KV_HEADS)
    qs = [jnp.concatenate([q_ref[:, (j * g + i) * hd:(j * g + i + 1) * hd] for i in range(g)], axis=0)
          for j in groups]
    init = (jnp.full((g * tq, 1), NEG_INF, F32), jnp.zeros((g * tq, 1), F32), jnp.zeros((g * tq, hd), F32))

    def chunk(j, carry, rows):
        return _flash_chunk(carry, qs[j], k_ref[rows, j * hd:(j + 1) * hd], v_ref[rows, j * hd:(j + 1) * hd])

    def finish(carries):
        for j, (_, l, acc) in enumerate(carries):
            o = (acc / l).astype(BF16)
            for i in range(g):
                o_ref[:, (j * g + i) * hd:(j * g + i + 1) * hd] = o[i * tq:(i + 1) * tq]

    ctx_carries = tuple(chunk(j, init, slice(n_lat, t_all)) for j in groups)
    is_ctx = pl.program_id(1) >= n_lat // tq

    @pl.when(is_ctx)
    def _():
        finish(ctx_carries)

    @pl.when(jnp.logical_not(is_ctx))
    def _():
        def body(c, carries):
            rows = pl.ds(pl.multiple_of(c * kchunk, kchunk), kchunk)
            return tuple(chunk(j, carries[j], rows) for j in groups)
        finish(lax.fori_loop(0, n_lat // kchunk, body, ctx_carries))


def _global_attention(q, k, v, *, n_lat, rows):
    b, t_all, d = q.shape
    nkv = k.shape[2]
    tq = ROW_TILE
    return pl.pallas_call(
        functools.partial(_gattn_kernel, n_lat=n_lat, kchunk=min(2048, n_lat)),
        grid=(b, rows // tq),
        in_specs=[pl.BlockSpec((None, tq, d), lambda i, t: (i, t, 0)),
                  pl.BlockSpec((None, t_all, nkv), lambda i, t: (i, 0, 0)),
                  pl.BlockSpec((None, t_all, nkv), lambda i, t: (i, 0, 0))],
        out_specs=pl.BlockSpec((None, tq, d), lambda i, t: (i, t, 0)),
        out_shape=jax.ShapeDtypeStruct((b, t_all, d), BF16),
        compiler_params=_params("parallel", "arbitrary"),
        name="global_attention",
    )(q, k, v)


def _wattn_kernel(q_ref, k_ref, v_ref, sink_ref, bias_ref, o_ref, *, n_lat):
    hd = B_HEAD_DIM
    g = B_HEADS // B_KV_HEADS
    tq = q_ref.shape[0]
    t_all = k_ref.shape[1]
    t = pl.program_id(1)
    n_tiles = n_lat // tq
    is_ctx = t >= n_tiles

    def group(j):
        q = jnp.concatenate([q_ref[:, (j * g + i) * hd:(j * g + i + 1) * hd] for i in range(g)], axis=0)
        sink = jnp.concatenate([jnp.broadcast_to(sink_ref[j, :, i:i + 1], (tq, 1)) for i in range(g)],
                               axis=0) * LOG2E
        kc, vc = k_ref[j, n_lat:t_all, :], v_ref[j, n_lat:t_all, :]
        return q, sink, _dot_nt(q, kc), vc

    def finish(j, o):
        o = o.astype(BF16)
        o_ref[:, j * g * hd:(j + 1) * g * hd] = jnp.concatenate([o[i * tq:(i + 1) * tq] for i in range(g)], axis=1)

    @pl.when(is_ctx)
    def _():
        for j in range(B_KV_HEADS):
            q, sink, s_ctx, vc = group(j)
            m = jnp.maximum(jnp.max(s_ctx, axis=-1, keepdims=True), sink)
            p = jnp.exp2(s_ctx - m)
            l = jnp.sum(p, axis=-1, keepdims=True) + jnp.exp2(sink - m)
            finish(j, jnp.dot(p.astype(BF16), vc, preferred_element_type=F32) / l)

    @pl.when(jnp.logical_not(is_ctx))
    def _():
        wlen = tq + 2 * WINDOW
        ws = jnp.clip(t * tq - WINDOW, 0, n_lat - wlen)
        start = pl.multiple_of(ws, 128)
        bias = bias_ref[jnp.where(t == 0, 0, jnp.where(t == n_tiles - 1, 2, 1))]
        bias = jnp.concatenate([bias] * g, axis=0)
        for j in range(B_KV_HEADS):
            q, sink, _, _ = group(j)
            keys = jnp.concatenate([k_ref[j, pl.ds(start, wlen), :], k_ref[j, n_lat:t_all, :]], axis=0)
            vals = jnp.concatenate([v_ref[j, pl.ds(start, wlen), :], v_ref[j, n_lat:t_all, :]], axis=0)
            s = _dot_nt(q, keys) + bias
            m = jnp.maximum(jnp.max(s, axis=-1, keepdims=True), sink)
            p = jnp.exp2(s - m)
            l = jnp.sum(p, axis=-1, keepdims=True) + jnp.exp2(sink - m)
            finish(j, jnp.dot(p.astype(BF16), vals, preferred_element_type=F32) / l)


def _band_bias(tq, n_ctx):
    row = jnp.arange(tq, dtype=I32)[:, None]
    col = jnp.arange(tq + 2 * WINDOW, dtype=I32)[None, :]
    band = jnp.stack([jnp.where(jnp.abs(col - off - row) <= WINDOW, 0.0, NEG_INF).astype(F32)
                      for off in (0, WINDOW, 2 * WINDOW)])
    return jnp.concatenate([band, jnp.zeros((3, tq, n_ctx), F32)], axis=2)


def _window_attention(q, k, v, sink, *, n_lat, rows):
    b, t_all, d = q.shape
    hd, g = B_HEAD_DIM, B_HEADS // B_KV_HEADS
    tq = Q_BLOCK
    wlen = tq + 2 * WINDOW
    return pl.pallas_call(
        functools.partial(_wattn_kernel, n_lat=n_lat),
        grid=(b, rows // tq),
        in_specs=[pl.BlockSpec((None, tq, d), lambda i, t: (i, t, 0)),
                  pl.BlockSpec((None, B_KV_HEADS, t_all, hd), lambda i, t: (i, 0, 0, 0)),
                  pl.BlockSpec((None, B_KV_HEADS, t_all, hd), lambda i, t: (i, 0, 0, 0)),
                  pl.BlockSpec((B_KV_HEADS, 1, g), lambda i, t: (0, 0, 0)),
                  pl.BlockSpec((3, tq, wlen + t_all - n_lat), lambda i, t: (0, 0, 0))],
        out_specs=pl.BlockSpec((None, tq, d), lambda i, t: (i, t, 0)),
        out_shape=jax.ShapeDtypeStruct((b, t_all, d), BF16),
        compiler_params=_params("parallel", "arbitrary"),
        name="window_attention",
    )(q, k, v, sink.reshape(B_KV_HEADS, 1, g), _band_bias(tq, t_all - n_lat))


def _proj_res_kernel(a_ref, w_ref, x_ref, mod_ref, o_ref):
    o_ref[...] = x_ref[...] + mod_ref[2:3, :] * _bdot(a_ref[...], w_ref[...])


def _proj_residual(a, w, x, modtab, *, n_lat, rows):
    b, t_all, d = x.shape
    tm = ROW_TILE
    return pl.pallas_call(
        _proj_res_kernel,
        grid=(b, rows // tm),
        in_specs=[pl.BlockSpec((None, tm, d), lambda i, t: (i, t, 0)),
                  pl.BlockSpec((d, d), lambda i, t: (0, 0)),
                  pl.BlockSpec((None, tm, d), lambda i, t: (i, t, 0)),
                  _mod_spec(n_lat // tm)],
        out_specs=pl.BlockSpec((None, tm, d), lambda i, t: (i, t, 0)),
        out_shape=jax.ShapeDtypeStruct((b, rows, d), F32),
        compiler_params=_params("parallel", "parallel"),
        name="proj_residual",
    )(a, w, x, modtab)


def _router_kernel(x_ref, mod_ref, rwt_ref, h_ref, aff_ref, afft_ref):
    h = _normmod(x_ref[...], mod_ref, 3, 4)
    h_ref[...] = h.astype(BF16)
    logits_t = lax.dot_general(rwt_ref[...], h, (((1,), (1,)), ((), ())),
                               precision=lax.Precision.HIGHEST, preferred_element_type=F32)
    e = jnp.exp(logits_t - jnp.max(logits_t, axis=0, keepdims=True))
    aff_t = e / jnp.sum(e, axis=0, keepdims=True)
    afft_ref[...] = aff_t
    aff_ref[...] = aff_t.T


def _router(x, modtab, router_w_t, *, n_lat):
    b, rows, d = x.shape
    tm = ROW_TILE
    ne = router_w_t.shape[0]
    return pl.pallas_call(
        _router_kernel,
        grid=(b, rows // tm),
        in_specs=[pl.BlockSpec((None, tm, d), lambda i, t: (i, t, 0)),
                  _mod_spec(n_lat // tm),
                  pl.BlockSpec((ne, d), lambda i, t: (0, 0))],
        out_specs=[pl.BlockSpec((None, tm, d), lambda i, t: (i, t, 0)),
                   pl.BlockSpec((None, tm, ne), lambda i, t: (i, t, 0)),
                   pl.BlockSpec((None, ne, tm), lambda i, t: (i, 0, t))],
        out_shape=[jax.ShapeDtypeStruct((b, rows, d), BF16),
                   jax.ShapeDtypeStruct((b, rows, ne), F32),
                   jax.ShapeDtypeStruct((b, ne, rows), F32)],
        compiler_params=_params("parallel", "parallel"),
        name="router",
    )(x, modtab, router_w_t)


def _lane_prefix(mask_f, tri):
    n = mask_f.shape[1]
    run = jnp.zeros((mask_f.shape[0], 1), F32)
    out = []
    for j in range(n // 128):
        blk = mask_f[:, j * 128:(j + 1) * 128]
        incl = jnp.dot(blk.astype(BF16), tri, preferred_element_type=F32)
        out.append(incl - blk + run)
        run = run + incl[:, 127:128]
    return jnp.concatenate(out, axis=1)


def _select_kernel(afft_ref, slot_ref, slot_tm_ref, *, segments):
    r = lax.broadcasted_iota(I32, (128, 128), 0)
    c = lax.broadcasted_iota(I32, (128, 128), 1)
    tri = (r <= c).astype(BF16)
    for off, n, cap, base in segments:
        bits = lax.bitcast_convert_type(afft_ref[:, off:off + n], I32)
        cap_f = jnp.float32(cap)

        def body(i, thr):
            cand = thr | jnp.left_shift(jnp.int32(1), 30 - i)
            cnt = jnp.sum((bits >= cand).astype(F32), axis=1, keepdims=True)
            return jnp.where(cnt >= cap_f, cand, thr)

        thr = lax.fori_loop(0, 31, body, jnp.zeros((bits.shape[0], 1), I32))
        gt = (bits > thr).astype(F32)
        eq = (bits == thr).astype(F32)
        need = cap_f - jnp.sum(gt, axis=1, keepdims=True)
        sel = gt + eq * (_lane_prefix(eq, tri) < need).astype(F32)
        slot = _lane_prefix(sel, tri).astype(I32) + base
        slot = jnp.where(sel > 0.5, slot, -1)
        slot_ref[:, off:off + n] = slot
        slot_tm_ref[off:off + n, :] = slot.astype(F32).T


def _select(aff_t, segments):
    b, ne, rows = aff_t.shape
    return pl.pallas_call(
        functools.partial(_select_kernel, segments=segments),
        grid=(b,),
        in_specs=[pl.BlockSpec((None, ne, rows), lambda i: (i, 0, 0))],
        out_specs=[pl.BlockSpec((None, ne, rows), lambda i: (i, 0, 0)),
                   pl.BlockSpec((None, rows, ne), lambda i: (i, 0, 0))],
        out_shape=[jax.ShapeDtypeStruct((b, ne, rows), I32), jax.ShapeDtypeStruct((b, rows, ne), F32)],
        compiler_params=_params("parallel"),
        name="expert_select",
    )(aff_t)


def _gather_kernel(slot_ref, h_ref, xs_ref, *, segments):
    for off, n, cap, base in segments:
        slot = slot_ref[:, off:off + n]
        hseg = h_ref[off:off + n, :]
        st = min(SLOT_TILE, cap)
        for j in range(cap // st):
            ids = base + j * st + lax.broadcasted_iota(I32, (st, 1), 0)
            onehot = jnp.where(ids == slot, 1.0, 0.0).astype(BF16)
            rows = jnp.dot(onehot, hseg, preferred_element_type=F32)
            xs_ref[base + j * st:base + (j + 1) * st, :] = rows.astype(BF16)


def _gather(slot, h, segments, n_slots):
    b, ne, rows = slot.shape
    d = h.shape[2]
    return pl.pallas_call(
        functools.partial(_gather_kernel, segments=segments),
        grid=(b, ne),
        in_specs=[pl.BlockSpec((None, None, 1, rows), lambda i, e: (i, e, 0, 0)),
                  pl.BlockSpec((None, rows, d), lambda i, e: (i, 0, 0))],
        out_specs=pl.BlockSpec((None, None, n_slots, d), lambda i, e: (e, i, 0, 0)),
        out_shape=jax.ShapeDtypeStruct((ne, b, n_slots, d), BF16),
        compiler_params=_params("parallel", "arbitrary"),
        name="expert_gather",
    )(slot.reshape(b, ne, 1, rows), h)


def _ffn_kernel(xs_ref, w1_ref, w3_ref, w2_ref, y_ref):
    xs = xs_ref[...]
    a = _bdot(xs, w1_ref[...])
    g = _bdot(xs, w3_ref[...])
    hid = (a * _sigmoid(a)) * g
    y_ref[...] = _bdot(hid, w2_ref[...]).astype(BF16)


def _expert_ffn(xs, w1, w3, w2, layer):
    ne, b, n_slots, d = xs.shape
    ff = w1.shape[3]
    return pl.pallas_call(
        _ffn_kernel,
        grid=(ne, b),
        in_specs=[pl.BlockSpec((None, None, n_slots, d), lambda e, i: (e, i, 0, 0)),
                  pl.BlockSpec((None, None, d, ff), lambda e, i: (layer, e, 0, 0)),
                  pl.BlockSpec((None, None, d, ff), lambda e, i: (layer, e, 0, 0)),
                  pl.BlockSpec((None, None, ff, d), lambda e, i: (layer, e, 0, 0))],
        out_specs=pl.BlockSpec((None, None, n_slots, d), lambda e, i: (e, i, 0, 0)),
        out_shape=jax.ShapeDtypeStruct((ne, b, n_slots, d), BF16),
        compiler_params=_params("parallel", "arbitrary", vmem_limit=FFN_VMEM_LIMIT),
        name="expert_ffn",
    )(xs, w1, w3, w2)


def _combine_kernel(slot_ref, aff_ref, y_ref, x_ref, mod_ref, fw_ref, o_ref, acc_ref, *, slot0, final):
    eg = pl.program_id(2)
    n_groups = pl.num_programs(2)
    group, width = y_ref.shape[:2]

    @pl.when(eg == 0)
    def _():
        acc_ref[...] = jnp.zeros_like(acc_ref)

    lane_e = lax.broadcasted_iota(I32, aff_ref.shape, 1)
    ids = slot0 + lax.broadcasted_iota(I32, (1, width), 1)
    total = None
    for j in range(group):
        e = eg * group + j
        gate = jnp.sum(jnp.where(lane_e == e, aff_ref[...], 0.0), axis=1, keepdims=True)
        slot = jnp.sum(jnp.where(lane_e == e, slot_ref[...], 0.0), axis=1, keepdims=True).astype(I32)
        onehot = jnp.where(slot == ids, 1.0, 0.0).astype(BF16)
        part = gate * jnp.dot(onehot, y_ref[j], preferred_element_type=F32)
        total = part if total is None else total + part
    acc_ref[...] += total

    @pl.when(eg == n_groups - 1)
    def _():
        out = x_ref[...] + mod_ref[5:6, :] * acc_ref[...]
        o_ref[...] = _rms(out) * fw_ref[...] if final else out


def _combine(slot_tm, aff, y, x, modtab, final_w, *, row0, rows, tm, slot0, width, seg, in_place, final=False):
    b, _, d = x.shape
    ne = aff.shape[2]
    group = COMBINE_EXPERTS
    t0, s0 = row0 // tm, slot0 // width
    out_rows = x.shape[1] if in_place else rows
    return pl.pallas_call(
        functools.partial(_combine_kernel, slot0=slot0, final=final),
        grid=(b, rows // tm, ne // group),
        in_specs=[pl.BlockSpec((None, tm, ne), lambda i, t, e: (i, t + t0, 0)),
                  pl.BlockSpec((None, tm, ne), lambda i, t, e: (i, t + t0, 0)),
                  pl.BlockSpec((group, None, width, d), lambda i, t, e: (e, i, s0, 0)),
                  pl.BlockSpec((None, tm, d), lambda i, t, e: (i, t + t0, 0)),
                  pl.BlockSpec((None, None, 6, d), lambda i, t, e: (i, seg, 0, 0)),
                  pl.BlockSpec((1, d), lambda i, t, e: (0, 0))],
        out_specs=pl.BlockSpec((None, tm, d), lambda i, t, e: (i, t + (t0 if in_place else 0), 0)),
        out_shape=jax.ShapeDtypeStruct((b, out_rows, d), F32),
        scratch_shapes=[pltpu.VMEM((tm, d), F32)],
        input_output_aliases={3: 0} if in_place else {},
        compiler_params=_params("parallel", "parallel", "arbitrary"),
        name="expert_combine",
    )(slot_tm, aff, y, x, modtab, final_w.reshape(1, d))


def _moe(x, modtab, router_w_t, w1, w3, w2, layer, final_w, *, n_lat, with_ctx):
    b, rows, d = x.shape
    ne = N_EXPERTS
    cap_l = CAPACITY_FACTOR * n_lat // ne
    h, aff, aff_t = _router(x, modtab, router_w_t, n_lat=n_lat)
    segments = ((0, n_lat, cap_l, 0),)
    n_slots = cap_l
    if with_ctx:
        n_ctx = rows - n_lat
        cap_c = CAPACITY_FACTOR * n_ctx // ne
        segments += ((n_lat, n_ctx, cap_c, cap_l),)
        n_slots += cap_c
    slot, slot_tm = _select(aff_t, segments)
    xs = _gather(slot, h, segments, n_slots)
    y = _expert_ffn(xs, w1, w3, w2, layer)
    out = _combine(slot_tm, aff, y, x, modtab, final_w, row0=0, rows=n_lat, tm=min(COMBINE_TILE, n_lat),
                   slot0=0, width=cap_l, seg=1, in_place=with_ctx, final=not with_ctx)
    if with_ctx:
        out = _combine(slot_tm, aff, y, out, modtab, final_w, row0=n_lat, rows=n_ctx, tm=n_ctx, slot0=cap_l,
                       width=cap_c, seg=0, in_place=True)
    return out


def _segsum64(x, bd):
    hi = x.astype(BF16)
    lo = (x - hi.astype(F32)).astype(BF16)
    out = []
    for j in range(x.shape[1] // 256):
        sl = slice(j * 256, (j + 1) * 256)
        out.append(jnp.dot(hi[:, sl], bd, preferred_element_type=F32)
                   + jnp.dot(lo[:, sl], bd, preferred_element_type=F32))
    return jnp.concatenate(out, axis=1)


def _block_diag_ones():
    r = lax.broadcasted_iota(I32, (256, 256), 0)
    c = lax.broadcasted_iota(I32, (256, 256), 1)
    return (r // 64 == c // 64).astype(BF16)


def _softplus(x):
    return jnp.maximum(x, 0.0) + jnp.log(1.0 + jnp.exp(-jnp.abs(x)))


def _rwkv_feat_kernel(x_ref, xp_ref, xn_ref, mod_ref, mu_ref, wrkv_ref, w0_ref, w1_ref, w2_ref,
                      a0_ref, a1_ref, a2_ref, g1_ref, g2_ref, kk_ref, ka_ref,
                      r_out, v_out, nkk_out, g_out, w_out, k_out, b_out, *, n_lat_tiles):
    t = pl.program_id(1)
    nt = pl.num_programs(1)
    tm = x_ref.shape[0]
    h = _normmod(x_ref[...], mod_ref, 0, 1)
    hp = _normmod(xp_ref[7:8, :], mod_ref, 0, 1)
    hn = _normmod(xn_ref[0:1, :], mod_ref, 0, 1)
    has_left = jnp.logical_and(t != 0, t != n_lat_tiles)
    has_right = jnp.logical_and(t != n_lat_tiles - 1, t != nt - 1)
    hp = jnp.where(has_left, hp, 0.0)
    hn = jnp.where(has_right, hn, 0.0)
    row = lax.broadcasted_iota(I32, h.shape, 0)
    left = jnp.where(row == 0, hp, pltpu.roll(h, 1, axis=0))
    right = jnp.where(row == tm - 1, hn, pltpu.roll(h, tm - 1, axis=0))
    xx = 0.5 * (left + right) - h
    xr, xw, xk, xv, xa, xg = (h + xx * mu_ref[i:i + 1, :] for i in range(6))
    r = _bdot(xr, wrkv_ref[0])
    k = _bdot(xk, wrkv_ref[1])
    v = _bdot(xv, wrkv_ref[2])
    g = _bdot(_sigmoid(_bdot(xg, g1_ref[...])), g2_ref[...])
    bd = _block_diag_ones()
    kk = k * kk_ref[...]
    kk = kk * lax.rsqrt(jnp.maximum(_segsum64(kk * kk, bd), 1e-24))
    r_out[...] = r
    v_out[...] = v
    nkk_out[...] = -kk
    g_out[...] = g
    for d in range(2):
        w_lora = _bdot(jnp.tanh(_bdot(xw, w1_ref[d])), w2_ref[d])
        log_w = -_softplus(-(w0_ref[d:d + 1, :] + w_lora)) - 0.5
        w_out[d] = jnp.exp(-jnp.exp(log_w))
        a = _sigmoid(a0_ref[d:d + 1, :] + _bdot(_bdot(xa, a1_ref[d]), a2_ref[d]))
        k_out[d] = k * (1.0 + (a - 1.0) * ka_ref[...])
        b_out[d] = kk * a


def _rwkv_features(x, modtab, p, *, n_lat):
    b, t_all, d = x.shape
    tm = ROW_TILE
    nt = t_all // tm
    tb = tm // 8
    full = lambda shape: pl.BlockSpec(shape, lambda i, t: (0,) * len(shape))
    tok = pl.BlockSpec((None, tm, d), lambda i, t: (i, t, 0))
    tok2 = pl.BlockSpec((2, None, tm, d), lambda i, t: (0, i, t, 0))
    one = jax.ShapeDtypeStruct((b, t_all, d), F32)
    two = jax.ShapeDtypeStruct((2, b, t_all, d), F32)
    return pl.pallas_call(
        functools.partial(_rwkv_feat_kernel, n_lat_tiles=n_lat // tm),
        grid=(b, nt),
        in_specs=[tok,
                  pl.BlockSpec((None, 8, d), lambda i, t: (i, jnp.maximum(t * tb - 1, 0), 0)),
                  pl.BlockSpec((None, 8, d), lambda i, t: (i, jnp.minimum((t + 1) * tb, nt * tb - 1), 0)),
                  _mod_spec(n_lat // tm),
                  full((6, d)), full((3, d, d)), full((2, d)), full(p["w1"].shape), full(p["w2"].shape),
                  full((2, d)), full(p["a1"].shape), full(p["a2"].shape), full(p["g1"].shape),
                  full(p["g2"].shape), full((1, d)), full((1, d))],
        out_specs=[tok, tok, tok, tok, tok2, tok2, tok2],
        out_shape=[one, one, one, one, two, two, two],
        compiler_params=_params("parallel", "parallel"),
        name="rwkv_features",
    )(x, x, x, modtab, p["mu"], p["w_rkv"], p["w0"], p["w1"], p["w2"], p["a0"], p["a1"], p["a2"],
      p["g1"], p["g2"], p["k_k"], p["k_a"])


def _scan_kernel(r_ref, w_ref, k_ref, v_ref, a_ref, b_ref, o_ref, s_ref, wr_ref, *, reverse):
    n = s_ref.shape[0]
    steps = r_ref.shape[0]

    @pl.when(pl.program_id(0) == 0)
    def _():
        s_ref[...] = jnp.zeros_like(s_ref)

    def step(i, carry):
        j = steps - 1 - i if reverse else i
        r = r_ref[j]
        wr_ref[...] = w_ref[j] * r
        br = jnp.sum(b_ref[j] * r, axis=0, keepdims=True)
        kr = jnp.sum(k_ref[j] * r, axis=0, keepdims=True)
        acc = [jnp.zeros(s_ref.shape[1:], F32) for _ in range(4)]
        for kk in range(n):
            s = s_ref[kk]
            acc[kk % 2] = acc[kk % 2] + s * a_ref[j, kk:kk + 1, :]
            acc[2 + kk % 2] = acc[2 + kk % 2] + s * wr_ref[kk:kk + 1, :]
        sa = acc[0] + acc[1]
        v = v_ref[j]
        o_ref[j] = acc[2] + acc[3] + sa * br + v * kr
        for kk in range(n):
            s_ref[kk] = (s_ref[kk] * w_ref[j, kk:kk + 1, :] + sa * b_ref[j, kk:kk + 1, :]
                         + v * k_ref[j, kk:kk + 1, :])
        return carry

    lax.fori_loop(0, steps, step, 0)


def _wkv_scan(r, w, k, v, a, b, *, n_lat, reverse):
    t_all, n, chains = r.shape
    tc = SCAN_CHUNK
    nlc, nch = n_lat // tc, t_all // tc
    ncc = nch - nlc
    if reverse:
        idx = lambda c: (jnp.where(c < ncc, nch - 1 - c, nlc - 1 - (c - ncc)), 0, 0)
    else:
        idx = lambda c: (jnp.where(c < ncc, nlc + c, c - ncc), 0, 0)
    spec = pl.BlockSpec((tc, n, chains), idx)
    return pl.pallas_call(
        functools.partial(_scan_kernel, reverse=reverse),
        grid=(nch,),
        in_specs=[spec] * 6,
        out_specs=spec,
        out_shape=jax.ShapeDtypeStruct((t_all, n, chains), F32),
        scratch_shapes=[pltpu.VMEM((n, n, chains), F32), pltpu.VMEM((n, chains), F32)],
        compiler_params=_params("arbitrary"),
        name="wkv_scan_bwd" if reverse else "wkv_scan_fwd",
    )(r, w, k, v, a, b)


def _rwkv_out_kernel(o_ref, r_ref, k_ref, v_ref, g_ref, rk_ref, lnw_ref, lnb_ref, wo_ref, x_ref, mod_ref,
                     out_ref):
    bd = _block_diag_ones()
    inv_n = 1.0 / C_HEAD_DIM
    o = o_ref[...]
    o = o - _segsum64(o, bd) * inv_n
    o = o * lax.rsqrt(_segsum64(o * o, bd) * inv_n + C_GN_EPS)
    o = o * lnw_ref[...] + lnb_ref[...]
    r = r_ref[...]
    bonus = _segsum64(r * k_ref[0] * rk_ref[0:1, :] + r * k_ref[1] * rk_ref[1:2, :], bd) * v_ref[...]
    y = _bdot((o + bonus) * g_ref[...], wo_ref[...])
    out_ref[...] = x_ref[...] + mod_ref[2:3, :] * y


def _rwkv_readout(o, r, k2, v, g, p, x, modtab, *, n_lat, rows):
    b, t_all, d = x.shape
    tm = ROW_TILE
    tok = pl.BlockSpec((None, tm, d), lambda i, t: (i, t, 0))
    full = lambda shape: pl.BlockSpec(shape, lambda i, t: (0,) * len(shape))
    return pl.pallas_call(
        _rwkv_out_kernel,
        grid=(b, rows // tm),
        in_specs=[tok, tok, pl.BlockSpec((2, None, tm, d), lambda i, t: (0, i, t, 0)), tok, tok,
                  full((2, d)), full((1, d)), full((1, d)), full((d, d)), tok, _mod_spec(n_lat // tm)],
        out_specs=tok,
        out_shape=jax.ShapeDtypeStruct((b, rows, d), F32),
        compiler_params=_params("parallel", "parallel"),
        name="rwkv_readout",
    )(o, r, k2, v, g, p["r_k"], p["ln_w"], p["ln_b"], p["w_o"], x, modtab)


def _to_scan_kernel(x_ref, o_ref, y_ref):
    nb, tt, d = x_ref.shape
    n, chains = o_ref.shape[1:]
    for b in range(nb):
        for p in range(d // 128):
            row = (b * (d // 128) + p) * 128
            y_ref[row:row + 128, :] = x_ref[b, :, p * 128:(p + 1) * 128].T
    for k in range(n):
        o_ref[:, k, :] = y_ref[pl.ds(k, chains, stride=n), :].T


def _to_scan_layout(a, d=None):
    b, t, dm = a.shape[-3:]
    tt = 128
    chains = b * dm // C_HEAD_DIM
    if d is None:
        spec = pl.BlockSpec((b, tt, dm), lambda i: (0, i, 0))
    else:
        spec = pl.BlockSpec((None, b, tt, dm), lambda i: (d, 0, i, 0))
    return pl.pallas_call(
        _to_scan_kernel,
        grid=(t // tt,),
        in_specs=[spec],
        out_specs=pl.BlockSpec((tt, C_HEAD_DIM, chains), lambda i: (i, 0, 0)),
        out_shape=jax.ShapeDtypeStruct((t, C_HEAD_DIM, chains), F32),
        scratch_shapes=[pltpu.VMEM((b * dm, tt), F32)],
        compiler_params=_params("parallel"),
        name="to_scan_layout",
    )(a)


def _from_scan_kernel(a_ref, b_ref, o_ref, y_ref):
    nb, tt, d = o_ref.shape
    n, chains = a_ref.shape[1:]
    for k in range(n):
        y_ref[pl.ds(k, chains, stride=n), :] = (a_ref[:, k, :] + b_ref[:, k, :]).T
    for b in range(nb):
        for p in range(d // 128):
            row = (b * (d // 128) + p) * 128
            o_ref[b, :, p * 128:(p + 1) * 128] = y_ref[row:row + 128, :].T


def _from_scan_layout(o_f, o_b, b):
    t, n, chains = o_f.shape
    tt = 128
    dm = n * chains // b
    spec = pl.BlockSpec((tt, n, chains), lambda i: (i, 0, 0))
    return pl.pallas_call(
        _from_scan_kernel,
        grid=(t // tt,),
        in_specs=[spec, spec],
        out_specs=pl.BlockSpec((b, tt, dm), lambda i: (0, i, 0)),
        out_shape=jax.ShapeDtypeStruct((b, t, dm), F32),
        scratch_shapes=[pltpu.VMEM((b * dm, tt), F32)],
        compiler_params=_params("parallel"),
        name="from_scan_layout",
    )(o_f, o_b)


def _rwkv_mixer(x, modtab, p, *, n_lat, rows):
    b = x.shape[0]
    r, v, nkk, g, w2, k2, b2 = _rwkv_features(x, modtab, p, n_lat=n_lat)
    rs, vs, as_ = _to_scan_layout(r), _to_scan_layout(v), _to_scan_layout(nkk)
    o_f, o_b = (_wkv_scan(rs, _to_scan_layout(w2, d), _to_scan_layout(k2, d), vs, as_, _to_scan_layout(b2, d),
                          n_lat=n_lat, reverse=(d == 1)) for d in range(2))
    return _rwkv_readout(_from_scan_layout(o_f, o_b, b), r, k2, v, g, p, x, modtab, n_lat=n_lat, rows=rows)


def _rope_tables(n_lat, n_ctx, head_dim):
    rows = jnp.repeat(jnp.arange(n_lat // GRID_W, dtype=I32), GRID_W).astype(F32)
    cols = jnp.tile(jnp.arange(GRID_W, dtype=I32), n_lat // GRID_W).astype(F32)
    n_freq = head_dim // 4
    inv_freq = ROPE_THETA ** (-jnp.arange(n_freq, dtype=F32) / n_freq)
    ang = jnp.concatenate([rows[:, None] * inv_freq, cols[:, None] * inv_freq], axis=-1)
    cos, sin = jnp.cos(ang), jnp.sin(ang)
    reps = 128 // head_dim
    cos = jnp.tile(jnp.concatenate([cos, cos], axis=-1), (1, reps))
    sin = jnp.tile(jnp.concatenate([-sin, sin], axis=-1), (1, reps))
    cos = jnp.concatenate([cos, jnp.ones((n_ctx, 128), F32)], axis=0)
    sin = jnp.concatenate([sin, jnp.zeros((n_ctx, 128), F32)], axis=0)
    return cos, sin


def kernel(x, c, ctx, c_ctx, mod_w, mod_b, a_w_qkv, a_w_o, a_q_norm, a_k_norm, b_w_qkv, b_w_o, b_sink,
           c_mu, c_w_rkv, c_w_o, c_w0, c_w1, c_w2, c_a0, c_a1, c_a2, c_g1, c_g2, c_k_k, c_k_a, c_r_k,
           c_ln_w, c_ln_b, router_w, ffn_w1, ffn_w3, ffn_w2, final_norm):
    b, n_lat, d = x.shape
    n_ctx = ctx.shape[1]
    t_all = n_lat + n_ctx
    depth = mod_w.shape[0]
    assert d == D_MODEL and n_ctx % ROW_TILE == 0 and n_lat % min(COMBINE_TILE, n_lat) == 0

    cond_rows = -(-(b + 1) // 8) * 8
    cond = jnp.zeros((cond_rows, d), F32).at[:b].set(c).at[b].set(c_ctx)
    mods = _mod_tables(cond, mod_w, mod_b).reshape(depth, cond_rows, 6, d)
    cos_a, sin_a = _rope_tables(n_lat, n_ctx, A_HEAD_DIM)
    cos_b, sin_b = _rope_tables(n_lat, n_ctx, B_HEAD_DIM)
    xs = jnp.concatenate([x, ctx], axis=1)
    for i in range(depth):
        last = i == depth - 1
        rows = n_lat if last else t_all
        kind, j = i % N_MIXERS, i // N_MIXERS
        modtab = jnp.stack([jnp.broadcast_to(mods[i, b], (b, 6, d)), mods[i, :b]], axis=1)
        if kind == 0:
            q, k, v = _qkv_proj(xs, modtab, a_w_qkv[j].astype(BF16), a_q_norm[j].reshape(1, -1),
                                a_k_norm[j].reshape(1, -1), cos_a, sin_a, kind=0, n_lat=n_lat)
            o = _global_attention(q, k, v, n_lat=n_lat, rows=rows)
            xs = _proj_residual(o, a_w_o[j].astype(BF16), xs, modtab, n_lat=n_lat, rows=rows)
        elif kind == 1:
            ones = jnp.ones((1, 128), F32)
            q, k, v = _qkv_proj(xs, modtab, b_w_qkv[j].astype(BF16), ones, ones, cos_b, sin_b,
                                kind=1, n_lat=n_lat)
            o = _window_attention(q, k, v, b_sink[j], n_lat=n_lat, rows=rows)
            xs = _proj_residual(o, b_w_o[j].astype(BF16), xs, modtab, n_lat=n_lat, rows=rows)
        else:
            p = dict(mu=c_mu[j], w_rkv=c_w_rkv[j].astype(BF16), w_o=c_w_o[j].astype(BF16), w0=c_w0[j],
                     w1=c_w1[j].astype(BF16), w2=c_w2[j].astype(BF16), a0=c_a0[j],
                     a1=c_a1[j].astype(BF16), a2=c_a2[j].astype(BF16), g1=c_g1[j].astype(BF16),
                     g2=c_g2[j].astype(BF16), k_k=c_k_k[j].reshape(1, d), k_a=c_k_a[j].reshape(1, d),
                     r_k=c_r_k[j].reshape(2, d), ln_w=c_ln_w[j].reshape(1, d), ln_b=c_ln_b[j].reshape(1, d))
            xs = _rwkv_mixer(xs, modtab, p, n_lat=n_lat, rows=rows)
        xs = _moe(xs, modtab, router_w[i].T, ffn_w1, ffn_w3, ffn_w2, i, final_norm, n_lat=n_lat,
                  with_ctx=not last)
    return xs
```

```python
import functools
import math

import jax
import jax.numpy as jnp
from jax import lax
from jax.experimental import pallas as pl
from jax.experimental.pallas import tpu as pltpu

F32 = jnp.float32
BF16 = jnp.bfloat16
I32 = jnp.int32

D_MODEL = 1024
GRID_W = 64
Q_BLOCK = 128
ROPE_THETA = 10000.0
NORM_EPS = 1e-6
NEG_INF = -1e30
A_HEADS, A_KV_HEADS, A_HEAD_DIM = 8, 2, 128
B_HEADS, B_KV_HEADS, B_HEAD_DIM = 16, 4, 64
WINDOW = 128
C_HEAD_DIM = 64
C_HEADS = D_MODEL // C_HEAD_DIM
C_GN_EPS = C_HEAD_DIM * 1e-5
N_EXPERTS = 16
EXPERT_FF = 2 * D_MODEL
CAPACITY_FACTOR = 2
N_MIXERS = 3
LOG2E = math.log2(math.e)

ROW_TILE = 256
COMBINE_TILE = 1024
COMBINE_EXPERTS = 8
SAMPLE_GROUP = 2
SLOT_TILE = 128
SCAN_CHUNK = 32
VMEM_LIMIT = 56 * 1024 * 1024
FFN_VMEM_LIMIT = 60 * 1024 * 1024


def _params(*sem, vmem_limit=VMEM_LIMIT):
    return pltpu.CompilerParams(dimension_semantics=sem, vmem_limit_bytes=vmem_limit)


def _rms(x):
    return x * lax.rsqrt(jnp.mean(x * x, axis=-1, keepdims=True) + NORM_EPS)


def _normmod(x, mod_ref, shift_row, scale_row):
    return _rms(x) * (1.0 + mod_ref[scale_row:scale_row + 1, :]) + mod_ref[shift_row:shift_row + 1, :]


def _bdot(a, b):
    return jnp.dot(a.astype(BF16), b.astype(BF16), preferred_element_type=F32)


def _dot_nt(a, b):
    return lax.dot_general(a.astype(BF16), b.astype(BF16), (((1,), (1,)), ((), ())),
                           preferred_element_type=F32)


def _sigmoid(x):
    return 1.0 / (1.0 + jnp.exp(-x))


def _mod_spec(n_lat_tiles, group=None):
    return pl.BlockSpec((group, None, 6, D_MODEL),
                        lambda b, t: (b, jnp.where(t < n_lat_tiles, 1, 0), 0, 0))


def _sample_group(b):
    return SAMPLE_GROUP if b % SAMPLE_GROUP == 0 else 1


def _mod_kernel(cond_ref, w_ref, b_ref, o_ref):
    c = cond_ref[...]
    a = c * _sigmoid(c)
    o_ref[0] = _bdot(a, w_ref[0]) + b_ref[0]


def _mod_tables(cond, mod_w, mod_b):
    depth, d, n = mod_w.shape
    rows = cond.shape[0]
    return pl.pallas_call(
        _mod_kernel,
        grid=(depth, n // d),
        in_specs=[pl.BlockSpec((rows, d), lambda i, j: (0, 0)),
                  pl.BlockSpec((1, d, d), lambda i, j: (i, 0, j)),
                  pl.BlockSpec((1, 1, d), lambda i, j: (i, 0, j))],
        out_specs=pl.BlockSpec((1, rows, d), lambda i, j: (i, 0, j)),
        out_shape=jax.ShapeDtypeStruct((depth, rows, n), F32),
        compiler_params=_params("arbitrary", "arbitrary"),
        name="mod_tables",
    )(cond, mod_w, mod_b.reshape(depth, 1, n))


def _rope128(x, cos, sin_signed, half):
    if half == 64:
        rot = pltpu.roll(x, 64, axis=1)
    else:
        lane = lax.broadcasted_iota(I32, x.shape, 1)
        rot = jnp.where((lane % 64) < 32, pltpu.roll(x, 96, axis=1), pltpu.roll(x, 32, axis=1))
    return x * cos + rot * sin_signed


def _qkv_kernel(x_ref, mod_ref, *refs, kind):
    shared, outs = refs[:5], refs[5:]
    for s in range(x_ref.shape[0]):
        _qkv_tile(x_ref.at[s], mod_ref.at[s], *shared, *(o.at[s] for o in outs), kind=kind)


def _qkv_tile(x_ref, mod_ref, w_ref, gq_ref, gk_ref, cos_ref, sin_ref, q_ref, k_ref, v_ref, *, kind):
    h = _normmod(x_ref[...], mod_ref, 0, 1)
    y = _bdot(h, w_ref[...])
    cos, sin = cos_ref[...], sin_ref[...]
    nq = D_MODEL
    nkv = (y.shape[1] - nq) // 2
    half = 64 if kind == 0 else 32
    q_scale = (2 * half) ** -0.5 * LOG2E
    for j in range((nq + nkv) // 128):
        s = y[:, j * 128:(j + 1) * 128]
        if kind == 0:
            gain = gq_ref[...] if j < nq // 128 else gk_ref[...]
            s = _rms(s) * gain
        s = _rope128(s, cos, sin, half)
        if j < nq // 128:
            q_ref[:, j * 128:(j + 1) * 128] = (s * q_scale).astype(BF16)
        elif kind == 0:
            k_ref[:, (j - nq // 128) * 128:(j - nq // 128 + 1) * 128] = s.astype(BF16)
        else:
            jj = (j - nq // 128) * 2
            k_ref[jj] = s[:, :64].astype(BF16)
            k_ref[jj + 1] = s[:, 64:].astype(BF16)
    v = y[:, nq + nkv:].astype(BF16)
    if kind == 0:
        v_ref[...] = v
    else:
        for jj in range(nkv // 64):
            v_ref[jj] = v[:, jj * 64:(jj + 1) * 64]


def _qkv_proj(x, modtab, w, gq, gk, cos, sin, *, kind, n_lat):
    b, t_all, d = x.shape
    tm = ROW_TILE
    nt = t_all // tm
    n = w.shape[1]
    nkv = (n - d) // 2
    grp = _sample_group(b)
    if kind == 0:
        kv_shape = jax.ShapeDtypeStruct((b, t_all, nkv), BF16)
        kv_spec = pl.BlockSpec((grp, tm, nkv), lambda i, t: (i, t, 0))
    else:
        kv_shape = jax.ShapeDtypeStruct((b, nkv // 64, t_all, 64), BF16)
        kv_spec = pl.BlockSpec((grp, nkv // 64, tm, 64), lambda i, t: (i, 0, t, 0))
    return pl.pallas_call(
        functools.partial(_qkv_kernel, kind=kind),
        grid=(b // grp, nt),
        in_specs=[pl.BlockSpec((grp, tm, d), lambda i, t: (i, t, 0)),
                  _mod_spec(n_lat // tm, grp),
                  pl.BlockSpec((d, n), lambda i, t: (0, 0)),
                  pl.BlockSpec((1, 128), lambda i, t: (0, 0)),
                  pl.BlockSpec((1, 128), lambda i, t: (0, 0)),
                  pl.BlockSpec((tm, 128), lambda i, t: (t, 0)),
                  pl.BlockSpec((tm, 128), lambda i, t: (t, 0))],
        out_specs=[pl.BlockSpec((grp, tm, d), lambda i, t: (i, t, 0)), kv_spec, kv_spec],
        out_shape=[jax.ShapeDtypeStruct((b, t_all, d), BF16), kv_shape, kv_shape],
        compiler_params=_params("parallel", "parallel"),
        name=f"qkv_proj_{kind}",
    )(x, modtab, w, gq, gk, cos, sin)


def _flash_chunk(carry, q, kc, vc):
    m, l, acc = carry
    s = _dot_nt(q, kc)
    m_new = jnp.maximum(m, jnp.max(s, axis=-1, keepdims=True))
    alpha = jnp.exp2(m - m_new)
    p = jnp.exp2(s - m_new)
    l = alpha * l + jnp.sum(p, axis=-1, keepdims=True)
    acc = alpha * acc + jnp.dot(p.astype(BF16), vc, preferred_element_type=F32)
    return m_new, l, acc


def _gattn_kernel(q_ref, k_ref, v_ref, o_ref, *, n_lat, kchunk):
    hd = A_HEAD_DIM
    g = A_HEADS // A_KV_HEADS
    tq = q_ref.shape[0]
    t_all = k_ref.shape[0]
    groups = range(A_KV_HEADS)
    qs = [jnp.concatenate([q_ref[:, (j * g + i) * hd:(j * g + i + 1) * hd] for i in range(g)], axis=0)
          for j in groups]
    init = (jnp.full((g * tq, 1), NEG_INF, F32), jnp.zeros((g * tq, 1), F32), jnp.zeros((g * tq, hd), F32))

    def chunk(j, carry, rows):
        return _flash_chunk(carry, qs[j], k_ref[rows, j * hd:(j + 1) * hd], v_ref[rows, j * hd:(j + 1) * hd])

    def finish(carries):
        for j, (_, l, acc) in enumerate(carries):
            o = (acc / l).astype(BF16)
            for i in range(g):
                o_ref[:, (j * g + i) * hd:(j * g + i + 1) * hd] = o[i * tq:(i + 1) * tq]

    ctx_carries = tuple(chunk(j, init, slice(n_lat, t_all)) for j in groups)
    is_ctx = pl.program_id(1) >= n_lat // tq

    @pl.when(is_ctx)
    def _():
        finish(ctx_carries)

    @pl.when(jnp.logical_not(is_ctx))
    def _():
        def body(c, carries):
            rows = pl.ds(pl.multiple_of(c * kchunk, kchunk), kchunk)
            return tuple(chunk(j, carries[j], rows) for j in groups)
        finish(lax.fori_loop(0, n_lat // kchunk, body, ctx_carries))


def _global_attention(q, k, v, *, n_lat, rows):
    b, t_all, d = q.shape
    nkv = k.shape[2]
    tq = ROW_TILE
    return pl.pallas_call(
        functools.partial(_gattn_kernel, n_lat=n_lat, kchunk=min(2048, n_lat)),
        grid=(b, rows // tq),
        in_specs=[pl.BlockSpec((None, tq, d), lambda i, t: (i, t, 0)),
                  pl.BlockSpec((None, t_all, nkv), lambda i, t: (i, 0, 0)),
                  pl.BlockSpec((None, t_all, nkv), lambda i, t: (i, 0, 0))],
        out_specs=pl.BlockSpec((None, tq, d), lambda i, t: (i, t, 0)),
        out_shape=jax.ShapeDtypeStruct((b, t_all, d), BF16),
        compiler_params=_params("parallel", "arbitrary"),
        name="global_attention",
    )(q, k, v)


def _wattn_kernel(q_ref, k_ref, v_ref, sink_ref, bias_ref, o_ref, *, n_lat):
    hd = B_HEAD_DIM
    g = B_HEADS // B_KV_HEADS
    tq = q_ref.shape[0]
    t_all = k_ref.shape[1]
    t = pl.program_id(1)
    n_tiles = n_lat // tq
    is_ctx = t >= n_tiles

    def group(j):
        q = jnp.concatenate([q_ref[:, (j * g + i) * hd:(j * g + i + 1) * hd] for i in range(g)], axis=0)
        sink = jnp.concatenate([jnp.broadcast_to(sink_ref[j, :, i:i + 1], (tq, 1)) for i in range(g)],
                               axis=0) * LOG2E
        kc, vc = k_ref[j, n_lat:t_all, :], v_ref[j, n_lat:t_all, :]
        return q, sink, _dot_nt(q, kc), vc

    def finish(j, o):
        o = o.astype(BF16)
        o_ref[:, j * g * hd:(j + 1) * g * hd] = jnp.concatenate([o[i * tq:(i + 1) * tq] for i in range(g)], axis=1)

    @pl.when(is_ctx)
    def _():
        for j in range(B_KV_HEADS):
            q, sink, s_ctx, vc = group(j)
            m = jnp.maximum(jnp.max(s_ctx, axis=-1, keepdims=True), sink)
            p = jnp.exp2(s_ctx - m)
            l = jnp.sum(p, axis=-1, keepdims=True) + jnp.exp2(sink - m)
            finish(j, jnp.dot(p.astype(BF16), vc, preferred_element_type=F32) / l)

    @pl.when(jnp.logical_not(is_ctx))
    def _():
        wlen = tq + 2 * WINDOW
        ws = jnp.clip(t * tq - WINDOW, 0, n_lat - wlen)
        start = pl.multiple_of(ws, 128)
        bias = bias_ref[jnp.where(t == 0, 0, jnp.where(t == n_tiles - 1, 2, 1))]
        bias = jnp.concatenate([bias] * g, axis=0)
        for j in range(B_KV_HEADS):
            q, sink, _, _ = group(j)
            keys = jnp.concatenate([k_ref[j, pl.ds(start, wlen), :], k_ref[j, n_lat:t_all, :]], axis=0)
            vals = jnp.concatenate([v_ref[j, pl.ds(start, wlen), :], v_ref[j, n_lat:t_all, :]], axis=0)
            s = _dot_nt(q, keys) + bias
            m = jnp.maximum(jnp.max(s, axis=-1, keepdims=True), sink)
            p = jnp.exp2(s - m)
            l = jnp.sum(p, axis=-1, keepdims=True) + jnp.exp2(sink - m)
            finish(j, jnp.dot(p.astype(BF16), vals, preferred_element_type=F32) / l)


def _band_bias(tq, n_ctx):
    row = jnp.arange(tq, dtype=I32)[:, None]
    col = jnp.arange(tq + 2 * WINDOW, dtype=I32)[None, :]
    band = jnp.stack([jnp.where(jnp.abs(col - off - row) <= WINDOW, 0.0, NEG_INF).astype(F32)
                      for off in (0, WINDOW, 2 * WINDOW)])
    return jnp.concatenate([band, jnp.zeros((3, tq, n_ctx), F32)], axis=2)


def _window_attention(q, k, v, sink, *, n_lat, rows):
    b, t_all, d = q.shape
    hd, g = B_HEAD_DIM, B_HEADS // B_KV_HEADS
    tq = Q_BLOCK
    wlen = tq + 2 * WINDOW
    return pl.pallas_call(
        functools.partial(_wattn_kernel, n_lat=n_lat),
        grid=(b, rows // tq),
        in_specs=[pl.BlockSpec((None, tq, d), lambda i, t: (i, t, 0)),
                  pl.BlockSpec((None, B_KV_HEADS, t_all, hd), lambda i, t: (i, 0, 0, 0)),
                  pl.BlockSpec((None, B_KV_HEADS, t_all, hd), lambda i, t: (i, 0, 0, 0)),
                  pl.BlockSpec((B_KV_HEADS, 1, g), lambda i, t: (0, 0, 0)),
                  pl.BlockSpec((3, tq, wlen + t_all - n_lat), lambda i, t: (0, 0, 0))],
        out_specs=pl.BlockSpec((None, tq, d), lambda i, t: (i, t, 0)),
        out_shape=jax.ShapeDtypeStruct((b, t_all, d), BF16),
        compiler_params=_params("parallel", "arbitrary"),
        name="window_attention",
    )(q, k, v, sink.reshape(B_KV_HEADS, 1, g), _band_bias(tq, t_all - n_lat))


def _proj_res_kernel(a_ref, w_ref, x_ref, mod_ref, o_ref):
    o_ref[...] = x_ref[...] + mod_ref[2:3, :] * _bdot(a_ref[...], w_ref[...])


def _proj_residual(a, w, x, modtab, *, n_lat, rows):
    b, t_all, d = x.shape
    tm = ROW_TILE
    return pl.pallas_call(
        _proj_res_kernel,
        grid=(b, rows // tm),
        in_specs=[pl.BlockSpec((None, tm, d), lambda i, t: (i, t, 0)),
                  pl.BlockSpec((d, d), lambda i, t: (0, 0)),
                  pl.BlockSpec((None, tm, d), lambda i, t: (i, t, 0)),
                  _mod_spec(n_lat // tm)],
        out_specs=pl.BlockSpec((None, tm, d), lambda i, t: (i, t, 0)),
        out_shape=jax.ShapeDtypeStruct((b, rows, d), F32),
        compiler_params=_params("parallel", "parallel"),
        name="proj_residual",
    )(a, w, x, modtab)


def _router_kernel(x_ref, mod_ref, rwt_ref, h_ref, aff_ref, afft_ref):
    for s in range(x_ref.shape[0]):
        h = _normmod(x_ref[s], mod_ref.at[s], 3, 4)
        h_ref[s] = h.astype(BF16)
        logits_t = lax.dot_general(rwt_ref[...], h, (((1,), (1,)), ((), ())),
                                   precision=lax.Precision.HIGHEST, preferred_element_type=F32)
        e = jnp.exp(logits_t - jnp.max(logits_t, axis=0, keepdims=True))
        aff_t = e / jnp.sum(e, axis=0, keepdims=True)
        afft_ref[s] = aff_t
        aff_ref[s] = aff_t.T


def _router(x, modtab, router_w_t, *, n_lat):
    b, rows, d = x.shape
    tm = ROW_TILE
    ne = router_w_t.shape[0]
    grp = _sample_group(b)
    return pl.pallas_call(
        _router_kernel,
        grid=(b // grp, rows // tm),
        in_specs=[pl.BlockSpec((grp, tm, d), lambda i, t: (i, t, 0)),
                  _mod_spec(n_lat // tm, grp),
                  pl.BlockSpec((ne, d), lambda i, t: (0, 0))],
        out_specs=[pl.BlockSpec((grp, tm, d), lambda i, t: (i, t, 0)),
                   pl.BlockSpec((grp, tm, ne), lambda i, t: (i, t, 0)),
                   pl.BlockSpec((grp, ne, tm), lambda i, t: (i, 0, t))],
        out_shape=[jax.ShapeDtypeStruct((b, rows, d), BF16),
                   jax.ShapeDtypeStruct((b, rows, ne), F32),
                   jax.ShapeDtypeStruct((b, ne, rows), F32)],
        compiler_params=_params("parallel", "parallel"),
        name="router",
    )(x, modtab, router_w_t)


def _lane_prefix(mask_f, tri):
    n = mask_f.shape[1]
    run = jnp.zeros((mask_f.shape[0], 1), F32)
    out = []
    for j in range(n // 128):
        blk = mask_f[:, j * 128:(j + 1) * 128]
        incl = jnp.dot(blk.astype(BF16), tri, preferred_element_type=F32)
        out.append(incl - blk + run)
        run = run + incl[:, 127:128]
    return jnp.concatenate(out, axis=1)


def _select_kernel(afft_ref, slot_ref, slot_tm_ref, *, segments):
    r = lax.broadcasted_iota(I32, (128, 128), 0)
    c = lax.broadcasted_iota(I32, (128, 128), 1)
    tri = (r <= c).astype(BF16)
    for off, n, cap, base in segments:
        bits = lax.bitcast_convert_type(afft_ref[:, off:off + n], I32)
        cap_f = jnp.float32(cap)

        def body(i, thr):
            cand = thr | jnp.left_shift(jnp.int32(1), 30 - i)
            cnt = jnp.sum((bits >= cand).astype(F32), axis=1, keepdims=True)
            return jnp.where(cnt >= cap_f, cand, thr)

        thr = lax.fori_loop(0, 31, body, jnp.zeros((bits.shape[0], 1), I32))
        gt = (bits > thr).astype(F32)
        eq = (bits == thr).astype(F32)
        need = cap_f - jnp.sum(gt, axis=1, keepdims=True)
        sel = gt + eq * (_lane_prefix(eq, tri) < need).astype(F32)
        slot = _lane_prefix(sel, tri).astype(I32) + base
        slot = jnp.where(sel > 0.5, slot, -1)
        slot_ref[:, off:off + n] = slot
        slot_tm_ref[off:off + n, :] = slot.astype(F32).T


def _select(aff_t, segments):
    b, ne, rows = aff_t.shape
    return pl.pallas_call(
        functools.partial(_select_kernel, segments=segments),
        grid=(b,),
        in_specs=[pl.BlockSpec((None, ne, rows), lambda i: (i, 0, 0))],
        out_specs=[pl.BlockSpec((None, ne, rows), lambda i: (i, 0, 0)),
                   pl.BlockSpec((None, rows, ne), lambda i: (i, 0, 0))],
        out_shape=[jax.ShapeDtypeStruct((b, ne, rows), I32), jax.ShapeDtypeStruct((b, rows, ne), F32)],
        compiler_params=_params("parallel"),
        name="expert_select",
    )(aff_t)


def _gather_kernel(slot_ref, h_ref, xs_ref, *, segments):
    for off, n, cap, base in segments:
        slot = slot_ref[:, off:off + n]
        hseg = h_ref[off:off + n, :]
        st = min(SLOT_TILE, cap)
        for j in range(cap // st):
            ids = base + j * st + lax.broadcasted_iota(I32, (st, 1), 0)
            onehot = jnp.where(ids == slot, 1.0, 0.0).astype(BF16)
            rows = jnp.dot(onehot, hseg, preferred_element_type=F32)
            xs_ref[base + j * st:base + (j + 1) * st, :] = rows.astype(BF16)


def _gather(slot, h, segments, n_slots):
    b, ne, rows = slot.shape
    d = h.shape[2]
    return pl.pallas_call(
        functools.partial(_gather_kernel, segments=segments),
        grid=(b, ne),
        in_specs=[pl.BlockSpec((None, None, 1, rows), lambda i, e: (i, e, 0, 0)),
                  pl.BlockSpec((None, rows, d), lambda i, e: (i, 0, 0))],
        out_specs=pl.BlockSpec((None, None, n_slots, d), lambda i, e: (e, i, 0, 0)),
        out_shape=jax.ShapeDtypeStruct((ne, b, n_slots, d), BF16),
        compiler_params=_params("parallel", "arbitrary"),
        name="expert_gather",
    )(slot.reshape(b, ne, 1, rows), h)


def _ffn_kernel(xs_ref, w1_ref, w3_ref, w2_ref, y_ref):
    xs = xs_ref[...]
    a = _bdot(xs, w1_ref[...])
    g = _bdot(xs, w3_ref[...])
    hid = (a * _sigmoid(a)) * g
    y_ref[...] = _bdot(hid, w2_ref[...]).astype(BF16)


def _expert_ffn(xs, w1, w3, w2, layer):
    ne, b, n_slots, d = xs.shape
    ff = w1.shape[3]
    return pl.pallas_call(
        _ffn_kernel,
        grid=(ne, b),
        in_specs=[pl.BlockSpec((None, None, n_slots, d), lambda e, i: (e, i, 0, 0)),
                  pl.BlockSpec((None, None, d, ff), lambda e, i: (layer, e, 0, 0)),
                  pl.BlockSpec((None, None, d, ff), lambda e, i: (layer, e, 0, 0)),
                  pl.BlockSpec((None, None, ff, d), lambda e, i: (layer, e, 0, 0))],
        out_specs=pl.BlockSpec((None, None, n_slots, d), lambda e, i: (e, i, 0, 0)),
        out_shape=jax.ShapeDtypeStruct((ne, b, n_slots, d), BF16),
        compiler_params=_params("parallel", "arbitrary", vmem_limit=FFN_VMEM_LIMIT),
        name="expert_ffn",
    )(xs, w1, w3, w2)


def _combine_kernel(slot_ref, aff_ref, y_ref, x_ref, mod_ref, fw_ref, o_ref, acc_ref, *, slot0, final):
    eg = pl.program_id(2)
    n_groups = pl.num_programs(2)
    group, width = y_ref.shape[:2]

    @pl.when(eg == 0)
    def _():
        acc_ref[...] = jnp.zeros_like(acc_ref)

    lane_e = lax.broadcasted_iota(I32, aff_ref.shape, 1)
    ids = slot0 + lax.broadcasted_iota(I32, (1, width), 1)
    total = None
    for j in range(group):
        e = eg * group + j
        gate = jnp.sum(jnp.where(lane_e == e, aff_ref[...], 0.0), axis=1, keepdims=True)
        slot = jnp.sum(jnp.where(lane_e == e, slot_ref[...], 0.0), axis=1, keepdims=True).astype(I32)
        onehot = jnp.where(slot == ids, 1.0, 0.0).astype(BF16)
        part = gate * jnp.dot(onehot, y_ref[j], preferred_element_type=F32)
        total = part if total is None else total + part
    acc_ref[...] += total

    @pl.when(eg == n_groups - 1)
    def _():
        out = x_ref[...] + mod_ref[5:6, :] * acc_ref[...]
        o_ref[...] = _rms(out) * fw_ref[...] if final else out


def _combine(slot_tm, aff, y, x, modtab, final_w, *, row0, rows, tm, slot0, width, seg, in_place, final=False):
    b, _, d = x.shape
    ne = aff.shape[2]
    group = COMBINE_EXPERTS
    t0, s0 = row0 // tm, slot0 // width
    out_rows = x.shape[1] if in_place else rows
    return pl.pallas_call(
        functools.partial(_combine_kernel, slot0=slot0, final=final),
        grid=(b, rows // tm, ne // group),
        in_specs=[pl.BlockSpec((None, tm, ne), lambda i, t, e: (i, t + t0, 0)),
                  pl.BlockSpec((None, tm, ne), lambda i, t, e: (i, t + t0, 0)),
                  pl.BlockSpec((group, None, width, d), lambda i, t, e: (e, i, s0, 0)),
                  pl.BlockSpec((None, tm, d), lambda i, t, e: (i, t + t0, 0)),
                  pl.BlockSpec((None, None, 6, d), lambda i, t, e: (i, seg, 0, 0)),
                  pl.BlockSpec((1, d), lambda i, t, e: (0, 0))],
        out_specs=pl.BlockSpec((None, tm, d), lambda i, t, e: (i, t + (t0 if in_place else 0), 0)),
        out_shape=jax.ShapeDtypeStruct((b, out_rows, d), F32),
        scratch_shapes=[pltpu.VMEM((tm, d), F32)],
        input_output_aliases={3: 0} if in_place else {},
        compiler_params=_params("parallel", "parallel", "arbitrary"),
        name="expert_combine",
    )(slot_tm, aff, y, x, modtab, final_w.reshape(1, d))


def _moe(x, modtab, router_w_t, w1, w3, w2, layer, final_w, *, n_lat, with_ctx):
    b, rows, d = x.shape
    ne = N_EXPERTS
    cap_l = CAPACITY_FACTOR * n_lat // ne
    h, aff, aff_t = _router(x, modtab, router_w_t, n_lat=n_lat)
    segments = ((0, n_lat, cap_l, 0),)
    n_slots = cap_l
    if with_ctx:
        n_ctx = rows - n_lat
        cap_c = CAPACITY_FACTOR * n_ctx // ne
        segments += ((n_lat, n_ctx, cap_c, cap_l),)
        n_slots += cap_c
    slot, slot_tm = _select(aff_t, segments)
    xs = _gather(slot, h, segments, n_slots)
    y = _expert_ffn(xs, w1, w3, w2, layer)
    out = _combine(slot_tm, aff, y, x, modtab, final_w, row0=0, rows=n_lat, tm=min(COMBINE_TILE, n_lat),
                   slot0=0, width=cap_l, seg=1, in_place=with_ctx, final=not with_ctx)
    if with_ctx:
        out = _combine(slot_tm, aff, y, out, modtab, final_w, row0=n_lat, rows=n_ctx, tm=n_ctx, slot0=cap_l,
                       width=cap_c, seg=0, in_place=True)
    return out


def _segsum64(x, bd):
    hi = x.astype(BF16)
    lo = (x - hi.astype(F32)).astype(BF16)
    out = []
    for j in range(x.shape[1] // 256):
        sl = slice(j * 256, (j + 1) * 256)
        out.append(jnp.dot(hi[:, sl], bd, preferred_element_type=F32)
                   + jnp.dot(lo[:, sl], bd, preferred_element_type=F32))
    return jnp.concatenate(out, axis=1)


def _block_diag_ones():
    r = lax.broadcasted_iota(I32, (256, 256), 0)
    c = lax.broadcasted_iota(I32, (256, 256), 1)
    return (r // 64 == c // 64).astype(BF16)


def _softplus(x):
    return jnp.maximum(x, 0.0) + jnp.log(1.0 + jnp.exp(-jnp.abs(x)))


def _rwkv_feat_kernel(x_ref, xp_ref, xn_ref, mod_ref, mu_ref, wrkv_ref, w0_ref, w1_ref, w2_ref,
                      a0_ref, a1_ref, a2_ref, g1_ref, g2_ref, kk_ref, ka_ref,
                      r_out, v_out, nkk_out, g_out, w_out, k_out, b_out, *, n_lat_tiles):
    t = pl.program_id(1)
    nt = pl.num_programs(1)
    tm = x_ref.shape[0]
    h = _normmod(x_ref[...], mod_ref, 0, 1)
    hp = _normmod(xp_ref[7:8, :], mod_ref, 0, 1)
    hn = _normmod(xn_ref[0:1, :], mod_ref, 0, 1)
    has_left = jnp.logical_and(t != 0, t != n_lat_tiles)
    has_right = jnp.logical_and(t != n_lat_tiles - 1, t != nt - 1)
    hp = jnp.where(has_left, hp, 0.0)
    hn = jnp.where(has_right, hn, 0.0)
    row = lax.broadcasted_iota(I32, h.shape, 0)
    left = jnp.where(row == 0, hp, pltpu.roll(h, 1, axis=0))
    right = jnp.where(row == tm - 1, hn, pltpu.roll(h, tm - 1, axis=0))
    xx = 0.5 * (left + right) - h
    xr, xw, xk, xv, xa, xg = (h + xx * mu_ref[i:i + 1, :] for i in range(6))
    r = _bdot(xr, wrkv_ref[0])
    k = _bdot(xk, wrkv_ref[1])
    v = _bdot(xv, wrkv_ref[2])
    g = _bdot(_sigmoid(_bdot(xg, g1_ref[...])), g2_ref[...])
    bd = _block_diag_ones()
    kk = k * kk_ref[...]
    kk = kk * lax.rsqrt(jnp.maximum(_segsum64(kk * kk, bd), 1e-24))
    r_out[...] = r
    v_out[...] = v
    nkk_out[...] = -kk
    g_out[...] = g
    for d in range(2):
        w_lora = _bdot(jnp.tanh(_bdot(xw, w1_ref[d])), w2_ref[d])
        log_w = -_softplus(-(w0_ref[d:d + 1, :] + w_lora)) - 0.5
        w_out[d] = jnp.exp(-jnp.exp(log_w))
        a = _sigmoid(a0_ref[d:d + 1, :] + _bdot(_bdot(xa, a1_ref[d]), a2_ref[d]))
        k_out[d] = k * (1.0 + (a - 1.0) * ka_ref[...])
        b_out[d] = kk * a


def _rwkv_features(x, modtab, p, *, n_lat):
    b, t_all, d = x.shape
    tm = ROW_TILE
    nt = t_all // tm
    tb = tm // 8
    full = lambda shape: pl.BlockSpec(shape, lambda i, t: (0,) * len(shape))
    tok = pl.BlockSpec((None, tm, d), lambda i, t: (i, t, 0))
    tok2 = pl.BlockSpec((2, None, tm, d), lambda i, t: (0, i, t, 0))
    one = jax.ShapeDtypeStruct((b, t_all, d), F32)
    two = jax.ShapeDtypeStruct((2, b, t_all, d), F32)
    return pl.pallas_call(
        functools.partial(_rwkv_feat_kernel, n_lat_tiles=n_lat // tm),
        grid=(b, nt),
        in_specs=[tok,
                  pl.BlockSpec((None, 8, d), lambda i, t: (i, jnp.maximum(t * tb - 1, 0), 0)),
                  pl.BlockSpec((None, 8, d), lambda i, t: (i, jnp.minimum((t + 1) * tb, nt * tb - 1), 0)),
                  _mod_spec(n_lat // tm),
                  full((6, d)), full((3, d, d)), full((2, d)), full(p["w1"].shape), full(p["w2"].shape),
                  full((2, d)), full(p["a1"].shape), full(p["a2"].shape), full(p["g1"].shape),
                  full(p["g2"].shape), full((1, d)), full((1, d))],
        out_specs=[tok, tok, tok, tok, tok2, tok2, tok2],
        out_shape=[one, one, one, one, two, two, two],
        compiler_params=_params("parallel", "parallel"),
        name="rwkv_features",
    )(x, x, x, modtab, p["mu"], p["w_rkv"], p["w0"], p["w1"], p["w2"], p["a0"], p["a1"], p["a2"],
      p["g1"], p["g2"], p["k_k"], p["k_a"])


def _scan_kernel(r_ref, w_ref, k_ref, v_ref, a_ref, b_ref, o_ref, s_ref, wr_ref, *, reverse):
    n = s_ref.shape[0]
    steps = r_ref.shape[0]

    @pl.when(pl.program_id(0) == 0)
    def _():
        s_ref[...] = jnp.zeros_like(s_ref)

    def step(i, carry):
        j = steps - 1 - i if reverse else i
        r = r_ref[j]
        wr_ref[...] = w_ref[j] * r
        br = jnp.sum(b_ref[j] * r, axis=0, keepdims=True)
        kr = jnp.sum(k_ref[j] * r, axis=0, keepdims=True)
        acc = [jnp.zeros(s_ref.shape[1:], F32) for _ in range(4)]
        for kk in range(n):
            s = s_ref[kk]
            acc[kk % 2] = acc[kk % 2] + s * a_ref[j, kk:kk + 1, :]
            acc[2 + kk % 2] = acc[2 + kk % 2] + s * wr_ref[kk:kk + 1, :]
        sa = acc[0] + acc[1]
        v = v_ref[j]
        o_ref[j] = acc[2] + acc[3] + sa * br + v * kr
        for kk in range(n):
            s_ref[kk] = (s_ref[kk] * w_ref[j, kk:kk + 1, :] + sa * b_ref[j, kk:kk + 1, :]
                         + v * k_ref[j, kk:kk + 1, :])
        return carry

    lax.fori_loop(0, steps, step, 0)


def _wkv_scan(r, w, k, v, a, b, *, n_lat, reverse):
    t_all, n, chains = r.shape
    tc = SCAN_CHUNK
    nlc, nch = n_lat // tc, t_all // tc
    ncc = nch - nlc
    if reverse:
        idx = lambda c: (jnp.where(c < ncc, nch - 1 - c, nlc - 1 - (c - ncc)), 0, 0)
    else:
        idx = lambda c: (jnp.where(c < ncc, nlc + c, c - ncc), 0, 0)
    spec = pl.BlockSpec((tc, n, chains), idx)
    return pl.pallas_call(
        functools.partial(_scan_kernel, reverse=reverse),
        grid=(nch,),
        in_specs=[spec] * 6,
        out_specs=spec,
        out_shape=jax.ShapeDtypeStruct((t_all, n, chains), F32),
        scratch_shapes=[pltpu.VMEM((n, n, chains), F32), pltpu.VMEM((n, chains), F32)],
        compiler_params=_params("arbitrary"),
        name="wkv_scan_bwd" if reverse else "wkv_scan_fwd",
    )(r, w, k, v, a, b)


def _rwkv_out_kernel(o_ref, r_ref, k_ref, v_ref, g_ref, rk_ref, lnw_ref, lnb_ref, wo_ref, x_ref, mod_ref,
                     out_ref):
    bd = _block_diag_ones()
    inv_n = 1.0 / C_HEAD_DIM
    o = o_ref[...]
    o = o - _segsum64(o, bd) * inv_n
    o = o * lax.rsqrt(_segsum64(o * o, bd) * inv_n + C_GN_EPS)
    o = o * lnw_ref[...] + lnb_ref[...]
    r = r_ref[...]
    bonus = _segsum64(r * k_ref[0] * rk_ref[0:1, :] + r * k_ref[1] * rk_ref[1:2, :], bd) * v_ref[...]
    y = _bdot((o + bonus) * g_ref[...], wo_ref[...])
    out_ref[...] = x_ref[...] + mod_ref[2:3, :] * y


def _rwkv_readout(o, r, k2, v, g, p, x, modtab, *, n_lat, rows):
    b, t_all, d = x.shape
    tm = ROW_TILE
    tok = pl.BlockSpec((None, tm, d), lambda i, t: (i, t, 0))
    full = lambda shape: pl.BlockSpec(shape, lambda i, t: (0,) * len(shape))
    return pl.pallas_call(
        _rwkv_out_kernel,
        grid=(b, rows // tm),
        in_specs=[tok, tok, pl.BlockSpec((2, None, tm, d), lambda i, t: (0, i, t, 0)), tok, tok,
                  full((2, d)), full((1, d)), full((1, d)), full((d, d)), tok, _mod_spec(n_lat // tm)],
        out_specs=tok,
        out_shape=jax.ShapeDtypeStruct((b, rows, d), F32),
        compiler_params=_params("parallel", "parallel"),
        name="rwkv_readout",
    )(o, r, k2, v, g, p["r_k"], p["ln_w"], p["ln_b"], p["w_o"], x, modtab)


def _to_scan_kernel(x_ref, o_ref, y_ref):
    nb, tt, d = x_ref.shape
    n, chains = o_ref.shape[1:]
    for b in range(nb):
        for p in range(d // 128):
            row = (b * (d // 128) + p) * 128
            y_ref[row:row + 128, :] = x_ref[b, :, p * 128:(p + 1) * 128].T
    for k in range(n):
        o_ref[:, k, :] = y_ref[pl.ds(k, chains, stride=n), :].T


def _to_scan_layout(a, d=None):
    b, t, dm = a.shape[-3:]
    tt = 128
    chains = b * dm // C_HEAD_DIM
    if d is None:
        spec = pl.BlockSpec((b, tt, dm), lambda i: (0, i, 0))
    else:
        spec = pl.BlockSpec((None, b, tt, dm), lambda i: (d, 0, i, 0))
    return pl.pallas_call(
        _to_scan_kernel,
        grid=(t // tt,),
        in_specs=[spec],
        out_specs=pl.BlockSpec((tt, C_HEAD_DIM, chains), lambda i: (i, 0, 0)),
        out_shape=jax.ShapeDtypeStruct((t, C_HEAD_DIM, chains), F32),
        scratch_shapes=[pltpu.VMEM((b * dm, tt), F32)],
        compiler_params=_params("parallel"),
        name="to_scan_layout",
    )(a)


def _from_scan_kernel(a_ref, b_ref, o_ref, y_ref):
    nb, tt, d = o_ref.shape
    n, chains = a_ref.shape[1:]
    for k in range(n):
        y_ref[pl.ds(k, chains, stride=n), :] = (a_ref[:, k, :] + b_ref[:, k, :]).T
    for b in range(nb):
        for p in range(d // 128):
            row = (b * (d // 128) + p) * 128
            o_ref[b, :, p * 128:(p + 1) * 128] = y_ref[row:row + 128, :].T


def _from_scan_layout(o_f, o_b, b):
    t, n, chains = o_f.shape
    tt = 128
    dm = n * chains // b
    spec = pl.BlockSpec((tt, n, chains), lambda i: (i, 0, 0))
    return pl.pallas_call(
        _from_scan_kernel,
        grid=(t // tt,),
        in_specs=[spec, spec],
        out_specs=pl.BlockSpec((b, tt, dm), lambda i: (0, i, 0)),
        out_shape=jax.ShapeDtypeStruct((b, t, dm), F32),
        scratch_shapes=[pltpu.VMEM((b * dm, tt), F32)],
        compiler_params=_params("parallel"),
        name="from_scan_layout",
    )(o_f, o_b)


def _rwkv_mixer(x, modtab, p, *, n_lat, rows):
    b = x.shape[0]
    r, v, nkk, g, w2, k2, b2 = _rwkv_features(x, modtab, p, n_lat=n_lat)
    rs, vs, as_ = _to_scan_layout(r), _to_scan_layout(v), _to_scan_layout(nkk)
    o_f, o_b = (_wkv_scan(rs, _to_scan_layout(w2, d), _to_scan_layout(k2, d), vs, as_, _to_scan_layout(b2, d),
                          n_lat=n_lat, reverse=(d == 1)) for d in range(2))
    return _rwkv_readout(_from_scan_layout(o_f, o_b, b), r, k2, v, g, p, x, modtab, n_lat=n_lat, rows=rows)


def _rope_tables(n_lat, n_ctx, head_dim):
    rows = jnp.repeat(jnp.arange(n_lat // GRID_W, dtype=I32), GRID_W).astype(F32)
    cols = jnp.tile(jnp.arange(GRID_W, dtype=I32), n_lat // GRID_W).astype(F32)
    n_freq = head_dim // 4
    inv_freq = ROPE_THETA ** (-jnp.arange(n_freq, dtype=F32) / n_freq)
    ang = jnp.concatenate([rows[:, None] * inv_freq, cols[:, None] * inv_freq], axis=-1)
    cos, sin = jnp.cos(ang), jnp.sin(ang)
    reps = 128 // head_dim
    cos = jnp.tile(jnp.concatenate([cos, cos], axis=-1), (1, reps))
    sin = jnp.tile(jnp.concatenate([-sin, sin], axis=-1), (1, reps))
    cos = jnp.concatenate([cos, jnp.ones((n_ctx, 128), F32)], axis=0)
    sin = jnp.concatenate([sin, jnp.zeros((n_ctx, 128), F32)], axis=0)
    return cos, sin


def kernel(x, c, ctx, c_ctx, mod_w, mod_b, a_w_qkv, a_w_o, a_q_norm, a_k_norm, b_w_qkv, b_w_o, b_sink,
           c_mu, c_w_rkv, c_w_o, c_w0, c_w1, c_w2, c_a0, c_a1, c_a2, c_g1, c_g2, c_k_k, c_k_a, c_r_k,
           c_ln_w, c_ln_b, router_w, ffn_w1, ffn_w3, ffn_w2, final_norm):
    b, n_lat, d = x.shape
    n_ctx = ctx.shape[1]
    t_all = n_lat + n_ctx
    depth = mod_w.shape[0]
    assert d == D_MODEL and n_ctx % ROW_TILE == 0 and n_lat % min(COMBINE_TILE, n_lat) == 0

    cond_rows = -(-(b + 1) // 8) * 8
    cond = jnp.zeros((cond_rows, d), F32).at[:b].set(c).at[b].set(c_ctx)
    mods = _mod_tables(cond, mod_w, mod_b).reshape(depth, cond_rows, 6, d)
    cos_a, sin_a = _rope_tables(n_lat, n_ctx, A_HEAD_DIM)
    cos_b, sin_b = _rope_tables(n_lat, n_ctx, B_HEAD_DIM)
    xs = jnp.concatenate([x, ctx], axis=1)
    for i in range(depth):
        last = i == depth - 1
        rows = n_lat if last else t_all
        kind, j = i % N_MIXERS, i // N_MIXERS
        modtab = jnp.stack([jnp.broadcast_to(mods[i, b], (b, 6, d)), mods[i, :b]], axis=1)
        if kind == 0:
            q, k, v = _qkv_proj(xs, modtab, a_w_qkv[j].astype(BF16), a_q_norm[j].reshape(1, -1),
                                a_k_norm[j].reshape(1, -1), cos_a, sin_a, kind=0, n_lat=n_lat)
            o = _global_attention(q, k, v, n_lat=n_lat, rows=rows)
            xs = _proj_residual(o, a_w_o[j].astype(BF16), xs, modtab, n_lat=n_lat, rows=rows)
        elif kind == 1:
            ones = jnp.ones((1, 128), F32)
            q, k, v = _qkv_proj(xs, modtab, b_w_qkv[j].astype(BF16), ones, ones, cos_b, sin_b,
                                kind=1, n_lat=n_lat)
            o = _window_attention(q, k, v, b_sink[j], n_lat=n_lat, rows=rows)
            xs = _proj_residual(o, b_w_o[j].astype(BF16), xs, modtab, n_lat=n_lat, rows=rows)
        else:
            p = dict(mu=c_mu[j], w_rkv=c_w_rkv[j].astype(BF16), w_o=c_w_o[j].astype(BF16), w0=c_w0[j],
                     w1=c_w1[j].astype(BF16), w2=c_w2[j].astype(BF16), a0=c_a0[j],
                     a1=c_a1[j].astype(BF16), a2=c_a2[j].astype(BF16), g1=c_g1[j].astype(BF16),
                     g2=c_g2[j].astype(BF16), k_k=c_k_k[j].reshape(1, d), k_a=c_k_a[j].reshape(1, d),
                     r_k=c_r_k[j].reshape(2, d), ln_w=c_ln_w[j].reshape(1, d), ln_b=c_ln_b[j].reshape(1, d))
            xs = _rwkv_mixer(xs, modtab, p, n_lat=n_lat, rows=rows)
        xs = _moe(xs, modtab, router_w[i].T, ffn_w1, ffn_w3, ffn_w2, i, final_norm, n_lat=n_lat,
                  with_ctx=not last)
    return xs
```

```python
import functools
import math

import jax
import jax.numpy as jnp
from jax import lax
from jax.experimental import pallas as pl
from jax.experimental.pallas import tpu as pltpu

F32 = jnp.float32
BF16 = jnp.bfloat16
I32 = jnp.int32

D_MODEL = 1024
GRID_W = 64
Q_BLOCK = 128
ROPE_THETA = 10000.0
NORM_EPS = 1e-6
NEG_INF = -1e30
A_HEADS, A_KV_HEADS, A_HEAD_DIM = 8, 2, 128
B_HEADS, B_KV_HEADS, B_HEAD_DIM = 16, 4, 64
WINDOW = 128
C_HEAD_DIM = 64
C_HEADS = D_MODEL // C_HEAD_DIM
C_GN_EPS = C_HEAD_DIM * 1e-5
N_EXPERTS = 16
EXPERT_FF = 2 * D_MODEL
CAPACITY_FACTOR = 2
N_MIXERS = 3
LOG2E = math.log2(math.e)

ROW_TILE = 256
COMBINE_TILE = 1024
COMBINE_EXPERTS = 8
SAMPLE_GROUP = 2
SLOT_TILE = 128
SCAN_CHUNK = 32
VMEM_LIMIT = 56 * 1024 * 1024
FFN_VMEM_LIMIT = 60 * 1024 * 1024


def _params(*sem, vmem_limit=VMEM_LIMIT):
    return pltpu.CompilerParams(dimension_semantics=sem, vmem_limit_bytes=vmem_limit)


def _rms(x):
    return x * lax.rsqrt(jnp.mean(x * x, axis=-1, keepdims=True) + NORM_EPS)


def _normmod(x, mod_ref, shift_row, scale_row):
    return _rms(x) * (1.0 + mod_ref[scale_row:scale_row + 1, :]) + mod_ref[shift_row:shift_row + 1, :]


def _bdot(a, b):
    return jnp.dot(a.astype(BF16), b.astype(BF16), preferred_element_type=F32)


def _dot_nt(a, b):
    return lax.dot_general(a.astype(BF16), b.astype(BF16), (((1,), (1,)), ((), ())),
                           preferred_element_type=F32)


def _sigmoid(x):
    return 1.0 / (1.0 + jnp.exp(-x))


def _mod_spec(n_lat_tiles, group=None):
    return pl.BlockSpec((group, None, 6, D_MODEL),
                        lambda b, t: (b, jnp.where(t < n_lat_tiles, 1, 0), 0, 0))


def _sample_group(b):
    return SAMPLE_GROUP if b % SAMPLE_GROUP == 0 else 1


def _mod_kernel(cond_ref, w_ref, b_ref, o_ref):
    c = cond_ref[...]
    a = c * _sigmoid(c)
    o_ref[0] = _bdot(a, w_ref[0]) + b_ref[0]


def _mod_tables(cond, mod_w, mod_b):
    depth, d, n = mod_w.shape
    rows = cond.shape[0]
    return pl.pallas_call(
        _mod_kernel,
        grid=(depth, n // d),
        in_specs=[pl.BlockSpec((rows, d), lambda i, j: (0, 0)),
                  pl.BlockSpec((1, d, d), lambda i, j: (i, 0, j)),
                  pl.BlockSpec((1, 1, d), lambda i, j: (i, 0, j))],
        out_specs=pl.BlockSpec((1, rows, d), lambda i, j: (i, 0, j)),
        out_shape=jax.ShapeDtypeStruct((depth, rows, n), F32),
        compiler_params=_params("arbitrary", "arbitrary"),
        name="mod_tables",
    )(cond, mod_w, mod_b.reshape(depth, 1, n))


def _rope128(x, cos, sin_signed, half):
    if half == 64:
        rot = pltpu.roll(x, 64, axis=1)
    else:
        lane = lax.broadcasted_iota(I32, x.shape, 1)
        rot = jnp.where((lane % 64) < 32, pltpu.roll(x, 96, axis=1), pltpu.roll(x, 32, axis=1))
    return x * cos + rot * sin_signed


def _qkv_kernel(x_ref, mod_ref, *refs, kind):
    shared, outs = refs[:5], refs[5:]
    for s in range(x_ref.shape[0]):
        _qkv_tile(x_ref.at[s], mod_ref.at[s], *shared, *(o.at[s] for o in outs), kind=kind)


def _qkv_tile(x_ref, mod_ref, w_ref, gq_ref, gk_ref, cos_ref, sin_ref, q_ref, k_ref, v_ref, *, kind):
    h = _normmod(x_ref[...], mod_ref, 0, 1)
    y = _bdot(h, w_ref[...])
    cos, sin = cos_ref[...], sin_ref[...]
    nq = D_MODEL
    nkv = (y.shape[1] - nq) // 2
    half = 64 if kind == 0 else 32
    q_scale = (2 * half) ** -0.5 * LOG2E
    for j in range((nq + nkv) // 128):
        s = y[:, j * 128:(j + 1) * 128]
        if kind == 0:
            gain = gq_ref[...] if j < nq // 128 else gk_ref[...]
            s = _rms(s) * gain
        s = _rope128(s, cos, sin, half)
        if j < nq // 128:
            q_ref[:, j * 128:(j + 1) * 128] = (s * q_scale).astype(BF16)
        elif kind == 0:
            k_ref[:, (j - nq // 128) * 128:(j - nq // 128 + 1) * 128] = s.astype(BF16)
        else:
            jj = (j - nq // 128) * 2
            k_ref[jj] = s[:, :64].astype(BF16)
            k_ref[jj + 1] = s[:, 64:].astype(BF16)
    v = y[:, nq + nkv:].astype(BF16)
    if kind == 0:
        v_ref[...] = v
    else:
        for jj in range(nkv // 64):
            v_ref[jj] = v[:, jj * 64:(jj + 1) * 64]


def _qkv_proj(x, modtab, w, gq, gk, cos, sin, *, kind, n_lat):
    b, t_all, d = x.shape
    tm = ROW_TILE
    nt = t_all // tm
    n = w.shape[1]
    nkv = (n - d) // 2
    grp = _sample_group(b)
    if kind == 0:
        kv_shape = jax.ShapeDtypeStruct((b, t_all, nkv), BF16)
        kv_spec = pl.BlockSpec((grp, tm, nkv), lambda i, t: (i, t, 0))
    else:
        kv_shape = jax.ShapeDtypeStruct((b, nkv // 64, t_all, 64), BF16)
        kv_spec = pl.BlockSpec((grp, nkv // 64, tm, 64), lambda i, t: (i, 0, t, 0))
    return pl.pallas_call(
        functools.partial(_qkv_kernel, kind=kind),
        grid=(b // grp, nt),
        in_specs=[pl.BlockSpec((grp, tm, d), lambda i, t: (i, t, 0)),
                  _mod_spec(n_lat // tm, grp),
                  pl.BlockSpec((d, n), lambda i, t: (0, 0)),
                  pl.BlockSpec((1, 128), lambda i, t: (0, 0)),
                  pl.BlockSpec((1, 128), lambda i, t: (0, 0)),
                  pl.BlockSpec((tm, 128), lambda i, t: (t, 0)),
                  pl.BlockSpec((tm, 128), lambda i, t: (t, 0))],
        out_specs=[pl.BlockSpec((grp, tm, d), lambda i, t: (i, t, 0)), kv_spec, kv_spec],
        out_shape=[jax.ShapeDtypeStruct((b, t_all, d), BF16), kv_shape, kv_shape],
        compiler_params=_params("parallel", "parallel"),
        name=f"qkv_proj_{kind}",
    )(x, modtab, w, gq, gk, cos, sin)


def _flash_chunk(carry, q, kc, vc):
    m, l, acc = carry
    s = _dot_nt(q, kc)
    m_new = jnp.maximum(m, jnp.max(s, axis=-1, keepdims=True))
    alpha = jnp.exp2(m - m_new)
    p = jnp.exp2(s - m_new)
    l = alpha * l + jnp.sum(p, axis=-1, keepdims=True)
    acc = alpha * acc + jnp.dot(p.astype(BF16), vc, preferred_element_type=F32)
    return m_new, l, acc


def _gattn_kernel(q_ref, k_ref, v_ref, wo_ref, x_ref, mod_ref, o_ref, att_ref, *, n_lat, kchunk):
    hd = A_HEAD_DIM
    g = A_HEADS // A_KV_HEADS
    tq = q_ref.shape[0]
    t_all = k_ref.shape[0]
    groups = range(A_KV_HEADS)
    qs = [jnp.concatenate([q_ref[:, (j * g + i) * hd:(j * g + i + 1) * hd] for i in range(g)], axis=0)
          for j in groups]
    init = (jnp.full((g * tq, 1), NEG_INF, F32), jnp.zeros((g * tq, 1), F32), jnp.zeros((g * tq, hd), F32))

    def chunk(j, carry, rows):
        return _flash_chunk(carry, qs[j], k_ref[rows, j * hd:(j + 1) * hd], v_ref[rows, j * hd:(j + 1) * hd])

    def finish(carries):
        for j, (_, l, acc) in enumerate(carries):
            o = (acc / l).astype(BF16)
            for i in range(g):
                att_ref[:, (j * g + i) * hd:(j * g + i + 1) * hd] = o[i * tq:(i + 1) * tq]

    ctx_carries = tuple(chunk(j, init, slice(n_lat, t_all)) for j in groups)
    is_ctx = pl.program_id(1) >= n_lat // tq

    @pl.when(is_ctx)
    def _():
        finish(ctx_carries)

    @pl.when(jnp.logical_not(is_ctx))
    def _():
        def body(c, carries):
            rows = pl.ds(pl.multiple_of(c * kchunk, kchunk), kchunk)
            return tuple(chunk(j, carries[j], rows) for j in groups)
        finish(lax.fori_loop(0, n_lat // kchunk, body, ctx_carries))

    _project_residual(att_ref, wo_ref, x_ref, mod_ref, o_ref)


def _project_residual(att_ref, wo_ref, x_ref, mod_ref, o_ref):
    o_ref[...] = x_ref[...] + mod_ref[2:3, :] * jnp.dot(att_ref[...], wo_ref[...], preferred_element_type=F32)


def _global_attention(q, k, v, w_o, x, modtab, *, n_lat, rows):
    b, t_all, d = q.shape
    nkv = k.shape[2]
    tq = ROW_TILE
    return pl.pallas_call(
        functools.partial(_gattn_kernel, n_lat=n_lat, kchunk=min(2048, n_lat)),
        grid=(b, rows // tq),
        in_specs=[pl.BlockSpec((None, tq, d), lambda i, t: (i, t, 0)),
                  pl.BlockSpec((None, t_all, nkv), lambda i, t: (i, 0, 0)),
                  pl.BlockSpec((None, t_all, nkv), lambda i, t: (i, 0, 0)),
                  pl.BlockSpec((d, d), lambda i, t: (0, 0)),
                  pl.BlockSpec((None, tq, d), lambda i, t: (i, t, 0)),
                  _mod_spec(n_lat // tq)],
        out_specs=pl.BlockSpec((None, tq, d), lambda i, t: (i, t, 0)),
        out_shape=jax.ShapeDtypeStruct((b, rows, d), F32),
        scratch_shapes=[pltpu.VMEM((tq, d), BF16)],
        compiler_params=_params("parallel", "arbitrary"),
        name="global_attention",
    )(q, k, v, w_o, x, modtab)


def _wattn_kernel(q_ref, k_ref, v_ref, sink_ref, bias_ref, wo_ref, x_ref, mod_ref, o_ref, att_ref, *, n_lat):
    hd = B_HEAD_DIM
    g = B_HEADS // B_KV_HEADS
    tq = q_ref.shape[0]
    t_all = k_ref.shape[1]
    t = pl.program_id(1)
    n_tiles = n_lat // tq
    is_ctx = t >= n_tiles

    def group(j):
        q = jnp.concatenate([q_ref[:, (j * g + i) * hd:(j * g + i + 1) * hd] for i in range(g)], axis=0)
        sink = jnp.concatenate([jnp.broadcast_to(sink_ref[j, :, i:i + 1], (tq, 1)) for i in range(g)],
                               axis=0) * LOG2E
        kc, vc = k_ref[j, n_lat:t_all, :], v_ref[j, n_lat:t_all, :]
        return q, sink, _dot_nt(q, kc), vc

    def finish(j, o):
        o = o.astype(BF16)
        att_ref[:, j * g * hd:(j + 1) * g * hd] = jnp.concatenate([o[i * tq:(i + 1) * tq] for i in range(g)], axis=1)

    @pl.when(is_ctx)
    def _():
        for j in range(B_KV_HEADS):
            q, sink, s_ctx, vc = group(j)
            m = jnp.maximum(jnp.max(s_ctx, axis=-1, keepdims=True), sink)
            p = jnp.exp2(s_ctx - m)
            l = jnp.sum(p, axis=-1, keepdims=True) + jnp.exp2(sink - m)
            finish(j, jnp.dot(p.astype(BF16), vc, preferred_element_type=F32) / l)

    @pl.when(jnp.logical_not(is_ctx))
    def _():
        wlen = tq + 2 * WINDOW
        ws = jnp.clip(t * tq - WINDOW, 0, n_lat - wlen)
        start = pl.multiple_of(ws, 128)
        bias = bias_ref[jnp.where(t == 0, 0, jnp.where(t == n_tiles - 1, 2, 1))]
        bias = jnp.concatenate([bias] * g, axis=0)
        for j in range(B_KV_HEADS):
            q, sink, _, _ = group(j)
            keys = jnp.concatenate([k_ref[j, pl.ds(start, wlen), :], k_ref[j, n_lat:t_all, :]], axis=0)
            vals = jnp.concatenate([v_ref[j, pl.ds(start, wlen), :], v_ref[j, n_lat:t_all, :]], axis=0)
            s = _dot_nt(q, keys) + bias
            m = jnp.maximum(jnp.max(s, axis=-1, keepdims=True), sink)
            p = jnp.exp2(s - m)
            l = jnp.sum(p, axis=-1, keepdims=True) + jnp.exp2(sink - m)
            finish(j, jnp.dot(p.astype(BF16), vals, preferred_element_type=F32) / l)

    _project_residual(att_ref, wo_ref, x_ref, mod_ref, o_ref)


def _band_bias(tq, n_ctx):
    row = jnp.arange(tq, dtype=I32)[:, None]
    col = jnp.arange(tq + 2 * WINDOW, dtype=I32)[None, :]
    band = jnp.stack([jnp.where(jnp.abs(col - off - row) <= WINDOW, 0.0, NEG_INF).astype(F32)
                      for off in (0, WINDOW, 2 * WINDOW)])
    return jnp.concatenate([band, jnp.zeros((3, tq, n_ctx), F32)], axis=2)


def _window_attention(q, k, v, sink, w_o, x, modtab, *, n_lat, rows):
    b, t_all, d = q.shape
    hd, g = B_HEAD_DIM, B_HEADS // B_KV_HEADS
    tq = Q_BLOCK
    wlen = tq + 2 * WINDOW
    return pl.pallas_call(
        functools.partial(_wattn_kernel, n_lat=n_lat),
        grid=(b, rows // tq),
        in_specs=[pl.BlockSpec((None, tq, d), lambda i, t: (i, t, 0)),
                  pl.BlockSpec((None, B_KV_HEADS, t_all, hd), lambda i, t: (i, 0, 0, 0)),
                  pl.BlockSpec((None, B_KV_HEADS, t_all, hd), lambda i, t: (i, 0, 0, 0)),
                  pl.BlockSpec((B_KV_HEADS, 1, g), lambda i, t: (0, 0, 0)),
                  pl.BlockSpec((3, tq, wlen + t_all - n_lat), lambda i, t: (0, 0, 0)),
                  pl.BlockSpec((d, d), lambda i, t: (0, 0)),
                  pl.BlockSpec((None, tq, d), lambda i, t: (i, t, 0)),
                  _mod_spec(n_lat // tq)],
        out_specs=pl.BlockSpec((None, tq, d), lambda i, t: (i, t, 0)),
        out_shape=jax.ShapeDtypeStruct((b, rows, d), F32),
        scratch_shapes=[pltpu.VMEM((tq, d), BF16)],
        compiler_params=_params("parallel", "arbitrary"),
        name="window_attention",
    )(q, k, v, sink.reshape(B_KV_HEADS, 1, g), _band_bias(tq, t_all - n_lat), w_o, x, modtab)


def _router_kernel(x_ref, mod_ref, rwt_ref, h_ref, aff_ref, afft_ref):
    for s in range(x_ref.shape[0]):
        h = _normmod(x_ref[s], mod_ref.at[s], 3, 4)
        h_ref[s] = h.astype(BF16)
        logits_t = lax.dot_general(rwt_ref[...], h, (((1,), (1,)), ((), ())),
                                   precision=lax.Precision.HIGHEST, preferred_element_type=F32)
        e = jnp.exp(logits_t - jnp.max(logits_t, axis=0, keepdims=True))
        aff_t = e / jnp.sum(e, axis=0, keepdims=True)
        afft_ref[s] = aff_t
        aff_ref[s] = aff_t.T


def _router(x, modtab, router_w_t, *, n_lat):
    b, rows, d = x.shape
    tm = ROW_TILE
    ne = router_w_t.shape[0]
    grp = _sample_group(b)
    return pl.pallas_call(
        _router_kernel,
        grid=(b // grp, rows // tm),
        in_specs=[pl.BlockSpec((grp, tm, d), lambda i, t: (i, t, 0)),
                  _mod_spec(n_lat // tm, grp),
                  pl.BlockSpec((ne, d), lambda i, t: (0, 0))],
        out_specs=[pl.BlockSpec((grp, tm, d), lambda i, t: (i, t, 0)),
                   pl.BlockSpec((grp, tm, ne), lambda i, t: (i, t, 0)),
                   pl.BlockSpec((grp, ne, tm), lambda i, t: (i, 0, t))],
        out_shape=[jax.ShapeDtypeStruct((b, rows, d), BF16),
                   jax.ShapeDtypeStruct((b, rows, ne), F32),
                   jax.ShapeDtypeStruct((b, ne, rows), F32)],
        compiler_params=_params("parallel", "parallel"),
        name="router",
    )(x, modtab, router_w_t)


def _lane_prefix(mask_f, tri):
    n = mask_f.shape[1]
    run = jnp.zeros((mask_f.shape[0], 1), F32)
    out = []
    for j in range(n // 128):
        blk = mask_f[:, j * 128:(j + 1) * 128]
        incl = jnp.dot(blk.astype(BF16), tri, preferred_element_type=F32)
        out.append(incl - blk + run)
        run = run + incl[:, 127:128]
    return jnp.concatenate(out, axis=1)


def _select_kernel(afft_ref, slot_ref, slot_tm_ref, *, segments):
    r = lax.broadcasted_iota(I32, (128, 128), 0)
    c = lax.broadcasted_iota(I32, (128, 128), 1)
    tri = (r <= c).astype(BF16)
    for off, n, cap, base in segments:
        bits = lax.bitcast_convert_type(afft_ref[:, off:off + n], I32)
        cap_f = jnp.float32(cap)

        def body(i, thr):
            cand = thr | jnp.left_shift(jnp.int32(1), 30 - i)
            cnt = jnp.sum((bits >= cand).astype(F32), axis=1, keepdims=True)
            return jnp.where(cnt >= cap_f, cand, thr)

        thr = lax.fori_loop(0, 31, body, jnp.zeros((bits.shape[0], 1), I32))
        gt = (bits > thr).astype(F32)
        eq = (bits == thr).astype(F32)
        need = cap_f - jnp.sum(gt, axis=1, keepdims=True)
        sel = gt + eq * (_lane_prefix(eq, tri) < need).astype(F32)
        slot = _lane_prefix(sel, tri).astype(I32) + base
        slot = jnp.where(sel > 0.5, slot, -1)
        slot_ref[:, off:off + n] = slot
        slot_tm_ref[off:off + n, :] = slot.astype(F32).T


def _select(aff_t, segments):
    b, ne, rows = aff_t.shape
    return pl.pallas_call(
        functools.partial(_select_kernel, segments=segments),
        grid=(b,),
        in_specs=[pl.BlockSpec((None, ne, rows), lambda i: (i, 0, 0))],
        out_specs=[pl.BlockSpec((None, ne, rows), lambda i: (i, 0, 0)),
                   pl.BlockSpec((None, rows, ne), lambda i: (i, 0, 0))],
        out_shape=[jax.ShapeDtypeStruct((b, ne, rows), I32), jax.ShapeDtypeStruct((b, rows, ne), F32)],
        compiler_params=_params("parallel"),
        name="expert_select",
    )(aff_t)


def _gather_kernel(slot_ref, h_ref, xs_ref, *, segments):
    for off, n, cap, base in segments:
        slot = slot_ref[:, off:off + n]
        hseg = h_ref[off:off + n, :]
        st = min(SLOT_TILE, cap)
        for j in range(cap // st):
            ids = base + j * st + lax.broadcasted_iota(I32, (st, 1), 0)
            onehot = jnp.where(ids == slot, 1.0, 0.0).astype(BF16)
            rows = jnp.dot(onehot, hseg, preferred_element_type=F32)
            xs_ref[base + j * st:base + (j + 1) * st, :] = rows.astype(BF16)


def _gather(slot, h, segments, n_slots):
    b, ne, rows = slot.shape
    d = h.shape[2]
    return pl.pallas_call(
        functools.partial(_gather_kernel, segments=segments),
        grid=(b, ne),
        in_specs=[pl.BlockSpec((None, None, 1, rows), lambda i, e: (i, e, 0, 0)),
                  pl.BlockSpec((None, rows, d), lambda i, e: (i, 0, 0))],
        out_specs=pl.BlockSpec((None, None, n_slots, d), lambda i, e: (e, i, 0, 0)),
        out_shape=jax.ShapeDtypeStruct((ne, b, n_slots, d), BF16),
        compiler_params=_params("parallel", "arbitrary"),
        name="expert_gather",
    )(slot.reshape(b, ne, 1, rows), h)


def _ffn_kernel(xs_ref, w1_ref, w3_ref, w2_ref, y_ref):
    xs = xs_ref[...]
    a = _bdot(xs, w1_ref[...])
    g = _bdot(xs, w3_ref[...])
    hid = (a * _sigmoid(a)) * g
    y_ref[...] = _bdot(hid, w2_ref[...]).astype(BF16)


def _expert_ffn(xs, w1, w3, w2, layer):
    ne, b, n_slots, d = xs.shape
    ff = w1.shape[3]
    return pl.pallas_call(
        _ffn_kernel,
        grid=(ne, b),
        in_specs=[pl.BlockSpec((None, None, n_slots, d), lambda e, i: (e, i, 0, 0)),
                  pl.BlockSpec((None, None, d, ff), lambda e, i: (layer, e, 0, 0)),
                  pl.BlockSpec((None, None, d, ff), lambda e, i: (layer, e, 0, 0)),
                  pl.BlockSpec((None, None, ff, d), lambda e, i: (layer, e, 0, 0))],
        out_specs=pl.BlockSpec((None, None, n_slots, d), lambda e, i: (e, i, 0, 0)),
        out_shape=jax.ShapeDtypeStruct((ne, b, n_slots, d), BF16),
        compiler_params=_params("parallel", "arbitrary", vmem_limit=FFN_VMEM_LIMIT),
        name="expert_ffn",
    )(xs, w1, w3, w2)


def _combine_kernel(slot_ref, aff_ref, y_ref, x_ref, mod_ref, fw_ref, o_ref, acc_ref, *, slot0, final):
    eg = pl.program_id(2)
    n_groups = pl.num_programs(2)
    group, width = y_ref.shape[:2]

    @pl.when(eg == 0)
    def _():
        acc_ref[...] = jnp.zeros_like(acc_ref)

    lane_e = lax.broadcasted_iota(I32, aff_ref.shape, 1)
    ids = slot0 + lax.broadcasted_iota(I32, (1, width), 1)
    total = None
    for j in range(group):
        e = eg * group + j
        gate = jnp.sum(jnp.where(lane_e == e, aff_ref[...], 0.0), axis=1, keepdims=True)
        slot = jnp.sum(jnp.where(lane_e == e, slot_ref[...], 0.0), axis=1, keepdims=True).astype(I32)
        onehot = jnp.where(slot == ids, 1.0, 0.0).astype(BF16)
        part = gate * jnp.dot(onehot, y_ref[j], preferred_element_type=F32)
        total = part if total is None else total + part
    acc_ref[...] += total

    @pl.when(eg == n_groups - 1)
    def _():
        out = x_ref[...] + mod_ref[5:6, :] * acc_ref[...]
        o_ref[...] = _rms(out) * fw_ref[...] if final else out


def _combine(slot_tm, aff, y, x, modtab, final_w, *, row0, rows, tm, slot0, width, seg, in_place, final=False):
    b, _, d = x.shape
    ne = aff.shape[2]
    group = COMBINE_EXPERTS
    t0, s0 = row0 // tm, slot0 // width
    out_rows = x.shape[1] if in_place else rows
    return pl.pallas_call(
        functools.partial(_combine_kernel, slot0=slot0, final=final),
        grid=(b, rows // tm, ne // group),
        in_specs=[pl.BlockSpec((None, tm, ne), lambda i, t, e: (i, t + t0, 0)),
                  pl.BlockSpec((None, tm, ne), lambda i, t, e: (i, t + t0, 0)),
                  pl.BlockSpec((group, None, width, d), lambda i, t, e: (e, i, s0, 0)),
                  pl.BlockSpec((None, tm, d), lambda i, t, e: (i, t + t0, 0)),
                  pl.BlockSpec((None, None, 6, d), lambda i, t, e: (i, seg, 0, 0)),
                  pl.BlockSpec((1, d), lambda i, t, e: (0, 0))],
        out_specs=pl.BlockSpec((None, tm, d), lambda i, t, e: (i, t + (t0 if in_place else 0), 0)),
        out_shape=jax.ShapeDtypeStruct((b, out_rows, d), F32),
        scratch_shapes=[pltpu.VMEM((tm, d), F32)],
        input_output_aliases={3: 0} if in_place else {},
        compiler_params=_params("parallel", "parallel", "arbitrary"),
        name="expert_combine",
    )(slot_tm, aff, y, x, modtab, final_w.reshape(1, d))


def _moe(x, modtab, router_w_t, w1, w3, w2, layer, final_w, *, n_lat, with_ctx):
    b, rows, d = x.shape
    ne = N_EXPERTS
    cap_l = CAPACITY_FACTOR * n_lat // ne
    h, aff, aff_t = _router(x, modtab, router_w_t, n_lat=n_lat)
    segments = ((0, n_lat, cap_l, 0),)
    n_slots = cap_l
    if with_ctx:
        n_ctx = rows - n_lat
        cap_c = CAPACITY_FACTOR * n_ctx // ne
        segments += ((n_lat, n_ctx, cap_c, cap_l),)
        n_slots += cap_c
    slot, slot_tm = _select(aff_t, segments)
    xs = _gather(slot, h, segments, n_slots)
    y = _expert_ffn(xs, w1, w3, w2, layer)
    out = _combine(slot_tm, aff, y, x, modtab, final_w, row0=0, rows=n_lat, tm=min(COMBINE_TILE, n_lat),
                   slot0=0, width=cap_l, seg=1, in_place=with_ctx, final=not with_ctx)
    if with_ctx:
        out = _combine(slot_tm, aff, y, out, modtab, final_w, row0=n_lat, rows=n_ctx, tm=n_ctx, slot0=cap_l,
                       width=cap_c, seg=0, in_place=True)
    return out


def _segsum64(x, bd):
    hi = x.astype(BF16)
    lo = (x - hi.astype(F32)).astype(BF16)
    out = []
    for j in range(x.shape[1] // 256):
        sl = slice(j * 256, (j + 1) * 256)
        out.append(jnp.dot(hi[:, sl], bd, preferred_element_type=F32)
                   + jnp.dot(lo[:, sl], bd, preferred_element_type=F32))
    return jnp.concatenate(out, axis=1)


def _block_diag_ones():
    r = lax.broadcasted_iota(I32, (256, 256), 0)
    c = lax.broadcasted_iota(I32, (256, 256), 1)
    return (r // 64 == c // 64).astype(BF16)


def _softplus(x):
    return jnp.maximum(x, 0.0) + jnp.log(1.0 + jnp.exp(-jnp.abs(x)))


def _rwkv_feat_kernel(x_ref, xp_ref, xn_ref, mod_ref, mu_ref, wrkv_ref, w0_ref, w1_ref, w2_ref,
                      a0_ref, a1_ref, a2_ref, g1_ref, g2_ref, kk_ref, ka_ref,
                      r_out, v_out, nkk_out, g_out, w_out, k_out, b_out, *, n_lat_tiles):
    t = pl.program_id(1)
    nt = pl.num_programs(1)
    tm = x_ref.shape[0]
    h = _normmod(x_ref[...], mod_ref, 0, 1)
    hp = _normmod(xp_ref[7:8, :], mod_ref, 0, 1)
    hn = _normmod(xn_ref[0:1, :], mod_ref, 0, 1)
    has_left = jnp.logical_and(t != 0, t != n_lat_tiles)
    has_right = jnp.logical_and(t != n_lat_tiles - 1, t != nt - 1)
    hp = jnp.where(has_left, hp, 0.0)
    hn = jnp.where(has_right, hn, 0.0)
    row = lax.broadcasted_iota(I32, h.shape, 0)
    left = jnp.where(row == 0, hp, pltpu.roll(h, 1, axis=0))
    right = jnp.where(row == tm - 1, hn, pltpu.roll(h, tm - 1, axis=0))
    xx = 0.5 * (left + right) - h
    xr, xw, xk, xv, xa, xg = (h + xx * mu_ref[i:i + 1, :] for i in range(6))
    r = _bdot(xr, wrkv_ref[0])
    k = _bdot(xk, wrkv_ref[1])
    v = _bdot(xv, wrkv_ref[2])
    g = _bdot(_sigmoid(_bdot(xg, g1_ref[...])), g2_ref[...])
    bd = _block_diag_ones()
    kk = k * kk_ref[...]
    kk = kk * lax.rsqrt(jnp.maximum(_segsum64(kk * kk, bd), 1e-24))
    r_out[...] = r
    v_out[...] = v
    nkk_out[...] = -kk
    g_out[...] = g
    for d in range(2):
        w_lora = _bdot(jnp.tanh(_bdot(xw, w1_ref[d])), w2_ref[d])
        log_w = -_softplus(-(w0_ref[d:d + 1, :] + w_lora)) - 0.5
        w_out[d] = jnp.exp(-jnp.exp(log_w))
        a = _sigmoid(a0_ref[d:d + 1, :] + _bdot(_bdot(xa, a1_ref[d]), a2_ref[d]))
        k_out[d] = k * (1.0 + (a - 1.0) * ka_ref[...])
        b_out[d] = kk * a


def _rwkv_features(x, modtab, p, *, n_lat):
    b, t_all, d = x.shape
    tm = ROW_TILE
    nt = t_all // tm
    tb = tm // 8
    full = lambda shape: pl.BlockSpec(shape, lambda i, t: (0,) * len(shape))
    tok = pl.BlockSpec((None, tm, d), lambda i, t: (i, t, 0))
    tok2 = pl.BlockSpec((2, None, tm, d), lambda i, t: (0, i, t, 0))
    one = jax.ShapeDtypeStruct((b, t_all, d), F32)
    two = jax.ShapeDtypeStruct((2, b, t_all, d), F32)
    return pl.pallas_call(
        functools.partial(_rwkv_feat_kernel, n_lat_tiles=n_lat // tm),
        grid=(b, nt),
        in_specs=[tok,
                  pl.BlockSpec((None, 8, d), lambda i, t: (i, jnp.maximum(t * tb - 1, 0), 0)),
                  pl.BlockSpec((None, 8, d), lambda i, t: (i, jnp.minimum((t + 1) * tb, nt * tb - 1), 0)),
                  _mod_spec(n_lat // tm),
                  full((6, d)), full((3, d, d)), full((2, d)), full(p["w1"].shape), full(p["w2"].shape),
                  full((2, d)), full(p["a1"].shape), full(p["a2"].shape), full(p["g1"].shape),
                  full(p["g2"].shape), full((1, d)), full((1, d))],
        out_specs=[tok, tok, tok, tok, tok2, tok2, tok2],
        out_shape=[one, one, one, one, two, two, two],
        compiler_params=_params("parallel", "parallel"),
        name="rwkv_features",
    )(x, x, x, modtab, p["mu"], p["w_rkv"], p["w0"], p["w1"], p["w2"], p["a0"], p["a1"], p["a2"],
      p["g1"], p["g2"], p["k_k"], p["k_a"])


def _scan_kernel(r_ref, w_ref, k_ref, v_ref, a_ref, b_ref, o_ref, s_ref, wr_ref, *, reverse):
    n = s_ref.shape[0]
    steps = r_ref.shape[0]

    @pl.when(pl.program_id(0) == 0)
    def _():
        s_ref[...] = jnp.zeros_like(s_ref)

    def step(i, carry):
        j = steps - 1 - i if reverse else i
        r = r_ref[j]
        wr_ref[...] = w_ref[j] * r
        br = jnp.sum(b_ref[j] * r, axis=0, keepdims=True)
        kr = jnp.sum(k_ref[j] * r, axis=0, keepdims=True)
        acc = [jnp.zeros(s_ref.shape[1:], F32) for _ in range(4)]
        for kk in range(n):
            s = s_ref[kk]
            acc[kk % 2] = acc[kk % 2] + s * a_ref[j, kk:kk + 1, :]
            acc[2 + kk % 2] = acc[2 + kk % 2] + s * wr_ref[kk:kk + 1, :]
        sa = acc[0] + acc[1]
        v = v_ref[j]
        o_ref[j] = acc[2] + acc[3] + sa * br + v * kr
        for kk in range(n):
            s_ref[kk] = (s_ref[kk] * w_ref[j, kk:kk + 1, :] + sa * b_ref[j, kk:kk + 1, :]
                         + v * k_ref[j, kk:kk + 1, :])
        return carry

    lax.fori_loop(0, steps, step, 0)


def _wkv_scan(r, w, k, v, a, b, *, n_lat, reverse):
    t_all, n, chains = r.shape
    tc = SCAN_CHUNK
    nlc, nch = n_lat // tc, t_all // tc
    ncc = nch - nlc
    if reverse:
        idx = lambda c: (jnp.where(c < ncc, nch - 1 - c, nlc - 1 - (c - ncc)), 0, 0)
    else:
        idx = lambda c: (jnp.where(c < ncc, nlc + c, c - ncc), 0, 0)
    spec = pl.BlockSpec((tc, n, chains), idx)
    return pl.pallas_call(
        functools.partial(_scan_kernel, reverse=reverse),
        grid=(nch,),
        in_specs=[spec] * 6,
        out_specs=spec,
        out_shape=jax.ShapeDtypeStruct((t_all, n, chains), F32),
        scratch_shapes=[pltpu.VMEM((n, n, chains), F32), pltpu.VMEM((n, chains), F32)],
        compiler_params=_params("arbitrary"),
        name="wkv_scan_bwd" if reverse else "wkv_scan_fwd",
    )(r, w, k, v, a, b)


def _rwkv_out_kernel(o_ref, r_ref, k_ref, v_ref, g_ref, rk_ref, lnw_ref, lnb_ref, wo_ref, x_ref, mod_ref,
                     out_ref):
    bd = _block_diag_ones()
    inv_n = 1.0 / C_HEAD_DIM
    o = o_ref[...]
    o = o - _segsum64(o, bd) * inv_n
    o = o * lax.rsqrt(_segsum64(o * o, bd) * inv_n + C_GN_EPS)
    o = o * lnw_ref[...] + lnb_ref[...]
    r = r_ref[...]
    bonus = _segsum64(r * k_ref[0] * rk_ref[0:1, :] + r * k_ref[1] * rk_ref[1:2, :], bd) * v_ref[...]
    y = _bdot((o + bonus) * g_ref[...], wo_ref[...])
    out_ref[...] = x_ref[...] + mod_ref[2:3, :] * y


def _rwkv_readout(o, r, k2, v, g, p, x, modtab, *, n_lat, rows):
    b, t_all, d = x.shape
    tm = ROW_TILE
    tok = pl.BlockSpec((None, tm, d), lambda i, t: (i, t, 0))
    full = lambda shape: pl.BlockSpec(shape, lambda i, t: (0,) * len(shape))
    return pl.pallas_call(
        _rwkv_out_kernel,
        grid=(b, rows // tm),
        in_specs=[tok, tok, pl.BlockSpec((2, None, tm, d), lambda i, t: (0, i, t, 0)), tok, tok,
                  full((2, d)), full((1, d)), full((1, d)), full((d, d)), tok, _mod_spec(n_lat // tm)],
        out_specs=tok,
        out_shape=jax.ShapeDtypeStruct((b, rows, d), F32),
        compiler_params=_params("parallel", "parallel"),
        name="rwkv_readout",
    )(o, r, k2, v, g, p["r_k"], p["ln_w"], p["ln_b"], p["w_o"], x, modtab)


def _to_scan_kernel(x_ref, o_ref, y_ref):
    nb, tt, d = x_ref.shape
    n, chains = o_ref.shape[1:]
    for b in range(nb):
        for p in range(d // 128):
            row = (b * (d // 128) + p) * 128
            y_ref[row:row + 128, :] = x_ref[b, :, p * 128:(p + 1) * 128].T
    for k in range(n):
        o_ref[:, k, :] = y_ref[pl.ds(k, chains, stride=n), :].T


def _to_scan_layout(a, d=None):
    b, t, dm = a.shape[-3:]
    tt = 128
    chains = b * dm // C_HEAD_DIM
    if d is None:
        spec = pl.BlockSpec((b, tt, dm), lambda i: (0, i, 0))
    else:
        spec = pl.BlockSpec((None, b, tt, dm), lambda i: (d, 0, i, 0))
    return pl.pallas_call(
        _to_scan_kernel,
        grid=(t // tt,),
        in_specs=[spec],
        out_specs=pl.BlockSpec((tt, C_HEAD_DIM, chains), lambda i: (i, 0, 0)),
        out_shape=jax.ShapeDtypeStruct((t, C_HEAD_DIM, chains), F32),
        scratch_shapes=[pltpu.VMEM((b * dm, tt), F32)],
        compiler_params=_params("parallel"),
        name="to_scan_layout",
    )(a)


def _from_scan_kernel(a_ref, b_ref, o_ref, y_ref):
    nb, tt, d = o_ref.shape
    n, chains = a_ref.shape[1:]
    for k in range(n):
        y_ref[pl.ds(k, chains, stride=n), :] = (a_ref[:, k, :] + b_ref[:, k, :]).T
    for b in range(nb):
        for p in range(d // 128):
            row = (b * (d // 128) + p) * 128
            o_ref[b, :, p * 128:(p + 1) * 128] = y_ref[row:row + 128, :].T


def _from_scan_layout(o_f, o_b, b):
    t, n, chains = o_f.shape
    tt = 128
    dm = n * chains // b
    spec = pl.BlockSpec((tt, n, chains), lambda i: (i, 0, 0))
    return pl.pallas_call(
        _from_scan_kernel,
        grid=(t // tt,),
        in_specs=[spec, spec],
        out_specs=pl.BlockSpec((b, tt, dm), lambda i: (0, i, 0)),
        out_shape=jax.ShapeDtypeStruct((b, t, dm), F32),
        scratch_shapes=[pltpu.VMEM((b * dm, tt), F32)],
        compiler_params=_params("parallel"),
        name="from_scan_layout",
    )(o_f, o_b)


def _rwkv_mixer(x, modtab, p, *, n_lat, rows):
    b = x.shape[0]
    r, v, nkk, g, w2, k2, b2 = _rwkv_features(x, modtab, p, n_lat=n_lat)
    rs, vs, as_ = _to_scan_layout(r), _to_scan_layout(v), _to_scan_layout(nkk)
    o_f, o_b = (_wkv_scan(rs, _to_scan_layout(w2, d), _to_scan_layout(k2, d), vs, as_, _to_scan_layout(b2, d),
                          n_lat=n_lat, reverse=(d == 1)) for d in range(2))
    return _rwkv_readout(_from_scan_layout(o_f, o_b, b), r, k2, v, g, p, x, modtab, n_lat=n_lat, rows=rows)


def _rope_tables(n_lat, n_ctx, head_dim):
    rows = jnp.repeat(jnp.arange(n_lat // GRID_W, dtype=I32), GRID_W).astype(F32)
    cols = jnp.tile(jnp.arange(GRID_W, dtype=I32), n_lat // GRID_W).astype(F32)
    n_freq = head_dim // 4
    inv_freq = ROPE_THETA ** (-jnp.arange(n_freq, dtype=F32) / n_freq)
    ang = jnp.concatenate([rows[:, None] * inv_freq, cols[:, None] * inv_freq], axis=-1)
    cos, sin = jnp.cos(ang), jnp.sin(ang)
    reps = 128 // head_dim
    cos = jnp.tile(jnp.concatenate([cos, cos], axis=-1), (1, reps))
    sin = jnp.tile(jnp.concatenate([-sin, sin], axis=-1), (1, reps))
    cos = jnp.concatenate([cos, jnp.ones((n_ctx, 128), F32)], axis=0)
    sin = jnp.concatenate([sin, jnp.zeros((n_ctx, 128), F32)], axis=0)
    return cos, sin


def kernel(x, c, ctx, c_ctx, mod_w, mod_b, a_w_qkv, a_w_o, a_q_norm, a_k_norm, b_w_qkv, b_w_o, b_sink,
           c_mu, c_w_rkv, c_w_o, c_w0, c_w1, c_w2, c_a0, c_a1, c_a2, c_g1, c_g2, c_k_k, c_k_a, c_r_k,
           c_ln_w, c_ln_b, router_w, ffn_w1, ffn_w3, ffn_w2, final_norm):
    b, n_lat, d = x.shape
    n_ctx = ctx.shape[1]
    t_all = n_lat + n_ctx
    depth = mod_w.shape[0]
    assert d == D_MODEL and n_ctx % ROW_TILE == 0 and n_lat % min(COMBINE_TILE, n_lat) == 0

    cond_rows = -(-(b + 1) // 8) * 8
    cond = jnp.zeros((cond_rows, d), F32).at[:b].set(c).at[b].set(c_ctx)
    mods = _mod_tables(cond, mod_w, mod_b).reshape(depth, cond_rows, 6, d)
    cos_a, sin_a = _rope_tables(n_lat, n_ctx, A_HEAD_DIM)
    cos_b, sin_b = _rope_tables(n_lat, n_ctx, B_HEAD_DIM)
    xs = jnp.concatenate([x, ctx], axis=1)
    for i in range(depth):
        last = i == depth - 1
        rows = n_lat if last else t_all
        kind, j = i % N_MIXERS, i // N_MIXERS
        modtab = jnp.stack([jnp.broadcast_to(mods[i, b], (b, 6, d)), mods[i, :b]], axis=1)
        if kind == 0:
            q, k, v = _qkv_proj(xs, modtab, a_w_qkv[j].astype(BF16), a_q_norm[j].reshape(1, -1),
                                a_k_norm[j].reshape(1, -1), cos_a, sin_a, kind=0, n_lat=n_lat)
            xs = _global_attention(q, k, v, a_w_o[j].astype(BF16), xs, modtab, n_lat=n_lat, rows=rows)
        elif kind == 1:
            ones = jnp.ones((1, 128), F32)
            q, k, v = _qkv_proj(xs, modtab, b_w_qkv[j].astype(BF16), ones, ones, cos_b, sin_b,
                                kind=1, n_lat=n_lat)
            xs = _window_attention(q, k, v, b_sink[j], b_w_o[j].astype(BF16), xs, modtab, n_lat=n_lat, rows=rows)
        else:
            p = dict(mu=c_mu[j], w_rkv=c_w_rkv[j].astype(BF16), w_o=c_w_o[j].astype(BF16), w0=c_w0[j],
                     w1=c_w1[j].astype(BF16), w2=c_w2[j].astype(BF16), a0=c_a0[j],
                     a1=c_a1[j].astype(BF16), a2=c_a2[j].astype(BF16), g1=c_g1[j].astype(BF16),
                     g2=c_g2[j].astype(BF16), k_k=c_k_k[j].reshape(1, d), k_a=c_k_a[j].reshape(1, d),
                     r_k=c_r_k[j].reshape(2, d), ln_w=c_ln_w[j].reshape(1, d), ln_b=c_ln_b[j].reshape(1, d))
            xs = _rwkv_mixer(xs, modtab, p, n_lat=n_lat, rows=rows)
        xs = _moe(xs, modtab, router_w[i].T, ffn_w1, ffn_w3, ffn_w2, i, final_norm, n_lat=n_lat,
                  with_ctx=not last)
    return xs
```

```python
import functools
import math

import jax
import jax.numpy as jnp
from jax import lax
from jax.experimental import pallas as pl
from jax.experimental.pallas import tpu as pltpu

F32 = jnp.float32
BF16 = jnp.bfloat16
I32 = jnp.int32

D_MODEL = 1024
GRID_W = 64
Q_BLOCK = 128
ROPE_THETA = 10000.0
NORM_EPS = 1e-6
NEG_INF = -1e30
A_HEADS, A_KV_HEADS, A_HEAD_DIM = 8, 2, 128
B_HEADS, B_KV_HEADS, B_HEAD_DIM = 16, 4, 64
WINDOW = 128
C_HEAD_DIM = 64
C_HEADS = D_MODEL // C_HEAD_DIM
C_GN_EPS = C_HEAD_DIM * 1e-5
N_EXPERTS = 16
EXPERT_FF = 2 * D_MODEL
CAPACITY_FACTOR = 2
N_MIXERS = 3
LOG2E = math.log2(math.e)

ROW_TILE = 256
COMBINE_TILE = 1024
COMBINE_EXPERTS = 8
SAMPLE_GROUP = 2
SLOT_TILE = 128
SCAN_CHUNK = 32
VMEM_LIMIT = 56 * 1024 * 1024
FFN_VMEM_LIMIT = 60 * 1024 * 1024


def _params(*sem, vmem_limit=VMEM_LIMIT):
    return pltpu.CompilerParams(dimension_semantics=sem, vmem_limit_bytes=vmem_limit)


def _rms(x):
    return x * lax.rsqrt(jnp.mean(x * x, axis=-1, keepdims=True) + NORM_EPS)


def _normmod(x, mod_ref, shift_row, scale_row):
    return _rms(x) * (1.0 + mod_ref[scale_row:scale_row + 1, :]) + mod_ref[shift_row:shift_row + 1, :]


def _bdot(a, b):
    return jnp.dot(a.astype(BF16), b.astype(BF16), preferred_element_type=F32)


def _dot_nt(a, b):
    return lax.dot_general(a.astype(BF16), b.astype(BF16), (((1,), (1,)), ((), ())),
                           preferred_element_type=F32)


def _sigmoid(x):
    return 1.0 / (1.0 + jnp.exp(-x))


def _mod_spec(n_lat_tiles, group=None):
    return pl.BlockSpec((group, None, 6, D_MODEL),
                        lambda b, t: (b, jnp.where(t < n_lat_tiles, 1, 0), 0, 0))


def _sample_group(b):
    return SAMPLE_GROUP if b % SAMPLE_GROUP == 0 else 1


def _mod_kernel(cond_ref, w_ref, b_ref, o_ref):
    c = cond_ref[...]
    a = c * _sigmoid(c)
    o_ref[0] = _bdot(a, w_ref[0]) + b_ref[0]


def _mod_tables(cond, mod_w, mod_b):
    depth, d, n = mod_w.shape
    rows = cond.shape[0]
    return pl.pallas_call(
        _mod_kernel,
        grid=(depth, n // d),
        in_specs=[pl.BlockSpec((rows, d), lambda i, j: (0, 0)),
                  pl.BlockSpec((1, d, d), lambda i, j: (i, 0, j)),
                  pl.BlockSpec((1, 1, d), lambda i, j: (i, 0, j))],
        out_specs=pl.BlockSpec((1, rows, d), lambda i, j: (i, 0, j)),
        out_shape=jax.ShapeDtypeStruct((depth, rows, n), F32),
        compiler_params=_params("arbitrary", "arbitrary"),
        name="mod_tables",
    )(cond, mod_w, mod_b.reshape(depth, 1, n))


def _rope128(x, cos, sin_signed, half):
    if half == 64:
        rot = pltpu.roll(x, 64, axis=1)
    else:
        lane = lax.broadcasted_iota(I32, x.shape, 1)
        rot = jnp.where((lane % 64) < 32, pltpu.roll(x, 96, axis=1), pltpu.roll(x, 32, axis=1))
    return x * cos + rot * sin_signed


def _qkv_kernel(x_ref, mod_ref, *refs, kind):
    shared, outs = refs[:5], refs[5:]
    for s in range(x_ref.shape[0]):
        _qkv_tile(x_ref.at[s], mod_ref.at[s], *shared, *(o.at[s] for o in outs), kind=kind)


def _qkv_tile(x_ref, mod_ref, w_ref, gq_ref, gk_ref, cos_ref, sin_ref, q_ref, k_ref, v_ref, *, kind):
    h = _normmod(x_ref[...], mod_ref, 0, 1)
    y = _bdot(h, w_ref[...])
    cos, sin = cos_ref[...], sin_ref[...]
    nq = D_MODEL
    nkv = (y.shape[1] - nq) // 2
    half = 64 if kind == 0 else 32
    q_scale = (2 * half) ** -0.5 * LOG2E
    for j in range((nq + nkv) // 128):
        s = y[:, j * 128:(j + 1) * 128]
        if kind == 0:
            gain = gq_ref[...] if j < nq // 128 else gk_ref[...]
            s = _rms(s) * gain
        s = _rope128(s, cos, sin, half)
        if j < nq // 128:
            q_ref[:, j * 128:(j + 1) * 128] = (s * q_scale).astype(BF16)
        elif kind == 0:
            k_ref[:, (j - nq // 128) * 128:(j - nq // 128 + 1) * 128] = s.astype(BF16)
        else:
            jj = (j - nq // 128) * 2
            k_ref[jj] = s[:, :64].astype(BF16)
            k_ref[jj + 1] = s[:, 64:].astype(BF16)
    v = y[:, nq + nkv:].astype(BF16)
    if kind == 0:
        v_ref[...] = v
    else:
        for jj in range(nkv // 64):
            v_ref[jj] = v[:, jj * 64:(jj + 1) * 64]


def _qkv_proj(x, modtab, w, gq, gk, cos, sin, *, kind, n_lat):
    b, t_all, d = x.shape
    tm = ROW_TILE
    nt = t_all // tm
    n = w.shape[1]
    nkv = (n - d) // 2
    grp = _sample_group(b)
    if kind == 0:
        kv_shape = jax.ShapeDtypeStruct((b, t_all, nkv), BF16)
        kv_spec = pl.BlockSpec((grp, tm, nkv), lambda i, t: (i, t, 0))
    else:
        kv_shape = jax.ShapeDtypeStruct((b, nkv // 64, t_all, 64), BF16)
        kv_spec = pl.BlockSpec((grp, nkv // 64, tm, 64), lambda i, t: (i, 0, t, 0))
    return pl.pallas_call(
        functools.partial(_qkv_kernel, kind=kind),
        grid=(b // grp, nt),
        in_specs=[pl.BlockSpec((grp, tm, d), lambda i, t: (i, t, 0)),
                  _mod_spec(n_lat // tm, grp),
                  pl.BlockSpec((d, n), lambda i, t: (0, 0)),
                  pl.BlockSpec((1, 128), lambda i, t: (0, 0)),
                  pl.BlockSpec((1, 128), lambda i, t: (0, 0)),
                  pl.BlockSpec((tm, 128), lambda i, t: (t, 0)),
                  pl.BlockSpec((tm, 128), lambda i, t: (t, 0))],
        out_specs=[pl.BlockSpec((grp, tm, d), lambda i, t: (i, t, 0)), kv_spec, kv_spec],
        out_shape=[jax.ShapeDtypeStruct((b, t_all, d), BF16), kv_shape, kv_shape],
        compiler_params=_params("parallel", "parallel"),
        name=f"qkv_proj_{kind}",
    )(x, modtab, w, gq, gk, cos, sin)


def _flash_chunk(carry, q, kc, vc):
    m, l, acc = carry
    s = _dot_nt(q, kc)
    m_new = jnp.maximum(m, jnp.max(s, axis=-1, keepdims=True))
    alpha = jnp.exp2(m - m_new)
    p = jnp.exp2(s - m_new)
    l = alpha * l + jnp.sum(p, axis=-1, keepdims=True)
    acc = alpha * acc + jnp.dot(p.astype(BF16), vc, preferred_element_type=F32)
    return m_new, l, acc


def _gattn_kernel(q_ref, k_ref, v_ref, wo_ref, x_ref, mod_ref, o_ref, att_ref, *, n_lat, kchunk):
    hd = A_HEAD_DIM
    g = A_HEADS // A_KV_HEADS
    tq = q_ref.shape[0]
    t_all = k_ref.shape[0]
    groups = range(A_KV_HEADS)
    qs = [jnp.concatenate([q_ref[:, (j * g + i) * hd:(j * g + i + 1) * hd] for i in range(g)], axis=0)
          for j in groups]
    init = (jnp.full((g * tq, 1), NEG_INF, F32), jnp.zeros((g * tq, 1), F32), jnp.zeros((g * tq, hd), F32))

    def chunk(j, carry, rows):
        return _flash_chunk(carry, qs[j], k_ref[rows, j * hd:(j + 1) * hd], v_ref[rows, j * hd:(j + 1) * hd])

    def finish(carries):
        for j, (_, l, acc) in enumerate(carries):
            o = (acc / l).astype(BF16)
            for i in range(g):
                att_ref[:, (j * g + i) * hd:(j * g + i + 1) * hd] = o[i * tq:(i + 1) * tq]

    ctx_carries = tuple(chunk(j, init, slice(n_lat, t_all)) for j in groups)
    is_ctx = pl.program_id(1) >= n_lat // tq

    @pl.when(is_ctx)
    def _():
        finish(ctx_carries)

    @pl.when(jnp.logical_not(is_ctx))
    def _():
        def body(c, carries):
            rows = pl.ds(pl.multiple_of(c * kchunk, kchunk), kchunk)
            return tuple(chunk(j, carries[j], rows) for j in groups)
        finish(lax.fori_loop(0, n_lat // kchunk, body, ctx_carries))

    _project_residual(att_ref, wo_ref, x_ref, mod_ref, o_ref)


def _project_residual(att_ref, wo_ref, x_ref, mod_ref, o_ref):
    o_ref[...] = x_ref[...] + mod_ref[2:3, :] * jnp.dot(att_ref[...], wo_ref[...], preferred_element_type=F32)


def _global_attention(q, k, v, w_o, x, modtab, *, n_lat, rows):
    b, t_all, d = q.shape
    nkv = k.shape[2]
    tq = ROW_TILE
    return pl.pallas_call(
        functools.partial(_gattn_kernel, n_lat=n_lat, kchunk=min(2048, n_lat)),
        grid=(b, rows // tq),
        in_specs=[pl.BlockSpec((None, tq, d), lambda i, t: (i, t, 0)),
                  pl.BlockSpec((None, t_all, nkv), lambda i, t: (i, 0, 0)),
                  pl.BlockSpec((None, t_all, nkv), lambda i, t: (i, 0, 0)),
                  pl.BlockSpec((d, d), lambda i, t: (0, 0)),
                  pl.BlockSpec((None, tq, d), lambda i, t: (i, t, 0)),
                  _mod_spec(n_lat // tq)],
        out_specs=pl.BlockSpec((None, tq, d), lambda i, t: (i, t, 0)),
        out_shape=jax.ShapeDtypeStruct((b, rows, d), F32),
        scratch_shapes=[pltpu.VMEM((tq, d), BF16)],
        compiler_params=_params("parallel", "arbitrary"),
        name="global_attention",
    )(q, k, v, w_o, x, modtab)


def _wattn_kernel(q_ref, k_ref, v_ref, sink_ref, bias_ref, wo_ref, x_ref, mod_ref, o_ref, att_ref, *, n_lat):
    hd = B_HEAD_DIM
    g = B_HEADS // B_KV_HEADS
    tq = q_ref.shape[0]
    t_all = k_ref.shape[1]
    t = pl.program_id(1)
    n_tiles = n_lat // tq
    is_ctx = t >= n_tiles

    def group(j):
        q = jnp.concatenate([q_ref[:, (j * g + i) * hd:(j * g + i + 1) * hd] for i in range(g)], axis=0)
        sink = jnp.concatenate([jnp.broadcast_to(sink_ref[j, :, i:i + 1], (tq, 1)) for i in range(g)],
                               axis=0) * LOG2E
        kc, vc = k_ref[j, n_lat:t_all, :], v_ref[j, n_lat:t_all, :]
        return q, sink, _dot_nt(q, kc), vc

    def finish(j, o):
        o = o.astype(BF16)
        att_ref[:, j * g * hd:(j + 1) * g * hd] = jnp.concatenate([o[i * tq:(i + 1) * tq] for i in range(g)], axis=1)

    @pl.when(is_ctx)
    def _():
        for j in range(B_KV_HEADS):
            q, sink, s_ctx, vc = group(j)
            m = jnp.maximum(jnp.max(s_ctx, axis=-1, keepdims=True), sink)
            p = jnp.exp2(s_ctx - m)
            l = jnp.sum(p, axis=-1, keepdims=True) + jnp.exp2(sink - m)
            finish(j, jnp.dot(p.astype(BF16), vc, preferred_element_type=F32) / l)

    @pl.when(jnp.logical_not(is_ctx))
    def _():
        wlen = tq + 2 * WINDOW
        ws = jnp.clip(t * tq - WINDOW, 0, n_lat - wlen)
        start = pl.multiple_of(ws, 128)
        bias = bias_ref[jnp.where(t == 0, 0, jnp.where(t == n_tiles - 1, 2, 1))]
        bias = jnp.concatenate([bias] * g, axis=0)
        for j in range(B_KV_HEADS):
            q, sink, _, _ = group(j)
            keys = jnp.concatenate([k_ref[j, pl.ds(start, wlen), :], k_ref[j, n_lat:t_all, :]], axis=0)
            vals = jnp.concatenate([v_ref[j, pl.ds(start, wlen), :], v_ref[j, n_lat:t_all, :]], axis=0)
            s = _dot_nt(q, keys) + bias
            m = jnp.maximum(jnp.max(s, axis=-1, keepdims=True), sink)
            p = jnp.exp2(s - m)
            l = jnp.sum(p, axis=-1, keepdims=True) + jnp.exp2(sink - m)
            finish(j, jnp.dot(p.astype(BF16), vals, preferred_element_type=F32) / l)

    _project_residual(att_ref, wo_ref, x_ref, mod_ref, o_ref)


def _band_bias(tq, n_ctx):
    row = jnp.arange(tq, dtype=I32)[:, None]
    col = jnp.arange(tq + 2 * WINDOW, dtype=I32)[None, :]
    band = jnp.stack([jnp.where(jnp.abs(col - off - row) <= WINDOW, 0.0, NEG_INF).astype(F32)
                      for off in (0, WINDOW, 2 * WINDOW)])
    return jnp.concatenate([band, jnp.zeros((3, tq, n_ctx), F32)], axis=2)


def _window_attention(q, k, v, sink, w_o, x, modtab, *, n_lat, rows):
    b, t_all, d = q.shape
    hd, g = B_HEAD_DIM, B_HEADS // B_KV_HEADS
    tq = ROW_TILE
    wlen = tq + 2 * WINDOW
    return pl.pallas_call(
        functools.partial(_wattn_kernel, n_lat=n_lat),
        grid=(b, rows // tq),
        in_specs=[pl.BlockSpec((None, tq, d), lambda i, t: (i, t, 0)),
                  pl.BlockSpec((None, B_KV_HEADS, t_all, hd), lambda i, t: (i, 0, 0, 0)),
                  pl.BlockSpec((None, B_KV_HEADS, t_all, hd), lambda i, t: (i, 0, 0, 0)),
                  pl.BlockSpec((B_KV_HEADS, 1, g), lambda i, t: (0, 0, 0)),
                  pl.BlockSpec((3, tq, wlen + t_all - n_lat), lambda i, t: (0, 0, 0)),
                  pl.BlockSpec((d, d), lambda i, t: (0, 0)),
                  pl.BlockSpec((None, tq, d), lambda i, t: (i, t, 0)),
                  _mod_spec(n_lat // tq)],
        out_specs=pl.BlockSpec((None, tq, d), lambda i, t: (i, t, 0)),
        out_shape=jax.ShapeDtypeStruct((b, rows, d), F32),
        scratch_shapes=[pltpu.VMEM((tq, d), BF16)],
        compiler_params=_params("parallel", "arbitrary"),
        name="window_attention",
    )(q, k, v, sink.reshape(B_KV_HEADS, 1, g), _band_bias(tq, t_all - n_lat), w_o, x, modtab)


def _router_kernel(x_ref, mod_ref, rwt_ref, h_ref, aff_ref, afft_ref):
    for s in range(x_ref.shape[0]):
        h = _normmod(x_ref[s], mod_ref.at[s], 3, 4)
        h_ref[s] = h.astype(BF16)
        logits_t = lax.dot_general(rwt_ref[...], h, (((1,), (1,)), ((), ())),
                                   precision=lax.Precision.HIGHEST, preferred_element_type=F32)
        e = jnp.exp(logits_t - jnp.max(logits_t, axis=0, keepdims=True))
        aff_t = e / jnp.sum(e, axis=0, keepdims=True)
        afft_ref[s] = aff_t
        aff_ref[s] = aff_t.T


def _router(x, modtab, router_w_t, *, n_lat):
    b, rows, d = x.shape
    tm = ROW_TILE
    ne = router_w_t.shape[0]
    grp = _sample_group(b)
    return pl.pallas_call(
        _router_kernel,
        grid=(b // grp, rows // tm),
        in_specs=[pl.BlockSpec((grp, tm, d), lambda i, t: (i, t, 0)),
                  _mod_spec(n_lat // tm, grp),
                  pl.BlockSpec((ne, d), lambda i, t: (0, 0))],
        out_specs=[pl.BlockSpec((grp, tm, d), lambda i, t: (i, t, 0)),
                   pl.BlockSpec((grp, tm, ne), lambda i, t: (i, t, 0)),
                   pl.BlockSpec((grp, ne, tm), lambda i, t: (i, 0, t))],
        out_shape=[jax.ShapeDtypeStruct((b, rows, d), BF16),
                   jax.ShapeDtypeStruct((b, rows, ne), F32),
                   jax.ShapeDtypeStruct((b, ne, rows), F32)],
        compiler_params=_params("parallel", "parallel"),
        name="router",
    )(x, modtab, router_w_t)


def _lane_prefix(mask_f, tri):
    n = mask_f.shape[1]
    run = jnp.zeros((mask_f.shape[0], 1), F32)
    out = []
    for j in range(n // 128):
        blk = mask_f[:, j * 128:(j + 1) * 128]
        incl = jnp.dot(blk.astype(BF16), tri, preferred_element_type=F32)
        out.append(incl - blk + run)
        run = run + incl[:, 127:128]
    return jnp.concatenate(out, axis=1)


def _select_kernel(afft_ref, slot_ref, slot_tm_ref, *, segments):
    r = lax.broadcasted_iota(I32, (128, 128), 0)
    c = lax.broadcasted_iota(I32, (128, 128), 1)
    tri = (r <= c).astype(BF16)
    for off, n, cap, base in segments:
        bits = lax.bitcast_convert_type(afft_ref[:, off:off + n], I32)
        cap_f = jnp.float32(cap)

        def body(i, thr):
            cand = thr | jnp.left_shift(jnp.int32(1), 30 - i)
            cnt = jnp.sum((bits >= cand).astype(F32), axis=1, keepdims=True)
            return jnp.where(cnt >= cap_f, cand, thr)

        thr = lax.fori_loop(0, 31, body, jnp.zeros((bits.shape[0], 1), I32))
        gt = (bits > thr).astype(F32)
        eq = (bits == thr).astype(F32)
        need = cap_f - jnp.sum(gt, axis=1, keepdims=True)
        sel = gt + eq * (_lane_prefix(eq, tri) < need).astype(F32)
        slot = _lane_prefix(sel, tri).astype(I32) + base
        slot = jnp.where(sel > 0.5, slot, -1)
        slot_ref[:, off:off + n] = slot
        slot_tm_ref[off:off + n, :] = slot.astype(F32).T


def _select(aff_t, segments):
    b, ne, rows = aff_t.shape
    return pl.pallas_call(
        functools.partial(_select_kernel, segments=segments),
        grid=(b,),
        in_specs=[pl.BlockSpec((None, ne, rows), lambda i: (i, 0, 0))],
        out_specs=[pl.BlockSpec((None, ne, rows), lambda i: (i, 0, 0)),
                   pl.BlockSpec((None, rows, ne), lambda i: (i, 0, 0))],
        out_shape=[jax.ShapeDtypeStruct((b, ne, rows), I32), jax.ShapeDtypeStruct((b, rows, ne), F32)],
        compiler_params=_params("parallel"),
        name="expert_select",
    )(aff_t)


def _gather_kernel(slot_ref, h_ref, xs_ref, *, segments):
    for off, n, cap, base in segments:
        slot = slot_ref[:, off:off + n]
        hseg = h_ref[off:off + n, :]
        st = min(SLOT_TILE, cap)
        for j in range(cap // st):
            ids = base + j * st + lax.broadcasted_iota(I32, (st, 1), 0)
            onehot = jnp.where(ids == slot, 1.0, 0.0).astype(BF16)
            rows = jnp.dot(onehot, hseg, preferred_element_type=F32)
            xs_ref[base + j * st:base + (j + 1) * st, :] = rows.astype(BF16)


def _gather(slot, h, segments, n_slots):
    b, ne, rows = slot.shape
    d = h.shape[2]
    return pl.pallas_call(
        functools.partial(_gather_kernel, segments=segments),
        grid=(b, ne),
        in_specs=[pl.BlockSpec((None, None, 1, rows), lambda i, e: (i, e, 0, 0)),
                  pl.BlockSpec((None, rows, d), lambda i, e: (i, 0, 0))],
        out_specs=pl.BlockSpec((None, None, n_slots, d), lambda i, e: (e, i, 0, 0)),
        out_shape=jax.ShapeDtypeStruct((ne, b, n_slots, d), BF16),
        compiler_params=_params("parallel", "arbitrary"),
        name="expert_gather",
    )(slot.reshape(b, ne, 1, rows), h)


def _ffn_kernel(xs_ref, w1_ref, w3_ref, w2_ref, y_ref):
    xs = xs_ref[...]
    a = _bdot(xs, w1_ref[...])
    g = _bdot(xs, w3_ref[...])
    hid = (a * _sigmoid(a)) * g
    y_ref[...] = _bdot(hid, w2_ref[...]).astype(BF16)


def _expert_ffn(xs, w1, w3, w2, layer):
    ne, b, n_slots, d = xs.shape
    ff = w1.shape[3]
    return pl.pallas_call(
        _ffn_kernel,
        grid=(ne, b),
        in_specs=[pl.BlockSpec((None, None, n_slots, d), lambda e, i: (e, i, 0, 0)),
                  pl.BlockSpec((None, None, d, ff), lambda e, i: (layer, e, 0, 0)),
                  pl.BlockSpec((None, None, d, ff), lambda e, i: (layer, e, 0, 0)),
                  pl.BlockSpec((None, None, ff, d), lambda e, i: (layer, e, 0, 0))],
        out_specs=pl.BlockSpec((None, None, n_slots, d), lambda e, i: (e, i, 0, 0)),
        out_shape=jax.ShapeDtypeStruct((ne, b, n_slots, d), BF16),
        compiler_params=_params("parallel", "arbitrary", vmem_limit=FFN_VMEM_LIMIT),
        name="expert_ffn",
    )(xs, w1, w3, w2)


def _combine_kernel(slot_ref, aff_ref, y_ref, x_ref, mod_ref, fw_ref, o_ref, acc_ref, *, slot0, final):
    eg = pl.program_id(2)
    n_groups = pl.num_programs(2)
    group, width = y_ref.shape[:2]

    @pl.when(eg == 0)
    def _():
        acc_ref[...] = jnp.zeros_like(acc_ref)

    lane_e = lax.broadcasted_iota(I32, aff_ref.shape, 1)
    ids = slot0 + lax.broadcasted_iota(I32, (1, width), 1)
    total = None
    for j in range(group):
        e = eg * group + j
        gate = jnp.sum(jnp.where(lane_e == e, aff_ref[...], 0.0), axis=1, keepdims=True)
        slot = jnp.sum(jnp.where(lane_e == e, slot_ref[...], 0.0), axis=1, keepdims=True).astype(I32)
        onehot = jnp.where(slot == ids, 1.0, 0.0).astype(BF16)
        part = gate * jnp.dot(onehot, y_ref[j], preferred_element_type=F32)
        total = part if total is None else total + part
    acc_ref[...] += total

    @pl.when(eg == n_groups - 1)
    def _():
        out = x_ref[...] + mod_ref[5:6, :] * acc_ref[...]
        o_ref[...] = _rms(out) * fw_ref[...] if final else out


def _combine(slot_tm, aff, y, x, modtab, final_w, *, row0, rows, tm, slot0, width, seg, in_place, final=False):
    b, _, d = x.shape
    ne = aff.shape[2]
    group = COMBINE_EXPERTS
    t0, s0 = row0 // tm, slot0 // width
    out_rows = x.shape[1] if in_place else rows
    return pl.pallas_call(
        functools.partial(_combine_kernel, slot0=slot0, final=final),
        grid=(b, rows // tm, ne // group),
        in_specs=[pl.BlockSpec((None, tm, ne), lambda i, t, e: (i, t + t0, 0)),
                  pl.BlockSpec((None, tm, ne), lambda i, t, e: (i, t + t0, 0)),
                  pl.BlockSpec((group, None, width, d), lambda i, t, e: (e, i, s0, 0)),
                  pl.BlockSpec((None, tm, d), lambda i, t, e: (i, t + t0, 0)),
                  pl.BlockSpec((None, None, 6, d), lambda i, t, e: (i, seg, 0, 0)),
                  pl.BlockSpec((1, d), lambda i, t, e: (0, 0))],
        out_specs=pl.BlockSpec((None, tm, d), lambda i, t, e: (i, t + (t0 if in_place else 0), 0)),
        out_shape=jax.ShapeDtypeStruct((b, out_rows, d), F32),
        scratch_shapes=[pltpu.VMEM((tm, d), F32)],
        input_output_aliases={3: 0} if in_place else {},
        compiler_params=_params("parallel", "parallel", "arbitrary"),
        name="expert_combine",
    )(slot_tm, aff, y, x, modtab, final_w.reshape(1, d))


def _moe(x, modtab, router_w_t, w1, w3, w2, layer, final_w, *, n_lat, with_ctx):
    b, rows, d = x.shape
    ne = N_EXPERTS
    cap_l = CAPACITY_FACTOR * n_lat // ne
    h, aff, aff_t = _router(x, modtab, router_w_t, n_lat=n_lat)
    segments = ((0, n_lat, cap_l, 0),)
    n_slots = cap_l
    if with_ctx:
        n_ctx = rows - n_lat
        cap_c = CAPACITY_FACTOR * n_ctx // ne
        segments += ((n_lat, n_ctx, cap_c, cap_l),)
        n_slots += cap_c
    slot, slot_tm = _select(aff_t, segments)
    xs = _gather(slot, h, segments, n_slots)
    y = _expert_ffn(xs, w1, w3, w2, layer)
    out = _combine(slot_tm, aff, y, x, modtab, final_w, row0=0, rows=n_lat, tm=min(COMBINE_TILE, n_lat),
                   slot0=0, width=cap_l, seg=1, in_place=with_ctx, final=not with_ctx)
    if with_ctx:
        out = _combine(slot_tm, aff, y, out, modtab, final_w, row0=n_lat, rows=n_ctx, tm=n_ctx, slot0=cap_l,
                       width=cap_c, seg=0, in_place=True)
    return out


def _segsum64(x, bd):
    hi = x.astype(BF16)
    lo = (x - hi.astype(F32)).astype(BF16)
    out = []
    for j in range(x.shape[1] // 256):
        sl = slice(j * 256, (j + 1) * 256)
        out.append(jnp.dot(hi[:, sl], bd, preferred_element_type=F32)
                   + jnp.dot(lo[:, sl], bd, preferred_element_type=F32))
    return jnp.concatenate(out, axis=1)


def _block_diag_ones():
    r = lax.broadcasted_iota(I32, (256, 256), 0)
    c = lax.broadcasted_iota(I32, (256, 256), 1)
    return (r // 64 == c // 64).astype(BF16)


def _rwkv_feat_kernel(x_ref, xp_ref, xn_ref, mod_ref, mu_ref, wrkv_ref, w0_ref, w1_ref, w2_ref,
                      a0_ref, a1_ref, a2_ref, g1_ref, g2_ref, kk_ref, ka_ref,
                      r_out, v_out, nkk_out, g_out, w_out, k_out, b_out, *, n_lat_tiles):
    t = pl.program_id(1)
    nt = pl.num_programs(1)
    tm = x_ref.shape[0]
    h = _normmod(x_ref[...], mod_ref, 0, 1)
    hp = _normmod(xp_ref[7:8, :], mod_ref, 0, 1)
    hn = _normmod(xn_ref[0:1, :], mod_ref, 0, 1)
    has_left = jnp.logical_and(t != 0, t != n_lat_tiles)
    has_right = jnp.logical_and(t != n_lat_tiles - 1, t != nt - 1)
    hp = jnp.where(has_left, hp, 0.0)
    hn = jnp.where(has_right, hn, 0.0)
    row = lax.broadcasted_iota(I32, h.shape, 0)
    left = jnp.where(row == 0, hp, pltpu.roll(h, 1, axis=0))
    right = jnp.where(row == tm - 1, hn, pltpu.roll(h, tm - 1, axis=0))
    xx = 0.5 * (left + right) - h
    xr, xw, xk, xv, xa, xg = (h + xx * mu_ref[i:i + 1, :] for i in range(6))
    r = _bdot(xr, wrkv_ref[0])
    k = _bdot(xk, wrkv_ref[1])
    v = _bdot(xv, wrkv_ref[2])
    g = _bdot(_sigmoid(_bdot(xg, g1_ref[...])), g2_ref[...])
    bd = _block_diag_ones()
    kk = k * kk_ref[...]
    kk = kk * lax.rsqrt(jnp.maximum(_segsum64(kk * kk, bd), 1e-24))
    r_out[...] = r
    v_out[...] = v
    nkk_out[...] = -kk
    g_out[...] = g
    for d in range(2):
        w_lora = _bdot(jnp.tanh(_bdot(xw, w1_ref[d])), w2_ref[d])
        w_out[d] = jnp.exp(-math.exp(-0.5) * _sigmoid(w0_ref[d:d + 1, :] + w_lora))
        a = _sigmoid(a0_ref[d:d + 1, :] + _bdot(_bdot(xa, a1_ref[d]), a2_ref[d]))
        k_out[d] = k * (1.0 + (a - 1.0) * ka_ref[...])
        b_out[d] = kk * a


def _rwkv_features(x, modtab, p, *, n_lat):
    b, t_all, d = x.shape
    tm = ROW_TILE
    nt = t_all // tm
    tb = tm // 8
    full = lambda shape: pl.BlockSpec(shape, lambda i, t: (0,) * len(shape))
    tok = pl.BlockSpec((None, tm, d), lambda i, t: (i, t, 0))
    tok2 = pl.BlockSpec((2, None, tm, d), lambda i, t: (0, i, t, 0))
    one = jax.ShapeDtypeStruct((b, t_all, d), F32)
    two = jax.ShapeDtypeStruct((2, b, t_all, d), F32)
    return pl.pallas_call(
        functools.partial(_rwkv_feat_kernel, n_lat_tiles=n_lat // tm),
        grid=(b, nt),
        in_specs=[tok,
                  pl.BlockSpec((None, 8, d), lambda i, t: (i, jnp.maximum(t * tb - 1, 0), 0)),
                  pl.BlockSpec((None, 8, d), lambda i, t: (i, jnp.minimum((t + 1) * tb, nt * tb - 1), 0)),
                  _mod_spec(n_lat // tm),
                  full((6, d)), full((3, d, d)), full((2, d)), full(p["w1"].shape), full(p["w2"].shape),
                  full((2, d)), full(p["a1"].shape), full(p["a2"].shape), full(p["g1"].shape),
                  full(p["g2"].shape), full((1, d)), full((1, d))],
        out_specs=[tok, tok, tok, tok, tok2, tok2, tok2],
        out_shape=[one, one, one, one, two, two, two],
        compiler_params=_params("parallel", "parallel"),
        name="rwkv_features",
    )(x, x, x, modtab, p["mu"], p["w_rkv"], p["w0"], p["w1"], p["w2"], p["a0"], p["a1"], p["a2"],
      p["g1"], p["g2"], p["k_k"], p["k_a"])


def _scan_kernel(r_ref, w_ref, k_ref, v_ref, a_ref, b_ref, o_ref, s_ref, wr_ref, *, reverse):
    n = s_ref.shape[0]
    steps = r_ref.shape[0]

    @pl.when(pl.program_id(0) == 0)
    def _():
        s_ref[...] = jnp.zeros_like(s_ref)

    def step(i, carry):
        j = steps - 1 - i if reverse else i
        r = r_ref[j]
        wr_ref[...] = w_ref[j] * r
        br = jnp.sum(b_ref[j] * r, axis=0, keepdims=True)
        kr = jnp.sum(k_ref[j] * r, axis=0, keepdims=True)
        acc = [jnp.zeros(s_ref.shape[1:], F32) for _ in range(4)]
        for kk in range(n):
            s = s_ref[kk]
            acc[kk % 2] = acc[kk % 2] + s * a_ref[j, kk:kk + 1, :]
            acc[2 + kk % 2] = acc[2 + kk % 2] + s * wr_ref[kk:kk + 1, :]
        sa = acc[0] + acc[1]
        v = v_ref[j]
        o_ref[j] = acc[2] + acc[3] + sa * br + v * kr
        for kk in range(n):
            s_ref[kk] = (s_ref[kk] * w_ref[j, kk:kk + 1, :] + sa * b_ref[j, kk:kk + 1, :]
                         + v * k_ref[j, kk:kk + 1, :])
        return carry

    lax.fori_loop(0, steps, step, 0)


def _wkv_scan(r, w, k, v, a, b, *, n_lat, reverse):
    t_all, n, chains = r.shape
    tc = SCAN_CHUNK
    nlc, nch = n_lat // tc, t_all // tc
    ncc = nch - nlc
    if reverse:
        idx = lambda c: (jnp.where(c < ncc, nch - 1 - c, nlc - 1 - (c - ncc)), 0, 0)
    else:
        idx = lambda c: (jnp.where(c < ncc, nlc + c, c - ncc), 0, 0)
    spec = pl.BlockSpec((tc, n, chains), idx)
    return pl.pallas_call(
        functools.partial(_scan_kernel, reverse=reverse),
        grid=(nch,),
        in_specs=[spec] * 6,
        out_specs=spec,
        out_shape=jax.ShapeDtypeStruct((t_all, n, chains), F32),
        scratch_shapes=[pltpu.VMEM((n, n, chains), F32), pltpu.VMEM((n, chains), F32)],
        compiler_params=_params("arbitrary"),
        name="wkv_scan_bwd" if reverse else "wkv_scan_fwd",
    )(r, w, k, v, a, b)


def _rwkv_out_kernel(o_ref, r_ref, k_ref, v_ref, g_ref, rk_ref, lnw_ref, lnb_ref, wo_ref, x_ref, mod_ref,
                     out_ref):
    bd = _block_diag_ones()
    inv_n = 1.0 / C_HEAD_DIM
    o = o_ref[...]
    o = o - _segsum64(o, bd) * inv_n
    o = o * lax.rsqrt(_segsum64(o * o, bd) * inv_n + C_GN_EPS)
    o = o * lnw_ref[...] + lnb_ref[...]
    r = r_ref[...]
    bonus = _segsum64(r * k_ref[0] * rk_ref[0:1, :] + r * k_ref[1] * rk_ref[1:2, :], bd) * v_ref[...]
    y = _bdot((o + bonus) * g_ref[...], wo_ref[...])
    out_ref[...] = x_ref[...] + mod_ref[2:3, :] * y


def _rwkv_readout(o, r, k2, v, g, p, x, modtab, *, n_lat, rows):
    b, t_all, d = x.shape
    tm = ROW_TILE
    tok = pl.BlockSpec((None, tm, d), lambda i, t: (i, t, 0))
    full = lambda shape: pl.BlockSpec(shape, lambda i, t: (0,) * len(shape))
    return pl.pallas_call(
        _rwkv_out_kernel,
        grid=(b, rows // tm),
        in_specs=[tok, tok, pl.BlockSpec((2, None, tm, d), lambda i, t: (0, i, t, 0)), tok, tok,
                  full((2, d)), full((1, d)), full((1, d)), full((d, d)), tok, _mod_spec(n_lat // tm)],
        out_specs=tok,
        out_shape=jax.ShapeDtypeStruct((b, rows, d), F32),
        compiler_params=_params("parallel", "parallel"),
        name="rwkv_readout",
    )(o, r, k2, v, g, p["r_k"], p["ln_w"], p["ln_b"], p["w_o"], x, modtab)


def _to_scan_kernel(x_ref, o_ref, y_ref):
    nb, tt, d = x_ref.shape
    n, chains = o_ref.shape[1:]
    for b in range(nb):
        for p in range(d // 128):
            row = (b * (d // 128) + p) * 128
            y_ref[row:row + 128, :] = x_ref[b, :, p * 128:(p + 1) * 128].T
    for k in range(n):
        o_ref[:, k, :] = y_ref[pl.ds(k, chains, stride=n), :].T


def _to_scan_layout(a, d=None):
    b, t, dm = a.shape[-3:]
    tt = 128
    chains = b * dm // C_HEAD_DIM
    if d is None:
        spec = pl.BlockSpec((b, tt, dm), lambda i: (0, i, 0))
    else:
        spec = pl.BlockSpec((None, b, tt, dm), lambda i: (d, 0, i, 0))
    return pl.pallas_call(
        _to_scan_kernel,
        grid=(t // tt,),
        in_specs=[spec],
        out_specs=pl.BlockSpec((tt, C_HEAD_DIM, chains), lambda i: (i, 0, 0)),
        out_shape=jax.ShapeDtypeStruct((t, C_HEAD_DIM, chains), F32),
        scratch_shapes=[pltpu.VMEM((b * dm, tt), F32)],
        compiler_params=_params("parallel"),
        name="to_scan_layout",
    )(a)


def _from_scan_kernel(a_ref, b_ref, o_ref, y_ref):
    nb, tt, d = o_ref.shape
    n, chains = a_ref.shape[1:]
    for k in range(n):
        y_ref[pl.ds(k, chains, stride=n), :] = (a_ref[:, k, :] + b_ref[:, k, :]).T
    for b in range(nb):
        for p in range(d // 128):
            row = (b * (d // 128) + p) * 128
            o_ref[b, :, p * 128:(p + 1) * 128] = y_ref[row:row + 128, :].T


def _from_scan_layout(o_f, o_b, b):
    t, n, chains = o_f.shape
    tt = 128
    dm = n * chains // b
    spec = pl.BlockSpec((tt, n, chains), lambda i: (i, 0, 0))
    return pl.pallas_call(
        _from_scan_kernel,
        grid=(t // tt,),
        in_specs=[spec, spec],
        out_specs=pl.BlockSpec((b, tt, dm), lambda i: (0, i, 0)),
        out_shape=jax.ShapeDtypeStruct((b, t, dm), F32),
        scratch_shapes=[pltpu.VMEM((b * dm, tt), F32)],
        compiler_params=_params("parallel"),
        name="from_scan_layout",
    )(o_f, o_b)


def _rwkv_mixer(x, modtab, p, *, n_lat, rows):
    b = x.shape[0]
    r, v, nkk, g, w2, k2, b2 = _rwkv_features(x, modtab, p, n_lat=n_lat)
    rs, vs, as_ = _to_scan_layout(r), _to_scan_layout(v), _to_scan_layout(nkk)
    o_f, o_b = (_wkv_scan(rs, _to_scan_layout(w2, d), _to_scan_layout(k2, d), vs, as_, _to_scan_layout(b2, d),
                          n_lat=n_lat, reverse=(d == 1)) for d in range(2))
    return _rwkv_readout(_from_scan_layout(o_f, o_b, b), r, k2, v, g, p, x, modtab, n_lat=n_lat, rows=rows)


def _rope_tables(n_lat, n_ctx, head_dim):
    rows = jnp.repeat(jnp.arange(n_lat // GRID_W, dtype=I32), GRID_W).astype(F32)
    cols = jnp.tile(jnp.arange(GRID_W, dtype=I32), n_lat // GRID_W).astype(F32)
    n_freq = head_dim // 4
    inv_freq = ROPE_THETA ** (-jnp.arange(n_freq, dtype=F32) / n_freq)
    ang = jnp.concatenate([rows[:, None] * inv_freq, cols[:, None] * inv_freq], axis=-1)
    cos, sin = jnp.cos(ang), jnp.sin(ang)
    reps = 128 // head_dim
    cos = jnp.tile(jnp.concatenate([cos, cos], axis=-1), (1, reps))
    sin = jnp.tile(jnp.concatenate([-sin, sin], axis=-1), (1, reps))
    cos = jnp.concatenate([cos, jnp.ones((n_ctx, 128), F32)], axis=0)
    sin = jnp.concatenate([sin, jnp.zeros((n_ctx, 128), F32)], axis=0)
    return cos, sin


def kernel(x, c, ctx, c_ctx, mod_w, mod_b, a_w_qkv, a_w_o, a_q_norm, a_k_norm, b_w_qkv, b_w_o, b_sink,
           c_mu, c_w_rkv, c_w_o, c_w0, c_w1, c_w2, c_a0, c_a1, c_a2, c_g1, c_g2, c_k_k, c_k_a, c_r_k,
           c_ln_w, c_ln_b, router_w, ffn_w1, ffn_w3, ffn_w2, final_norm):
    b, n_lat, d = x.shape
    n_ctx = ctx.shape[1]
    t_all = n_lat + n_ctx
    depth = mod_w.shape[0]
    assert d == D_MODEL and n_ctx % ROW_TILE == 0 and n_lat % min(COMBINE_TILE, n_lat) == 0

    cond_rows = -(-(b + 1) // 8) * 8
    cond = jnp.zeros((cond_rows, d), F32).at[:b].set(c).at[b].set(c_ctx)
    mods = _mod_tables(cond, mod_w, mod_b).reshape(depth, cond_rows, 6, d)
    cos_a, sin_a = _rope_tables(n_lat, n_ctx, A_HEAD_DIM)
    cos_b, sin_b = _rope_tables(n_lat, n_ctx, B_HEAD_DIM)
    xs = jnp.concatenate([x, ctx], axis=1)
    for i in range(depth):
        last = i == depth - 1
        rows = n_lat if last else t_all
        kind, j = i % N_MIXERS, i // N_MIXERS
        modtab = jnp.stack([jnp.broadcast_to(mods[i, b], (b, 6, d)), mods[i, :b]], axis=1)
        if kind == 0:
            q, k, v = _qkv_proj(xs, modtab, a_w_qkv[j].astype(BF16), a_q_norm[j].reshape(1, -1),
                                a_k_norm[j].reshape(1, -1), cos_a, sin_a, kind=0, n_lat=n_lat)
            xs = _global_attention(q, k, v, a_w_o[j].astype(BF16), xs, modtab, n_lat=n_lat, rows=rows)
        elif kind == 1:
            ones = jnp.ones((1, 128), F32)
            q, k, v = _qkv_proj(xs, modtab, b_w_qkv[j].astype(BF16), ones, ones, cos_b, sin_b,
                                kind=1, n_lat=n_lat)
            xs = _window_attention(q, k, v, b_sink[j], b_w_o[j].astype(BF16), xs, modtab, n_lat=n_lat, rows=rows)
        else:
            p = dict(mu=c_mu[j], w_rkv=c_w_rkv[j].astype(BF16), w_o=c_w_o[j].astype(BF16), w0=c_w0[j],
                     w1=c_w1[j].astype(BF16), w2=c_w2[j].astype(BF16), a0=c_a0[j],
                     a1=c_a1[j].astype(BF16), a2=c_a2[j].astype(BF16), g1=c_g1[j].astype(BF16),
                     g2=c_g2[j].astype(BF16), k_k=c_k_k[j].reshape(1, d), k_a=c_k_a[j].reshape(1, d),
                     r_k=c_r_k[j].reshape(2, d), ln_w=c_ln_w[j].reshape(1, d), ln_b=c_ln_b[j].reshape(1, d))
            xs = _rwkv_mixer(xs, modtab, p, n_lat=n_lat, rows=rows)
        xs = _moe(xs, modtab, router_w[i].T, ffn_w1, ffn_w3, ffn_w2, i, final_norm, n_lat=n_lat,
                  with_ctx=not last)
    return xs
```

```python
import functools
import math

import jax
import jax.numpy as jnp
from jax import lax
from jax.experimental import pallas as pl
from jax.experimental.pallas import tpu as pltpu

F32 = jnp.float32
BF16 = jnp.bfloat16
I32 = jnp.int32

D_MODEL = 1024
GRID_W = 64
Q_BLOCK = 128
ROPE_THETA = 10000.0
NORM_EPS = 1e-6
NEG_INF = -1e30
A_HEADS, A_KV_HEADS, A_HEAD_DIM = 8, 2, 128
B_HEADS, B_KV_HEADS, B_HEAD_DIM = 16, 4, 64
WINDOW = 128
C_HEAD_DIM = 64
C_HEADS = D_MODEL // C_HEAD_DIM
C_GN_EPS = C_HEAD_DIM * 1e-5
N_EXPERTS = 16
EXPERT_FF = 2 * D_MODEL
CAPACITY_FACTOR = 2
N_MIXERS = 3
LOG2E = math.log2(math.e)

ROW_TILE = 256
COMBINE_TILE = 1024
COMBINE_EXPERTS = 8
SAMPLE_GROUP = 2
SLOT_TILE = 128
SCAN_CHUNK = 32
VMEM_LIMIT = 56 * 1024 * 1024
FFN_VMEM_LIMIT = 60 * 1024 * 1024


def _params(*sem, vmem_limit=VMEM_LIMIT):
    return pltpu.CompilerParams(dimension_semantics=sem, vmem_limit_bytes=vmem_limit)


def _rms(x):
    return x * lax.rsqrt(jnp.mean(x * x, axis=-1, keepdims=True) + NORM_EPS)


def _normmod(x, mod_ref, shift_row, scale_row):
    return _rms(x) * (1.0 + mod_ref[scale_row:scale_row + 1, :]) + mod_ref[shift_row:shift_row + 1, :]


def _bdot(a, b):
    return jnp.dot(a.astype(BF16), b.astype(BF16), preferred_element_type=F32)


def _dot_nt(a, b):
    return lax.dot_general(a.astype(BF16), b.astype(BF16), (((1,), (1,)), ((), ())),
                           preferred_element_type=F32)


def _sigmoid(x):
    return 1.0 / (1.0 + jnp.exp(-x))


def _mod_spec(n_lat_tiles, group=None):
    return pl.BlockSpec((group, None, 6, D_MODEL),
                        lambda b, t: (b, jnp.where(t < n_lat_tiles, 1, 0), 0, 0))


def _sample_group(b):
    return SAMPLE_GROUP if b % SAMPLE_GROUP == 0 else 1


def _mod_kernel(cond_ref, w_ref, b_ref, o_ref):
    c = cond_ref[...]
    a = c * _sigmoid(c)
    o_ref[0] = _bdot(a, w_ref[0]) + b_ref[0]


def _mod_tables(cond, mod_w, mod_b):
    depth, d, n = mod_w.shape
    rows = cond.shape[0]
    return pl.pallas_call(
        _mod_kernel,
        grid=(depth, n // d),
        in_specs=[pl.BlockSpec((rows, d), lambda i, j: (0, 0)),
                  pl.BlockSpec((1, d, d), lambda i, j: (i, 0, j)),
                  pl.BlockSpec((1, 1, d), lambda i, j: (i, 0, j))],
        out_specs=pl.BlockSpec((1, rows, d), lambda i, j: (i, 0, j)),
        out_shape=jax.ShapeDtypeStruct((depth, rows, n), F32),
        compiler_params=_params("arbitrary", "arbitrary"),
        name="mod_tables",
    )(cond, mod_w, mod_b.reshape(depth, 1, n))


def _rope128(x, cos, sin_signed, half):
    if half == 64:
        rot = pltpu.roll(x, 64, axis=1)
    else:
        lane = lax.broadcasted_iota(I32, x.shape, 1)
        rot = jnp.where((lane % 64) < 32, pltpu.roll(x, 96, axis=1), pltpu.roll(x, 32, axis=1))
    return x * cos + rot * sin_signed


def _qkv_kernel(x_ref, mod_ref, *refs, kind):
    shared, outs = refs[:5], refs[5:]
    for s in range(x_ref.shape[0]):
        _qkv_tile(x_ref.at[s], mod_ref.at[s], *shared, *(o.at[s] for o in outs), kind=kind)


def _qkv_tile(x_ref, mod_ref, w_ref, gq_ref, gk_ref, cos_ref, sin_ref, q_ref, k_ref, v_ref, *, kind):
    h = _normmod(x_ref[...], mod_ref, 0, 1)
    y = _bdot(h, w_ref[...])
    cos, sin = cos_ref[...], sin_ref[...]
    nq = D_MODEL
    nkv = (y.shape[1] - nq) // 2
    half = 64 if kind == 0 else 32
    q_scale = (2 * half) ** -0.5 * LOG2E
    for j in range((nq + nkv) // 128):
        s = y[:, j * 128:(j + 1) * 128]
        if kind == 0:
            gain = gq_ref[...] if j < nq // 128 else gk_ref[...]
            s = _rms(s) * gain
        s = _rope128(s, cos, sin, half)
        if j < nq // 128:
            q_ref[:, j * 128:(j + 1) * 128] = (s * q_scale).astype(BF16)
        elif kind == 0:
            k_ref[:, (j - nq // 128) * 128:(j - nq // 128 + 1) * 128] = s.astype(BF16)
        else:
            jj = (j - nq // 128) * 2
            k_ref[jj] = s[:, :64].astype(BF16)
            k_ref[jj + 1] = s[:, 64:].astype(BF16)
    v = y[:, nq + nkv:].astype(BF16)
    if kind == 0:
        v_ref[...] = v
    else:
        for jj in range(nkv // 64):
            v_ref[jj] = v[:, jj * 64:(jj + 1) * 64]


def _qkv_proj(x, modtab, w, gq, gk, cos, sin, *, kind, n_lat):
    b, t_all, d = x.shape
    tm = ROW_TILE
    nt = t_all // tm
    n = w.shape[1]
    nkv = (n - d) // 2
    grp = _sample_group(b)
    if kind == 0:
        kv_shape = jax.ShapeDtypeStruct((b, t_all, nkv), BF16)
        kv_spec = pl.BlockSpec((grp, tm, nkv), lambda i, t: (i, t, 0))
    else:
        kv_shape = jax.ShapeDtypeStruct((b, nkv // 64, t_all, 64), BF16)
        kv_spec = pl.BlockSpec((grp, nkv // 64, tm, 64), lambda i, t: (i, 0, t, 0))
    return pl.pallas_call(
        functools.partial(_qkv_kernel, kind=kind),
        grid=(b // grp, nt),
        in_specs=[pl.BlockSpec((grp, tm, d), lambda i, t: (i, t, 0)),
                  _mod_spec(n_lat // tm, grp),
                  pl.BlockSpec((d, n), lambda i, t: (0, 0)),
                  pl.BlockSpec((1, 128), lambda i, t: (0, 0)),
                  pl.BlockSpec((1, 128), lambda i, t: (0, 0)),
                  pl.BlockSpec((tm, 128), lambda i, t: (t, 0)),
                  pl.BlockSpec((tm, 128), lambda i, t: (t, 0))],
        out_specs=[pl.BlockSpec((grp, tm, d), lambda i, t: (i, t, 0)), kv_spec, kv_spec],
        out_shape=[jax.ShapeDtypeStruct((b, t_all, d), BF16), kv_shape, kv_shape],
        compiler_params=_params("parallel", "parallel"),
        name=f"qkv_proj_{kind}",
    )(x, modtab, w, gq, gk, cos, sin)


def _flash_chunk(carry, q, kc, vc):
    m, l, acc = carry
    s = _dot_nt(q, kc)
    m_new = jnp.maximum(m, jnp.max(s, axis=-1, keepdims=True))
    alpha = jnp.exp2(m - m_new)
    p = jnp.exp2(s - m_new)
    l = alpha * l + jnp.sum(p, axis=-1, keepdims=True)
    acc = alpha * acc + jnp.dot(p.astype(BF16), vc, preferred_element_type=F32)
    return m_new, l, acc


def _gattn_kernel(q_ref, k_ref, v_ref, wo_ref, x_ref, mod_ref, o_ref, att_ref, *, n_lat, kchunk):
    hd = A_HEAD_DIM
    g = A_HEADS // A_KV_HEADS
    tq = q_ref.shape[0]
    t_all = k_ref.shape[0]
    groups = range(A_KV_HEADS)
    qs = [jnp.concatenate([q_ref[:, (j * g + i) * hd:(j * g + i + 1) * hd] for i in range(g)], axis=0)
          for j in groups]
    init = (jnp.full((g * tq, 1), NEG_INF, F32), jnp.zeros((g * tq, 1), F32), jnp.zeros((g * tq, hd), F32))

    def chunk(j, carry, rows):
        return _flash_chunk(carry, qs[j], k_ref[rows, j * hd:(j + 1) * hd], v_ref[rows, j * hd:(j + 1) * hd])

    def finish(carries):
        for j, (_, l, acc) in enumerate(carries):
            o = (acc / l).astype(BF16)
            for i in range(g):
                att_ref[:, (j * g + i) * hd:(j * g + i + 1) * hd] = o[i * tq:(i + 1) * tq]

    ctx_carries = tuple(chunk(j, init, slice(n_lat, t_all)) for j in groups)
    is_ctx = pl.program_id(1) >= n_lat // tq

    @pl.when(is_ctx)
    def _():
        finish(ctx_carries)

    @pl.when(jnp.logical_not(is_ctx))
    def _():
        def body(c, carries):
            rows = pl.ds(pl.multiple_of(c * kchunk, kchunk), kchunk)
            return tuple(chunk(j, carries[j], rows) for j in groups)
        finish(lax.fori_loop(0, n_lat // kchunk, body, ctx_carries))

    _project_residual(att_ref, wo_ref, x_ref, mod_ref, o_ref)


def _project_residual(att_ref, wo_ref, x_ref, mod_ref, o_ref):
    o_ref[...] = x_ref[...] + mod_ref[2:3, :] * jnp.dot(att_ref[...], wo_ref[...], preferred_element_type=F32)


def _global_attention(q, k, v, w_o, x, modtab, *, n_lat, rows):
    b, t_all, d = q.shape
    nkv = k.shape[2]
    tq = ROW_TILE
    return pl.pallas_call(
        functools.partial(_gattn_kernel, n_lat=n_lat, kchunk=min(2048, n_lat)),
        grid=(b, rows // tq),
        in_specs=[pl.BlockSpec((None, tq, d), lambda i, t: (i, t, 0)),
                  pl.BlockSpec((None, t_all, nkv), lambda i, t: (i, 0, 0)),
                  pl.BlockSpec((None, t_all, nkv), lambda i, t: (i, 0, 0)),
                  pl.BlockSpec((d, d), lambda i, t: (0, 0)),
                  pl.BlockSpec((None, tq, d), lambda i, t: (i, t, 0)),
                  _mod_spec(n_lat // tq)],
        out_specs=pl.BlockSpec((None, tq, d), lambda i, t: (i, t, 0)),
        out_shape=jax.ShapeDtypeStruct((b, rows, d), F32),
        scratch_shapes=[pltpu.VMEM((tq, d), BF16)],
        compiler_params=_params("parallel", "arbitrary"),
        name="global_attention",
    )(q, k, v, w_o, x, modtab)


def _wattn_kernel(q_ref, k_ref, v_ref, sink_ref, bias_ref, wo_ref, x_ref, mod_ref, o_ref, att_ref, *, n_lat):
    hd = B_HEAD_DIM
    g = B_HEADS // B_KV_HEADS
    tq = q_ref.shape[0]
    t_all = k_ref.shape[1]
    t = pl.program_id(1)
    n_tiles = n_lat // tq
    is_ctx = t >= n_tiles

    def group(j):
        q = jnp.concatenate([q_ref[:, (j * g + i) * hd:(j * g + i + 1) * hd] for i in range(g)], axis=0)
        sink = jnp.concatenate([jnp.broadcast_to(sink_ref[j, :, i:i + 1], (tq, 1)) for i in range(g)],
                               axis=0) * LOG2E
        kc, vc = k_ref[j, n_lat:t_all, :], v_ref[j, n_lat:t_all, :]
        return q, sink, _dot_nt(q, kc), vc

    def finish(j, o):
        o = o.astype(BF16)
        att_ref[:, j * g * hd:(j + 1) * g * hd] = jnp.concatenate([o[i * tq:(i + 1) * tq] for i in range(g)], axis=1)

    @pl.when(is_ctx)
    def _():
        for j in range(B_KV_HEADS):
            q, sink, s_ctx, vc = group(j)
            m = jnp.maximum(jnp.max(s_ctx, axis=-1, keepdims=True), sink)
            p = jnp.exp2(s_ctx - m)
            l = jnp.sum(p, axis=-1, keepdims=True) + jnp.exp2(sink - m)
            finish(j, jnp.dot(p.astype(BF16), vc, preferred_element_type=F32) / l)

    @pl.when(jnp.logical_not(is_ctx))
    def _():
        wlen = tq + 2 * WINDOW
        ws = jnp.clip(t * tq - WINDOW, 0, n_lat - wlen)
        start = pl.multiple_of(ws, 128)
        bias = bias_ref[jnp.where(t == 0, 0, jnp.where(t == n_tiles - 1, 2, 1))]
        bias = jnp.concatenate([bias] * g, axis=0)
        for j in range(B_KV_HEADS):
            q, sink, _, _ = group(j)
            keys = jnp.concatenate([k_ref[j, pl.ds(start, wlen), :], k_ref[j, n_lat:t_all, :]], axis=0)
            vals = jnp.concatenate([v_ref[j, pl.ds(start, wlen), :], v_ref[j, n_lat:t_all, :]], axis=0)
            s = _dot_nt(q, keys) + bias
            m = jnp.maximum(jnp.max(s, axis=-1, keepdims=True), sink)
            p = jnp.exp2(s - m)
            l = jnp.sum(p, axis=-1, keepdims=True) + jnp.exp2(sink - m)
            finish(j, jnp.dot(p.astype(BF16), vals, preferred_element_type=F32) / l)

    _project_residual(att_ref, wo_ref, x_ref, mod_ref, o_ref)


def _band_bias(tq, n_ctx):
    row = jnp.arange(tq, dtype=I32)[:, None]
    col = jnp.arange(tq + 2 * WINDOW, dtype=I32)[None, :]
    band = jnp.stack([jnp.where(jnp.abs(col - off - row) <= WINDOW, 0.0, NEG_INF).astype(F32)
                      for off in (0, WINDOW, 2 * WINDOW)])
    return jnp.concatenate([band, jnp.zeros((3, tq, n_ctx), F32)], axis=2)


def _window_attention(q, k, v, sink, w_o, x, modtab, *, n_lat, rows):
    b, t_all, d = q.shape
    hd, g = B_HEAD_DIM, B_HEADS // B_KV_HEADS
    tq = Q_BLOCK
    wlen = tq + 2 * WINDOW
    return pl.pallas_call(
        functools.partial(_wattn_kernel, n_lat=n_lat),
        grid=(b, rows // tq),
        in_specs=[pl.BlockSpec((None, tq, d), lambda i, t: (i, t, 0)),
                  pl.BlockSpec((None, B_KV_HEADS, t_all, hd), lambda i, t: (i, 0, 0, 0)),
                  pl.BlockSpec((None, B_KV_HEADS, t_all, hd), lambda i, t: (i, 0, 0, 0)),
                  pl.BlockSpec((B_KV_HEADS, 1, g), lambda i, t: (0, 0, 0)),
                  pl.BlockSpec((3, tq, wlen + t_all - n_lat), lambda i, t: (0, 0, 0)),
                  pl.BlockSpec((d, d), lambda i, t: (0, 0)),
                  pl.BlockSpec((None, tq, d), lambda i, t: (i, t, 0)),
                  _mod_spec(n_lat // tq)],
        out_specs=pl.BlockSpec((None, tq, d), lambda i, t: (i, t, 0)),
        out_shape=jax.ShapeDtypeStruct((b, rows, d), F32),
        scratch_shapes=[pltpu.VMEM((tq, d), BF16)],
        compiler_params=_params("parallel", "arbitrary"),
        name="window_attention",
    )(q, k, v, sink.reshape(B_KV_HEADS, 1, g), _band_bias(tq, t_all - n_lat), w_o, x, modtab)


def _router_kernel(x_ref, mod_ref, rwt_ref, h_ref, aff_ref, afft_ref):
    for s in range(x_ref.shape[0]):
        h = _normmod(x_ref[s], mod_ref.at[s], 3, 4)
        h_ref[s] = h.astype(BF16)
        logits_t = lax.dot_general(rwt_ref[...], h, (((1,), (1,)), ((), ())),
                                   precision=lax.Precision.HIGHEST, preferred_element_type=F32)
        e = jnp.exp(logits_t - jnp.max(logits_t, axis=0, keepdims=True))
        aff_t = e / jnp.sum(e, axis=0, keepdims=True)
        afft_ref[s] = aff_t
        aff_ref[s] = aff_t.T


def _router(x, modtab, router_w_t, *, n_lat):
    b, rows, d = x.shape
    tm = ROW_TILE
    ne = router_w_t.shape[0]
    grp = _sample_group(b)
    return pl.pallas_call(
        _router_kernel,
        grid=(b // grp, rows // tm),
        in_specs=[pl.BlockSpec((grp, tm, d), lambda i, t: (i, t, 0)),
                  _mod_spec(n_lat // tm, grp),
                  pl.BlockSpec((ne, d), lambda i, t: (0, 0))],
        out_specs=[pl.BlockSpec((grp, tm, d), lambda i, t: (i, t, 0)),
                   pl.BlockSpec((grp, tm, ne), lambda i, t: (i, t, 0)),
                   pl.BlockSpec((grp, ne, tm), lambda i, t: (i, 0, t))],
        out_shape=[jax.ShapeDtypeStruct((b, rows, d), BF16),
                   jax.ShapeDtypeStruct((b, rows, ne), F32),
                   jax.ShapeDtypeStruct((b, ne, rows), F32)],
        compiler_params=_params("parallel", "parallel"),
        name="router",
    )(x, modtab, router_w_t)


def _lane_prefix(mask_f, tri):
    n = mask_f.shape[1]
    run = jnp.zeros((mask_f.shape[0], 1), F32)
    out = []
    for j in range(n // 128):
        blk = mask_f[:, j * 128:(j + 1) * 128]
        incl = jnp.dot(blk.astype(BF16), tri, preferred_element_type=F32)
        out.append(incl - blk + run)
        run = run + incl[:, 127:128]
    return jnp.concatenate(out, axis=1)


def _select_kernel(afft_ref, slot_ref, slot_tm_ref, *, segments):
    r = lax.broadcasted_iota(I32, (128, 128), 0)
    c = lax.broadcasted_iota(I32, (128, 128), 1)
    tri = (r <= c).astype(BF16)
    for off, n, cap, base in segments:
        bits = lax.bitcast_convert_type(afft_ref[:, off:off + n], I32)
        cap_f = jnp.float32(cap)

        def body(i, thr):
            cand = thr | jnp.left_shift(jnp.int32(1), 30 - i)
            cnt = jnp.sum((bits >= cand).astype(F32), axis=1, keepdims=True)
            return jnp.where(cnt >= cap_f, cand, thr)

        thr = lax.fori_loop(0, 31, body, jnp.zeros((bits.shape[0], 1), I32))
        gt = (bits > thr).astype(F32)
        eq = (bits == thr).astype(F32)
        need = cap_f - jnp.sum(gt, axis=1, keepdims=True)
        sel = gt + eq * (_lane_prefix(eq, tri) < need).astype(F32)
        slot = _lane_prefix(sel, tri).astype(I32) + base
        slot = jnp.where(sel > 0.5, slot, -1)
        slot_ref[:, off:off + n] = slot
        slot_tm_ref[off:off + n, :] = slot.astype(F32).T


def _select(aff_t, segments):
    b, ne, rows = aff_t.shape
    return pl.pallas_call(
        functools.partial(_select_kernel, segments=segments),
        grid=(b,),
        in_specs=[pl.BlockSpec((None, ne, rows), lambda i: (i, 0, 0))],
        out_specs=[pl.BlockSpec((None, ne, rows), lambda i: (i, 0, 0)),
                   pl.BlockSpec((None, rows, ne), lambda i: (i, 0, 0))],
        out_shape=[jax.ShapeDtypeStruct((b, ne, rows), I32), jax.ShapeDtypeStruct((b, rows, ne), F32)],
        compiler_params=_params("parallel"),
        name="expert_select",
    )(aff_t)


def _gather_kernel(slot_ref, h_ref, xs_ref, *, segments):
    for off, n, cap, base in segments:
        slot = slot_ref[:, off:off + n]
        hseg = h_ref[off:off + n, :]
        st = min(SLOT_TILE, cap)
        for j in range(cap // st):
            ids = base + j * st + lax.broadcasted_iota(I32, (st, 1), 0)
            onehot = jnp.where(ids == slot, 1.0, 0.0).astype(BF16)
            rows = jnp.dot(onehot, hseg, preferred_element_type=F32)
            xs_ref[base + j * st:base + (j + 1) * st, :] = rows.astype(BF16)


def _gather(slot, h, segments, n_slots):
    b, ne, rows = slot.shape
    d = h.shape[2]
    return pl.pallas_call(
        functools.partial(_gather_kernel, segments=segments),
        grid=(b, ne),
        in_specs=[pl.BlockSpec((None, None, 1, rows), lambda i, e: (i, e, 0, 0)),
                  pl.BlockSpec((None, rows, d), lambda i, e: (i, 0, 0))],
        out_specs=pl.BlockSpec((None, None, n_slots, d), lambda i, e: (e, i, 0, 0)),
        out_shape=jax.ShapeDtypeStruct((ne, b, n_slots, d), BF16),
        compiler_params=_params("parallel", "arbitrary"),
        name="expert_gather",
    )(slot.reshape(b, ne, 1, rows), h)


def _ffn_kernel(xs_ref, w1_ref, w3_ref, w2_ref, y_ref):
    xs = xs_ref[...]
    a = _bdot(xs, w1_ref[...])
    g = _bdot(xs, w3_ref[...])
    hid = (a * _sigmoid(a)) * g
    y_ref[...] = _bdot(hid, w2_ref[...]).astype(BF16)


def _expert_ffn(xs, w1, w3, w2, layer):
    ne, b, n_slots, d = xs.shape
    ff = w1.shape[3]
    return pl.pallas_call(
        _ffn_kernel,
        grid=(ne, b),
        in_specs=[pl.BlockSpec((None, None, n_slots, d), lambda e, i: (e, i, 0, 0)),
                  pl.BlockSpec((None, None, d, ff), lambda e, i: (layer, e, 0, 0)),
                  pl.BlockSpec((None, None, d, ff), lambda e, i: (layer, e, 0, 0)),
                  pl.BlockSpec((None, None, ff, d), lambda e, i: (layer, e, 0, 0))],
        out_specs=pl.BlockSpec((None, None, n_slots, d), lambda e, i: (e, i, 0, 0)),
        out_shape=jax.ShapeDtypeStruct((ne, b, n_slots, d), BF16),
        compiler_params=_params("parallel", "arbitrary", vmem_limit=FFN_VMEM_LIMIT),
        name="expert_ffn",
    )(xs, w1, w3, w2)


def _combine_kernel(slot_ref, aff_ref, y_ref, x_ref, mod_ref, fw_ref, o_ref, acc_ref, *, slot0, final):
    eg = pl.program_id(2)
    n_groups = pl.num_programs(2)
    group, width = y_ref.shape[:2]

    @pl.when(eg == 0)
    def _():
        acc_ref[...] = jnp.zeros_like(acc_ref)

    lane_e = lax.broadcasted_iota(I32, aff_ref.shape, 1)
    ids = slot0 + lax.broadcasted_iota(I32, (1, width), 1)
    total = None
    for j in range(group):
        e = eg * group + j
        gate = jnp.sum(jnp.where(lane_e == e, aff_ref[...], 0.0), axis=1, keepdims=True)
        slot = jnp.sum(jnp.where(lane_e == e, slot_ref[...], 0.0), axis=1, keepdims=True).astype(I32)
        onehot = jnp.where(slot == ids, 1.0, 0.0).astype(BF16)
        part = gate * jnp.dot(onehot, y_ref[j], preferred_element_type=F32)
        total = part if total is None else total + part
    acc_ref[...] += total

    @pl.when(eg == n_groups - 1)
    def _():
        out = x_ref[...] + mod_ref[5:6, :] * acc_ref[...]
        o_ref[...] = _rms(out) * fw_ref[...] if final else out


def _combine(slot_tm, aff, y, x, modtab, final_w, *, row0, rows, tm, slot0, width, seg, in_place, final=False):
    b, _, d = x.shape
    ne = aff.shape[2]
    group = COMBINE_EXPERTS
    t0, s0 = row0 // tm, slot0 // width
    out_rows = x.shape[1] if in_place else rows
    return pl.pallas_call(
        functools.partial(_combine_kernel, slot0=slot0, final=final),
        grid=(b, rows // tm, ne // group),
        in_specs=[pl.BlockSpec((None, tm, ne), lambda i, t, e: (i, t + t0, 0)),
                  pl.BlockSpec((None, tm, ne), lambda i, t, e: (i, t + t0, 0)),
                  pl.BlockSpec((group, None, width, d), lambda i, t, e: (e, i, s0, 0)),
                  pl.BlockSpec((None, tm, d), lambda i, t, e: (i, t + t0, 0)),
                  pl.BlockSpec((None, None, 6, d), lambda i, t, e: (i, seg, 0, 0)),
                  pl.BlockSpec((1, d), lambda i, t, e: (0, 0))],
        out_specs=pl.BlockSpec((None, tm, d), lambda i, t, e: (i, t + (t0 if in_place else 0), 0)),
        out_shape=jax.ShapeDtypeStruct((b, out_rows, d), F32),
        scratch_shapes=[pltpu.VMEM((tm, d), F32)],
        input_output_aliases={3: 0} if in_place else {},
        compiler_params=_params("parallel", "parallel", "arbitrary"),
        name="expert_combine",
    )(slot_tm, aff, y, x, modtab, final_w.reshape(1, d))


def _moe(x, modtab, router_w_t, w1, w3, w2, layer, final_w, *, n_lat, with_ctx):
    b, rows, d = x.shape
    ne = N_EXPERTS
    cap_l = CAPACITY_FACTOR * n_lat // ne
    h, aff, aff_t = _router(x, modtab, router_w_t, n_lat=n_lat)
    segments = ((0, n_lat, cap_l, 0),)
    n_slots = cap_l
    if with_ctx:
        n_ctx = rows - n_lat
        cap_c = CAPACITY_FACTOR * n_ctx // ne
        segments += ((n_lat, n_ctx, cap_c, cap_l),)
        n_slots += cap_c
    slot, slot_tm = _select(aff_t, segments)
    xs = _gather(slot, h, segments, n_slots)
    y = _expert_ffn(xs, w1, w3, w2, layer)
    out = _combine(slot_tm, aff, y, x, modtab, final_w, row0=0, rows=n_lat, tm=min(COMBINE_TILE, n_lat),
                   slot0=0, width=cap_l, seg=1, in_place=with_ctx, final=not with_ctx)
    if with_ctx:
        out = _combine(slot_tm, aff, y, out, modtab, final_w, row0=n_lat, rows=n_ctx, tm=n_ctx, slot0=cap_l,
                       width=cap_c, seg=0, in_place=True)
    return out


def _segsum64(x, bd):
    hi = x.astype(BF16)
    lo = (x - hi.astype(F32)).astype(BF16)
    out = []
    for j in range(x.shape[1] // 256):
        sl = slice(j * 256, (j + 1) * 256)
        out.append(jnp.dot(hi[:, sl], bd, preferred_element_type=F32)
                   + jnp.dot(lo[:, sl], bd, preferred_element_type=F32))
    return jnp.concatenate(out, axis=1)


def _block_diag_ones():
    r = lax.broadcasted_iota(I32, (256, 256), 0)
    c = lax.broadcasted_iota(I32, (256, 256), 1)
    return (r // 64 == c // 64).astype(BF16)


def _rwkv_feat_kernel(x_ref, xp_ref, xn_ref, mod_ref, mu_ref, wrkv_ref, w0_ref, w1_ref, w2_ref,
                      a0_ref, a1_ref, a2_ref, g1_ref, g2_ref, kk_ref, ka_ref,
                      r_out, v_out, nkk_out, g_out, w_out, k_out, b_out, *, n_lat_tiles):
    t = pl.program_id(1)
    nt = pl.num_programs(1)
    tm = x_ref.shape[0]
    h = _normmod(x_ref[...], mod_ref, 0, 1)
    hp = _normmod(xp_ref[7:8, :], mod_ref, 0, 1)
    hn = _normmod(xn_ref[0:1, :], mod_ref, 0, 1)
    has_left = jnp.logical_and(t != 0, t != n_lat_tiles)
    has_right = jnp.logical_and(t != n_lat_tiles - 1, t != nt - 1)
    hp = jnp.where(has_left, hp, 0.0)
    hn = jnp.where(has_right, hn, 0.0)
    row = lax.broadcasted_iota(I32, h.shape, 0)
    left = jnp.where(row == 0, hp, pltpu.roll(h, 1, axis=0))
    right = jnp.where(row == tm - 1, hn, pltpu.roll(h, tm - 1, axis=0))
    xx = 0.5 * (left + right) - h
    xr, xw, xk, xv, xa, xg = (h + xx * mu_ref[i:i + 1, :] for i in range(6))
    r = _bdot(xr, wrkv_ref[0])
    k = _bdot(xk, wrkv_ref[1])
    v = _bdot(xv, wrkv_ref[2])
    g = _bdot(_sigmoid(_bdot(xg, g1_ref[...])), g2_ref[...])
    bd = _block_diag_ones()
    kk = k * kk_ref[...]
    kk = kk * lax.rsqrt(jnp.maximum(_segsum64(kk * kk, bd), 1e-24))
    r_out[...] = r
    v_out[...] = v
    nkk_out[...] = -kk
    g_out[...] = g
    for d in range(2):
        w_lora = _bdot(jnp.tanh(_bdot(xw, w1_ref[d])), w2_ref[d])
        w_out[d] = jnp.exp(-math.exp(-0.5) * _sigmoid(w0_ref[d:d + 1, :] + w_lora))
        a = _sigmoid(a0_ref[d:d + 1, :] + _bdot(_bdot(xa, a1_ref[d]), a2_ref[d]))
        k_out[d] = k * (1.0 + (a - 1.0) * ka_ref[...])
        b_out[d] = kk * a


def _rwkv_features(x, modtab, p, *, n_lat):
    b, t_all, d = x.shape
    tm = ROW_TILE
    nt = t_all // tm
    tb = tm // 8
    full = lambda shape: pl.BlockSpec(shape, lambda i, t: (0,) * len(shape))
    tok = pl.BlockSpec((None, tm, d), lambda i, t: (i, t, 0))
    tok2 = pl.BlockSpec((2, None, tm, d), lambda i, t: (0, i, t, 0))
    one = jax.ShapeDtypeStruct((b, t_all, d), F32)
    two = jax.ShapeDtypeStruct((2, b, t_all, d), F32)
    return pl.pallas_call(
        functools.partial(_rwkv_feat_kernel, n_lat_tiles=n_lat // tm),
        grid=(b, nt),
        in_specs=[tok,
                  pl.BlockSpec((None, 8, d), lambda i, t: (i, jnp.maximum(t * tb - 1, 0), 0)),
                  pl.BlockSpec((None, 8, d), lambda i, t: (i, jnp.minimum((t + 1) * tb, nt * tb - 1), 0)),
                  _mod_spec(n_lat // tm),
                  full((6, d)), full((3, d, d)), full((2, d)), full(p["w1"].shape), full(p["w2"].shape),
                  full((2, d)), full(p["a1"].shape), full(p["a2"].shape), full(p["g1"].shape),
                  full(p["g2"].shape), full((1, d)), full((1, d))],
        out_specs=[tok, tok, tok, tok, tok2, tok2, tok2],
        out_shape=[one, one, one, one, two, two, two],
        compiler_params=_params("parallel", "parallel"),
        name="rwkv_features",
    )(x, x, x, modtab, p["mu"], p["w_rkv"], p["w0"], p["w1"], p["w2"], p["a0"], p["a1"], p["a2"],
      p["g1"], p["g2"], p["k_k"], p["k_a"])


def _scan_kernel(r_ref, w_ref, k_ref, v_ref, a_ref, b_ref, o_ref, s_ref, wr_ref, *, reverse):
    n = s_ref.shape[0]
    steps = r_ref.shape[0]

    @pl.when(pl.program_id(0) == 0)
    def _():
        s_ref[...] = jnp.zeros_like(s_ref)

    def step(i, carry):
        j = steps - 1 - i if reverse else i
        r = r_ref[j]
        wr_ref[...] = w_ref[j] * r
        br = jnp.sum(b_ref[j] * r, axis=0, keepdims=True)
        kr = jnp.sum(k_ref[j] * r, axis=0, keepdims=True)
        acc = [jnp.zeros(s_ref.shape[1:], F32) for _ in range(4)]
        for kk in range(n):
            s = s_ref[kk]
            acc[kk % 2] = acc[kk % 2] + s * a_ref[j, kk:kk + 1, :]
            acc[2 + kk % 2] = acc[2 + kk % 2] + s * wr_ref[kk:kk + 1, :]
        sa = acc[0] + acc[1]
        v = v_ref[j]
        o_ref[j] = acc[2] + acc[3] + sa * br + v * kr
        for kk in range(n):
            s_ref[kk] = (s_ref[kk] * w_ref[j, kk:kk + 1, :] + sa * b_ref[j, kk:kk + 1, :]
                         + v * k_ref[j, kk:kk + 1, :])
        return carry

    lax.fori_loop(0, steps, step, 0)


def _wkv_scan(r, w, k, v, a, b, *, n_lat, reverse):
    t_all, n, chains = r.shape
    tc = SCAN_CHUNK
    nlc, nch = n_lat // tc, t_all // tc
    ncc = nch - nlc
    if reverse:
        idx = lambda c: (jnp.where(c < ncc, nch - 1 - c, nlc - 1 - (c - ncc)), 0, 0)
    else:
        idx = lambda c: (jnp.where(c < ncc, nlc + c, c - ncc), 0, 0)
    spec = pl.BlockSpec((tc, n, chains), idx)
    return pl.pallas_call(
        functools.partial(_scan_kernel, reverse=reverse),
        grid=(nch,),
        in_specs=[spec] * 6,
        out_specs=spec,
        out_shape=jax.ShapeDtypeStruct((t_all, n, chains), F32),
        scratch_shapes=[pltpu.VMEM((n, n, chains), F32), pltpu.VMEM((n, chains), F32)],
        compiler_params=_params("arbitrary"),
        name="wkv_scan_bwd" if reverse else "wkv_scan_fwd",
    )(r, w, k, v, a, b)


def _rwkv_out_kernel(o_ref, r_ref, k_ref, v_ref, g_ref, rk_ref, lnw_ref, lnb_ref, wo_ref, x_ref, mod_ref,
                     out_ref):
    bd = _block_diag_ones()
    inv_n = 1.0 / C_HEAD_DIM
    o = o_ref[...]
    o = o - _segsum64(o, bd) * inv_n
    o = o * lax.rsqrt(_segsum64(o * o, bd) * inv_n + C_GN_EPS)
    o = o * lnw_ref[...] + lnb_ref[...]
    r = r_ref[...]
    bonus = _segsum64(r * k_ref[0] * rk_ref[0:1, :] + r * k_ref[1] * rk_ref[1:2, :], bd) * v_ref[...]
    y = _bdot((o + bonus) * g_ref[...], wo_ref[...])
    out_ref[...] = x_ref[...] + mod_ref[2:3, :] * y


def _rwkv_readout(o, r, k2, v, g, p, x, modtab, *, n_lat, rows):
    b, t_all, d = x.shape
    tm = ROW_TILE
    tok = pl.BlockSpec((None, tm, d), lambda i, t: (i, t, 0))
    full = lambda shape: pl.BlockSpec(shape, lambda i, t: (0,) * len(shape))
    return pl.pallas_call(
        _rwkv_out_kernel,
        grid=(b, rows // tm),
        in_specs=[tok, tok, pl.BlockSpec((2, None, tm, d), lambda i, t: (0, i, t, 0)), tok, tok,
                  full((2, d)), full((1, d)), full((1, d)), full((d, d)), tok, _mod_spec(n_lat // tm)],
        out_specs=tok,
        out_shape=jax.ShapeDtypeStruct((b, rows, d), F32),
        compiler_params=_params("parallel", "parallel"),
        name="rwkv_readout",
    )(o, r, k2, v, g, p["r_k"], p["ln_w"], p["ln_b"], p["w_o"], x, modtab)


def _to_scan_kernel(x_ref, o_ref, y_ref):
    nb, tt, d = x_ref.shape
    n, chains = o_ref.shape[1:]
    for b in range(nb):
        for p in range(d // 128):
            row = (b * (d // 128) + p) * 128
            y_ref[row:row + 128, :] = x_ref[b, :, p * 128:(p + 1) * 128].T
    for k in range(n):
        o_ref[:, k, :] = y_ref[pl.ds(k, chains, stride=n), :].T


def _to_scan_layout(a, d=None):
    b, t, dm = a.shape[-3:]
    tt = 128
    chains = b * dm // C_HEAD_DIM
    if d is None:
        spec = pl.BlockSpec((b, tt, dm), lambda i: (0, i, 0))
    else:
        spec = pl.BlockSpec((None, b, tt, dm), lambda i: (d, 0, i, 0))
    return pl.pallas_call(
        _to_scan_kernel,
        grid=(t // tt,),
        in_specs=[spec],
        out_specs=pl.BlockSpec((tt, C_HEAD_DIM, chains), lambda i: (i, 0, 0)),
        out_shape=jax.ShapeDtypeStruct((t, C_HEAD_DIM, chains), F32),
        scratch_shapes=[pltpu.VMEM((b * dm, tt), F32)],
        compiler_params=_params("parallel"),
        name="to_scan_layout",
    )(a)


def _from_scan_kernel(a_ref, b_ref, o_ref, y_ref):
    nb, tt, d = o_ref.shape
    n, chains = a_ref.shape[1:]
    for k in range(n):
        y_ref[pl.ds(k, chains, stride=n), :] = (a_ref[:, k, :] + b_ref[:, k, :]).T
    for b in range(nb):
        for p in range(d // 128):
            row = (b * (d // 128) + p) * 128
            o_ref[b, :, p * 128:(p + 1) * 128] = y_ref[row:row + 128, :].T


def _from_scan_layout(o_f, o_b, b):
    t, n, chains = o_f.shape
    tt = 128
    dm = n * chains // b
    spec = pl.BlockSpec((tt, n, chains), lambda i: (i, 0, 0))
    return pl.pallas_call(
        _from_scan_kernel,
        grid=(t // tt,),
        in_specs=[spec, spec],
        out_specs=pl.BlockSpec((b, tt, dm), lambda i: (0, i, 0)),
        out_shape=jax.ShapeDtypeStruct((b, t, dm), F32),
        scratch_shapes=[pltpu.VMEM((b * dm, tt), F32)],
        compiler_params=_params("parallel"),
        name="from_scan_layout",
    )(o_f, o_b)


def _rwkv_mixer(x, modtab, p, *, n_lat, rows):
    b = x.shape[0]
    r, v, nkk, g, w2, k2, b2 = _rwkv_features(x, modtab, p, n_lat=n_lat)
    rs, vs, as_ = _to_scan_layout(r), _to_scan_layout(v), _to_scan_layout(nkk)
    o_f, o_b = (_wkv_scan(rs, _to_scan_layout(w2, d), _to_scan_layout(k2, d), vs, as_, _to_scan_layout(b2, d),
                          n_lat=n_lat, reverse=(d == 1)) for d in range(2))
    return _rwkv_readout(_from_scan_layout(o_f, o_b, b), r, k2, v, g, p, x, modtab, n_lat=n_lat, rows=rows)


def _rope_tables(n_lat, n_ctx, head_dim):
    rows = jnp.repeat(jnp.arange(n_lat // GRID_W, dtype=I32), GRID_W).astype(F32)
    cols = jnp.tile(jnp.arange(GRID_W, dtype=I32), n_lat // GRID_W).astype(F32)
    n_freq = head_dim // 4
    inv_freq = ROPE_THETA ** (-jnp.arange(n_freq, dtype=F32) / n_freq)
    ang = jnp.concatenate([rows[:, None] * inv_freq, cols[:, None] * inv_freq], axis=-1)
    cos, sin = jnp.cos(ang), jnp.sin(ang)
    reps = 128 // head_dim
    cos = jnp.tile(jnp.concatenate([cos, cos], axis=-1), (1, reps))
    sin = jnp.tile(jnp.concatenate([-sin, sin], axis=-1), (1, reps))
    cos = jnp.concatenate([cos, jnp.ones((n_ctx, 128), F32)], axis=0)
    sin = jnp.concatenate([sin, jnp.zeros((n_ctx, 128), F32)], axis=0)
    return cos, sin


def kernel(x, c, ctx, c_ctx, mod_w, mod_b, a_w_qkv, a_w_o, a_q_norm, a_k_norm, b_w_qkv, b_w_o, b_sink,
           c_mu, c_w_rkv, c_w_o, c_w0, c_w1, c_w2, c_a0, c_a1, c_a2, c_g1, c_g2, c_k_k, c_k_a, c_r_k,
           c_ln_w, c_ln_b, router_w, ffn_w1, ffn_w3, ffn_w2, final_norm):
    b, n_lat, d = x.shape
    n_ctx = ctx.shape[1]
    t_all = n_lat + n_ctx
    depth = mod_w.shape[0]
    assert d == D_MODEL and n_ctx % ROW_TILE == 0 and n_lat % min(COMBINE_TILE, n_lat) == 0

    cond_rows = -(-(b + 1) // 8) * 8
    cond = jnp.zeros((cond_rows, d), F32).at[:b].set(c).at[b].set(c_ctx)
    mods = _mod_tables(cond, mod_w, mod_b).reshape(depth, cond_rows, 6, d)
    cos_a, sin_a = _rope_tables(n_lat, n_ctx, A_HEAD_DIM)
    cos_b, sin_b = _rope_tables(n_lat, n_ctx, B_HEAD_DIM)
    xs = jnp.concatenate([x, ctx], axis=1)
    for i in range(depth):
        last = i == depth - 1
        rows = n_lat if last else t_all
        kind, j = i % N_MIXERS, i // N_MIXERS
        modtab = jnp.stack([jnp.broadcast_to(mods[i, b], (b, 6, d)), mods[i, :b]], axis=1)
        if kind == 0:
            q, k, v = _qkv_proj(xs, modtab, a_w_qkv[j].astype(BF16), a_q_norm[j].reshape(1, -1),
                                a_k_norm[j].reshape(1, -1), cos_a, sin_a, kind=0, n_lat=n_lat)
            xs = _global_attention(q, k, v, a_w_o[j].astype(BF16), xs, modtab, n_lat=n_lat, rows=rows)
        elif kind == 1:
            ones = jnp.ones((1, 128), F32)
            q, k, v = _qkv_proj(xs, modtab, b_w_qkv[j].astype(BF16), ones, ones, cos_b, sin_b,
                                kind=1, n_lat=n_lat)
            xs = _window_attention(q, k, v, b_sink[j], b_w_o[j].astype(BF16), xs, modtab, n_lat=n_lat, rows=rows)
        else:
            p = dict(mu=c_mu[j], w_rkv=c_w_rkv[j].astype(BF16), w_o=c_w_o[j].astype(BF16), w0=c_w0[j],
                     w1=c_w1[j].astype(BF16), w2=c_w2[j].astype(BF16), a0=c_a0[j],
                     a1=c_a1[j].astype(BF16), a2=c_a2[j].astype(BF16), g1=c_g1[j].astype(BF16),
                     g2=c_g2[j].astype(BF16), k_k=c_k_k[j].reshape(1, d), k_a=c_k_a[j].reshape(1, d),
                     r_k=c_r_k[j].reshape(2, d), ln_w=c_ln_w[j].reshape(1, d), ln_b=c_ln_b[j].reshape(1, d))
            xs = _rwkv_mixer(xs, modtab, p, n_lat=n_lat, rows=rows)
        xs = _moe(xs, modtab, router_w[i].T, ffn_w1, ffn_w3, ffn_w2, i, final_norm, n_lat=n_lat,
                  with_ctx=not last)
    return xs
```

```python
import functools
import math

import jax
import jax.numpy as jnp
from jax import lax
from jax.experimental import pallas as pl
from jax.experimental.pallas import tpu as pltpu

F32 = jnp.float32
BF16 = jnp.bfloat16
I32 = jnp.int32

D_MODEL = 1024
GRID_W = 64
Q_BLOCK = 128
ROPE_THETA = 10000.0
NORM_EPS = 1e-6
NEG_INF = -1e30
A_HEADS, A_KV_HEADS, A_HEAD_DIM = 8, 2, 128
B_HEADS, B_KV_HEADS, B_HEAD_DIM = 16, 4, 64
WINDOW = 128
C_HEAD_DIM = 64
C_HEADS = D_MODEL // C_HEAD_DIM
C_GN_EPS = C_HEAD_DIM * 1e-5
N_EXPERTS = 16
EXPERT_FF = 2 * D_MODEL
CAPACITY_FACTOR = 2
N_MIXERS = 3
LOG2E = math.log2(math.e)

ROW_TILE = 256
COMBINE_TILE = 1024
COMBINE_EXPERTS = 8
SAMPLE_GROUP = 2
SLOT_TILE = 128
SCAN_CHUNK = 32
VMEM_LIMIT = 56 * 1024 * 1024
FFN_VMEM_LIMIT = 60 * 1024 * 1024


def _params(*sem, vmem_limit=VMEM_LIMIT):
    return pltpu.CompilerParams(dimension_semantics=sem, vmem_limit_bytes=vmem_limit)


def _rms(x):
    return x * lax.rsqrt(jnp.mean(x * x, axis=-1, keepdims=True) + NORM_EPS)


def _normmod(x, mod_ref, shift_row, scale_row):
    return _rms(x) * (1.0 + mod_ref[scale_row:scale_row + 1, :]) + mod_ref[shift_row:shift_row + 1, :]


def _bdot(a, b):
    return jnp.dot(a.astype(BF16), b.astype(BF16), preferred_element_type=F32)


def _dot_nt(a, b):
    return lax.dot_general(a.astype(BF16), b.astype(BF16), (((1,), (1,)), ((), ())),
                           preferred_element_type=F32)


def _sigmoid(x):
    return 1.0 / (1.0 + jnp.exp(-x))


def _mod_spec(n_lat_tiles, group=None):
    return pl.BlockSpec((group, None, 6, D_MODEL),
                        lambda b, t: (b, jnp.where(t < n_lat_tiles, 1, 0), 0, 0))


def _sample_group(b):
    return SAMPLE_GROUP if b % SAMPLE_GROUP == 0 else 1


def _mod_kernel(cond_ref, w_ref, b_ref, o_ref):
    c = cond_ref[...]
    a = c * _sigmoid(c)
    o_ref[0] = _bdot(a, w_ref[0]) + b_ref[0]


def _mod_tables(cond, mod_w, mod_b):
    depth, d, n = mod_w.shape
    rows = cond.shape[0]
    return pl.pallas_call(
        _mod_kernel,
        grid=(depth, n // d),
        in_specs=[pl.BlockSpec((rows, d), lambda i, j: (0, 0)),
                  pl.BlockSpec((1, d, d), lambda i, j: (i, 0, j)),
                  pl.BlockSpec((1, 1, d), lambda i, j: (i, 0, j))],
        out_specs=pl.BlockSpec((1, rows, d), lambda i, j: (i, 0, j)),
        out_shape=jax.ShapeDtypeStruct((depth, rows, n), F32),
        compiler_params=_params("arbitrary", "arbitrary"),
        name="mod_tables",
    )(cond, mod_w, mod_b.reshape(depth, 1, n))


def _rope128(x, cos, sin_signed, half):
    if half == 64:
        rot = pltpu.roll(x, 64, axis=1)
    else:
        lane = lax.broadcasted_iota(I32, x.shape, 1)
        rot = jnp.where((lane % 64) < 32, pltpu.roll(x, 96, axis=1), pltpu.roll(x, 32, axis=1))
    return x * cos + rot * sin_signed


def _qkv_kernel(x_ref, mod_ref, *refs, kind):
    shared, outs = refs[:5], refs[5:]
    for s in range(x_ref.shape[0]):
        _qkv_tile(x_ref.at[s], mod_ref.at[s], *shared, *(o.at[s] for o in outs), kind=kind)


def _qkv_tile(x_ref, mod_ref, w_ref, gq_ref, gk_ref, cos_ref, sin_ref, q_ref, k_ref, v_ref, *, kind):
    h = _normmod(x_ref[...], mod_ref, 0, 1)
    y = _bdot(h, w_ref[...])
    cos, sin = cos_ref[...], sin_ref[...]
    nq = D_MODEL
    nkv = (y.shape[1] - nq) // 2
    half = 64 if kind == 0 else 32
    q_scale = (2 * half) ** -0.5 * LOG2E
    for j in range((nq + nkv) // 128):
        s = y[:, j * 128:(j + 1) * 128]
        if kind == 0:
            gain = gq_ref[...] if j < nq // 128 else gk_ref[...]
            s = _rms(s) * gain
        s = _rope128(s, cos, sin, half)
        if j < nq // 128:
            q_ref[:, j * 128:(j + 1) * 128] = (s * q_scale).astype(BF16)
        elif kind == 0:
            k_ref[:, (j - nq // 128) * 128:(j - nq // 128 + 1) * 128] = s.astype(BF16)
        else:
            jj = (j - nq // 128) * 2
            k_ref[jj] = s[:, :64].astype(BF16)
            k_ref[jj + 1] = s[:, 64:].astype(BF16)
    v = y[:, nq + nkv:].astype(BF16)
    if kind == 0:
        v_ref[...] = v
    else:
        for jj in range(nkv // 64):
            v_ref[jj] = v[:, jj * 64:(jj + 1) * 64]


def _qkv_proj(x, modtab, w, gq, gk, cos, sin, *, kind, n_lat):
    b, t_all, d = x.shape
    tm = ROW_TILE
    nt = t_all // tm
    n = w.shape[1]
    nkv = (n - d) // 2
    grp = _sample_group(b)
    if kind == 0:
        kv_shape = jax.ShapeDtypeStruct((b, t_all, nkv), BF16)
        kv_spec = pl.BlockSpec((grp, tm, nkv), lambda i, t: (i, t, 0))
    else:
        kv_shape = jax.ShapeDtypeStruct((b, nkv // 64, t_all, 64), BF16)
        kv_spec = pl.BlockSpec((grp, nkv // 64, tm, 64), lambda i, t: (i, 0, t, 0))
    return pl.pallas_call(
        functools.partial(_qkv_kernel, kind=kind),
        grid=(b // grp, nt),
        in_specs=[pl.BlockSpec((grp, tm, d), lambda i, t: (i, t, 0)),
                  _mod_spec(n_lat // tm, grp),
                  pl.BlockSpec((d, n), lambda i, t: (0, 0)),
                  pl.BlockSpec((1, 128), lambda i, t: (0, 0)),
                  pl.BlockSpec((1, 128), lambda i, t: (0, 0)),
                  pl.BlockSpec((tm, 128), lambda i, t: (t, 0)),
                  pl.BlockSpec((tm, 128), lambda i, t: (t, 0))],
        out_specs=[pl.BlockSpec((grp, tm, d), lambda i, t: (i, t, 0)), kv_spec, kv_spec],
        out_shape=[jax.ShapeDtypeStruct((b, t_all, d), BF16), kv_shape, kv_shape],
        compiler_params=_params("parallel", "parallel"),
        name=f"qkv_proj_{kind}",
    )(x, modtab, w, gq, gk, cos, sin)


def _flash_chunk(carry, q, kc, vc):
    m, l, acc = carry
    s = _dot_nt(q, kc)
    m_new = jnp.maximum(m, jnp.max(s, axis=-1, keepdims=True))
    alpha = jnp.exp2(m - m_new)
    p = jnp.exp2(s - m_new)
    l = alpha * l + jnp.sum(p, axis=-1, keepdims=True)
    acc = alpha * acc + jnp.dot(p.astype(BF16), vc, preferred_element_type=F32)
    return m_new, l, acc


def _gattn_kernel(q_ref, k_ref, v_ref, wo_ref, x_ref, mod_ref, o_ref, att_ref, *, n_lat, kchunk):
    hd = A_HEAD_DIM
    g = A_HEADS // A_KV_HEADS
    tq = q_ref.shape[0]
    t_all = k_ref.shape[0]
    groups = range(A_KV_HEADS)
    qs = [jnp.concatenate([q_ref[:, (j * g + i) * hd:(j * g + i + 1) * hd] for i in range(g)], axis=0)
          for j in groups]
    init = (jnp.full((g * tq, 1), NEG_INF, F32), jnp.zeros((g * tq, 1), F32), jnp.zeros((g * tq, hd), F32))

    def chunk(j, carry, rows):
        return _flash_chunk(carry, qs[j], k_ref[rows, j * hd:(j + 1) * hd], v_ref[rows, j * hd:(j + 1) * hd])

    def finish(carries):
        for j, (_, l, acc) in enumerate(carries):
            o = (acc / l).astype(BF16)
            for i in range(g):
                att_ref[:, (j * g + i) * hd:(j * g + i + 1) * hd] = o[i * tq:(i + 1) * tq]

    ctx_carries = tuple(chunk(j, init, slice(n_lat, t_all)) for j in groups)
    is_ctx = pl.program_id(1) >= n_lat // tq

    @pl.when(is_ctx)
    def _():
        finish(ctx_carries)

    @pl.when(jnp.logical_not(is_ctx))
    def _():
        def body(c, carries):
            rows = pl.ds(pl.multiple_of(c * kchunk, kchunk), kchunk)
            return tuple(chunk(j, carries[j], rows) for j in groups)
        finish(lax.fori_loop(0, n_lat // kchunk, body, ctx_carries))

    _project_residual(att_ref, wo_ref, x_ref, mod_ref, o_ref)


def _project_residual(att_ref, wo_ref, x_ref, mod_ref, o_ref):
    o_ref[...] = x_ref[...] + mod_ref[2:3, :] * jnp.dot(att_ref[...], wo_ref[...], preferred_element_type=F32)


def _global_attention(q, k, v, w_o, x, modtab, *, n_lat, rows):
    b, t_all, d = q.shape
    nkv = k.shape[2]
    tq = ROW_TILE
    return pl.pallas_call(
        functools.partial(_gattn_kernel, n_lat=n_lat, kchunk=min(2048, n_lat)),
        grid=(b, rows // tq),
        in_specs=[pl.BlockSpec((None, tq, d), lambda i, t: (i, t, 0)),
                  pl.BlockSpec((None, t_all, nkv), lambda i, t: (i, 0, 0)),
                  pl.BlockSpec((None, t_all, nkv), lambda i, t: (i, 0, 0)),
                  pl.BlockSpec((d, d), lambda i, t: (0, 0)),
                  pl.BlockSpec((None, tq, d), lambda i, t: (i, t, 0)),
                  _mod_spec(n_lat // tq)],
        out_specs=pl.BlockSpec((None, tq, d), lambda i, t: (i, t, 0)),
        out_shape=jax.ShapeDtypeStruct((b, rows, d), F32),
        scratch_shapes=[pltpu.VMEM((tq, d), BF16)],
        compiler_params=_params("parallel", "arbitrary"),
        name="global_attention",
    )(q, k, v, w_o, x, modtab)


def _wattn_kernel(q_ref, k_ref, v_ref, sink_ref, bias_ref, wo_ref, x_ref, mod_ref, o_ref, att_ref, *, n_lat):
    hd = B_HEAD_DIM
    g = B_HEADS // B_KV_HEADS
    tq = q_ref.shape[0]
    t_all = k_ref.shape[1]
    t = pl.program_id(1)
    n_tiles = n_lat // tq
    is_ctx = t >= n_tiles

    def group(j):
        q = jnp.concatenate([q_ref[:, (j * g + i) * hd:(j * g + i + 1) * hd] for i in range(g)], axis=0)
        sink = jnp.concatenate([jnp.broadcast_to(sink_ref[j, :, i:i + 1], (tq, 1)) for i in range(g)],
                               axis=0) * LOG2E
        kc, vc = k_ref[j, n_lat:t_all, :], v_ref[j, n_lat:t_all, :]
        return q, sink, _dot_nt(q, kc), vc

    def finish(j, o):
        o = o.astype(BF16)
        att_ref[:, j * g * hd:(j + 1) * g * hd] = jnp.concatenate([o[i * tq:(i + 1) * tq] for i in range(g)], axis=1)

    @pl.when(is_ctx)
    def _():
        for j in range(B_KV_HEADS):
            q, sink, s_ctx, vc = group(j)
            m = jnp.maximum(jnp.max(s_ctx, axis=-1, keepdims=True), sink)
            p = jnp.exp2(s_ctx - m)
            l = jnp.sum(p, axis=-1, keepdims=True) + jnp.exp2(sink - m)
            finish(j, jnp.dot(p.astype(BF16), vc, preferred_element_type=F32) / l)

    @pl.when(jnp.logical_not(is_ctx))
    def _():
        wlen = tq + 2 * WINDOW
        ws = jnp.clip(t * tq - WINDOW, 0, n_lat - wlen)
        start = pl.multiple_of(ws, 128)
        bias = bias_ref[jnp.where(t == 0, 0, jnp.where(t == n_tiles - 1, 2, 1))]
        bias = jnp.concatenate([bias] * g, axis=0)
        for j in range(B_KV_HEADS):
            q, sink, _, _ = group(j)
            keys = jnp.concatenate([k_ref[j, pl.ds(start, wlen), :], k_ref[j, n_lat:t_all, :]], axis=0)
            vals = jnp.concatenate([v_ref[j, pl.ds(start, wlen), :], v_ref[j, n_lat:t_all, :]], axis=0)
            s = _dot_nt(q, keys) + bias
            m = jnp.maximum(jnp.max(s, axis=-1, keepdims=True), sink)
            p = jnp.exp2(s - m)
            l = jnp.sum(p, axis=-1, keepdims=True) + jnp.exp2(sink - m)
            finish(j, jnp.dot(p.astype(BF16), vals, preferred_element_type=F32) / l)

    _project_residual(att_ref, wo_ref, x_ref, mod_ref, o_ref)


def _band_bias(tq, n_ctx):
    row = jnp.arange(tq, dtype=I32)[:, None]
    col = jnp.arange(tq + 2 * WINDOW, dtype=I32)[None, :]
    band = jnp.stack([jnp.where(jnp.abs(col - off - row) <= WINDOW, 0.0, NEG_INF).astype(F32)
                      for off in (0, WINDOW, 2 * WINDOW)])
    return jnp.concatenate([band, jnp.zeros((3, tq, n_ctx), F32)], axis=2)


def _window_attention(q, k, v, sink, w_o, x, modtab, *, n_lat, rows):
    b, t_all, d = q.shape
    hd, g = B_HEAD_DIM, B_HEADS // B_KV_HEADS
    tq = Q_BLOCK
    wlen = tq + 2 * WINDOW
    return pl.pallas_call(
        functools.partial(_wattn_kernel, n_lat=n_lat),
        grid=(b, rows // tq),
        in_specs=[pl.BlockSpec((None, tq, d), lambda i, t: (i, t, 0)),
                  pl.BlockSpec((None, B_KV_HEADS, t_all, hd), lambda i, t: (i, 0, 0, 0)),
                  pl.BlockSpec((None, B_KV_HEADS, t_all, hd), lambda i, t: (i, 0, 0, 0)),
                  pl.BlockSpec((B_KV_HEADS, 1, g), lambda i, t: (0, 0, 0)),
                  pl.BlockSpec((3, tq, wlen + t_all - n_lat), lambda i, t: (0, 0, 0)),
                  pl.BlockSpec((d, d), lambda i, t: (0, 0)),
                  pl.BlockSpec((None, tq, d), lambda i, t: (i, t, 0)),
                  _mod_spec(n_lat // tq)],
        out_specs=pl.BlockSpec((None, tq, d), lambda i, t: (i, t, 0)),
        out_shape=jax.ShapeDtypeStruct((b, rows, d), F32),
        scratch_shapes=[pltpu.VMEM((tq, d), BF16)],
        compiler_params=_params("parallel", "arbitrary"),
        name="window_attention",
    )(q, k, v, sink.reshape(B_KV_HEADS, 1, g), _band_bias(tq, t_all - n_lat), w_o, x, modtab)


def _router_kernel(x_ref, mod_ref, rwt_ref, h_ref, aff_ref, afft_ref):
    for s in range(x_ref.shape[0]):
        h = _normmod(x_ref[s], mod_ref.at[s], 3, 4)
        h_ref[s] = h.astype(BF16)
        logits_t = lax.dot_general(rwt_ref[...], h, (((1,), (1,)), ((), ())),
                                   precision=lax.Precision.HIGHEST, preferred_element_type=F32)
        e = jnp.exp(logits_t - jnp.max(logits_t, axis=0, keepdims=True))
        aff_t = e / jnp.sum(e, axis=0, keepdims=True)
        afft_ref[s] = aff_t
        aff_ref[s] = aff_t.T


def _router(x, modtab, router_w_t, *, n_lat):
    b, rows, d = x.shape
    tm = ROW_TILE
    ne = router_w_t.shape[0]
    grp = _sample_group(b)
    return pl.pallas_call(
        _router_kernel,
        grid=(b // grp, rows // tm),
        in_specs=[pl.BlockSpec((grp, tm, d), lambda i, t: (i, t, 0)),
                  _mod_spec(n_lat // tm, grp),
                  pl.BlockSpec((ne, d), lambda i, t: (0, 0))],
        out_specs=[pl.BlockSpec((grp, tm, d), lambda i, t: (i, t, 0)),
                   pl.BlockSpec((grp, tm, ne), lambda i, t: (i, t, 0)),
                   pl.BlockSpec((grp, ne, tm), lambda i, t: (i, 0, t))],
        out_shape=[jax.ShapeDtypeStruct((b, rows, d), BF16),
                   jax.ShapeDtypeStruct((b, rows, ne), F32),
                   jax.ShapeDtypeStruct((b, ne, rows), F32)],
        compiler_params=_params("parallel", "parallel"),
        name="router",
    )(x, modtab, router_w_t)


def _lane_prefix(mask_f, tri):
    n = mask_f.shape[1]
    run = jnp.zeros((mask_f.shape[0], 1), F32)
    out = []
    for j in range(n // 128):
        blk = mask_f[:, j * 128:(j + 1) * 128]
        incl = jnp.dot(blk.astype(BF16), tri, preferred_element_type=F32)
        out.append(incl - blk + run)
        run = run + incl[:, 127:128]
    return jnp.concatenate(out, axis=1)


def _select_kernel(afft_ref, slot_ref, slot_tm_ref, *, segments):
    r = lax.broadcasted_iota(I32, (128, 128), 0)
    c = lax.broadcasted_iota(I32, (128, 128), 1)
    tri = (r <= c).astype(BF16)
    for off, n, cap, base in segments:
        bits = lax.bitcast_convert_type(afft_ref[:, off:off + n], I32)
        cap_f = jnp.float32(cap)

        def body(i, thr):
            cand = thr | jnp.left_shift(jnp.int32(1), 30 - i)
            cnt = jnp.sum((bits >= cand).astype(F32), axis=1, keepdims=True)
            return jnp.where(cnt >= cap_f, cand, thr)

        thr = lax.fori_loop(0, 31, body, jnp.zeros((bits.shape[0], 1), I32))
        gt = (bits > thr).astype(F32)
        eq = (bits == thr).astype(F32)
        need = cap_f - jnp.sum(gt, axis=1, keepdims=True)
        sel = gt + eq * (_lane_prefix(eq, tri) < need).astype(F32)
        slot = _lane_prefix(sel, tri).astype(I32) + base
        slot = jnp.where(sel > 0.5, slot, -1)
        slot_ref[:, off:off + n] = slot
        slot_tm_ref[off:off + n, :] = slot.astype(F32).T


def _select(aff_t, segments):
    b, ne, rows = aff_t.shape
    return pl.pallas_call(
        functools.partial(_select_kernel, segments=segments),
        grid=(b,),
        in_specs=[pl.BlockSpec((None, ne, rows), lambda i: (i, 0, 0))],
        out_specs=[pl.BlockSpec((None, ne, rows), lambda i: (i, 0, 0)),
                   pl.BlockSpec((None, rows, ne), lambda i: (i, 0, 0))],
        out_shape=[jax.ShapeDtypeStruct((b, ne, rows), I32), jax.ShapeDtypeStruct((b, rows, ne), F32)],
        compiler_params=_params("parallel"),
        name="expert_select",
    )(aff_t)


def _gather_kernel(slot_ref, h_ref, xs_ref, *, segments):
    for off, n, cap, base in segments:
        slot = slot_ref[:, off:off + n]
        hseg = h_ref[off:off + n, :]
        st = min(SLOT_TILE, cap)
        for j in range(cap // st):
            ids = base + j * st + lax.broadcasted_iota(I32, (st, 1), 0)
            onehot = jnp.where(ids == slot, 1.0, 0.0).astype(BF16)
            rows = jnp.dot(onehot, hseg, preferred_element_type=F32)
            xs_ref[base + j * st:base + (j + 1) * st, :] = rows.astype(BF16)


def _gather(slot, h, segments, n_slots):
    b, ne, rows = slot.shape
    d = h.shape[2]
    return pl.pallas_call(
        functools.partial(_gather_kernel, segments=segments),
        grid=(b, ne),
        in_specs=[pl.BlockSpec((None, None, 1, rows), lambda i, e: (i, e, 0, 0)),
                  pl.BlockSpec((None, rows, d), lambda i, e: (i, 0, 0))],
        out_specs=pl.BlockSpec((None, None, n_slots, d), lambda i, e: (e, i, 0, 0)),
        out_shape=jax.ShapeDtypeStruct((ne, b, n_slots, d), BF16),
        compiler_params=_params("parallel", "arbitrary"),
        name="expert_gather",
    )(slot.reshape(b, ne, 1, rows), h)


def _ffn_kernel(xs_ref, w1_ref, w3_ref, w2_ref, y_ref):
    xs = xs_ref[...]
    a = _bdot(xs, w1_ref[...])
    g = _bdot(xs, w3_ref[...])
    hid = (a * _sigmoid(a)) * g
    y_ref[...] = _bdot(hid, w2_ref[...]).astype(BF16)


def _expert_ffn(xs, w1, w3, w2, layer):
    ne, b, n_slots, d = xs.shape
    ff = w1.shape[3]
    ahead = pl.Buffered(2, use_lookahead=True)
    in_specs = [pl.BlockSpec((None, None, n_slots, d), lambda e, i: (e, i, 0, 0)),
                pl.BlockSpec((None, None, d, ff), lambda e, i: (layer, e, 0, 0), pipeline_mode=ahead),
                pl.BlockSpec((None, None, d, ff), lambda e, i: (layer, e, 0, 0), pipeline_mode=ahead),
                pl.BlockSpec((None, None, ff, d), lambda e, i: (layer, e, 0, 0), pipeline_mode=ahead)]
    out_specs = [pl.BlockSpec((None, None, n_slots, d), lambda e, i: (e, i, 0, 0))]

    def pipeline(xs_hbm, w1_hbm, w3_hbm, w2_hbm, y_hbm):
        pltpu.emit_pipeline(_ffn_kernel, grid=(ne, b), in_specs=in_specs, out_specs=out_specs)(
            xs_hbm, w1_hbm, w3_hbm, w2_hbm, y_hbm)

    return pl.pallas_call(
        pipeline,
        in_specs=[pl.BlockSpec(memory_space=pl.ANY)] * 4,
        out_specs=pl.BlockSpec(memory_space=pl.ANY),
        out_shape=jax.ShapeDtypeStruct((ne, b, n_slots, d), BF16),
        compiler_params=pltpu.CompilerParams(vmem_limit_bytes=FFN_VMEM_LIMIT),
        name="expert_ffn",
    )(xs, w1, w3, w2)


def _combine_kernel(slot_ref, aff_ref, y_ref, x_ref, mod_ref, fw_ref, o_ref, acc_ref, *, slot0, final):
    eg = pl.program_id(2)
    n_groups = pl.num_programs(2)
    group, width = y_ref.shape[:2]

    @pl.when(eg == 0)
    def _():
        acc_ref[...] = jnp.zeros_like(acc_ref)

    lane_e = lax.broadcasted_iota(I32, aff_ref.shape, 1)
    ids = slot0 + lax.broadcasted_iota(I32, (1, width), 1)
    total = None
    for j in range(group):
        e = eg * group + j
        gate = jnp.sum(jnp.where(lane_e == e, aff_ref[...], 0.0), axis=1, keepdims=True)
        slot = jnp.sum(jnp.where(lane_e == e, slot_ref[...], 0.0), axis=1, keepdims=True).astype(I32)
        onehot = jnp.where(slot == ids, 1.0, 0.0).astype(BF16)
        part = gate * jnp.dot(onehot, y_ref[j], preferred_element_type=F32)
        total = part if total is None else total + part
    acc_ref[...] += total

    @pl.when(eg == n_groups - 1)
    def _():
        out = x_ref[...] + mod_ref[5:6, :] * acc_ref[...]
        o_ref[...] = _rms(out) * fw_ref[...] if final else out


def _combine(slot_tm, aff, y, x, modtab, final_w, *, row0, rows, tm, slot0, width, seg, in_place, final=False):
    b, _, d = x.shape
    ne = aff.shape[2]
    group = COMBINE_EXPERTS
    t0, s0 = row0 // tm, slot0 // width
    out_rows = x.shape[1] if in_place else rows
    return pl.pallas_call(
        functools.partial(_combine_kernel, slot0=slot0, final=final),
        grid=(b, rows // tm, ne // group),
        in_specs=[pl.BlockSpec((None, tm, ne), lambda i, t, e: (i, t + t0, 0)),
                  pl.BlockSpec((None, tm, ne), lambda i, t, e: (i, t + t0, 0)),
                  pl.BlockSpec((group, None, width, d), lambda i, t, e: (e, i, s0, 0)),
                  pl.BlockSpec((None, tm, d), lambda i, t, e: (i, t + t0, 0)),
                  pl.BlockSpec((None, None, 6, d), lambda i, t, e: (i, seg, 0, 0)),
                  pl.BlockSpec((1, d), lambda i, t, e: (0, 0))],
        out_specs=pl.BlockSpec((None, tm, d), lambda i, t, e: (i, t + (t0 if in_place else 0), 0)),
        out_shape=jax.ShapeDtypeStruct((b, out_rows, d), F32),
        scratch_shapes=[pltpu.VMEM((tm, d), F32)],
        input_output_aliases={3: 0} if in_place else {},
        compiler_params=_params("parallel", "parallel", "arbitrary"),
        name="expert_combine",
    )(slot_tm, aff, y, x, modtab, final_w.reshape(1, d))


def _moe(x, modtab, router_w_t, w1, w3, w2, layer, final_w, *, n_lat, with_ctx):
    b, rows, d = x.shape
    ne = N_EXPERTS
    cap_l = CAPACITY_FACTOR * n_lat // ne
    h, aff, aff_t = _router(x, modtab, router_w_t, n_lat=n_lat)
    segments = ((0, n_lat, cap_l, 0),)
    n_slots = cap_l
    if with_ctx:
        n_ctx = rows - n_lat
        cap_c = CAPACITY_FACTOR * n_ctx // ne
        segments += ((n_lat, n_ctx, cap_c, cap_l),)
        n_slots += cap_c
    slot, slot_tm = _select(aff_t, segments)
    xs = _gather(slot, h, segments, n_slots)
    y = _expert_ffn(xs, w1, w3, w2, layer)
    out = _combine(slot_tm, aff, y, x, modtab, final_w, row0=0, rows=n_lat, tm=min(COMBINE_TILE, n_lat),
                   slot0=0, width=cap_l, seg=1, in_place=with_ctx, final=not with_ctx)
    if with_ctx:
        out = _combine(slot_tm, aff, y, out, modtab, final_w, row0=n_lat, rows=n_ctx, tm=n_ctx, slot0=cap_l,
                       width=cap_c, seg=0, in_place=True)
    return out


def _segsum64(x, bd):
    hi = x.astype(BF16)
    lo = (x - hi.astype(F32)).astype(BF16)
    out = []
    for j in range(x.shape[1] // 256):
        sl = slice(j * 256, (j + 1) * 256)
        out.append(jnp.dot(hi[:, sl], bd, preferred_element_type=F32)
                   + jnp.dot(lo[:, sl], bd, preferred_element_type=F32))
    return jnp.concatenate(out, axis=1)


def _block_diag_ones():
    r = lax.broadcasted_iota(I32, (256, 256), 0)
    c = lax.broadcasted_iota(I32, (256, 256), 1)
    return (r // 64 == c // 64).astype(BF16)


def _rwkv_feat_kernel(x_ref, xp_ref, xn_ref, mod_ref, mu_ref, wrkv_ref, w0_ref, w1_ref, w2_ref,
                      a0_ref, a1_ref, a2_ref, g1_ref, g2_ref, kk_ref, ka_ref,
                      r_out, v_out, nkk_out, g_out, w_out, k_out, b_out, *, n_lat_tiles):
    t = pl.program_id(1)
    nt = pl.num_programs(1)
    tm = x_ref.shape[0]
    h = _normmod(x_ref[...], mod_ref, 0, 1)
    hp = _normmod(xp_ref[7:8, :], mod_ref, 0, 1)
    hn = _normmod(xn_ref[0:1, :], mod_ref, 0, 1)
    has_left = jnp.logical_and(t != 0, t != n_lat_tiles)
    has_right = jnp.logical_and(t != n_lat_tiles - 1, t != nt - 1)
    hp = jnp.where(has_left, hp, 0.0)
    hn = jnp.where(has_right, hn, 0.0)
    row = lax.broadcasted_iota(I32, h.shape, 0)
    left = jnp.where(row == 0, hp, pltpu.roll(h, 1, axis=0))
    right = jnp.where(row == tm - 1, hn, pltpu.roll(h, tm - 1, axis=0))
    xx = 0.5 * (left + right) - h
    xr, xw, xk, xv, xa, xg = (h + xx * mu_ref[i:i + 1, :] for i in range(6))
    r = _bdot(xr, wrkv_ref[0])
    k = _bdot(xk, wrkv_ref[1])
    v = _bdot(xv, wrkv_ref[2])
    g = _bdot(_sigmoid(_bdot(xg, g1_ref[...])), g2_ref[...])
    bd = _block_diag_ones()
    kk = k * kk_ref[...]
    kk = kk * lax.rsqrt(jnp.maximum(_segsum64(kk * kk, bd), 1e-24))
    r_out[...] = r
    v_out[...] = v
    nkk_out[...] = -kk
    g_out[...] = g
    for d in range(2):
        w_lora = _bdot(jnp.tanh(_bdot(xw, w1_ref[d])), w2_ref[d])
        w_out[d] = jnp.exp(-math.exp(-0.5) * _sigmoid(w0_ref[d:d + 1, :] + w_lora))
        a = _sigmoid(a0_ref[d:d + 1, :] + _bdot(_bdot(xa, a1_ref[d]), a2_ref[d]))
        k_out[d] = k * (1.0 + (a - 1.0) * ka_ref[...])
        b_out[d] = kk * a


def _rwkv_features(x, modtab, p, *, n_lat):
    b, t_all, d = x.shape
    tm = ROW_TILE
    nt = t_all // tm
    tb = tm // 8
    full = lambda shape: pl.BlockSpec(shape, lambda i, t: (0,) * len(shape))
    tok = pl.BlockSpec((None, tm, d), lambda i, t: (i, t, 0))
    tok2 = pl.BlockSpec((2, None, tm, d), lambda i, t: (0, i, t, 0))
    one = jax.ShapeDtypeStruct((b, t_all, d), F32)
    two = jax.ShapeDtypeStruct((2, b, t_all, d), F32)
    return pl.pallas_call(
        functools.partial(_rwkv_feat_kernel, n_lat_tiles=n_lat // tm),
        grid=(b, nt),
        in_specs=[tok,
                  pl.BlockSpec((None, 8, d), lambda i, t: (i, jnp.maximum(t * tb - 1, 0), 0)),
                  pl.BlockSpec((None, 8, d), lambda i, t: (i, jnp.minimum((t + 1) * tb, nt * tb - 1), 0)),
                  _mod_spec(n_lat // tm),
                  full((6, d)), full((3, d, d)), full((2, d)), full(p["w1"].shape), full(p["w2"].shape),
                  full((2, d)), full(p["a1"].shape), full(p["a2"].shape), full(p["g1"].shape),
                  full(p["g2"].shape), full((1, d)), full((1, d))],
        out_specs=[tok, tok, tok, tok, tok2, tok2, tok2],
        out_shape=[one, one, one, one, two, two, two],
        compiler_params=_params("parallel", "parallel"),
        name="rwkv_features",
    )(x, x, x, modtab, p["mu"], p["w_rkv"], p["w0"], p["w1"], p["w2"], p["a0"], p["a1"], p["a2"],
      p["g1"], p["g2"], p["k_k"], p["k_a"])


def _scan_kernel(r_ref, w_ref, k_ref, v_ref, a_ref, b_ref, o_ref, s_ref, wr_ref, *, reverse):
    n = s_ref.shape[0]
    steps = r_ref.shape[0]

    @pl.when(pl.program_id(0) == 0)
    def _():
        s_ref[...] = jnp.zeros_like(s_ref)

    def step(i, carry):
        j = steps - 1 - i if reverse else i
        r = r_ref[j]
        wr_ref[...] = w_ref[j] * r
        br = jnp.sum(b_ref[j] * r, axis=0, keepdims=True)
        kr = jnp.sum(k_ref[j] * r, axis=0, keepdims=True)
        acc = [jnp.zeros(s_ref.shape[1:], F32) for _ in range(4)]
        for kk in range(n):
            s = s_ref[kk]
            acc[kk % 2] = acc[kk % 2] + s * a_ref[j, kk:kk + 1, :]
            acc[2 + kk % 2] = acc[2 + kk % 2] + s * wr_ref[kk:kk + 1, :]
        sa = acc[0] + acc[1]
        v = v_ref[j]
        o_ref[j] = acc[2] + acc[3] + sa * br + v * kr
        for kk in range(n):
            s_ref[kk] = (s_ref[kk] * w_ref[j, kk:kk + 1, :] + sa * b_ref[j, kk:kk + 1, :]
                         + v * k_ref[j, kk:kk + 1, :])
        return carry

    lax.fori_loop(0, steps, step, 0)


def _wkv_scan(r, w, k, v, a, b, *, n_lat, reverse):
    t_all, n, chains = r.shape
    tc = SCAN_CHUNK
    nlc, nch = n_lat // tc, t_all // tc
    ncc = nch - nlc
    if reverse:
        idx = lambda c: (jnp.where(c < ncc, nch - 1 - c, nlc - 1 - (c - ncc)), 0, 0)
    else:
        idx = lambda c: (jnp.where(c < ncc, nlc + c, c - ncc), 0, 0)
    spec = pl.BlockSpec((tc, n, chains), idx)
    return pl.pallas_call(
        functools.partial(_scan_kernel, reverse=reverse),
        grid=(nch,),
        in_specs=[spec] * 6,
        out_specs=spec,
        out_shape=jax.ShapeDtypeStruct((t_all, n, chains), F32),
        scratch_shapes=[pltpu.VMEM((n, n, chains), F32), pltpu.VMEM((n, chains), F32)],
        compiler_params=_params("arbitrary"),
        name="wkv_scan_bwd" if reverse else "wkv_scan_fwd",
    )(r, w, k, v, a, b)


def _rwkv_out_kernel(o_ref, r_ref, k_ref, v_ref, g_ref, rk_ref, lnw_ref, lnb_ref, wo_ref, x_ref, mod_ref,
                     out_ref):
    bd = _block_diag_ones()
    inv_n = 1.0 / C_HEAD_DIM
    o = o_ref[...]
    o = o - _segsum64(o, bd) * inv_n
    o = o * lax.rsqrt(_segsum64(o * o, bd) * inv_n + C_GN_EPS)
    o = o * lnw_ref[...] + lnb_ref[...]
    r = r_ref[...]
    bonus = _segsum64(r * k_ref[0] * rk_ref[0:1, :] + r * k_ref[1] * rk_ref[1:2, :], bd) * v_ref[...]
    y = _bdot((o + bonus) * g_ref[...], wo_ref[...])
    out_ref[...] = x_ref[...] + mod_ref[2:3, :] * y


def _rwkv_readout(o, r, k2, v, g, p, x, modtab, *, n_lat, rows):
    b, t_all, d = x.shape
    tm = ROW_TILE
    tok = pl.BlockSpec((None, tm, d), lambda i, t: (i, t, 0))
    full = lambda shape: pl.BlockSpec(shape, lambda i, t: (0,) * len(shape))
    return pl.pallas_call(
        _rwkv_out_kernel,
        grid=(b, rows // tm),
        in_specs=[tok, tok, pl.BlockSpec((2, None, tm, d), lambda i, t: (0, i, t, 0)), tok, tok,
                  full((2, d)), full((1, d)), full((1, d)), full((d, d)), tok, _mod_spec(n_lat // tm)],
        out_specs=tok,
        out_shape=jax.ShapeDtypeStruct((b, rows, d), F32),
        compiler_params=_params("parallel", "parallel"),
        name="rwkv_readout",
    )(o, r, k2, v, g, p["r_k"], p["ln_w"], p["ln_b"], p["w_o"], x, modtab)


def _to_scan_kernel(x_ref, o_ref, y_ref):
    nb, tt, d = x_ref.shape
    n, chains = o_ref.shape[1:]
    for b in range(nb):
        for p in range(d // 128):
            row = (b * (d // 128) + p) * 128
            y_ref[row:row + 128, :] = x_ref[b, :, p * 128:(p + 1) * 128].T
    for k in range(n):
        o_ref[:, k, :] = y_ref[pl.ds(k, chains, stride=n), :].T


def _to_scan_layout(a, d=None):
    b, t, dm = a.shape[-3:]
    tt = 128
    chains = b * dm // C_HEAD_DIM
    if d is None:
        spec = pl.BlockSpec((b, tt, dm), lambda i: (0, i, 0))
    else:
        spec = pl.BlockSpec((None, b, tt, dm), lambda i: (d, 0, i, 0))
    return pl.pallas_call(
        _to_scan_kernel,
        grid=(t // tt,),
        in_specs=[spec],
        out_specs=pl.BlockSpec((tt, C_HEAD_DIM, chains), lambda i: (i, 0, 0)),
        out_shape=jax.ShapeDtypeStruct((t, C_HEAD_DIM, chains), F32),
        scratch_shapes=[pltpu.VMEM((b * dm, tt), F32)],
        compiler_params=_params("parallel"),
        name="to_scan_layout",
    )(a)


def _from_scan_kernel(a_ref, b_ref, o_ref, y_ref):
    nb, tt, d = o_ref.shape
    n, chains = a_ref.shape[1:]
    for k in range(n):
        y_ref[pl.ds(k, chains, stride=n), :] = (a_ref[:, k, :] + b_ref[:, k, :]).T
    for b in range(nb):
        for p in range(d // 128):
            row = (b * (d // 128) + p) * 128
            o_ref[b, :, p * 128:(p + 1) * 128] = y_ref[row:row + 128, :].T


def _from_scan_layout(o_f, o_b, b):
    t, n, chains = o_f.shape
    tt = 128
    dm = n * chains // b
    spec = pl.BlockSpec((tt, n, chains), lambda i: (i, 0, 0))
    return pl.pallas_call(
        _from_scan_kernel,
        grid=(t // tt,),
        in_specs=[spec, spec],
        out_specs=pl.BlockSpec((b, tt, dm), lambda i: (0, i, 0)),
        out_shape=jax.ShapeDtypeStruct((b, t, dm), F32),
        scratch_shapes=[pltpu.VMEM((b * dm, tt), F32)],
        compiler_params=_params("parallel"),
        name="from_scan_layout",
    )(o_f, o_b)


def _rwkv_mixer(x, modtab, p, *, n_lat, rows):
    b = x.shape[0]
    r, v, nkk, g, w2, k2, b2 = _rwkv_features(x, modtab, p, n_lat=n_lat)
    rs, vs, as_ = _to_scan_layout(r), _to_scan_layout(v), _to_scan_layout(nkk)
    o_f, o_b = (_wkv_scan(rs, _to_scan_layout(w2, d), _to_scan_layout(k2, d), vs, as_, _to_scan_layout(b2, d),
                          n_lat=n_lat, reverse=(d == 1)) for d in range(2))
    return _rwkv_readout(_from_scan_layout(o_f, o_b, b), r, k2, v, g, p, x, modtab, n_lat=n_lat, rows=rows)


def _rope_tables(n_lat, n_ctx, head_dim):
    rows = jnp.repeat(jnp.arange(n_lat // GRID_W, dtype=I32), GRID_W).astype(F32)
    cols = jnp.tile(jnp.arange(GRID_W, dtype=I32), n_lat // GRID_W).astype(F32)
    n_freq = head_dim // 4
    inv_freq = ROPE_THETA ** (-jnp.arange(n_freq, dtype=F32) / n_freq)
    ang = jnp.concatenate([rows[:, None] * inv_freq, cols[:, None] * inv_freq], axis=-1)
    cos, sin = jnp.cos(ang), jnp.sin(ang)
    reps = 128 // head_dim
    cos = jnp.tile(jnp.concatenate([cos, cos], axis=-1), (1, reps))
    sin = jnp.tile(jnp.concatenate([-sin, sin], axis=-1), (1, reps))
    cos = jnp.concatenate([cos, jnp.ones((n_ctx, 128), F32)], axis=0)
    sin = jnp.concatenate([sin, jnp.zeros((n_ctx, 128), F32)], axis=0)
    return cos, sin


def kernel(x, c, ctx, c_ctx, mod_w, mod_b, a_w_qkv, a_w_o, a_q_norm, a_k_norm, b_w_qkv, b_w_o, b_sink,
           c_mu, c_w_rkv, c_w_o, c_w0, c_w1, c_w2, c_a0, c_a1, c_a2, c_g1, c_g2, c_k_k, c_k_a, c_r_k,
           c_ln_w, c_ln_b, router_w, ffn_w1, ffn_w3, ffn_w2, final_norm):
    b, n_lat, d = x.shape
    n_ctx = ctx.shape[1]
    t_all = n_lat + n_ctx
    depth = mod_w.shape[0]
    assert d == D_MODEL and n_ctx % ROW_TILE == 0 and n_lat % min(COMBINE_TILE, n_lat) == 0

    cond_rows = -(-(b + 1) // 8) * 8
    cond = jnp.zeros((cond_rows, d), F32).at[:b].set(c).at[b].set(c_ctx)
    mods = _mod_tables(cond, mod_w, mod_b).reshape(depth, cond_rows, 6, d)
    cos_a, sin_a = _rope_tables(n_lat, n_ctx, A_HEAD_DIM)
    cos_b, sin_b = _rope_tables(n_lat, n_ctx, B_HEAD_DIM)
    xs = jnp.concatenate([x, ctx], axis=1)
    for i in range(depth):
        last = i == depth - 1
        rows = n_lat if last else t_all
        kind, j = i % N_MIXERS, i // N_MIXERS
        modtab = jnp.stack([jnp.broadcast_to(mods[i, b], (b, 6, d)), mods[i, :b]], axis=1)
        if kind == 0:
            q, k, v = _qkv_proj(xs, modtab, a_w_qkv[j].astype(BF16), a_q_norm[j].reshape(1, -1),
                                a_k_norm[j].reshape(1, -1), cos_a, sin_a, kind=0, n_lat=n_lat)
            xs = _global_attention(q, k, v, a_w_o[j].astype(BF16), xs, modtab, n_lat=n_lat, rows=rows)
        elif kind == 1:
            ones = jnp.ones((1, 128), F32)
            q, k, v = _qkv_proj(xs, modtab, b_w_qkv[j].astype(BF16), ones, ones, cos_b, sin_b,
                                kind=1, n_lat=n_lat)
            xs = _window_attention(q, k, v, b_sink[j], b_w_o[j].astype(BF16), xs, modtab, n_lat=n_lat, rows=rows)
        else:
            p = dict(mu=c_mu[j], w_rkv=c_w_rkv[j].astype(BF16), w_o=c_w_o[j].astype(BF16), w0=c_w0[j],
                     w1=c_w1[j].astype(BF16), w2=c_w2[j].astype(BF16), a0=c_a0[j],
                     a1=c_a1[j].astype(BF16), a2=c_a2[j].astype(BF16), g1=c_g1[j].astype(BF16),
                     g2=c_g2[j].astype(BF16), k_k=c_k_k[j].reshape(1, d), k_a=c_k_a[j].reshape(1, d),
                     r_k=c_r_k[j].reshape(2, d), ln_w=c_ln_w[j].reshape(1, d), ln_b=c_ln_b[j].reshape(1, d))
            xs = _rwkv_mixer(xs, modtab, p, n_lat=n_lat, rows=rows)
        xs = _moe(xs, modtab, router_w[i].T, ffn_w1, ffn_w3, ffn_w2, i, final_norm, n_lat=n_lat,
                  with_ctx=not last)
    return xs
```

```python
import functools
import math

import jax
import jax.numpy as jnp
from jax import lax
from jax.experimental import pallas as pl
from jax.experimental.pallas import tpu as pltpu

F32 = jnp.float32
BF16 = jnp.bfloat16
I32 = jnp.int32

D_MODEL = 1024
GRID_W = 64
Q_BLOCK = 128
ROPE_THETA = 10000.0
NORM_EPS = 1e-6
NEG_INF = -1e30
A_HEADS, A_KV_HEADS, A_HEAD_DIM = 8, 2, 128
B_HEADS, B_KV_HEADS, B_HEAD_DIM = 16, 4, 64
WINDOW = 128
C_HEAD_DIM = 64
C_HEADS = D_MODEL // C_HEAD_DIM
C_GN_EPS = C_HEAD_DIM * 1e-5
N_EXPERTS = 16
EXPERT_FF = 2 * D_MODEL
CAPACITY_FACTOR = 2
N_MIXERS = 3
LOG2E = math.log2(math.e)

ROW_TILE = 256
COMBINE_TILE = 1024
COMBINE_EXPERTS = 8
SAMPLE_GROUP = 2
SLOT_TILE = 128
SCAN_CHUNK = 64
VMEM_LIMIT = 56 * 1024 * 1024
FFN_VMEM_LIMIT = 60 * 1024 * 1024


def _params(*sem, vmem_limit=VMEM_LIMIT):
    return pltpu.CompilerParams(dimension_semantics=sem, vmem_limit_bytes=vmem_limit)


def _rms(x):
    return x * lax.rsqrt(jnp.mean(x * x, axis=-1, keepdims=True) + NORM_EPS)


def _normmod(x, mod_ref, shift_row, scale_row):
    return _rms(x) * (1.0 + mod_ref[scale_row:scale_row + 1, :]) + mod_ref[shift_row:shift_row + 1, :]


def _bdot(a, b):
    return jnp.dot(a.astype(BF16), b.astype(BF16), preferred_element_type=F32)


def _dot_nt(a, b):
    return lax.dot_general(a.astype(BF16), b.astype(BF16), (((1,), (1,)), ((), ())),
                           preferred_element_type=F32)


def _sigmoid(x):
    return 1.0 / (1.0 + jnp.exp(-x))


def _mod_spec(n_lat_tiles, group=None):
    return pl.BlockSpec((group, None, 6, D_MODEL),
                        lambda b, t: (b, jnp.where(t < n_lat_tiles, 1, 0), 0, 0))


def _sample_group(b):
    return SAMPLE_GROUP if b % SAMPLE_GROUP == 0 else 1


def _mod_kernel(cond_ref, w_ref, b_ref, o_ref):
    c = cond_ref[...]
    a = c * _sigmoid(c)
    o_ref[0] = _bdot(a, w_ref[0]) + b_ref[0]


def _mod_tables(cond, mod_w, mod_b):
    depth, d, n = mod_w.shape
    rows = cond.shape[0]
    return pl.pallas_call(
        _mod_kernel,
        grid=(depth, n // d),
        in_specs=[pl.BlockSpec((rows, d), lambda i, j: (0, 0)),
                  pl.BlockSpec((1, d, d), lambda i, j: (i, 0, j)),
                  pl.BlockSpec((1, 1, d), lambda i, j: (i, 0, j))],
        out_specs=pl.BlockSpec((1, rows, d), lambda i, j: (i, 0, j)),
        out_shape=jax.ShapeDtypeStruct((depth, rows, n), F32),
        compiler_params=_params("arbitrary", "arbitrary"),
        name="mod_tables",
    )(cond, mod_w, mod_b.reshape(depth, 1, n))


def _rope128(x, cos, sin_signed, half):
    if half == 64:
        rot = pltpu.roll(x, 64, axis=1)
    else:
        lane = lax.broadcasted_iota(I32, x.shape, 1)
        rot = jnp.where((lane % 64) < 32, pltpu.roll(x, 96, axis=1), pltpu.roll(x, 32, axis=1))
    return x * cos + rot * sin_signed


def _qkv_kernel(x_ref, mod_ref, *refs, kind):
    shared, outs = refs[:5], refs[5:]
    for s in range(x_ref.shape[0]):
        _qkv_tile(x_ref.at[s], mod_ref.at[s], *shared, *(o.at[s] for o in outs), kind=kind)


def _qkv_tile(x_ref, mod_ref, w_ref, gq_ref, gk_ref, cos_ref, sin_ref, q_ref, k_ref, v_ref, *, kind):
    h = _normmod(x_ref[...], mod_ref, 0, 1)
    y = _bdot(h, w_ref[...])
    cos, sin = cos_ref[...], sin_ref[...]
    nq = D_MODEL
    nkv = (y.shape[1] - nq) // 2
    half = 64 if kind == 0 else 32
    q_scale = (2 * half) ** -0.5 * LOG2E
    for j in range((nq + nkv) // 128):
        s = y[:, j * 128:(j + 1) * 128]
        if kind == 0:
            gain = gq_ref[...] if j < nq // 128 else gk_ref[...]
            s = _rms(s) * gain
        s = _rope128(s, cos, sin, half)
        if j < nq // 128:
            q_ref[:, j * 128:(j + 1) * 128] = (s * q_scale).astype(BF16)
        elif kind == 0:
            k_ref[:, (j - nq // 128) * 128:(j - nq // 128 + 1) * 128] = s.astype(BF16)
        else:
            jj = (j - nq // 128) * 2
            k_ref[jj] = s[:, :64].astype(BF16)
            k_ref[jj + 1] = s[:, 64:].astype(BF16)
    v = y[:, nq + nkv:].astype(BF16)
    if kind == 0:
        v_ref[...] = v
    else:
        for jj in range(nkv // 64):
            v_ref[jj] = v[:, jj * 64:(jj + 1) * 64]


def _qkv_proj(x, modtab, w, gq, gk, cos, sin, *, kind, n_lat):
    b, t_all, d = x.shape
    tm = ROW_TILE
    nt = t_all // tm
    n = w.shape[1]
    nkv = (n - d) // 2
    grp = _sample_group(b)
    if kind == 0:
        kv_shape = jax.ShapeDtypeStruct((b, t_all, nkv), BF16)
        kv_spec = pl.BlockSpec((grp, tm, nkv), lambda i, t: (i, t, 0))
    else:
        kv_shape = jax.ShapeDtypeStruct((b, nkv // 64, t_all, 64), BF16)
        kv_spec = pl.BlockSpec((grp, nkv // 64, tm, 64), lambda i, t: (i, 0, t, 0))
    return pl.pallas_call(
        functools.partial(_qkv_kernel, kind=kind),
        grid=(b // grp, nt),
        in_specs=[pl.BlockSpec((grp, tm, d), lambda i, t: (i, t, 0)),
                  _mod_spec(n_lat // tm, grp),
                  pl.BlockSpec((d, n), lambda i, t: (0, 0)),
                  pl.BlockSpec((1, 128), lambda i, t: (0, 0)),
                  pl.BlockSpec((1, 128), lambda i, t: (0, 0)),
                  pl.BlockSpec((tm, 128), lambda i, t: (t, 0)),
                  pl.BlockSpec((tm, 128), lambda i, t: (t, 0))],
        out_specs=[pl.BlockSpec((grp, tm, d), lambda i, t: (i, t, 0)), kv_spec, kv_spec],
        out_shape=[jax.ShapeDtypeStruct((b, t_all, d), BF16), kv_shape, kv_shape],
        compiler_params=_params("parallel", "parallel"),
        name=f"qkv_proj_{kind}",
    )(x, modtab, w, gq, gk, cos, sin)


def _flash_chunk(carry, q, kc, vc):
    m, l, acc = carry
    s = _dot_nt(q, kc)
    m_new = jnp.maximum(m, jnp.max(s, axis=-1, keepdims=True))
    alpha = jnp.exp2(m - m_new)
    p = jnp.exp2(s - m_new)
    l = alpha * l + jnp.sum(p, axis=-1, keepdims=True)
    acc = alpha * acc + jnp.dot(p.astype(BF16), vc, preferred_element_type=F32)
    return m_new, l, acc


def _gattn_kernel(q_ref, k_ref, v_ref, wo_ref, x_ref, mod_ref, o_ref, att_ref, *, n_lat, kchunk):
    hd = A_HEAD_DIM
    g = A_HEADS // A_KV_HEADS
    tq = q_ref.shape[0]
    t_all = k_ref.shape[0]
    groups = range(A_KV_HEADS)
    qs = [jnp.concatenate([q_ref[:, (j * g + i) * hd:(j * g + i + 1) * hd] for i in range(g)], axis=0)
          for j in groups]
    init = (jnp.full((g * tq, 1), NEG_INF, F32), jnp.zeros((g * tq, 1), F32), jnp.zeros((g * tq, hd), F32))

    def chunk(j, carry, rows):
        return _flash_chunk(carry, qs[j], k_ref[rows, j * hd:(j + 1) * hd], v_ref[rows, j * hd:(j + 1) * hd])

    def finish(carries):
        for j, (_, l, acc) in enumerate(carries):
            o = (acc / l).astype(BF16)
            for i in range(g):
                att_ref[:, (j * g + i) * hd:(j * g + i + 1) * hd] = o[i * tq:(i + 1) * tq]

    ctx_carries = tuple(chunk(j, init, slice(n_lat, t_all)) for j in groups)
    is_ctx = pl.program_id(1) >= n_lat // tq

    @pl.when(is_ctx)
    def _():
        finish(ctx_carries)

    @pl.when(jnp.logical_not(is_ctx))
    def _():
        def body(c, carries):
            rows = pl.ds(pl.multiple_of(c * kchunk, kchunk), kchunk)
            return tuple(chunk(j, carries[j], rows) for j in groups)
        finish(lax.fori_loop(0, n_lat // kchunk, body, ctx_carries))

    _project_residual(att_ref, wo_ref, x_ref, mod_ref, o_ref)


def _project_residual(att_ref, wo_ref, x_ref, mod_ref, o_ref):
    o_ref[...] = x_ref[...] + mod_ref[2:3, :] * jnp.dot(att_ref[...], wo_ref[...], preferred_element_type=F32)


def _global_attention(q, k, v, w_o, x, modtab, *, n_lat, rows):
    b, t_all, d = q.shape
    nkv = k.shape[2]
    tq = ROW_TILE
    return pl.pallas_call(
        functools.partial(_gattn_kernel, n_lat=n_lat, kchunk=min(2048, n_lat)),
        grid=(b, rows // tq),
        in_specs=[pl.BlockSpec((None, tq, d), lambda i, t: (i, t, 0)),
                  pl.BlockSpec((None, t_all, nkv), lambda i, t: (i, 0, 0)),
                  pl.BlockSpec((None, t_all, nkv), lambda i, t: (i, 0, 0)),
                  pl.BlockSpec((d, d), lambda i, t: (0, 0)),
                  pl.BlockSpec((None, tq, d), lambda i, t: (i, t, 0)),
                  _mod_spec(n_lat // tq)],
        out_specs=pl.BlockSpec((None, tq, d), lambda i, t: (i, t, 0)),
        out_shape=jax.ShapeDtypeStruct((b, rows, d), F32),
        scratch_shapes=[pltpu.VMEM((tq, d), BF16)],
        compiler_params=_params("parallel", "arbitrary"),
        name="global_attention",
    )(q, k, v, w_o, x, modtab)


def _wattn_kernel(q_ref, k_ref, v_ref, sink_ref, bias_ref, wo_ref, x_ref, mod_ref, o_ref, att_ref, *, n_lat):
    hd = B_HEAD_DIM
    g = B_HEADS // B_KV_HEADS
    tq = q_ref.shape[0]
    t_all = k_ref.shape[1]
    t = pl.program_id(1)
    n_tiles = n_lat // tq
    is_ctx = t >= n_tiles

    def group(j):
        q = jnp.concatenate([q_ref[:, (j * g + i) * hd:(j * g + i + 1) * hd] for i in range(g)], axis=0)
        sink = jnp.concatenate([jnp.broadcast_to(sink_ref[j, :, i:i + 1], (tq, 1)) for i in range(g)],
                               axis=0) * LOG2E
        kc, vc = k_ref[j, n_lat:t_all, :], v_ref[j, n_lat:t_all, :]
        return q, sink, _dot_nt(q, kc), vc

    def finish(j, o):
        o = o.astype(BF16)
        att_ref[:, j * g * hd:(j + 1) * g * hd] = jnp.concatenate([o[i * tq:(i + 1) * tq] for i in range(g)], axis=1)

    @pl.when(is_ctx)
    def _():
        for j in range(B_KV_HEADS):
            q, sink, s_ctx, vc = group(j)
            m = jnp.maximum(jnp.max(s_ctx, axis=-1, keepdims=True), sink)
            p = jnp.exp2(s_ctx - m)
            l = jnp.sum(p, axis=-1, keepdims=True) + jnp.exp2(sink - m)
            finish(j, jnp.dot(p.astype(BF16), vc, preferred_element_type=F32) / l)

    @pl.when(jnp.logical_not(is_ctx))
    def _():
        wlen = tq + 2 * WINDOW
        ws = jnp.clip(t * tq - WINDOW, 0, n_lat - wlen)
        start = pl.multiple_of(ws, 128)
        bias = bias_ref[jnp.where(t == 0, 0, jnp.where(t == n_tiles - 1, 2, 1))]
        bias = jnp.concatenate([bias] * g, axis=0)
        for j in range(B_KV_HEADS):
            q, sink, _, _ = group(j)
            keys = jnp.concatenate([k_ref[j, pl.ds(start, wlen), :], k_ref[j, n_lat:t_all, :]], axis=0)
            vals = jnp.concatenate([v_ref[j, pl.ds(start, wlen), :], v_ref[j, n_lat:t_all, :]], axis=0)
            s = _dot_nt(q, keys) + bias
            m = jnp.maximum(jnp.max(s, axis=-1, keepdims=True), sink)
            p = jnp.exp2(s - m)
            l = jnp.sum(p, axis=-1, keepdims=True) + jnp.exp2(sink - m)
            finish(j, jnp.dot(p.astype(BF16), vals, preferred_element_type=F32) / l)

    _project_residual(att_ref, wo_ref, x_ref, mod_ref, o_ref)


def _band_bias(tq, n_ctx):
    row = jnp.arange(tq, dtype=I32)[:, None]
    col = jnp.arange(tq + 2 * WINDOW, dtype=I32)[None, :]
    band = jnp.stack([jnp.where(jnp.abs(col - off - row) <= WINDOW, 0.0, NEG_INF).astype(F32)
                      for off in (0, WINDOW, 2 * WINDOW)])
    return jnp.concatenate([band, jnp.zeros((3, tq, n_ctx), F32)], axis=2)


def _window_attention(q, k, v, sink, w_o, x, modtab, *, n_lat, rows):
    b, t_all, d = q.shape
    hd, g = B_HEAD_DIM, B_HEADS // B_KV_HEADS
    tq = Q_BLOCK
    wlen = tq + 2 * WINDOW
    return pl.pallas_call(
        functools.partial(_wattn_kernel, n_lat=n_lat),
        grid=(b, rows // tq),
        in_specs=[pl.BlockSpec((None, tq, d), lambda i, t: (i, t, 0)),
                  pl.BlockSpec((None, B_KV_HEADS, t_all, hd), lambda i, t: (i, 0, 0, 0)),
                  pl.BlockSpec((None, B_KV_HEADS, t_all, hd), lambda i, t: (i, 0, 0, 0)),
                  pl.BlockSpec((B_KV_HEADS, 1, g), lambda i, t: (0, 0, 0)),
                  pl.BlockSpec((3, tq, wlen + t_all - n_lat), lambda i, t: (0, 0, 0)),
                  pl.BlockSpec((d, d), lambda i, t: (0, 0)),
                  pl.BlockSpec((None, tq, d), lambda i, t: (i, t, 0)),
                  _mod_spec(n_lat // tq)],
        out_specs=pl.BlockSpec((None, tq, d), lambda i, t: (i, t, 0)),
        out_shape=jax.ShapeDtypeStruct((b, rows, d), F32),
        scratch_shapes=[pltpu.VMEM((tq, d), BF16)],
        compiler_params=_params("parallel", "arbitrary"),
        name="window_attention",
    )(q, k, v, sink.reshape(B_KV_HEADS, 1, g), _band_bias(tq, t_all - n_lat), w_o, x, modtab)


def _router_kernel(x_ref, mod_ref, rwt_ref, h_ref, aff_ref, afft_ref):
    for s in range(x_ref.shape[0]):
        h = _normmod(x_ref[s], mod_ref.at[s], 3, 4)
        h_ref[s] = h.astype(BF16)
        logits_t = lax.dot_general(rwt_ref[...], h, (((1,), (1,)), ((), ())),
                                   precision=lax.Precision.HIGHEST, preferred_element_type=F32)
        e = jnp.exp(logits_t - jnp.max(logits_t, axis=0, keepdims=True))
        aff_t = e / jnp.sum(e, axis=0, keepdims=True)
        afft_ref[s] = aff_t
        aff_ref[s] = aff_t.T


def _router(x, modtab, router_w_t, *, n_lat):
    b, rows, d = x.shape
    tm = ROW_TILE
    ne = router_w_t.shape[0]
    grp = _sample_group(b)
    return pl.pallas_call(
        _router_kernel,
        grid=(b // grp, rows // tm),
        in_specs=[pl.BlockSpec((grp, tm, d), lambda i, t: (i, t, 0)),
                  _mod_spec(n_lat // tm, grp),
                  pl.BlockSpec((ne, d), lambda i, t: (0, 0))],
        out_specs=[pl.BlockSpec((grp, tm, d), lambda i, t: (i, t, 0)),
                   pl.BlockSpec((grp, tm, ne), lambda i, t: (i, t, 0)),
                   pl.BlockSpec((grp, ne, tm), lambda i, t: (i, 0, t))],
        out_shape=[jax.ShapeDtypeStruct((b, rows, d), BF16),
                   jax.ShapeDtypeStruct((b, rows, ne), F32),
                   jax.ShapeDtypeStruct((b, ne, rows), F32)],
        compiler_params=_params("parallel", "parallel"),
        name="router",
    )(x, modtab, router_w_t)


def _lane_prefix(mask_f, tri):
    n = mask_f.shape[1]
    run = jnp.zeros((mask_f.shape[0], 1), F32)
    out = []
    for j in range(n // 128):
        blk = mask_f[:, j * 128:(j + 1) * 128]
        incl = jnp.dot(blk.astype(BF16), tri, preferred_element_type=F32)
        out.append(incl - blk + run)
        run = run + incl[:, 127:128]
    return jnp.concatenate(out, axis=1)


def _select_kernel(afft_ref, slot_ref, slot_tm_ref, *, segments):
    r = lax.broadcasted_iota(I32, (128, 128), 0)
    c = lax.broadcasted_iota(I32, (128, 128), 1)
    tri = (r <= c).astype(BF16)
    for off, n, cap, base in segments:
        bits = lax.bitcast_convert_type(afft_ref[:, off:off + n], I32)
        cap_f = jnp.float32(cap)

        def body(i, thr):
            cand = thr | jnp.left_shift(jnp.int32(1), 30 - i)
            cnt = jnp.sum((bits >= cand).astype(F32), axis=1, keepdims=True)
            return jnp.where(cnt >= cap_f, cand, thr)

        thr = lax.fori_loop(0, 31, body, jnp.zeros((bits.shape[0], 1), I32))
        gt = (bits > thr).astype(F32)
        eq = (bits == thr).astype(F32)
        need = cap_f - jnp.sum(gt, axis=1, keepdims=True)
        sel = gt + eq * (_lane_prefix(eq, tri) < need).astype(F32)
        slot = _lane_prefix(sel, tri).astype(I32) + base
        slot = jnp.where(sel > 0.5, slot, -1)
        slot_ref[:, off:off + n] = slot
        slot_tm_ref[off:off + n, :] = slot.astype(F32).T


def _select(aff_t, segments):
    b, ne, rows = aff_t.shape
    return pl.pallas_call(
        functools.partial(_select_kernel, segments=segments),
        grid=(b,),
        in_specs=[pl.BlockSpec((None, ne, rows), lambda i: (i, 0, 0))],
        out_specs=[pl.BlockSpec((None, ne, rows), lambda i: (i, 0, 0)),
                   pl.BlockSpec((None, rows, ne), lambda i: (i, 0, 0))],
        out_shape=[jax.ShapeDtypeStruct((b, ne, rows), I32), jax.ShapeDtypeStruct((b, rows, ne), F32)],
        compiler_params=_params("parallel"),
        name="expert_select",
    )(aff_t)


def _gather_kernel(slot_ref, h_ref, xs_ref, *, segments):
    for off, n, cap, base in segments:
        slot = slot_ref[:, off:off + n]
        hseg = h_ref[off:off + n, :]
        st = min(SLOT_TILE, cap)
        for j in range(cap // st):
            ids = base + j * st + lax.broadcasted_iota(I32, (st, 1), 0)
            onehot = jnp.where(ids == slot, 1.0, 0.0).astype(BF16)
            rows = jnp.dot(onehot, hseg, preferred_element_type=F32)
            xs_ref[base + j * st:base + (j + 1) * st, :] = rows.astype(BF16)


def _gather(slot, h, segments, n_slots):
    b, ne, rows = slot.shape
    d = h.shape[2]
    return pl.pallas_call(
        functools.partial(_gather_kernel, segments=segments),
        grid=(b, ne),
        in_specs=[pl.BlockSpec((None, None, 1, rows), lambda i, e: (i, e, 0, 0)),
                  pl.BlockSpec((None, rows, d), lambda i, e: (i, 0, 0))],
        out_specs=pl.BlockSpec((None, None, n_slots, d), lambda i, e: (e, i, 0, 0)),
        out_shape=jax.ShapeDtypeStruct((ne, b, n_slots, d), BF16),
        compiler_params=_params("parallel", "arbitrary"),
        name="expert_gather",
    )(slot.reshape(b, ne, 1, rows), h)


def _ffn_kernel(xs_ref, w1_ref, w3_ref, w2_ref, y_ref):
    xs = xs_ref[...]
    a = _bdot(xs, w1_ref[...])
    g = _bdot(xs, w3_ref[...])
    hid = (a * _sigmoid(a)) * g
    y_ref[...] = _bdot(hid, w2_ref[...]).astype(BF16)


def _expert_ffn(xs, w1, w3, w2, layer):
    ne, b, n_slots, d = xs.shape
    ff = w1.shape[3]
    ahead = pl.Buffered(2, use_lookahead=True)
    in_specs = [pl.BlockSpec((None, None, n_slots, d), lambda e, i: (e, i, 0, 0)),
                pl.BlockSpec((None, None, d, ff), lambda e, i: (layer, e, 0, 0), pipeline_mode=ahead),
                pl.BlockSpec((None, None, d, ff), lambda e, i: (layer, e, 0, 0), pipeline_mode=ahead),
                pl.BlockSpec((None, None, ff, d), lambda e, i: (layer, e, 0, 0), pipeline_mode=ahead)]
    out_specs = [pl.BlockSpec((None, None, n_slots, d), lambda e, i: (e, i, 0, 0))]

    def pipeline(xs_hbm, w1_hbm, w3_hbm, w2_hbm, y_hbm):
        pltpu.emit_pipeline(_ffn_kernel, grid=(ne, b), in_specs=in_specs, out_specs=out_specs)(
            xs_hbm, w1_hbm, w3_hbm, w2_hbm, y_hbm)

    return pl.pallas_call(
        pipeline,
        in_specs=[pl.BlockSpec(memory_space=pl.ANY)] * 4,
        out_specs=pl.BlockSpec(memory_space=pl.ANY),
        out_shape=jax.ShapeDtypeStruct((ne, b, n_slots, d), BF16),
        compiler_params=pltpu.CompilerParams(vmem_limit_bytes=FFN_VMEM_LIMIT),
        name="expert_ffn",
    )(xs, w1, w3, w2)


def _combine_kernel(slot_ref, aff_ref, y_ref, x_ref, mod_ref, fw_ref, o_ref, acc_ref, *, slot0, final):
    eg = pl.program_id(2)
    n_groups = pl.num_programs(2)
    group, width = y_ref.shape[:2]

    @pl.when(eg == 0)
    def _():
        acc_ref[...] = jnp.zeros_like(acc_ref)

    lane_e = lax.broadcasted_iota(I32, aff_ref.shape, 1)
    ids = slot0 + lax.broadcasted_iota(I32, (1, width), 1)
    total = None
    for j in range(group):
        e = eg * group + j
        gate = jnp.sum(jnp.where(lane_e == e, aff_ref[...], 0.0), axis=1, keepdims=True)
        slot = jnp.sum(jnp.where(lane_e == e, slot_ref[...], 0.0), axis=1, keepdims=True).astype(I32)
        onehot = jnp.where(slot == ids, 1.0, 0.0).astype(BF16)
        part = gate * jnp.dot(onehot, y_ref[j], preferred_element_type=F32)
        total = part if total is None else total + part
    acc_ref[...] += total

    @pl.when(eg == n_groups - 1)
    def _():
        out = x_ref[...] + mod_ref[5:6, :] * acc_ref[...]
        o_ref[...] = _rms(out) * fw_ref[...] if final else out


def _combine(slot_tm, aff, y, x, modtab, final_w, *, row0, rows, tm, slot0, width, seg, in_place, final=False):
    b, _, d = x.shape
    ne = aff.shape[2]
    group = COMBINE_EXPERTS
    t0, s0 = row0 // tm, slot0 // width
    out_rows = x.shape[1] if in_place else rows
    return pl.pallas_call(
        functools.partial(_combine_kernel, slot0=slot0, final=final),
        grid=(b, rows // tm, ne // group),
        in_specs=[pl.BlockSpec((None, tm, ne), lambda i, t, e: (i, t + t0, 0)),
                  pl.BlockSpec((None, tm, ne), lambda i, t, e: (i, t + t0, 0)),
                  pl.BlockSpec((group, None, width, d), lambda i, t, e: (e, i, s0, 0)),
                  pl.BlockSpec((None, tm, d), lambda i, t, e: (i, t + t0, 0)),
                  pl.BlockSpec((None, None, 6, d), lambda i, t, e: (i, seg, 0, 0)),
                  pl.BlockSpec((1, d), lambda i, t, e: (0, 0))],
        out_specs=pl.BlockSpec((None, tm, d), lambda i, t, e: (i, t + (t0 if in_place else 0), 0)),
        out_shape=jax.ShapeDtypeStruct((b, out_rows, d), F32),
        scratch_shapes=[pltpu.VMEM((tm, d), F32)],
        input_output_aliases={3: 0} if in_place else {},
        compiler_params=_params("parallel", "parallel", "arbitrary"),
        name="expert_combine",
    )(slot_tm, aff, y, x, modtab, final_w.reshape(1, d))


def _moe(x, modtab, router_w_t, w1, w3, w2, layer, final_w, *, n_lat, with_ctx):
    b, rows, d = x.shape
    ne = N_EXPERTS
    cap_l = CAPACITY_FACTOR * n_lat // ne
    h, aff, aff_t = _router(x, modtab, router_w_t, n_lat=n_lat)
    segments = ((0, n_lat, cap_l, 0),)
    n_slots = cap_l
    if with_ctx:
        n_ctx = rows - n_lat
        cap_c = CAPACITY_FACTOR * n_ctx // ne
        segments += ((n_lat, n_ctx, cap_c, cap_l),)
        n_slots += cap_c
    slot, slot_tm = _select(aff_t, segments)
    xs = _gather(slot, h, segments, n_slots)
    y = _expert_ffn(xs, w1, w3, w2, layer)
    out = _combine(slot_tm, aff, y, x, modtab, final_w, row0=0, rows=n_lat, tm=min(COMBINE_TILE, n_lat),
                   slot0=0, width=cap_l, seg=1, in_place=with_ctx, final=not with_ctx)
    if with_ctx:
        out = _combine(slot_tm, aff, y, out, modtab, final_w, row0=n_lat, rows=n_ctx, tm=n_ctx, slot0=cap_l,
                       width=cap_c, seg=0, in_place=True)
    return out


def _segsum64(x, bd):
    hi = x.astype(BF16)
    lo = (x - hi.astype(F32)).astype(BF16)
    out = []
    for j in range(x.shape[1] // 256):
        sl = slice(j * 256, (j + 1) * 256)
        out.append(jnp.dot(hi[:, sl], bd, preferred_element_type=F32)
                   + jnp.dot(lo[:, sl], bd, preferred_element_type=F32))
    return jnp.concatenate(out, axis=1)


def _block_diag_ones():
    r = lax.broadcasted_iota(I32, (256, 256), 0)
    c = lax.broadcasted_iota(I32, (256, 256), 1)
    return (r // 64 == c // 64).astype(BF16)


def _rwkv_feat_kernel(x_ref, xp_ref, xn_ref, mod_ref, mu_ref, wrkv_ref, w0_ref, w1_ref, w2_ref,
                      a0_ref, a1_ref, a2_ref, g1_ref, g2_ref, kk_ref, ka_ref,
                      r_out, v_out, nkk_out, g_out, w_out, k_out, b_out, *, n_lat_tiles):
    t = pl.program_id(1)
    nt = pl.num_programs(1)
    tm = x_ref.shape[0]
    h = _normmod(x_ref[...], mod_ref, 0, 1)
    hp = _normmod(xp_ref[7:8, :], mod_ref, 0, 1)
    hn = _normmod(xn_ref[0:1, :], mod_ref, 0, 1)
    has_left = jnp.logical_and(t != 0, t != n_lat_tiles)
    has_right = jnp.logical_and(t != n_lat_tiles - 1, t != nt - 1)
    hp = jnp.where(has_left, hp, 0.0)
    hn = jnp.where(has_right, hn, 0.0)
    row = lax.broadcasted_iota(I32, h.shape, 0)
    left = jnp.where(row == 0, hp, pltpu.roll(h, 1, axis=0))
    right = jnp.where(row == tm - 1, hn, pltpu.roll(h, tm - 1, axis=0))
    xx = 0.5 * (left + right) - h
    xr, xw, xk, xv, xa, xg = (h + xx * mu_ref[i:i + 1, :] for i in range(6))
    r = _bdot(xr, wrkv_ref[0])
    k = _bdot(xk, wrkv_ref[1])
    v = _bdot(xv, wrkv_ref[2])
    g = _bdot(_sigmoid(_bdot(xg, g1_ref[...])), g2_ref[...])
    bd = _block_diag_ones()
    kk = k * kk_ref[...]
    kk = kk * lax.rsqrt(jnp.maximum(_segsum64(kk * kk, bd), 1e-24))
    r_out[...] = r
    v_out[...] = v
    nkk_out[...] = -kk
    g_out[...] = g
    for d in range(2):
        w_lora = _bdot(jnp.tanh(_bdot(xw, w1_ref[d])), w2_ref[d])
        w_out[d] = jnp.exp(-math.exp(-0.5) * _sigmoid(w0_ref[d:d + 1, :] + w_lora))
        a = _sigmoid(a0_ref[d:d + 1, :] + _bdot(_bdot(xa, a1_ref[d]), a2_ref[d]))
        k_out[d] = k * (1.0 + (a - 1.0) * ka_ref[...])
        b_out[d] = kk * a


def _rwkv_features(x, modtab, p, *, n_lat):
    b, t_all, d = x.shape
    tm = ROW_TILE
    nt = t_all // tm
    tb = tm // 8
    full = lambda shape: pl.BlockSpec(shape, lambda i, t: (0,) * len(shape))
    tok = pl.BlockSpec((None, tm, d), lambda i, t: (i, t, 0))
    tok2 = pl.BlockSpec((2, None, tm, d), lambda i, t: (0, i, t, 0))
    one = jax.ShapeDtypeStruct((b, t_all, d), F32)
    two = jax.ShapeDtypeStruct((2, b, t_all, d), F32)
    return pl.pallas_call(
        functools.partial(_rwkv_feat_kernel, n_lat_tiles=n_lat // tm),
        grid=(b, nt),
        in_specs=[tok,
                  pl.BlockSpec((None, 8, d), lambda i, t: (i, jnp.maximum(t * tb - 1, 0), 0)),
                  pl.BlockSpec((None, 8, d), lambda i, t: (i, jnp.minimum((t + 1) * tb, nt * tb - 1), 0)),
                  _mod_spec(n_lat // tm),
                  full((6, d)), full((3, d, d)), full((2, d)), full(p["w1"].shape), full(p["w2"].shape),
                  full((2, d)), full(p["a1"].shape), full(p["a2"].shape), full(p["g1"].shape),
                  full(p["g2"].shape), full((1, d)), full((1, d))],
        out_specs=[tok, tok, tok, tok, tok2, tok2, tok2],
        out_shape=[one, one, one, one, two, two, two],
        compiler_params=_params("parallel", "parallel"),
        name="rwkv_features",
    )(x, x, x, modtab, p["mu"], p["w_rkv"], p["w0"], p["w1"], p["w2"], p["a0"], p["a1"], p["a2"],
      p["g1"], p["g2"], p["k_k"], p["k_a"])


def _scan_kernel(r_ref, w_ref, k_ref, v_ref, a_ref, b_ref, o_ref, s_ref, wr_ref, *, reverse):
    n = s_ref.shape[0]
    steps = r_ref.shape[0]

    @pl.when(pl.program_id(0) == 0)
    def _():
        s_ref[...] = jnp.zeros_like(s_ref)

    def step(i, carry):
        j = steps - 1 - i if reverse else i
        r = r_ref[j]
        wr_ref[...] = w_ref[j] * r
        br = jnp.sum(b_ref[j] * r, axis=0, keepdims=True)
        kr = jnp.sum(k_ref[j] * r, axis=0, keepdims=True)
        acc = [jnp.zeros(s_ref.shape[1:], F32) for _ in range(4)]
        for kk in range(n):
            s = s_ref[kk]
            acc[kk % 2] = acc[kk % 2] + s * a_ref[j, kk:kk + 1, :]
            acc[2 + kk % 2] = acc[2 + kk % 2] + s * wr_ref[kk:kk + 1, :]
        sa = acc[0] + acc[1]
        v = v_ref[j]
        o_ref[j] = acc[2] + acc[3] + sa * br + v * kr
        for kk in range(n):
            s_ref[kk] = (s_ref[kk] * w_ref[j, kk:kk + 1, :] + sa * b_ref[j, kk:kk + 1, :]
                         + v * k_ref[j, kk:kk + 1, :])
        return carry

    lax.fori_loop(0, steps, step, 0)


def _wkv_scan(r, w, k, v, a, b, *, n_lat, reverse):
    t_all, n, chains = r.shape
    tc = SCAN_CHUNK
    nlc, nch = n_lat // tc, t_all // tc
    ncc = nch - nlc
    if reverse:
        idx = lambda c: (jnp.where(c < ncc, nch - 1 - c, nlc - 1 - (c - ncc)), 0, 0)
    else:
        idx = lambda c: (jnp.where(c < ncc, nlc + c, c - ncc), 0, 0)
    spec = pl.BlockSpec((tc, n, chains), idx)
    return pl.pallas_call(
        functools.partial(_scan_kernel, reverse=reverse),
        grid=(nch,),
        in_specs=[spec] * 6,
        out_specs=spec,
        out_shape=jax.ShapeDtypeStruct((t_all, n, chains), F32),
        scratch_shapes=[pltpu.VMEM((n, n, chains), F32), pltpu.VMEM((n, chains), F32)],
        compiler_params=_params("arbitrary"),
        name="wkv_scan_bwd" if reverse else "wkv_scan_fwd",
    )(r, w, k, v, a, b)


def _rwkv_out_kernel(o_ref, r_ref, k_ref, v_ref, g_ref, rk_ref, lnw_ref, lnb_ref, wo_ref, x_ref, mod_ref,
                     out_ref):
    bd = _block_diag_ones()
    inv_n = 1.0 / C_HEAD_DIM
    o = o_ref[...]
    o = o - _segsum64(o, bd) * inv_n
    o = o * lax.rsqrt(_segsum64(o * o, bd) * inv_n + C_GN_EPS)
    o = o * lnw_ref[...] + lnb_ref[...]
    r = r_ref[...]
    bonus = _segsum64(r * k_ref[0] * rk_ref[0:1, :] + r * k_ref[1] * rk_ref[1:2, :], bd) * v_ref[...]
    y = _bdot((o + bonus) * g_ref[...], wo_ref[...])
    out_ref[...] = x_ref[...] + mod_ref[2:3, :] * y


def _rwkv_readout(o, r, k2, v, g, p, x, modtab, *, n_lat, rows):
    b, t_all, d = x.shape
    tm = ROW_TILE
    tok = pl.BlockSpec((None, tm, d), lambda i, t: (i, t, 0))
    full = lambda shape: pl.BlockSpec(shape, lambda i, t: (0,) * len(shape))
    return pl.pallas_call(
        _rwkv_out_kernel,
        grid=(b, rows // tm),
        in_specs=[tok, tok, pl.BlockSpec((2, None, tm, d), lambda i, t: (0, i, t, 0)), tok, tok,
                  full((2, d)), full((1, d)), full((1, d)), full((d, d)), tok, _mod_spec(n_lat // tm)],
        out_specs=tok,
        out_shape=jax.ShapeDtypeStruct((b, rows, d), F32),
        compiler_params=_params("parallel", "parallel"),
        name="rwkv_readout",
    )(o, r, k2, v, g, p["r_k"], p["ln_w"], p["ln_b"], p["w_o"], x, modtab)


def _to_scan_kernel(x_ref, o_ref, y_ref):
    nb, tt, d = x_ref.shape
    n, chains = o_ref.shape[1:]
    for b in range(nb):
        for p in range(d // 128):
            row = (b * (d // 128) + p) * 128
            y_ref[row:row + 128, :] = x_ref[b, :, p * 128:(p + 1) * 128].T
    for k in range(n):
        o_ref[:, k, :] = y_ref[pl.ds(k, chains, stride=n), :].T


def _to_scan_layout(a, d=None):
    b, t, dm = a.shape[-3:]
    tt = 128
    chains = b * dm // C_HEAD_DIM
    if d is None:
        spec = pl.BlockSpec((b, tt, dm), lambda i: (0, i, 0))
    else:
        spec = pl.BlockSpec((None, b, tt, dm), lambda i: (d, 0, i, 0))
    return pl.pallas_call(
        _to_scan_kernel,
        grid=(t // tt,),
        in_specs=[spec],
        out_specs=pl.BlockSpec((tt, C_HEAD_DIM, chains), lambda i: (i, 0, 0)),
        out_shape=jax.ShapeDtypeStruct((t, C_HEAD_DIM, chains), F32),
        scratch_shapes=[pltpu.VMEM((b * dm, tt), F32)],
        compiler_params=_params("parallel"),
        name="to_scan_layout",
    )(a)


def _from_scan_kernel(a_ref, b_ref, o_ref, y_ref):
    nb, tt, d = o_ref.shape
    n, chains = a_ref.shape[1:]
    for k in range(n):
        y_ref[pl.ds(k, chains, stride=n), :] = (a_ref[:, k, :] + b_ref[:, k, :]).T
    for b in range(nb):
        for p in range(d // 128):
            row = (b * (d // 128) + p) * 128
            o_ref[b, :, p * 128:(p + 1) * 128] = y_ref[row:row + 128, :].T


def _from_scan_layout(o_f, o_b, b):
    t, n, chains = o_f.shape
    tt = 128
    dm = n * chains // b
    spec = pl.BlockSpec((tt, n, chains), lambda i: (i, 0, 0))
    return pl.pallas_call(
        _from_scan_kernel,
        grid=(t // tt,),
        in_specs=[spec, spec],
        out_specs=pl.BlockSpec((b, tt, dm), lambda i: (0, i, 0)),
        out_shape=jax.ShapeDtypeStruct((b, t, dm), F32),
        scratch_shapes=[pltpu.VMEM((b * dm, tt), F32)],
        compiler_params=_params("parallel"),
        name="from_scan_layout",
    )(o_f, o_b)


def _rwkv_mixer(x, modtab, p, *, n_lat, rows):
    b = x.shape[0]
    r, v, nkk, g, w2, k2, b2 = _rwkv_features(x, modtab, p, n_lat=n_lat)
    rs, vs, as_ = _to_scan_layout(r), _to_scan_layout(v), _to_scan_layout(nkk)
    o_f, o_b = (_wkv_scan(rs, _to_scan_layout(w2, d), _to_scan_layout(k2, d), vs, as_, _to_scan_layout(b2, d),
                          n_lat=n_lat, reverse=(d == 1)) for d in range(2))
    return _rwkv_readout(_from_scan_layout(o_f, o_b, b), r, k2, v, g, p, x, modtab, n_lat=n_lat, rows=rows)


def _rope_tables(n_lat, n_ctx, head_dim):
    rows = jnp.repeat(jnp.arange(n_lat // GRID_W, dtype=I32), GRID_W).astype(F32)
    cols = jnp.tile(jnp.arange(GRID_W, dtype=I32), n_lat // GRID_W).astype(F32)
    n_freq = head_dim // 4
    inv_freq = ROPE_THETA ** (-jnp.arange(n_freq, dtype=F32) / n_freq)
    ang = jnp.concatenate([rows[:, None] * inv_freq, cols[:, None] * inv_freq], axis=-1)
    cos, sin = jnp.cos(ang), jnp.sin(ang)
    reps = 128 // head_dim
    cos = jnp.tile(jnp.concatenate([cos, cos], axis=-1), (1, reps))
    sin = jnp.tile(jnp.concatenate([-sin, sin], axis=-1), (1, reps))
    cos = jnp.concatenate([cos, jnp.ones((n_ctx, 128), F32)], axis=0)
    sin = jnp.concatenate([sin, jnp.zeros((n_ctx, 128), F32)], axis=0)
    return cos, sin


def kernel(x, c, ctx, c_ctx, mod_w, mod_b, a_w_qkv, a_w_o, a_q_norm, a_k_norm, b_w_qkv, b_w_o, b_sink,
           c_mu, c_w_rkv, c_w_o, c_w0, c_w1, c_w2, c_a0, c_a1, c_a2, c_g1, c_g2, c_k_k, c_k_a, c_r_k,
           c_ln_w, c_ln_b, router_w, ffn_w1, ffn_w3, ffn_w2, final_norm):
    b, n_lat, d = x.shape
    n_ctx = ctx.shape[1]
    t_all = n_lat + n_ctx
    depth = mod_w.shape[0]
    assert d == D_MODEL and n_ctx % ROW_TILE == 0 and n_lat % min(COMBINE_TILE, n_lat) == 0

    cond_rows = -(-(b + 1) // 8) * 8
    cond = jnp.zeros((cond_rows, d), F32).at[:b].set(c).at[b].set(c_ctx)
    mods = _mod_tables(cond, mod_w, mod_b).reshape(depth, cond_rows, 6, d)
    cos_a, sin_a = _rope_tables(n_lat, n_ctx, A_HEAD_DIM)
    cos_b, sin_b = _rope_tables(n_lat, n_ctx, B_HEAD_DIM)
    xs = jnp.concatenate([x, ctx], axis=1)
    for i in range(depth):
        last = i == depth - 1
        rows = n_lat if last else t_all
        kind, j = i % N_MIXERS, i // N_MIXERS
        modtab = jnp.stack([jnp.broadcast_to(mods[i, b], (b, 6, d)), mods[i, :b]], axis=1)
        if kind == 0:
            q, k, v = _qkv_proj(xs, modtab, a_w_qkv[j].astype(BF16), a_q_norm[j].reshape(1, -1),
                                a_k_norm[j].reshape(1, -1), cos_a, sin_a, kind=0, n_lat=n_lat)
            xs = _global_attention(q, k, v, a_w_o[j].astype(BF16), xs, modtab, n_lat=n_lat, rows=rows)
        elif kind == 1:
            ones = jnp.ones((1, 128), F32)
            q, k, v = _qkv_proj(xs, modtab, b_w_qkv[j].astype(BF16), ones, ones, cos_b, sin_b,
                                kind=1, n_lat=n_lat)
            xs = _window_attention(q, k, v, b_sink[j], b_w_o[j].astype(BF16), xs, modtab, n_lat=n_lat, rows=rows)
        else:
            p = dict(mu=c_mu[j], w_rkv=c_w_rkv[j].astype(BF16), w_o=c_w_o[j].astype(BF16), w0=c_w0[j],
                     w1=c_w1[j].astype(BF16), w2=c_w2[j].astype(BF16), a0=c_a0[j],
                     a1=c_a1[j].astype(BF16), a2=c_a2[j].astype(BF16), g1=c_g1[j].astype(BF16),
                     g2=c_g2[j].astype(BF16), k_k=c_k_k[j].reshape(1, d), k_a=c_k_a[j].reshape(1, d),
                     r_k=c_r_k[j].reshape(2, d), ln_w=c_ln_w[j].reshape(1, d), ln_b=c_ln_b[j].reshape(1, d))
            xs = _rwkv_mixer(xs, modtab, p, n_lat=n_lat, rows=rows)
        xs = _moe(xs, modtab, router_w[i].T, ffn_w1, ffn_w3, ffn_w2, i, final_norm, n_lat=n_lat,
                  with_ctx=not last)
    return xs
```
